```python
import math
import jax, jax.numpy as jnp
from jax import lax
import numpy as np

D_MODEL = 2048
BATCH = 2
SEQ = 16384
DEPTH = 2

NORM_EPS = 1e-6
ROPE_THETA = 10000.0
Q_BLOCK = 128
NEG_INF = -1e30
D_FF = 5632
N_BRANCH = 3

MLA_HEADS = 6
MLA_Q_LORA = 768
MLA_KV_LORA = 512
MLA_NOPE = 128
MLA_ROPE = 64
MLA_V = 128
MLA_OUT = MLA_HEADS * MLA_V

NSA_HEADS = 4
NSA_DK = 192
NSA_DV = 128
NSA_CMP_LEN = 32
NSA_CMP_STRIDE = 16
NSA_CMP_HIDDEN = 256
NSA_SEL_LEN = 64
NSA_TOPK = 16
NSA_WINDOW = 512
NSA_FORCE_SCORE = 1e6
NSA_OUT = NSA_HEADS * NSA_DV

DIFF_HEADS = 4
DIFF_HD = 96
DIFF_VD = 2 * DIFF_HD
DIFF_OUT = DIFF_HEADS * DIFF_VD

IN_SPLITS = (
    MLA_Q_LORA, MLA_KV_LORA, MLA_ROPE,
    NSA_HEADS * NSA_DK, NSA_DK, NSA_DV, NSA_DK, NSA_DV, NSA_DK, NSA_DV, NSA_HEADS * 3,
    DIFF_HEADS * 2 * DIFF_HD, DIFF_HEADS * 2 * DIFF_HD, DIFF_HEADS * DIFF_VD,
    N_BRANCH * D_MODEL,
)
D_IN = sum(IN_SPLITS)

kernel_name = "hybrid_mla_nsa_diff_macaron"


def rms_norm(x, w):
    xf = x.astype(jnp.float32)
    y = xf * lax.rsqrt(jnp.mean(xf * xf, axis=-1, keepdims=True) + NORM_EPS)
    return (y * w.astype(jnp.float32)).astype(x.dtype)


def rope_table(dim, seq):
    inv = ROPE_THETA ** (-jnp.arange(0, dim, 2, dtype=jnp.float32) / dim)
    ang = jnp.arange(seq, dtype=jnp.float32)[:, None] * inv[None, :]
    return jnp.cos(ang), jnp.sin(ang)


def apply_rope(x, cos, sin):
    d2 = x.shape[-1] // 2
    x1, x2 = x[..., :d2], x[..., d2:]
    c = cos.astype(x.dtype)
    s = sin.astype(x.dtype)
    return jnp.concatenate([x1 * c - x2 * s, x1 * s + x2 * c], axis=-1)


def swiglu(x, w_in, w_out):
    g, u = jnp.split(x @ w_in, 2, axis=-1)
    return (jax.nn.silu(g) * u) @ w_out


def split_cols(x, sizes):
    offs = np.cumsum(np.array(sizes))[:-1].tolist()
    return jnp.split(x, offs, axis=-1)


def blockwise(block_fn, seq):
    out = lax.map(block_fn, jnp.arange(seq // Q_BLOCK))
    nb, b, qb, w = out.shape
    return jnp.moveaxis(out, 0, 1).reshape(b, nb * qb, w)


def mla_mixer(c_q, c_kv, k_rope, q_norm_w, kv_norm_w, w_uq, w_ukv, cos, sin):
    B, S, _ = c_q.shape
    q = (rms_norm(c_q, q_norm_w) @ w_uq).reshape(B, S, MLA_HEADS, MLA_NOPE + MLA_ROPE).transpose(0, 2, 1, 3)
    q_nope = q[..., :MLA_NOPE]
    q_pe = apply_rope(q[..., MLA_NOPE:], cos, sin)
    kv = (rms_norm(c_kv, kv_norm_w) @ w_ukv).reshape(B, S, MLA_HEADS, MLA_NOPE + MLA_V).transpose(0, 2, 1, 3)
    k_nope, v = kv[..., :MLA_NOPE], kv[..., MLA_NOPE:]
    k_pe = apply_rope(k_rope, cos, sin)
    scale = (MLA_NOPE + MLA_ROPE) ** -0.5
    key_pos = jnp.arange(S)

    def block(i):
        qs = i * Q_BLOCK
        qn = lax.dynamic_slice_in_dim(q_nope, qs, Q_BLOCK, axis=2)
        qp = lax.dynamic_slice_in_dim(q_pe, qs, Q_BLOCK, axis=2)
        s = (jnp.einsum('bhqd,bhkd->bhqk', qn, k_nope)
             + jnp.einsum('bhqd,bkd->bhqk', qp, k_pe)).astype(jnp.float32) * scale
        q_pos = qs + jnp.arange(Q_BLOCK)
        s = jnp.where(key_pos[None, :] <= q_pos[:, None], s, NEG_INF)
        p = jax.nn.softmax(s, axis=-1).astype(v.dtype)
        o = jnp.einsum('bhqk,bhkd->bqhd', p, v)
        return o.reshape(B, Q_BLOCK, MLA_OUT)

    return blockwise(block, S)


def nsa_mixer(q, k_cmp, v_cmp, k_slc, v_slc, k_win, v_win, gate_logits,
              cmp_k_pos, cmp_k_w1, cmp_k_w2, cmp_v_pos, cmp_v_w1, cmp_v_w2, cos, sin):
    B, S, _ = q.shape
    q = apply_rope(q.reshape(B, S, NSA_HEADS, NSA_DK).transpose(0, 2, 1, 3), cos, sin)
    k_cmp = apply_rope(k_cmp, cos, sin)
    k_slc = apply_rope(k_slc, cos, sin)
    k_win = apply_rope(k_win, cos, sin)
    gates = jax.nn.sigmoid(gate_logits.astype(jnp.float32)).astype(q.dtype).reshape(B, S, NSA_HEADS, 3)
    scale = NSA_DK ** -0.5

    n_cmp = (S - NSA_CMP_LEN) // NSA_CMP_STRIDE + 1
    tok = jnp.arange(n_cmp)[:, None] * NSA_CMP_STRIDE + jnp.arange(NSA_CMP_LEN)[None, :]

    def compress(t, pos, w1, w2):
        blk = t[:, tok] + pos
        return jax.nn.silu(blk.reshape(B, n_cmp, -1) @ w1) @ w2

    kc = compress(k_cmp, cmp_k_pos, cmp_k_w1, cmp_k_w2)
    vc = compress(v_cmp, cmp_v_pos, cmp_v_w1, cmp_v_w2)
    cmp_end = jnp.arange(n_cmp) * NSA_CMP_STRIDE + NSA_CMP_LEN - 1

    n_sel = S // NSA_SEL_LEN
    k_top = min(NSA_TOPK, n_sel)
    r_c = NSA_CMP_LEN // NSA_CMP_STRIDE
    ratio = NSA_SEL_LEN // NSA_CMP_STRIDE
    overlap_w = [max(0, min(o * NSA_CMP_STRIDE + NSA_CMP_LEN, NSA_SEL_LEN) - max(o * NSA_CMP_STRIDE, 0)) / NSA_CMP_STRIDE
                 for o in range(-(r_c - 1), ratio)]
    end_pad = n_sel * ratio - n_cmp
    ks_blk = k_slc.reshape(B, n_sel, NSA_SEL_LEN, NSA_DK)
    vs_blk = v_slc.reshape(B, n_sel, NSA_SEL_LEN, NSA_DV)
    blk_ids = jnp.arange(n_sel)

    kw_pad = jnp.pad(k_win, ((0, 0), (NSA_WINDOW, 0), (0, 0)))
    vw_pad = jnp.pad(v_win, ((0, 0), (NSA_WINDOW, 0), (0, 0)))

    def block(i):
        qs = i * Q_BLOCK
        qb = lax.dynamic_slice_in_dim(q, qs, Q_BLOCK, axis=2)
        q_pos = qs + jnp.arange(Q_BLOCK)

        valid_c = cmp_end[None, :] <= q_pos[:, None]
        s_c = jnp.einsum('bhqd,bcd->bhqc', qb, kc).astype(jnp.float32) * scale
        p_c = jax.nn.softmax(jnp.where(valid_c, s_c, NEG_INF), axis=-1) * valid_c
        o_c = jnp.einsum('bhqc,bcd->bhqd', p_c.astype(vc.dtype), vc)

        imp = jnp.pad(p_c.sum(axis=1), ((0, 0), (0, 0), (r_c - 1, end_pad)))
        imp_sel = sum(w_u * imp[..., u:u + n_sel * ratio:ratio] for u, w_u in enumerate(overlap_w))
        cur = q_pos // NSA_SEL_LEN
        forced = (blk_ids[None, :] == 0) | (blk_ids[None, :] == cur[:, None]) | (blk_ids[None, :] == cur[:, None] - 1)
        valid_s = blk_ids[None, :] * NSA_SEL_LEN <= q_pos[:, None]
        imp_sel = jnp.where(valid_s, jnp.where(forced, NSA_FORCE_SCORE, imp_sel), NEG_INF)
        _, idx = lax.top_k(imp_sel, k_top)
        ksel = jax.vmap(lambda kb, ib: kb[ib])(ks_blk, idx).reshape(B, Q_BLOCK, k_top * NSA_SEL_LEN, NSA_DK)
        vsel = jax.vmap(lambda vb, ib: vb[ib])(vs_blk, idx).reshape(B, Q_BLOCK, k_top * NSA_SEL_LEN, NSA_DV)
        sel_pos = (idx[..., None] * NSA_SEL_LEN + jnp.arange(NSA_SEL_LEN)).reshape(B, Q_BLOCK, k_top * NSA_SEL_LEN)
        s_s = jnp.einsum('bhqd,bqkd->bhqk', qb, ksel).astype(jnp.float32) * scale
        s_s = jnp.where((sel_pos <= q_pos[None, :, None])[:, None], s_s, NEG_INF)
        o_s = jnp.einsum('bhqk,bqkd->bhqd', jax.nn.softmax(s_s, axis=-1).astype(vsel.dtype), vsel)

        kw = lax.dynamic_slice_in_dim(kw_pad, qs, Q_BLOCK + NSA_WINDOW, axis=1)
        vw = lax.dynamic_slice_in_dim(vw_pad, qs, Q_BLOCK + NSA_WINDOW, axis=1)
        kpos = qs - NSA_WINDOW + jnp.arange(Q_BLOCK + NSA_WINDOW)
        valid_w = ((kpos[None, :] <= q_pos[:, None]) & (kpos[None, :] > q_pos[:, None] - NSA_WINDOW)
                   & (kpos[None, :] >= 0))
        s_w = jnp.einsum('bhqd,bkd->bhqk', qb, kw).astype(jnp.float32) * scale
        s_w = jnp.where(valid_w, s_w, NEG_INF)
        o_w = jnp.einsum('bhqk,bkd->bhqd', jax.nn.softmax(s_w, axis=-1).astype(vw.dtype), vw)

        g = lax.dynamic_slice_in_dim(gates, qs, Q_BLOCK, axis=1).transpose(0, 2, 1, 3)
        o = g[..., 0:1] * o_c + g[..., 1:2] * o_s + g[..., 2:3] * o_w
        return o.transpose(0, 2, 1, 3).reshape(B, Q_BLOCK, NSA_OUT)

    return blockwise(block, S)


def diff_mixer(q, k, v, lam_q1, lam_k1, lam_q2, lam_k2, subln_w, lambda_init, cos, sin):
    B, S, _ = q.shape
    q = apply_rope(q.reshape(B, S, DIFF_HEADS, 2, DIFF_HD).transpose(0, 2, 3, 1, 4), cos, sin)
    k = apply_rope(k.reshape(B, S, DIFF_HEADS, 2, DIFF_HD).transpose(0, 2, 3, 1, 4), cos, sin)
    v = v.reshape(B, S, DIFF_HEADS, DIFF_VD).transpose(0, 2, 1, 3)
    f32 = jnp.float32
    lam = (jnp.exp(jnp.sum(lam_q1.astype(f32) * lam_k1.astype(f32)))
           - jnp.exp(jnp.sum(lam_q2.astype(f32) * lam_k2.astype(f32))) + lambda_init)
    scale = DIFF_HD ** -0.5
    key_pos = jnp.arange(S)

    def block(i):
        qs = i * Q_BLOCK
        qb = lax.dynamic_slice_in_dim(q, qs, Q_BLOCK, axis=3)
        s = jnp.einsum('bhmqd,bhmkd->bhmqk', qb, k).astype(f32) * scale
        q_pos = qs + jnp.arange(Q_BLOCK)
        p = jax.nn.softmax(jnp.where(key_pos[None, :] <= q_pos[:, None], s, NEG_INF), axis=-1)
        a = p[:, :, 0] - lam * p[:, :, 1]
        o = jnp.einsum('bhqk,bhkd->bqhd', a.astype(v.dtype), v)
        o = rms_norm(o, subln_w) * (1.0 - lambda_init)
        return o.reshape(B, Q_BLOCK, DIFF_OUT)

    return blockwise(block, S)


def setup_inputs(seed: int = 0) -> dict:
    key = jax.random.key(seed)
    ks = iter(jax.random.split(key, 32))
    L = DEPTH

    def w(shape, fan_in):
        return jax.random.normal(next(ks), shape, jnp.float32) * fan_in ** -0.5

    def gain(shape):
        return 1.0 + 0.01 * jax.random.normal(next(ks), shape, jnp.float32)

    def small(shape, s):
        return s * jax.random.normal(next(ks), shape, jnp.float32)

    return {
        "x": jax.random.normal(next(ks), (BATCH, SEQ, D_MODEL), jnp.float32),
        "ffn1_norm": gain((L, D_MODEL)),
        "ffn1_w_in": w((L, D_MODEL, 2 * D_FF), D_MODEL),
        "ffn1_w_out": w((L, D_FF, D_MODEL), D_FF),
        "mix_norm": gain((L, D_MODEL)),
        "w_in": w((L, D_MODEL, D_IN), D_MODEL),
        "mla_q_norm": gain((L, MLA_Q_LORA)),
        "mla_kv_norm": gain((L, MLA_KV_LORA)),
        "mla_w_uq": w((L, MLA_Q_LORA, MLA_HEADS * (MLA_NOPE + MLA_ROPE)), MLA_Q_LORA),
        "mla_w_ukv": w((L, MLA_KV_LORA, MLA_HEADS * (MLA_NOPE + MLA_V)), MLA_KV_LORA),
        "nsa_cmp_k_pos": small((L, NSA_CMP_LEN, NSA_DK), 0.1),
        "nsa_cmp_k_w1": w((L, NSA_CMP_LEN * NSA_DK, NSA_CMP_HIDDEN), NSA_CMP_LEN * NSA_DK),
        "nsa_cmp_k_w2": w((L, NSA_CMP_HIDDEN, NSA_DK), NSA_CMP_HIDDEN),
        "nsa_cmp_v_pos": small((L, NSA_CMP_LEN, NSA_DV), 0.1),
        "nsa_cmp_v_w1": w((L, NSA_CMP_LEN * NSA_DV, NSA_CMP_HIDDEN), NSA_CMP_LEN * NSA_DV),
        "nsa_cmp_v_w2": w((L, NSA_CMP_HIDDEN, NSA_DV), NSA_CMP_HIDDEN),
        "diff_lam_q1": small((L, DIFF_HD), 0.1),
        "diff_lam_k1": small((L, DIFF_HD), 0.1),
        "diff_lam_q2": small((L, DIFF_HD), 0.1),
        "diff_lam_k2": small((L, DIFF_HD), 0.1),
        "diff_subln": gain((L, DIFF_VD)),
        "w_br_mla": w((L, MLA_OUT, D_MODEL), MLA_OUT),
        "w_br_nsa": w((L, NSA_OUT, D_MODEL), NSA_OUT),
        "w_br_diff": w((L, DIFF_OUT, D_MODEL), DIFF_OUT),
        "w_out": w((L, D_MODEL, D_MODEL), D_MODEL),
        "ffn2_norm": gain((L, D_MODEL)),
        "ffn2_w_in": w((L, D_MODEL, 2 * D_FF), D_MODEL),
        "ffn2_w_out": w((L, D_FF, D_MODEL), D_FF),
        "final_norm": gain((D_MODEL,)),
    }


def reference(x, ffn1_norm, ffn1_w_in, ffn1_w_out, mix_norm, w_in, mla_q_norm, mla_kv_norm,
              mla_w_uq, mla_w_ukv, nsa_cmp_k_pos, nsa_cmp_k_w1, nsa_cmp_k_w2, nsa_cmp_v_pos,
              nsa_cmp_v_w1, nsa_cmp_v_w2, diff_lam_q1, diff_lam_k1, diff_lam_q2, diff_lam_k2,
              diff_subln, w_br_mla, w_br_nsa, w_br_diff, w_out, ffn2_norm, ffn2_w_in, ffn2_w_out,
              final_norm):
    B, S, _ = x.shape
    cos_mla, sin_mla = rope_table(MLA_ROPE, S)
    cos_nsa, sin_nsa = rope_table(NSA_DK, S)
    cos_diff, sin_diff = rope_table(DIFF_HD, S)

    for l in range(DEPTH):
        x = x + 0.5 * swiglu(rms_norm(x, ffn1_norm[l]), ffn1_w_in[l], ffn1_w_out[l])

        n = rms_norm(x, mix_norm[l])
        (c_q, c_kv, k_rope, nsa_q, nsa_kc, nsa_vc, nsa_ks, nsa_vs, nsa_kw, nsa_vw, nsa_g,
         d_q, d_k, d_v, merge_logits) = split_cols(n @ w_in[l], IN_SPLITS)

        o_mla = mla_mixer(c_q, c_kv, k_rope, mla_q_norm[l], mla_kv_norm[l], mla_w_uq[l], mla_w_ukv[l],
                          cos_mla, sin_mla)
        o_nsa = nsa_mixer(nsa_q, nsa_kc, nsa_vc, nsa_ks, nsa_vs, nsa_kw, nsa_vw, nsa_g,
                          nsa_cmp_k_pos[l], nsa_cmp_k_w1[l], nsa_cmp_k_w2[l],
                          nsa_cmp_v_pos[l], nsa_cmp_v_w1[l], nsa_cmp_v_w2[l], cos_nsa, sin_nsa)
        lambda_init = 0.8 - 0.6 * math.exp(-0.3 * l)
        o_diff = diff_mixer(d_q, d_k, d_v, diff_lam_q1[l], diff_lam_k1[l], diff_lam_q2[l], diff_lam_k2[l],
                            diff_subln[l], lambda_init, cos_diff, sin_diff)

        g = jax.nn.sigmoid(merge_logits.astype(jnp.float32)).astype(x.dtype).reshape(B, S, N_BRANCH, D_MODEL)
        mixed = (g[:, :, 0] * (o_mla @ w_br_mla[l])
                 + g[:, :, 1] * (o_nsa @ w_br_nsa[l])
                 + g[:, :, 2] * (o_diff @ w_br_diff[l]))
        x = x + mixed @ w_out[l]

        x = x + 0.5 * swiglu(rms_norm(x, ffn2_norm[l]), ffn2_w_in[l], ffn2_w_out[l])

    return rms_norm(x, final_norm)
```

```python
import functools
import math

import numpy as np
import jax
import jax.numpy as jnp
from jax import lax
from jax.experimental import pallas as pl
from jax.experimental.pallas import tpu as pltpu

F32 = jnp.float32
BF16 = jnp.bfloat16

NORM_EPS = 1e-6
ROPE_THETA = 10000.0
NEG_INF = -1e30
REMOVED = -3e38
N_BRANCH = 3

MLA_HEADS = 6
MLA_Q_LORA = 768
MLA_KV_LORA = 512
MLA_NOPE = 128
MLA_ROPE = 64
MLA_V = 128
MLA_QK = MLA_NOPE + MLA_ROPE

NSA_HEADS = 4
NSA_DK = 192
NSA_DV = 128
NSA_CMP_LEN = 32
NSA_CMP_STRIDE = 16
NSA_CMP_HIDDEN = 256
NSA_SEL_LEN = 64
NSA_TOPK = 16
NSA_WINDOW = 512
NSA_FORCE_SCORE = 1e6

DIFF_HEADS = 4
DIFF_HD = 96
DIFF_VD = 2 * DIFF_HD

LANES = 128
VMEM_LIMIT_MB = 56


def _cparams(dims, vmem_mb=VMEM_LIMIT_MB):
    return pltpu.CompilerParams(dimension_semantics=dims, vmem_limit_bytes=vmem_mb * 2**20)


def _sigmoid(x):
    return 1.0 / (1.0 + jnp.exp(-x))


def _dot(a, b):
    return jnp.dot(a, b, preferred_element_type=F32)


def _dot_nt(a, b):
    return lax.dot_general(a, b, (((1,), (1,)), ((), ())), preferred_element_type=F32)


def _tile(n, pref):
    t = min(n, pref)
    assert n % t == 0, (n, t)
    return t


def _rmsnorm_kernel(x_ref, w_ref, o_ref):
    x = x_ref[...].astype(F32)
    y = x * lax.rsqrt(jnp.mean(x * x, axis=-1, keepdims=True) + NORM_EPS)
    o_ref[...] = (y * w_ref[...]).astype(o_ref.dtype)


def rmsnorm(x2d, w, out_dtype):
    m, d = x2d.shape
    tm = _tile(m, 1024)
    return pl.pallas_call(
        _rmsnorm_kernel,
        grid=(m // tm,),
        in_specs=[pl.BlockSpec((tm, d), lambda i: (i, 0)), pl.BlockSpec((1, d), lambda i: (0, 0))],
        out_specs=pl.BlockSpec((tm, d), lambda i: (i, 0)),
        out_shape=jax.ShapeDtypeStruct((m, d), out_dtype),
        compiler_params=_cparams(("parallel",)),
        name="rmsnorm",
    )(x2d, w.reshape(1, d).astype(F32))


def _mla_norm_kernel(lat_ref, wq_ref, wkv_ref, nq_ref, nkv_ref):
    lat = lat_ref[...]
    cq = lat[:, :MLA_Q_LORA]
    ckv = lat[:, MLA_Q_LORA:]
    nq = cq * lax.rsqrt(jnp.mean(cq * cq, axis=-1, keepdims=True) + NORM_EPS)
    nkv = ckv * lax.rsqrt(jnp.mean(ckv * ckv, axis=-1, keepdims=True) + NORM_EPS)
    nq_ref[...] = (nq * wq_ref[...]).astype(BF16)
    nkv_ref[...] = (nkv * wkv_ref[...]).astype(BF16)


def mla_norm(lat, wq, wkv):
    m, d = lat.shape
    tm = _tile(m, 1024)
    return pl.pallas_call(
        _mla_norm_kernel,
        grid=(m // tm,),
        in_specs=[pl.BlockSpec((tm, d), lambda i: (i, 0)),
                  pl.BlockSpec((1, MLA_Q_LORA), lambda i: (0, 0)),
                  pl.BlockSpec((1, MLA_KV_LORA), lambda i: (0, 0))],
        out_specs=[pl.BlockSpec((tm, MLA_Q_LORA), lambda i: (i, 0)),
                   pl.BlockSpec((tm, MLA_KV_LORA), lambda i: (i, 0))],
        out_shape=[jax.ShapeDtypeStruct((m, MLA_Q_LORA), BF16), jax.ShapeDtypeStruct((m, MLA_KV_LORA), BF16)],
        compiler_params=_cparams(("parallel",)),
        name="mla_norm",
    )(lat, wq.reshape(1, -1).astype(F32), wkv.reshape(1, -1).astype(F32))


def _ffn_kernel(x_ref, nw_ref, wg_ref, wu_ref, wo_ref, o_ref, n_scr, *, n_f):
    f = pl.program_id(1)

    @pl.when(f == 0)
    def _():
        x = x_ref[...]
        y = x * lax.rsqrt(jnp.mean(x * x, axis=-1, keepdims=True) + NORM_EPS)
        n_scr[...] = (y * nw_ref[...]).astype(BF16)
        o_ref[...] = jnp.zeros_like(o_ref)

    n = n_scr[...]
    g = _dot(n, wg_ref[...])
    u = _dot(n, wu_ref[...])
    h = (g * _sigmoid(g) * u).astype(BF16)
    o_ref[...] += _dot(h, wo_ref[...])

    @pl.when(f == n_f - 1)
    def _():
        o_ref[...] = x_ref[...] + 0.5 * o_ref[...]


def ffn(x2d, norm_w, w_in, w_out):
    m, d = x2d.shape
    f_dim = w_out.shape[0]
    tm = _tile(m, 512)
    tf = _tile(f_dim, 512)
    n_f = f_dim // tf
    return pl.pallas_call(
        functools.partial(_ffn_kernel, n_f=n_f),
        grid=(m // tm, n_f),
        in_specs=[pl.BlockSpec((tm, d), lambda i, f: (i, 0)),
                  pl.BlockSpec((1, d), lambda i, f: (0, 0)),
                  pl.BlockSpec((d, tf), lambda i, f: (0, f)),
                  pl.BlockSpec((d, tf), lambda i, f: (0, f + n_f)),
                  pl.BlockSpec((tf, d), lambda i, f: (f, 0))],
        out_specs=pl.BlockSpec((tm, d), lambda i, f: (i, 0)),
        out_shape=jax.ShapeDtypeStruct((m, d), F32),
        scratch_shapes=[pltpu.VMEM((tm, d), BF16)],
        compiler_params=_cparams(("parallel", "arbitrary")),
        name="ffn",
    )(x2d, norm_w.reshape(1, d).astype(F32), w_in, w_in, w_out)


def _proj_plain_kernel(n_ref, w_ref, o_ref, *, sigmoid):
    y = _dot(n_ref[...], w_ref[...])
    if sigmoid:
        y = _sigmoid(y)
    o_ref[...] = y.astype(o_ref.dtype)


def proj_plain(n2d, w, out_dtype, sigmoid=False):
    m, k = n2d.shape
    n_out = w.shape[1]
    tm = _tile(m, 1024)
    tn = _tile(n_out, 512) if n_out % 512 == 0 else _tile(n_out, LANES)
    return pl.pallas_call(
        functools.partial(_proj_plain_kernel, sigmoid=sigmoid),
        grid=(m // tm, n_out // tn),
        in_specs=[pl.BlockSpec((tm, k), lambda i, j: (i, 0)),
                  pl.BlockSpec((k, tn), lambda i, j: (0, j))],
        out_specs=pl.BlockSpec((tm, tn), lambda i, j: (i, j)),
        out_shape=jax.ShapeDtypeStruct((m, n_out), out_dtype),
        compiler_params=_cparams(("parallel", "arbitrary")),
        name="proj_plain",
    )(n2d, w)


def _proj_heads_kernel(*refs, rope, has_add, scale):
    n_ref, w_ref = refs[0], refs[1]
    pos = 2
    n = n_ref[...]
    y = _dot(n, w_ref[0])
    if rope:
        wr_ref, cos_ref, sin_ref = refs[pos:pos + 3]
        pos += 3
        y = y * cos_ref[...] + _dot(n, wr_ref[0]) * sin_ref[...]
    if has_add:
        y = y + refs[pos][0, 0].astype(F32)
        pos += 1
    if scale != 1.0:
        y = y * scale
    o_ref = refs[pos]
    o_ref[0, 0] = y.astype(o_ref.dtype)


def proj_heads(n2d, w, batch, seq, rope=None, add=None, scale=1.0):
    m, k = n2d.shape
    g, _, dh = w.shape
    tm = _tile(seq, 1024)
    ns = seq // tm
    in_specs = [pl.BlockSpec((tm, k), lambda b, i, h: (b * ns + i, 0)),
                pl.BlockSpec((1, k, dh), lambda b, i, h: (h, 0, 0))]
    args = [n2d, w]
    if rope is not None:
        w_rot, cos, sin = rope
        in_specs += [pl.BlockSpec((1, k, dh), lambda b, i, h: (h, 0, 0)),
                     pl.BlockSpec((tm, dh), lambda b, i, h: (i, 0)),
                     pl.BlockSpec((tm, dh), lambda b, i, h: (i, 0))]
        args += [w_rot, cos, sin]
    if add is not None:
        in_specs.append(pl.BlockSpec((1, 1, tm, dh), lambda b, i, h: (b, 0, i, 0)))
        args.append(add)
    return pl.pallas_call(
        functools.partial(_proj_heads_kernel, rope=rope is not None, has_add=add is not None, scale=scale),
        grid=(batch, ns, g),
        in_specs=in_specs,
        out_specs=pl.BlockSpec((1, 1, tm, dh), lambda b, i, h: (b, h, i, 0)),
        out_shape=jax.ShapeDtypeStruct((batch, g, seq, dh), BF16),
        compiler_params=_cparams(("parallel", "parallel", "arbitrary")),
        name="proj_heads",
    )(*args)


def _online_softmax_step(s, v, m_scr, l_scr, acc_scr):
    m_prev = m_scr[...]
    m_new = jnp.maximum(m_prev, jnp.max(s, axis=-1, keepdims=True))
    alpha = jnp.exp(m_prev - m_new)
    p = jnp.exp(s - m_new)
    l_scr[...] = alpha * l_scr[...] + jnp.sum(p, axis=-1, keepdims=True)
    acc_scr[...] = alpha * acc_scr[...] + _dot(p.astype(BF16), v)
    m_scr[...] = m_new


def _causal_mask(t):
    row = lax.broadcasted_iota(jnp.int32, (t, t), 0)
    col = lax.broadcasted_iota(jnp.int32, (t, t), 1)
    return col <= row


def _flash_kernel(q_ref, k_ref, v_ref, o_ref, m_scr, l_scr, acc_scr, *, t):
    i = pl.program_id(2)
    q = q_ref[0, 0]
    m_scr[...] = jnp.full_like(m_scr, NEG_INF)
    l_scr[...] = jnp.zeros_like(l_scr)
    acc_scr[...] = jnp.zeros_like(acc_scr)

    def step(j, masked):
        off = pl.multiple_of(j * t, t)
        k = k_ref[0, 0, pl.ds(off, t), :]
        v = v_ref[0, 0, pl.ds(off, t), :]
        s = _dot_nt(q, k)
        if masked:
            s = jnp.where(_causal_mask(t), s, NEG_INF)
        _online_softmax_step(s, v, m_scr, l_scr, acc_scr)

    def body(j, c):
        step(j, False)
        return c

    lax.fori_loop(0, i, body, 0)
    step(i, True)
    o_ref[0] = (acc_scr[...] / l_scr[...]).astype(o_ref.dtype)


def flash_causal(q, k, v):
    b, h, s, dk = q.shape
    dv = v.shape[-1]
    t = _tile(s, 512)
    return pl.pallas_call(
        functools.partial(_flash_kernel, t=t),
        grid=(b, h, s // t),
        in_specs=[pl.BlockSpec((1, 1, t, dk), lambda b_, h_, i: (b_, h_, i, 0)),
                  pl.BlockSpec((1, 1, s, dk), lambda b_, h_, i: (b_, h_, 0, 0)),
                  pl.BlockSpec((1, 1, s, dv), lambda b_, h_, i: (b_, h_, 0, 0))],
        out_specs=pl.BlockSpec((1, t, dv), lambda b_, h_, i: (b_, i, h_)),
        out_shape=jax.ShapeDtypeStruct((b, s, h * dv), BF16),
        scratch_shapes=[pltpu.VMEM((t, 1), F32), pltpu.VMEM((t, 1), F32), pltpu.VMEM((t, dv), F32)],
        compiler_params=_cparams(("parallel", "parallel", "arbitrary")),
        name="mla_flash",
    )(q, k, v)


def _diff_kernel(q_ref, k_ref, v_ref, lq1_ref, lk1_ref, lq2_ref, lk2_ref, sub_ref, o_ref,
                 m_scr, l_scr, acc_scr, *, t, lambda_init):
    i = pl.program_id(2)
    for c in range(2):
        m_scr[c] = jnp.full(m_scr.shape[1:], NEG_INF, F32)
        l_scr[c] = jnp.zeros(l_scr.shape[1:], F32)
        acc_scr[c] = jnp.zeros(acc_scr.shape[1:], F32)

    def step(j, masked):
        off = pl.multiple_of(j * t, t)
        v = v_ref[0, 0, pl.ds(off, t), :]
        for c in range(2):
            k = k_ref[0, c, pl.ds(off, t), :]
            s = _dot_nt(q_ref[0, c], k)
            if masked:
                s = jnp.where(_causal_mask(t), s, NEG_INF)
            _online_softmax_step(s, v, m_scr.at[c], l_scr.at[c], acc_scr.at[c])

    def body(j, c):
        step(j, False)
        return c

    lax.fori_loop(0, i, body, 0)
    step(i, True)

    lam = (jnp.exp(jnp.sum(lq1_ref[...] * lk1_ref[...], axis=-1, keepdims=True))
           - jnp.exp(jnp.sum(lq2_ref[...] * lk2_ref[...], axis=-1, keepdims=True)) + lambda_init)
    o = acc_scr[0] / l_scr[0] - lam * (acc_scr[1] / l_scr[1])
    o = o * lax.rsqrt(jnp.mean(o * o, axis=-1, keepdims=True) + NORM_EPS)
    o_ref[0, 0] = (o * sub_ref[...] * (1.0 - lambda_init)).astype(o_ref.dtype)


def diff_attention(q, k, v, lq1, lk1, lq2, lk2, subln, lambda_init):
    b, h2, s, hd = q.shape
    h = h2 // 2
    vd = v.shape[-1]
    t = _tile(s, 512)
    vec = lambda a: a.reshape(1, -1).astype(F32)
    small = lambda n: pl.BlockSpec((1, n), lambda b_, h_, i: (0, 0))
    return pl.pallas_call(
        functools.partial(_diff_kernel, t=t, lambda_init=lambda_init),
        grid=(b, h, s // t),
        in_specs=[pl.BlockSpec((1, 2, t, hd), lambda b_, h_, i: (b_, h_, i, 0)),
                  pl.BlockSpec((1, 2, s, hd), lambda b_, h_, i: (b_, h_, 0, 0)),
                  pl.BlockSpec((1, 1, s, vd), lambda b_, h_, i: (b_, h_, 0, 0)),
                  small(hd), small(hd), small(hd), small(hd), small(vd)],
        out_specs=pl.BlockSpec((1, 1, t, vd), lambda b_, h_, i: (b_, h_, i, 0)),
        out_shape=jax.ShapeDtypeStruct((b, h, s, vd), BF16),
        scratch_shapes=[pltpu.VMEM((2, t, 1), F32), pltpu.VMEM((2, t, 1), F32), pltpu.VMEM((2, t, vd), F32)],
        compiler_params=_cparams(("parallel", "parallel", "arbitrary")),
        name="diff_flash",
    )(q, k, v, vec(lq1), vec(lk1), vec(lq2), vec(lk2), vec(subln))


def _compress_kernel(a_ref, pa_ref, pb_ref, w1a_ref, w1b_ref, w2_ref, o_ref):
    a = a_ref[0, 0].astype(F32)
    u = _dot((a + pa_ref[...]).astype(BF16), w1a_ref[...])
    v = _dot((a + pb_ref[...]).astype(BF16), w1b_ref[...])
    ng = v.shape[0]
    hdn = u + pltpu.roll(v, ng - 1, 0)
    hdn = hdn * _sigmoid(hdn)
    o_ref[0] = _dot(hdn.astype(BF16), w2_ref[...]).astype(o_ref.dtype)


def nsa_compress(tok, g_idx, pos, w1, w2):
    assert NSA_CMP_LEN == 2 * NSA_CMP_STRIDE
    b, _, ng, wd = tok.shape
    d = wd // NSA_CMP_STRIDE
    pos_flat = pos.astype(F32).reshape(1, NSA_CMP_LEN * d)
    w1 = w1.astype(BF16)
    full = lambda shape: pl.BlockSpec(shape, lambda b_: (0,) * len(shape))
    return pl.pallas_call(
        _compress_kernel,
        grid=(b,),
        in_specs=[pl.BlockSpec((1, 1, ng, wd), lambda b_: (b_, g_idx, 0, 0)),
                  full((1, wd)), full((1, wd)), full((wd, NSA_CMP_HIDDEN)), full((wd, NSA_CMP_HIDDEN)),
                  full((NSA_CMP_HIDDEN, d))],
        out_specs=pl.BlockSpec((1, ng, d), lambda b_: (b_, 0, 0)),
        out_shape=jax.ShapeDtypeStruct((b, ng, d), BF16),
        compiler_params=_cparams(("parallel",)),
        name="nsa_compress",
    )(tok, pos_flat[:, :wd], pos_flat[:, wd:], w1[:wd], w1[wd:], w2.astype(BF16))


def _split3(x):
    hi = x.astype(BF16)
    r = x - hi.astype(F32)
    mid = r.astype(BF16)
    lo = (r - mid.astype(F32)).astype(BF16)
    return hi, mid, lo


def _cmp_topk_kernel(q_ref, kc_ref, vc_ref, wsel_ref, oc_ref, sel_ref, *, tq, k_top):
    i = pl.program_id(1)
    qs = i * tq
    nc = kc_ref.shape[1]
    n_sel = wsel_ref.shape[1]
    kc = kc_ref[0]
    vc = vc_ref[0]
    qpos_c = qs + lax.broadcasted_iota(jnp.int32, (tq, nc), 0)
    cend = lax.broadcasted_iota(jnp.int32, (tq, nc), 1) * NSA_CMP_STRIDE + (NSA_CMP_LEN - 1)
    valid_c = cend <= qpos_c
    imp = jnp.zeros((tq, nc), F32)
    for h in range(NSA_HEADS):
        s = jnp.where(valid_c, _dot_nt(q_ref[0, h], kc), NEG_INF)
        e = jnp.exp(s - jnp.max(s, axis=-1, keepdims=True))
        p = jnp.where(valid_c, e / jnp.sum(e, axis=-1, keepdims=True), 0.0)
        oc_ref[0, :, h * NSA_DV:(h + 1) * NSA_DV] = _dot(p.astype(BF16), vc)
        imp = imp + p

    wsel = wsel_ref[...]
    hi, mid, lo = _split3(imp)
    imp_sel = _dot(hi, wsel) + _dot(mid, wsel) + _dot(lo, wsel)

    blk = lax.broadcasted_iota(jnp.int32, (tq, n_sel), 1)
    qpos = qs + lax.broadcasted_iota(jnp.int32, (tq, n_sel), 0)
    cur = lax.shift_right_arithmetic(qpos, int(math.log2(NSA_SEL_LEN)))
    forced = (blk == 0) | (blk == cur) | (blk == cur - 1)
    valid_s = blk * NSA_SEL_LEN <= qpos
    score = jnp.where(valid_s, jnp.where(forced, NSA_FORCE_SCORE, imp_sel), NEG_INF)
    blk_f = blk.astype(F32)
    sel = jnp.zeros((tq, n_sel), F32)
    for _ in range(k_top):
        mx = jnp.max(score, axis=-1, keepdims=True)
        first = jnp.min(jnp.where(score == mx, blk_f, float(n_sel)), axis=-1, keepdims=True)
        hit = blk_f == first
        sel = jnp.where(hit, 1.0, sel)
        score = jnp.where(hit, REMOVED, score)
    sel_ref[0] = sel.astype(sel_ref.dtype)


def _sel_weight_matrix(n_pad, n_sel):
    r_c = NSA_CMP_LEN // NSA_CMP_STRIDE
    ratio = NSA_SEL_LEN // NSA_CMP_STRIDE
    overlap_w = [max(0, min(o * NSA_CMP_STRIDE + NSA_CMP_LEN, NSA_SEL_LEN) - max(o * NSA_CMP_STRIDE, 0))
                 / NSA_CMP_STRIDE for o in range(-(r_c - 1), ratio)]
    w = np.zeros((n_pad, n_sel), np.float32)
    for n in range(n_sel):
        for u, w_u in enumerate(overlap_w):
            c = ratio * n + u - (r_c - 1)
            if 0 <= c < n_pad:
                w[c, n] = w_u
    return w


def nsa_cmp_topk(q, kc, vc):
    b, h, s, dk = q.shape
    nc = kc.shape[1]
    n_sel = s // NSA_SEL_LEN
    assert NSA_SEL_LEN & (NSA_SEL_LEN - 1) == 0
    tq = _tile(s, 256)
    wsel = jnp.asarray(_sel_weight_matrix(nc, n_sel), BF16)
    return pl.pallas_call(
        functools.partial(_cmp_topk_kernel, tq=tq, k_top=min(NSA_TOPK, n_sel)),
        grid=(b, s // tq),
        in_specs=[pl.BlockSpec((1, h, tq, dk), lambda b_, i: (b_, 0, i, 0)),
                  pl.BlockSpec((1, nc, dk), lambda b_, i: (b_, 0, 0)),
                  pl.BlockSpec((1, nc, NSA_DV), lambda b_, i: (b_, 0, 0)),
                  pl.BlockSpec((nc, n_sel), lambda b_, i: (0, 0))],
        out_specs=[pl.BlockSpec((1, tq, h * NSA_DV), lambda b_, i: (b_, i, 0)),
                   pl.BlockSpec((1, tq, n_sel), lambda b_, i: (b_, i, 0))],
        out_shape=[jax.ShapeDtypeStruct((b, s, h * NSA_DV), F32), jax.ShapeDtypeStruct((b, s, n_sel), BF16)],
        compiler_params=_cparams(("parallel", "parallel")),
        name="nsa_cmp_topk",
    )(q, kc, vc, wsel)


def _nsa_sel_kernel(q_ref, k_ref, v_ref, sel_ref, o_ref, m_scr, l_scr, acc_scr, *, t):
    i = pl.program_id(1)
    n_sel = sel_ref.shape[2]
    bpt = t // NSA_SEL_LEN
    shift = int(math.log2(NSA_SEL_LEN))
    sel = sel_ref[0]
    for h in range(NSA_HEADS):
        m_scr[h] = jnp.full(m_scr.shape[1:], NEG_INF, F32)
        l_scr[h] = jnp.zeros(l_scr.shape[1:], F32)
        acc_scr[h] = jnp.zeros(acc_scr.shape[1:], F32)

    def step(j, diag):
        off = pl.multiple_of(j * t, t)
        k = k_ref[0, 0, pl.ds(off, t), :]
        v = v_ref[0, pl.ds(off, t), :]
        blk_row = lax.broadcasted_iota(jnp.int32, (n_sel, t), 0)
        blk_col = lax.shift_right_arithmetic(lax.broadcasted_iota(jnp.int32, (n_sel, t), 1), shift) + j * bpt
        expand = jnp.where(blk_row == blk_col, 1.0, 0.0).astype(BF16)
        keep = _dot(sel, expand) > 0.5
        if diag:
            keep = keep & _causal_mask(t)
        for h in range(NSA_HEADS):
            s = jnp.where(keep, _dot_nt(q_ref[0, h], k), NEG_INF)
            _online_softmax_step(s, v, m_scr.at[h], l_scr.at[h], acc_scr.at[h])

    def body(j, c):
        step(j, False)
        return c

    lax.fori_loop(0, i, body, 0)
    step(i, True)
    for h in range(NSA_HEADS):
        o_ref[0, :, h * NSA_DV:(h + 1) * NSA_DV] = acc_scr[h] / l_scr[h]


def nsa_selected(q, k3, k_idx, v3, v_idx, sel):
    b, h, s, dk = q.shape
    n_sel = sel.shape[-1]
    t = _tile(s, 512)
    assert t % NSA_SEL_LEN == 0
    return pl.pallas_call(
        functools.partial(_nsa_sel_kernel, t=t),
        grid=(b, s // t),
        in_specs=[pl.BlockSpec((1, h, t, dk), lambda b_, i: (b_, 0, i, 0)),
                  pl.BlockSpec((1, 1, s, dk), lambda b_, i: (b_, k_idx, 0, 0)),
                  pl.BlockSpec((1, s, NSA_DV), lambda b_, i: (b_, 0, v_idx)),
                  pl.BlockSpec((1, t, n_sel), lambda b_, i: (b_, i, 0))],
        out_specs=pl.BlockSpec((1, t, h * NSA_DV), lambda b_, i: (b_, i, 0)),
        out_shape=jax.ShapeDtypeStruct((b, s, h * NSA_DV), F32),
        scratch_shapes=[pltpu.VMEM((h, t, 1), F32), pltpu.VMEM((h, t, 1), F32), pltpu.VMEM((h, t, NSA_DV), F32)],
        compiler_params=_cparams(("parallel", "arbitrary")),
        name="nsa_selected",
    )(q, k3, v3, sel)


def _nsa_win_kernel(q_ref, kp_ref, kc_ref, vp_ref, vc_ref, o_ref, *, t):
    i = pl.program_id(1)
    row = lax.broadcasted_iota(jnp.int32, (t, t), 0)
    col = lax.broadcasted_iota(jnp.int32, (t, t), 1)
    keep_prev = (col > row) & (i > 0)
    keep_cur = col <= row
    kp, kc, vp, vc = kp_ref[0, 0], kc_ref[0, 0], vp_ref[0], vc_ref[0]
    for h in range(NSA_HEADS):
        q = q_ref[0, h]
        sp = jnp.where(keep_prev, _dot_nt(q, kp), NEG_INF)
        sc = jnp.where(keep_cur, _dot_nt(q, kc), NEG_INF)
        m = jnp.maximum(jnp.max(sp, axis=-1, keepdims=True), jnp.max(sc, axis=-1, keepdims=True))
        pp = jnp.exp(sp - m)
        pc = jnp.exp(sc - m)
        l = jnp.sum(pp, axis=-1, keepdims=True) + jnp.sum(pc, axis=-1, keepdims=True)
        o = _dot(pp.astype(BF16), vp) + _dot(pc.astype(BF16), vc)
        o_ref[0, :, h * NSA_DV:(h + 1) * NSA_DV] = o / l


def nsa_window(q, k3, k_idx, v3, v_idx):
    b, h, s, dk = q.shape
    t = _tile(s, NSA_WINDOW)
    assert t == NSA_WINDOW, "window kernel needs the query tile to equal the window"
    prev = lambda i: jnp.maximum(i - 1, 0)
    return pl.pallas_call(
        functools.partial(_nsa_win_kernel, t=t),
        grid=(b, s // t),
        in_specs=[pl.BlockSpec((1, h, t, dk), lambda b_, i: (b_, 0, i, 0)),
                  pl.BlockSpec((1, 1, t, dk), lambda b_, i: (b_, k_idx, prev(i), 0)),
                  pl.BlockSpec((1, 1, t, dk), lambda b_, i: (b_, k_idx, i, 0)),
                  pl.BlockSpec((1, t, NSA_DV), lambda b_, i: (b_, prev(i), v_idx)),
                  pl.BlockSpec((1, t, NSA_DV), lambda b_, i: (b_, i, v_idx))],
        out_specs=pl.BlockSpec((1, t, h * NSA_DV), lambda b_, i: (b_, i, 0)),
        out_shape=jax.ShapeDtypeStruct((b, s, h * NSA_DV), F32),
        compiler_params=_cparams(("parallel", "parallel")),
        name="nsa_window",
    )(q, k3, k3, v3, v3)


def _merge_kernel(x_ref, om_ref, oc_ref, os_ref, ow_ref, gn_ref, od_ref, gm_ref, gs_ref, gd_ref,
                  wm_ref, wn_ref, wd_ref, wo_ref, ex_ref, o_ref, onsa_scr, *, n_j):
    j = pl.program_id(1)
    nsa_out = NSA_HEADS * NSA_DV

    @pl.when(j == 0)
    def _():
        g = gn_ref[...]
        hi = g.astype(BF16)
        lo = (g - hi.astype(F32)).astype(BF16)
        ge = _dot(hi, ex_ref[...]) + _dot(lo, ex_ref[...])
        onsa = (ge[:, :nsa_out] * oc_ref[...] + ge[:, nsa_out:2 * nsa_out] * os_ref[...]
                + ge[:, 2 * nsa_out:] * ow_ref[...])
        onsa_scr[...] = onsa.astype(BF16)
        o_ref[...] = jnp.zeros_like(o_ref)

    ym = _dot(om_ref[...], wm_ref[...])
    yn = _dot(onsa_scr[...], wn_ref[...])
    yd = _dot(od_ref[0, 0], wd_ref[0])
    for h in range(1, DIFF_HEADS):
        yd = yd + _dot(od_ref[0, h], wd_ref[h])
    mixed = (gm_ref[...].astype(F32) * ym + gs_ref[...].astype(F32) * yn + gd_ref[...].astype(F32) * yd)
    o_ref[...] += _dot(mixed.astype(BF16), wo_ref[...])

    @pl.when(j == n_j - 1)
    def _():
        o_ref[...] = x_ref[...] + o_ref[...]


def _gate_expand_matrix():
    nsa_out = NSA_HEADS * NSA_DV
    e = np.zeros((LANES, 3 * nsa_out), np.float32)
    for h in range(NSA_HEADS):
        for c in range(3):
            e[h * 3 + c, c * nsa_out + h * NSA_DV:c * nsa_out + (h + 1) * NSA_DV] = 1.0
    return e


def merge(x2d, o_mla, o_c, o_s, o_w, g_nsa, o_diff, g_merge, w_br_mla, w_br_nsa, w_br_diff, w_out, seq):
    m, d = x2d.shape
    tm = _tile(seq, 512)
    ns = seq // tm
    tn = _tile(d, 512)
    n_j = d // tn
    nsa_out = NSA_HEADS * NSA_DV
    mla_out = o_mla.shape[-1]
    ex = jnp.asarray(_gate_expand_matrix(), BF16)
    row = lambda w: pl.BlockSpec((tm, w), lambda i, j: (i, 0))
    return pl.pallas_call(
        functools.partial(_merge_kernel, n_j=n_j),
        grid=(m // tm, n_j),
        in_specs=[row(d), row(mla_out), row(nsa_out), row(nsa_out), row(nsa_out), row(LANES),
                  pl.BlockSpec((1, DIFF_HEADS, tm, DIFF_VD), lambda i, j: (i // ns, 0, i % ns, 0)),
                  pl.BlockSpec((tm, tn), lambda i, j: (i, j)),
                  pl.BlockSpec((tm, tn), lambda i, j: (i, n_j + j)),
                  pl.BlockSpec((tm, tn), lambda i, j: (i, 2 * n_j + j)),
                  pl.BlockSpec((mla_out, tn), lambda i, j: (0, j)),
                  pl.BlockSpec((nsa_out, tn), lambda i, j: (0, j)),
                  pl.BlockSpec((DIFF_HEADS, DIFF_VD, tn), lambda i, j: (0, 0, j)),
                  pl.BlockSpec((tn, d), lambda i, j: (j, 0)),
                  pl.BlockSpec((LANES, 3 * nsa_out), lambda i, j: (0, 0))],
        out_specs=pl.BlockSpec((tm, d), lambda i, j: (i, 0)),
        out_shape=jax.ShapeDtypeStruct((m, d), F32),
        scratch_shapes=[pltpu.VMEM((tm, nsa_out), BF16)],
        compiler_params=_cparams(("parallel", "arbitrary")),
        name="merge",
    )(x2d, o_mla, o_c, o_s, o_w, g_nsa, o_diff, g_merge, g_merge, g_merge,
      w_br_mla, w_br_nsa, w_br_diff, w_out, ex)


def _rope_tables(dim, seq, lead=0):
    inv = ROPE_THETA ** (-jnp.arange(0, dim, 2, dtype=F32) / dim)
    ang = jnp.arange(seq, dtype=F32)[:, None] * inv[None, :]
    cos, sin = jnp.cos(ang), jnp.sin(ang)
    cos = jnp.concatenate([jnp.ones((seq, lead), F32), cos, cos], axis=-1)
    sin = jnp.concatenate([jnp.zeros((seq, lead), F32), sin, sin], axis=-1)
    return cos, sin


def _heads(w, g, dh):
    return w.reshape(w.shape[0], g, dh).transpose(1, 0, 2)


def _rot_cols(w, lead=0):
    half = (w.shape[-1] - lead) // 2
    x1 = w[..., lead:lead + half]
    x2 = w[..., lead + half:]
    return jnp.concatenate([jnp.zeros_like(w[..., :lead]), -x2, x1], axis=-1)


def _col_offsets():
    sizes = (MLA_Q_LORA, MLA_KV_LORA, MLA_ROPE,
             NSA_HEADS * NSA_DK, NSA_DK, NSA_DV, NSA_DK, NSA_DV, NSA_DK, NSA_DV, NSA_HEADS * 3,
             DIFF_HEADS * 2 * DIFF_HD, DIFF_HEADS * 2 * DIFF_HD, DIFF_HEADS * DIFF_VD)
    names = ("c_q", "c_kv", "k_rope", "nsa_q", "nsa_kc", "nsa_vc", "nsa_ks", "nsa_vs", "nsa_kw", "nsa_vw",
             "nsa_g", "d_q", "d_k", "d_v")
    offs = {}
    o = 0
    for nme, sz in zip(names, sizes):
        offs[nme] = (o, o + sz)
        o += sz
    offs["merge"] = (o, None)
    return offs


def _mixers(n2d, batch, seq, layer, w_in, p):
    offs = _col_offsets()
    col = lambda name: w_in[:, offs[name][0]:offs[name][1]]
    bf = lambda a: a.astype(BF16)

    lat = proj_plain(n2d, bf(jnp.concatenate([col("c_q"), col("c_kv")], axis=1)), F32)
    n_q, n_kv = mla_norm(lat, p["mla_q_norm"], p["mla_kv_norm"])
    cos_m, sin_m = _rope_tables(MLA_ROPE, seq, lead=MLA_NOPE)
    w_kr = jnp.concatenate([jnp.zeros((w_in.shape[0], MLA_NOPE), F32), col("k_rope")], axis=1)[None]
    kpe = proj_heads(n2d, bf(w_kr), batch, seq, rope=(bf(_rot_cols(w_kr, MLA_NOPE)), cos_m, sin_m))
    w_uq = _heads(p["mla_w_uq"], MLA_HEADS, MLA_QK)
    q_mla = proj_heads(n_q, bf(w_uq), batch, seq, rope=(bf(_rot_cols(w_uq, MLA_NOPE)), cos_m, sin_m),
                       scale=MLA_QK ** -0.5)
    w_ukv = _heads(p["mla_w_ukv"], MLA_HEADS, MLA_NOPE + MLA_V)
    w_uk = jnp.concatenate([w_ukv[..., :MLA_NOPE], jnp.zeros(w_ukv.shape[:2] + (MLA_ROPE,), F32)], axis=-1)
    k_mla = proj_heads(n_kv, bf(w_uk), batch, seq, add=kpe)
    v_mla = proj_heads(n_kv, bf(w_ukv[..., MLA_NOPE:]), batch, seq)
    o_mla = flash_causal(q_mla, k_mla, v_mla)

    cos_d, sin_d = _rope_tables(DIFF_HD, seq)
    w_dq = _heads(col("d_q"), 2 * DIFF_HEADS, DIFF_HD)
    w_dk = _heads(col("d_k"), 2 * DIFF_HEADS, DIFF_HD)
    q_d = proj_heads(n2d, bf(w_dq), batch, seq, rope=(bf(_rot_cols(w_dq)), cos_d, sin_d), scale=DIFF_HD ** -0.5)
    k_d = proj_heads(n2d, bf(w_dk), batch, seq, rope=(bf(_rot_cols(w_dk)), cos_d, sin_d))
    v_d = proj_heads(n2d, bf(_heads(col("d_v"), DIFF_HEADS, DIFF_VD)), batch, seq)
    lambda_init = 0.8 - 0.6 * math.exp(-0.3 * layer)
    o_diff = diff_attention(q_d, k_d, v_d, p["diff_lam_q1"], p["diff_lam_k1"], p["diff_lam_q2"],
                            p["diff_lam_k2"], p["diff_subln"], lambda_init)

    cos_n, sin_n = _rope_tables(NSA_DK, seq)
    w_nq = _heads(col("nsa_q"), NSA_HEADS, NSA_DK)
    q_n = proj_heads(n2d, bf(w_nq), batch, seq, rope=(bf(_rot_cols(w_nq)), cos_n, sin_n), scale=NSA_DK ** -0.5)
    w_nk = jnp.stack([col("nsa_kc"), col("nsa_ks"), col("nsa_kw")], axis=0)
    k_n = proj_heads(n2d, bf(w_nk), batch, seq, rope=(bf(_rot_cols(w_nk)), cos_n, sin_n))
    v_n = proj_plain(n2d, bf(jnp.concatenate([col("nsa_vc"), col("nsa_vs"), col("nsa_vw")], axis=1)), BF16)
    v_n = v_n.reshape(batch, seq, 3 * NSA_DV)
    w_g = jnp.concatenate([col("nsa_g"), jnp.zeros((w_in.shape[0], LANES - NSA_HEADS * 3), F32)], axis=1)
    g_nsa = proj_plain(n2d, bf(w_g), F32, sigmoid=True)

    ng = seq // NSA_CMP_STRIDE
    kc = nsa_compress(k_n.reshape(batch, 3, ng, NSA_CMP_STRIDE * NSA_DK), 0,
                      p["nsa_cmp_k_pos"], p["nsa_cmp_k_w1"], p["nsa_cmp_k_w2"])
    vc_tok = v_n[:, :, :NSA_DV].reshape(batch, 1, ng, NSA_CMP_STRIDE * NSA_DV)
    vc = nsa_compress(vc_tok, 0, p["nsa_cmp_v_pos"], p["nsa_cmp_v_w1"], p["nsa_cmp_v_w2"])
    o_c, sel = nsa_cmp_topk(q_n, kc, vc)
    o_s = nsa_selected(q_n, k_n, 1, v_n, 1, sel)
    o_w = nsa_window(q_n, k_n, 2, v_n, 2)

    g_merge = proj_plain(n2d, bf(w_in[:, offs["merge"][0]:]), BF16, sigmoid=True)
    m = batch * seq
    return (o_mla.reshape(m, -1), o_c.reshape(m, -1), o_s.reshape(m, -1), o_w.reshape(m, -1), g_nsa,
            o_diff, g_merge)


def kernel(x, ffn1_norm, ffn1_w_in, ffn1_w_out, mix_norm, w_in, mla_q_norm, mla_kv_norm, mla_w_uq, mla_w_ukv, nsa_cmp_k_pos, nsa_cmp_k_w1, nsa_cmp_k_w2, nsa_cmp_v_pos, nsa_cmp_v_w1, nsa_cmp_v_w2, diff_lam_q1, diff_lam_k1, diff_lam_q2, diff_lam_k2, diff_subln, w_br_mla, w_br_nsa, w_br_diff, w_out, ffn2_norm, ffn2_w_in, ffn2_w_out, final_norm):
    batch, seq, d = x.shape
    depth = w_in.shape[0]
    x2d = x.reshape(batch * seq, d)
    bf = lambda a: a.astype(BF16)
    for l in range(depth):
        p = {"mla_q_norm": mla_q_norm[l], "mla_kv_norm": mla_kv_norm[l], "mla_w_uq": mla_w_uq[l],
             "mla_w_ukv": mla_w_ukv[l], "nsa_cmp_k_pos": nsa_cmp_k_pos[l], "nsa_cmp_k_w1": nsa_cmp_k_w1[l],
             "nsa_cmp_k_w2": nsa_cmp_k_w2[l], "nsa_cmp_v_pos": nsa_cmp_v_pos[l], "nsa_cmp_v_w1": nsa_cmp_v_w1[l],
             "nsa_cmp_v_w2": nsa_cmp_v_w2[l], "diff_lam_q1": diff_lam_q1[l], "diff_lam_k1": diff_lam_k1[l],
             "diff_lam_q2": diff_lam_q2[l], "diff_lam_k2": diff_lam_k2[l], "diff_subln": diff_subln[l]}
        x2d = ffn(x2d, ffn1_norm[l], bf(ffn1_w_in[l]), bf(ffn1_w_out[l]))
        n2d = rmsnorm(x2d, mix_norm[l], BF16)
        o_mla, o_c, o_s, o_w, g_nsa, o_diff, g_merge = _mixers(n2d, batch, seq, l, w_in[l], p)
        x2d = merge(x2d, o_mla, o_c, o_s, o_w, g_nsa, o_diff, g_merge, bf(w_br_mla[l]), bf(w_br_nsa[l]),
                    bf(w_br_diff[l]).reshape(DIFF_HEADS, DIFF_VD, d), bf(w_out[l]), seq)
        x2d = ffn(x2d, ffn2_norm[l], bf(ffn2_w_in[l]), bf(ffn2_w_out[l]))
    return rmsnorm(x2d, final_norm, F32).reshape(batch, seq, d)
```

```python
import functools
import math

import numpy as np
import jax
import jax.numpy as jnp
from jax import lax
from jax.experimental import pallas as pl
from jax.experimental.pallas import tpu as pltpu

F32 = jnp.float32
BF16 = jnp.bfloat16

NORM_EPS = 1e-6
ROPE_THETA = 10000.0
NEG_INF = -1e30
REMOVED = -3e38
N_BRANCH = 3

MLA_HEADS = 6
MLA_Q_LORA = 768
MLA_KV_LORA = 512
MLA_NOPE = 128
MLA_ROPE = 64
MLA_V = 128
MLA_QK = MLA_NOPE + MLA_ROPE

NSA_HEADS = 4
NSA_DK = 192
NSA_DV = 128
NSA_CMP_LEN = 32
NSA_CMP_STRIDE = 16
NSA_CMP_HIDDEN = 256
NSA_SEL_LEN = 64
NSA_TOPK = 16
NSA_WINDOW = 512
NSA_FORCE_SCORE = 1e6

DIFF_HEADS = 4
DIFF_HD = 96
DIFF_VD = 2 * DIFF_HD

LANES = 128
ONES_PAD = 16
VMEM_LIMIT_MB = 56


def _cparams(dims, vmem_mb=VMEM_LIMIT_MB):
    return pltpu.CompilerParams(dimension_semantics=dims, vmem_limit_bytes=vmem_mb * 2**20)


def _sigmoid(x):
    return 1.0 / (1.0 + jnp.exp(-x))


def _dot(a, b):
    return jnp.dot(a, b, preferred_element_type=F32)


def _dot_nt(a, b):
    return lax.dot_general(a, b, (((1,), (1,)), ((), ())), preferred_element_type=F32)


def _tile(n, pref):
    t = min(n, pref)
    assert n % t == 0, (n, t)
    return t


def _rmsnorm_kernel(x_ref, w_ref, o_ref):
    x = x_ref[...].astype(F32)
    y = x * lax.rsqrt(jnp.mean(x * x, axis=-1, keepdims=True) + NORM_EPS)
    o_ref[...] = (y * w_ref[...]).astype(o_ref.dtype)


def rmsnorm(x2d, w, out_dtype):
    m, d = x2d.shape
    tm = _tile(m, 1024)
    return pl.pallas_call(
        _rmsnorm_kernel,
        grid=(m // tm,),
        in_specs=[pl.BlockSpec((tm, d), lambda i: (i, 0)), pl.BlockSpec((1, d), lambda i: (0, 0))],
        out_specs=pl.BlockSpec((tm, d), lambda i: (i, 0)),
        out_shape=jax.ShapeDtypeStruct((m, d), out_dtype),
        compiler_params=_cparams(("parallel",)),
        name="rmsnorm",
    )(x2d, w.reshape(1, d).astype(F32))


def _mla_norm_kernel(lat_ref, wq_ref, wkv_ref, nq_ref, nkv_ref):
    lat = lat_ref[...]
    cq = lat[:, :MLA_Q_LORA]
    ckv = lat[:, MLA_Q_LORA:]
    nq = cq * lax.rsqrt(jnp.mean(cq * cq, axis=-1, keepdims=True) + NORM_EPS)
    nkv = ckv * lax.rsqrt(jnp.mean(ckv * ckv, axis=-1, keepdims=True) + NORM_EPS)
    nq_ref[...] = (nq * wq_ref[...]).astype(BF16)
    nkv_ref[...] = (nkv * wkv_ref[...]).astype(BF16)


def mla_norm(lat, wq, wkv):
    m, d = lat.shape
    tm = _tile(m, 1024)
    return pl.pallas_call(
        _mla_norm_kernel,
        grid=(m // tm,),
        in_specs=[pl.BlockSpec((tm, d), lambda i: (i, 0)),
                  pl.BlockSpec((1, MLA_Q_LORA), lambda i: (0, 0)),
                  pl.BlockSpec((1, MLA_KV_LORA), lambda i: (0, 0))],
        out_specs=[pl.BlockSpec((tm, MLA_Q_LORA), lambda i: (i, 0)),
                   pl.BlockSpec((tm, MLA_KV_LORA), lambda i: (i, 0))],
        out_shape=[jax.ShapeDtypeStruct((m, MLA_Q_LORA), BF16), jax.ShapeDtypeStruct((m, MLA_KV_LORA), BF16)],
        compiler_params=_cparams(("parallel",)),
        name="mla_norm",
    )(lat, wq.reshape(1, -1).astype(F32), wkv.reshape(1, -1).astype(F32))


def _ffn_kernel(x_ref, nw_ref, wg_ref, wu_ref, wo_ref, o_ref, n_scr, *, n_f):
    f = pl.program_id(1)

    @pl.when(f == 0)
    def _():
        x = x_ref[...]
        y = x * lax.rsqrt(jnp.mean(x * x, axis=-1, keepdims=True) + NORM_EPS)
        n_scr[...] = (y * nw_ref[...]).astype(BF16)
        o_ref[...] = jnp.zeros_like(o_ref)

    n = n_scr[...]
    g = _dot(n, wg_ref[...])
    u = _dot(n, wu_ref[...])
    h = (g * _sigmoid(g) * u).astype(BF16)
    o_ref[...] += _dot(h, wo_ref[...])

    @pl.when(f == n_f - 1)
    def _():
        o_ref[...] = x_ref[...] + 0.5 * o_ref[...]


def ffn(x2d, norm_w, w_in, w_out):
    m, d = x2d.shape
    f_dim = w_out.shape[0]
    tm = _tile(m, 512)
    tf = _tile(f_dim, 512)
    n_f = f_dim // tf
    return pl.pallas_call(
        functools.partial(_ffn_kernel, n_f=n_f),
        grid=(m // tm, n_f),
        in_specs=[pl.BlockSpec((tm, d), lambda i, f: (i, 0)),
                  pl.BlockSpec((1, d), lambda i, f: (0, 0)),
                  pl.BlockSpec((d, tf), lambda i, f: (0, f)),
                  pl.BlockSpec((d, tf), lambda i, f: (0, f + n_f)),
                  pl.BlockSpec((tf, d), lambda i, f: (f, 0))],
        out_specs=pl.BlockSpec((tm, d), lambda i, f: (i, 0)),
        out_shape=jax.ShapeDtypeStruct((m, d), F32),
        scratch_shapes=[pltpu.VMEM((tm, d), BF16)],
        compiler_params=_cparams(("parallel", "arbitrary")),
        name="ffn",
    )(x2d, norm_w.reshape(1, d).astype(F32), w_in, w_in, w_out)


def _proj_plain_kernel(n_ref, w_ref, o_ref, *, sigmoid):
    y = _dot(n_ref[...], w_ref[...])
    if sigmoid:
        y = _sigmoid(y)
    o_ref[...] = y.astype(o_ref.dtype)


def proj_plain(n2d, w, out_dtype, sigmoid=False):
    m, k = n2d.shape
    n_out = w.shape[1]
    tm = _tile(m, 1024)
    tn = _tile(n_out, 512) if n_out % 512 == 0 else _tile(n_out, LANES)
    return pl.pallas_call(
        functools.partial(_proj_plain_kernel, sigmoid=sigmoid),
        grid=(m // tm, n_out // tn),
        in_specs=[pl.BlockSpec((tm, k), lambda i, j: (i, 0)),
                  pl.BlockSpec((k, tn), lambda i, j: (0, j))],
        out_specs=pl.BlockSpec((tm, tn), lambda i, j: (i, j)),
        out_shape=jax.ShapeDtypeStruct((m, n_out), out_dtype),
        compiler_params=_cparams(("parallel", "arbitrary")),
        name="proj_plain",
    )(n2d, w)


def _proj_heads_kernel(*refs, rope, has_add, scale):
    n_ref, w_ref = refs[0], refs[1]
    pos = 2
    n = n_ref[...]
    y = _dot(n, w_ref[0])
    if rope:
        wr_ref, cos_ref, sin_ref = refs[pos:pos + 3]
        pos += 3
        y = y * cos_ref[...] + _dot(n, wr_ref[0]) * sin_ref[...]
    if has_add:
        y = y + refs[pos][0, 0].astype(F32)
        pos += 1
    if scale != 1.0:
        y = y * scale
    o_ref = refs[pos]
    o_ref[0, 0] = y.astype(o_ref.dtype)


def proj_heads(n2d, w, batch, seq, rope=None, add=None, scale=1.0):
    m, k = n2d.shape
    g, _, dh = w.shape
    tm = _tile(seq, 1024)
    ns = seq // tm
    in_specs = [pl.BlockSpec((tm, k), lambda b, i, h: (b * ns + i, 0)),
                pl.BlockSpec((1, k, dh), lambda b, i, h: (h, 0, 0))]
    args = [n2d, w]
    if rope is not None:
        w_rot, cos, sin = rope
        in_specs += [pl.BlockSpec((1, k, dh), lambda b, i, h: (h, 0, 0)),
                     pl.BlockSpec((tm, dh), lambda b, i, h: (i, 0)),
                     pl.BlockSpec((tm, dh), lambda b, i, h: (i, 0))]
        args += [w_rot, cos, sin]
    if add is not None:
        in_specs.append(pl.BlockSpec((1, 1, tm, dh), lambda b, i, h: (b, 0, i, 0)))
        args.append(add)
    return pl.pallas_call(
        functools.partial(_proj_heads_kernel, rope=rope is not None, has_add=add is not None, scale=scale),
        grid=(batch, ns, g),
        in_specs=in_specs,
        out_specs=pl.BlockSpec((1, 1, tm, dh), lambda b, i, h: (b, h, i, 0)),
        out_shape=jax.ShapeDtypeStruct((batch, g, seq, dh), BF16),
        compiler_params=_cparams(("parallel", "parallel", "arbitrary")),
        name="proj_heads",
    )(*args)


def _proj_heads_t_kernel(n_ref, wt_ref, o_ref):
    dh = wt_ref.shape[1]
    o_ref[0, 0, :dh, :] = _dot_nt(wt_ref[0], n_ref[...]).astype(o_ref.dtype)
    pad = o_ref.shape[2] - dh
    row = lax.broadcasted_iota(jnp.int32, (pad, o_ref.shape[3]), 0)
    o_ref[0, 0, dh:, :] = jnp.where(row == 0, 1.0, 0.0).astype(o_ref.dtype)


def proj_heads_t(n2d, w, batch, seq):
    m, k = n2d.shape
    g, _, dh = w.shape
    tm = _tile(seq, 1024)
    ns = seq // tm
    return pl.pallas_call(
        _proj_heads_t_kernel,
        grid=(batch, ns, g),
        in_specs=[pl.BlockSpec((tm, k), lambda b, i, h: (b * ns + i, 0)),
                  pl.BlockSpec((1, dh, k), lambda b, i, h: (h, 0, 0))],
        out_specs=pl.BlockSpec((1, 1, dh + ONES_PAD, tm), lambda b, i, h: (b, h, 0, i)),
        out_shape=jax.ShapeDtypeStruct((batch, g, dh + ONES_PAD, seq), BF16),
        compiler_params=_cparams(("parallel", "parallel", "arbitrary")),
        name="proj_heads_t",
    )(n2d, jnp.swapaxes(w, 1, 2))


def _softmax_step_t(st, vt, m_scr, acc_scr):
    m_prev = m_scr[...]
    m_new = jnp.maximum(m_prev, jnp.max(st, axis=0, keepdims=True))
    alpha = jnp.exp(m_prev - m_new)
    pt = jnp.exp((st - m_new).astype(BF16))
    acc_scr[...] = alpha * acc_scr[...] + _dot(vt, pt)
    m_scr[...] = m_new


def _diag_visibility(d, c, tq, tk):
    if d * tk >= (c + 1) * tq:
        return "none"
    if (d + 1) * tk - 1 <= c * tq:
        return "full"
    return "partial"


def _causal_where(st, d, c):
    tk, tq = st.shape
    key = lax.broadcasted_iota(jnp.int32, (tk, tq), 0) + d * tk
    qry = lax.broadcasted_iota(jnp.int32, (tk, tq), 1) + c * tq
    return jnp.where(key <= qry, st, NEG_INF)


def _flash_sweep(n, r, scores, vt_tile, m_scr, acc_scr, st_scr):
    n_chain = m_scr.shape[0]
    for c in range(n_chain):
        m_scr[c] = jnp.full(m_scr.shape[1:], NEG_INF, F32)
        acc_scr[c] = jnp.zeros(acc_scr.shape[1:], F32)

    @pl.when(n > 0)
    def _():
        for c, st in enumerate(scores(0, None)):
            st_scr[c] = st

    def body(j, carry):
        nxt = scores(jnp.minimum(j + 1, n - 1), None)
        vt = vt_tile(j)
        for c in range(n_chain):
            st = st_scr[c]
            st_scr[c] = nxt[c]
            _softmax_step_t(st, vt, m_scr.at[c], acc_scr.at[c])
        return carry

    lax.fori_loop(0, n, body, 0)
    for d in range(r):
        sts = scores(n + d, d)
        vt = vt_tile(n + d)
        for c in range(n_chain):
            if sts[c] is not None:
                _softmax_step_t(sts[c], vt, m_scr.at[c], acc_scr.at[c])


def _flash_kernel(q_ref, k_ref, vt_ref, o_ref, m_scr, acc_scr, st_scr, *, n_sub, tq, tk):
    i = pl.program_id(2)
    dv = o_ref.shape[2]
    r = n_sub * tq // tk

    def scores(j, d):
        k = k_ref[0, 0, pl.ds(pl.multiple_of(j * tk, tk), tk), :]
        out = []
        for c in range(n_sub):
            vis = "full" if d is None else _diag_visibility(d, c, tq, tk)
            if vis == "none":
                out.append(None)
                continue
            st = _dot_nt(k, q_ref[0, 0, c * tq:(c + 1) * tq, :])
            out.append(_causal_where(st, d, c) if vis == "partial" else st)
        return out

    def vt_tile(j):
        return vt_ref[0, 0, :, pl.ds(pl.multiple_of(j * tk, tk), tk)]

    _flash_sweep(i * r, r, scores, vt_tile, m_scr, acc_scr, st_scr)
    for c in range(n_sub):
        o_t = acc_scr[c, :dv, :] / acc_scr[c, dv:dv + 1, :]
        o_ref[0, c * tq:(c + 1) * tq, :] = o_t.T.astype(o_ref.dtype)


def _flash_tiles(s):
    tq = _tile(s, 512)
    n_sub = 2 if s % (2 * tq) == 0 else 1
    return n_sub, tq, tq


def flash_causal(q, k, vt):
    b, h, s, dk = q.shape
    dve = vt.shape[2]
    dv = dve - ONES_PAD
    n_sub, tq, tk = _flash_tiles(s)
    t = n_sub * tq
    return pl.pallas_call(
        functools.partial(_flash_kernel, n_sub=n_sub, tq=tq, tk=tk),
        grid=(b, h, s // t),
        in_specs=[pl.BlockSpec((1, 1, t, dk), lambda b_, h_, i: (b_, h_, i, 0)),
                  pl.BlockSpec((1, 1, s, dk), lambda b_, h_, i: (b_, h_, 0, 0)),
                  pl.BlockSpec((1, 1, dve, s), lambda b_, h_, i: (b_, h_, 0, 0))],
        out_specs=pl.BlockSpec((1, t, dv), lambda b_, h_, i: (b_, i, h_)),
        out_shape=jax.ShapeDtypeStruct((b, s, h * dv), BF16),
        scratch_shapes=[pltpu.VMEM((n_sub, 1, tq), F32), pltpu.VMEM((n_sub, dve, tq), F32),
                        pltpu.VMEM((n_sub, tk, tq), F32)],
        compiler_params=_cparams(("parallel", "parallel", "arbitrary")),
        name="mla_flash",
    )(q, k, vt)


def _diff_kernel(q_ref, k_ref, vt_ref, lq1_ref, lk1_ref, lq2_ref, lk2_ref, sub_ref, o_ref,
                 m_scr, acc_scr, st_scr, *, n_sub, tq, tk, lambda_init):
    i = pl.program_id(2)
    vd = o_ref.shape[3]
    r = n_sub * tq // tk

    def scores(j, d):
        off = pl.multiple_of(j * tk, tk)
        out = []
        for comp in range(2):
            k = k_ref[0, comp, pl.ds(off, tk), :]
            for c in range(n_sub):
                vis = "full" if d is None else _diag_visibility(d, c, tq, tk)
                if vis == "none":
                    out.append(None)
                    continue
                st = _dot_nt(k, q_ref[0, comp, c * tq:(c + 1) * tq, :])
                out.append(_causal_where(st, d, c) if vis == "partial" else st)
        return out

    def vt_tile(j):
        return vt_ref[0, 0, :, pl.ds(pl.multiple_of(j * tk, tk), tk)]

    _flash_sweep(i * r, r, scores, vt_tile, m_scr, acc_scr, st_scr)

    lam = (jnp.exp(jnp.sum(lq1_ref[...] * lk1_ref[...], axis=-1, keepdims=True))
           - jnp.exp(jnp.sum(lq2_ref[...] * lk2_ref[...], axis=-1, keepdims=True)) + lambda_init)
    lane_pad = (-vd) % LANES
    for c in range(n_sub):
        o_t = (acc_scr[c, :vd, :] / acc_scr[c, vd:vd + 1, :]
               - lam * (acc_scr[n_sub + c, :vd, :] / acc_scr[n_sub + c, vd:vd + 1, :]))
        o_t = o_t * lax.rsqrt(jnp.mean(o_t * o_t, axis=0, keepdims=True) + NORM_EPS)
        o_t = jnp.concatenate([o_t, jnp.zeros((lane_pad, tq), F32)], axis=0)
        o_ref[0, 0, c * tq:(c + 1) * tq, :] = (
            o_t.T[:, :vd] * sub_ref[...] * (1.0 - lambda_init)).astype(o_ref.dtype)


def diff_attention(q, k, vt, lq1, lk1, lq2, lk2, subln, lambda_init):
    b, h2, s, hd = q.shape
    h = h2 // 2
    vde = vt.shape[2]
    vd = vde - ONES_PAD
    n_sub, tq, tk = _flash_tiles(s)
    t = n_sub * tq
    vec = lambda a: a.reshape(1, -1).astype(F32)
    small = lambda n: pl.BlockSpec((1, n), lambda b_, h_, i: (0, 0))
    return pl.pallas_call(
        functools.partial(_diff_kernel, n_sub=n_sub, tq=tq, tk=tk, lambda_init=lambda_init),
        grid=(b, h, s // t),
        in_specs=[pl.BlockSpec((1, 2, t, hd), lambda b_, h_, i: (b_, h_, i, 0)),
                  pl.BlockSpec((1, 2, s, hd), lambda b_, h_, i: (b_, h_, 0, 0)),
                  pl.BlockSpec((1, 1, vde, s), lambda b_, h_, i: (b_, h_, 0, 0)),
                  small(hd), small(hd), small(hd), small(hd), small(vd)],
        out_specs=pl.BlockSpec((1, 1, t, vd), lambda b_, h_, i: (b_, h_, i, 0)),
        out_shape=jax.ShapeDtypeStruct((b, h, s, vd), BF16),
        scratch_shapes=[pltpu.VMEM((2 * n_sub, 1, tq), F32), pltpu.VMEM((2 * n_sub, vde, tq), F32),
                        pltpu.VMEM((2 * n_sub, tk, tq), F32)],
        compiler_params=_cparams(("parallel", "parallel", "arbitrary")),
        name="diff_flash",
    )(q, k, vt, vec(lq1), vec(lk1), vec(lq2), vec(lk2), vec(subln))


def _compress_kernel(a_ref, pa_ref, pb_ref, w1a_ref, w1b_ref, w2_ref, o_ref):
    a = a_ref[0, 0].astype(F32)
    u = _dot((a + pa_ref[...]).astype(BF16), w1a_ref[...])
    v = _dot((a + pb_ref[...]).astype(BF16), w1b_ref[...])
    ng = v.shape[0]
    hdn = u + pltpu.roll(v, ng - 1, 0)
    hdn = hdn * _sigmoid(hdn)
    o_ref[0] = _dot(hdn.astype(BF16), w2_ref[...]).astype(o_ref.dtype)


def nsa_compress(tok, g_idx, pos, w1, w2):
    assert NSA_CMP_LEN == 2 * NSA_CMP_STRIDE
    b, _, ng, wd = tok.shape
    d = wd // NSA_CMP_STRIDE
    pos_flat = pos.astype(F32).reshape(1, NSA_CMP_LEN * d)
    w1 = w1.astype(BF16)
    full = lambda shape: pl.BlockSpec(shape, lambda b_: (0,) * len(shape))
    return pl.pallas_call(
        _compress_kernel,
        grid=(b,),
        in_specs=[pl.BlockSpec((1, 1, ng, wd), lambda b_: (b_, g_idx, 0, 0)),
                  full((1, wd)), full((1, wd)), full((wd, NSA_CMP_HIDDEN)), full((wd, NSA_CMP_HIDDEN)),
                  full((NSA_CMP_HIDDEN, d))],
        out_specs=pl.BlockSpec((1, ng, d), lambda b_: (b_, 0, 0)),
        out_shape=jax.ShapeDtypeStruct((b, ng, d), BF16),
        compiler_params=_cparams(("parallel",)),
        name="nsa_compress",
    )(tok, pos_flat[:, :wd], pos_flat[:, wd:], w1[:wd], w1[wd:], w2.astype(BF16))


def _split3(x):
    hi = x.astype(BF16)
    r = x - hi.astype(F32)
    mid = r.astype(BF16)
    lo = (r - mid.astype(F32)).astype(BF16)
    return hi, mid, lo


def _cmp_topk_kernel(q_ref, kc_ref, vc_ref, wsel_ref, oc_ref, sel_ref, *, tq, k_top):
    i = pl.program_id(1)
    qs = i * tq
    nc = kc_ref.shape[1]
    n_sel = wsel_ref.shape[1]
    kc = kc_ref[0]
    vc = vc_ref[0]
    qpos_c = qs + lax.broadcasted_iota(jnp.int32, (tq, nc), 0)
    cend = lax.broadcasted_iota(jnp.int32, (tq, nc), 1) * NSA_CMP_STRIDE + (NSA_CMP_LEN - 1)
    valid_c = cend <= qpos_c
    imp = jnp.zeros((tq, nc), F32)
    for h in range(NSA_HEADS):
        s = jnp.where(valid_c, _dot_nt(q_ref[0, h], kc), NEG_INF)
        e = jnp.exp(s - jnp.max(s, axis=-1, keepdims=True))
        p = jnp.where(valid_c, e / jnp.sum(e, axis=-1, keepdims=True), 0.0)
        oc_ref[0, :, h * NSA_DV:(h + 1) * NSA_DV] = _dot(p.astype(BF16), vc)
        imp = imp + p

    wsel = wsel_ref[...]
    hi, mid, lo = _split3(imp)
    imp_sel = _dot(hi, wsel) + _dot(mid, wsel) + _dot(lo, wsel)

    blk = lax.broadcasted_iota(jnp.int32, (tq, n_sel), 1)
    qpos = qs + lax.broadcasted_iota(jnp.int32, (tq, n_sel), 0)
    cur = lax.shift_right_arithmetic(qpos, int(math.log2(NSA_SEL_LEN)))
    forced = (blk == 0) | (blk == cur) | (blk == cur - 1)
    valid_s = blk * NSA_SEL_LEN <= qpos
    score = jnp.where(valid_s, jnp.where(forced, NSA_FORCE_SCORE, imp_sel), NEG_INF)
    blk_f = blk.astype(F32)
    sel = jnp.zeros((tq, n_sel), F32)
    for _ in range(k_top):
        mx = jnp.max(score, axis=-1, keepdims=True)
        first = jnp.min(jnp.where(score == mx, blk_f, float(n_sel)), axis=-1, keepdims=True)
        hit = blk_f == first
        sel = jnp.where(hit, 1.0, sel)
        score = jnp.where(hit, REMOVED, score)
    sel_ref[0] = sel.astype(sel_ref.dtype)


def _sel_weight_matrix(n_pad, n_sel):
    r_c = NSA_CMP_LEN // NSA_CMP_STRIDE
    ratio = NSA_SEL_LEN // NSA_CMP_STRIDE
    overlap_w = [max(0, min(o * NSA_CMP_STRIDE + NSA_CMP_LEN, NSA_SEL_LEN) - max(o * NSA_CMP_STRIDE, 0))
                 / NSA_CMP_STRIDE for o in range(-(r_c - 1), ratio)]
    w = np.zeros((n_pad, n_sel), np.float32)
    for n in range(n_sel):
        for u, w_u in enumerate(overlap_w):
            c = ratio * n + u - (r_c - 1)
            if 0 <= c < n_pad:
                w[c, n] = w_u
    return w


def nsa_cmp_topk(q, kc, vc):
    b, h, s, dk = q.shape
    nc = kc.shape[1]
    n_sel = s // NSA_SEL_LEN
    assert NSA_SEL_LEN & (NSA_SEL_LEN - 1) == 0
    tq = _tile(s, 256)
    wsel = jnp.asarray(_sel_weight_matrix(nc, n_sel), BF16)
    return pl.pallas_call(
        functools.partial(_cmp_topk_kernel, tq=tq, k_top=min(NSA_TOPK, n_sel)),
        grid=(b, s // tq),
        in_specs=[pl.BlockSpec((1, h, tq, dk), lambda b_, i: (b_, 0, i, 0)),
                  pl.BlockSpec((1, nc, dk), lambda b_, i: (b_, 0, 0)),
                  pl.BlockSpec((1, nc, NSA_DV), lambda b_, i: (b_, 0, 0)),
                  pl.BlockSpec((nc, n_sel), lambda b_, i: (0, 0))],
        out_specs=[pl.BlockSpec((1, tq, h * NSA_DV), lambda b_, i: (b_, i, 0)),
                   pl.BlockSpec((1, tq, n_sel), lambda b_, i: (b_, i, 0))],
        out_shape=[jax.ShapeDtypeStruct((b, s, h * NSA_DV), F32), jax.ShapeDtypeStruct((b, s, n_sel), BF16)],
        compiler_params=_cparams(("parallel", "parallel")),
        name="nsa_cmp_topk",
    )(q, kc, vc, wsel)


def _nsa_sel_kernel(q_ref, k_ref, vt_ref, sel_ref, o_ref, m_scr, acc_scr, st_scr, *, t):
    i = pl.program_id(1)
    n_sel = sel_ref.shape[2]
    bpt = t // NSA_SEL_LEN
    shift = int(math.log2(NSA_SEL_LEN))

    def scores(j, d):
        k = k_ref[0, 0, pl.ds(pl.multiple_of(j * t, t), t), :]
        blk_key = lax.shift_right_arithmetic(lax.broadcasted_iota(jnp.int32, (t, n_sel), 0), shift) + j * bpt
        blk_col = lax.broadcasted_iota(jnp.int32, (t, n_sel), 1)
        expand_t = jnp.where(blk_key == blk_col, 1.0, 0.0).astype(BF16)
        flags_t = _dot_nt(expand_t, sel_ref[0])
        if d is not None:
            flags_t = _causal_where(flags_t, 0, 0)
        keep_t = flags_t > 0.5
        return [jnp.where(keep_t, _dot_nt(k, q_ref[0, h]), NEG_INF) for h in range(NSA_HEADS)]

    def vt_tile(j):
        return vt_ref[0, 0, :, pl.ds(pl.multiple_of(j * t, t), t)]

    _flash_sweep(i, 1, scores, vt_tile, m_scr, acc_scr, st_scr)
    for h in range(NSA_HEADS):
        o_t = acc_scr[h, :NSA_DV, :] / acc_scr[h, NSA_DV:NSA_DV + 1, :]
        o_ref[0, :, h * NSA_DV:(h + 1) * NSA_DV] = o_t.T


def nsa_selected(q, k3, k_idx, vt, sel):
    b, h, s, dk = q.shape
    n_sel = sel.shape[-1]
    dve = vt.shape[2]
    t = _tile(s, 512)
    assert t % NSA_SEL_LEN == 0
    return pl.pallas_call(
        functools.partial(_nsa_sel_kernel, t=t),
        grid=(b, s // t),
        in_specs=[pl.BlockSpec((1, h, t, dk), lambda b_, i: (b_, 0, i, 0)),
                  pl.BlockSpec((1, 1, s, dk), lambda b_, i: (b_, k_idx, 0, 0)),
                  pl.BlockSpec((1, 1, dve, s), lambda b_, i: (b_, 0, 0, 0)),
                  pl.BlockSpec((1, t, n_sel), lambda b_, i: (b_, i, 0))],
        out_specs=pl.BlockSpec((1, t, h * NSA_DV), lambda b_, i: (b_, i, 0)),
        out_shape=jax.ShapeDtypeStruct((b, s, h * NSA_DV), F32),
        scratch_shapes=[pltpu.VMEM((h, 1, t), F32), pltpu.VMEM((h, dve, t), F32), pltpu.VMEM((h, t, t), F32)],
        compiler_params=_cparams(("parallel", "arbitrary")),
        name="nsa_selected",
    )(q, k3, vt, sel)


def _nsa_win_kernel(q_ref, kp_ref, kc_ref, vp_ref, vc_ref, o_ref, *, t):
    i = pl.program_id(1)
    row = lax.broadcasted_iota(jnp.int32, (t, t), 0)
    col = lax.broadcasted_iota(jnp.int32, (t, t), 1)
    keep_prev = (col > row) & (i > 0)
    keep_cur = col <= row
    kp, kc, vp, vc = kp_ref[0, 0], kc_ref[0, 0], vp_ref[0], vc_ref[0]
    for h in range(NSA_HEADS):
        q = q_ref[0, h]
        sp = jnp.where(keep_prev, _dot_nt(q, kp), NEG_INF)
        sc = jnp.where(keep_cur, _dot_nt(q, kc), NEG_INF)
        m = jnp.maximum(jnp.max(sp, axis=-1, keepdims=True), jnp.max(sc, axis=-1, keepdims=True))
        pp = jnp.exp(sp - m)
        pc = jnp.exp(sc - m)
        l = jnp.sum(pp, axis=-1, keepdims=True) + jnp.sum(pc, axis=-1, keepdims=True)
        o = _dot(pp.astype(BF16), vp) + _dot(pc.astype(BF16), vc)
        o_ref[0, :, h * NSA_DV:(h + 1) * NSA_DV] = o / l


def nsa_window(q, k3, k_idx, v3, v_idx):
    b, h, s, dk = q.shape
    t = _tile(s, NSA_WINDOW)
    assert t == NSA_WINDOW, "window kernel needs the query tile to equal the window"
    prev = lambda i: jnp.maximum(i - 1, 0)
    return pl.pallas_call(
        functools.partial(_nsa_win_kernel, t=t),
        grid=(b, s // t),
        in_specs=[pl.BlockSpec((1, h, t, dk), lambda b_, i: (b_, 0, i, 0)),
                  pl.BlockSpec((1, 1, t, dk), lambda b_, i: (b_, k_idx, prev(i), 0)),
                  pl.BlockSpec((1, 1, t, dk), lambda b_, i: (b_, k_idx, i, 0)),
                  pl.BlockSpec((1, t, NSA_DV), lambda b_, i: (b_, prev(i), v_idx)),
                  pl.BlockSpec((1, t, NSA_DV), lambda b_, i: (b_, i, v_idx))],
        out_specs=pl.BlockSpec((1, t, h * NSA_DV), lambda b_, i: (b_, i, 0)),
        out_shape=jax.ShapeDtypeStruct((b, s, h * NSA_DV), F32),
        compiler_params=_cparams(("parallel", "parallel")),
        name="nsa_window",
    )(q, k3, k3, v3, v3)


def _merge_kernel(x_ref, om_ref, oc_ref, os_ref, ow_ref, gn_ref, od_ref, gm_ref, gs_ref, gd_ref,
                  wm_ref, wn_ref, wd_ref, wo_ref, ex_ref, o_ref, onsa_scr, *, n_j):
    j = pl.program_id(1)
    nsa_out = NSA_HEADS * NSA_DV

    @pl.when(j == 0)
    def _():
        g = gn_ref[...]
        hi = g.astype(BF16)
        lo = (g - hi.astype(F32)).astype(BF16)
        ge = _dot(hi, ex_ref[...]) + _dot(lo, ex_ref[...])
        onsa = (ge[:, :nsa_out] * oc_ref[...] + ge[:, nsa_out:2 * nsa_out] * os_ref[...]
                + ge[:, 2 * nsa_out:] * ow_ref[...])
        onsa_scr[...] = onsa.astype(BF16)
        o_ref[...] = jnp.zeros_like(o_ref)

    ym = _dot(om_ref[...], wm_ref[...])
    yn = _dot(onsa_scr[...], wn_ref[...])
    yd = _dot(od_ref[0, 0], wd_ref[0])
    for h in range(1, DIFF_HEADS):
        yd = yd + _dot(od_ref[0, h], wd_ref[h])
    mixed = (gm_ref[...].astype(F32) * ym + gs_ref[...].astype(F32) * yn + gd_ref[...].astype(F32) * yd)
    o_ref[...] += _dot(mixed.astype(BF16), wo_ref[...])

    @pl.when(j == n_j - 1)
    def _():
        o_ref[...] = x_ref[...] + o_ref[...]


def _gate_expand_matrix():
    nsa_out = NSA_HEADS * NSA_DV
    e = np.zeros((LANES, 3 * nsa_out), np.float32)
    for h in range(NSA_HEADS):
        for c in range(3):
            e[h * 3 + c, c * nsa_out + h * NSA_DV:c * nsa_out + (h + 1) * NSA_DV] = 1.0
    return e


def merge(x2d, o_mla, o_c, o_s, o_w, g_nsa, o_diff, g_merge, w_br_mla, w_br_nsa, w_br_diff, w_out, seq):
    m, d = x2d.shape
    tm = _tile(seq, 512)
    ns = seq // tm
    tn = _tile(d, 512)
    n_j = d // tn
    nsa_out = NSA_HEADS * NSA_DV
    mla_out = o_mla.shape[-1]
    ex = jnp.asarray(_gate_expand_matrix(), BF16)
    row = lambda w: pl.BlockSpec((tm, w), lambda i, j: (i, 0))
    return pl.pallas_call(
        functools.partial(_merge_kernel, n_j=n_j),
        grid=(m // tm, n_j),
        in_specs=[row(d), row(mla_out), row(nsa_out), row(nsa_out), row(nsa_out), row(LANES),
                  pl.BlockSpec((1, DIFF_HEADS, tm, DIFF_VD), lambda i, j: (i // ns, 0, i % ns, 0)),
                  pl.BlockSpec((tm, tn), lambda i, j: (i, j)),
                  pl.BlockSpec((tm, tn), lambda i, j: (i, n_j + j)),
                  pl.BlockSpec((tm, tn), lambda i, j: (i, 2 * n_j + j)),
                  pl.BlockSpec((mla_out, tn), lambda i, j: (0, j)),
                  pl.BlockSpec((nsa_out, tn), lambda i, j: (0, j)),
                  pl.BlockSpec((DIFF_HEADS, DIFF_VD, tn), lambda i, j: (0, 0, j)),
                  pl.BlockSpec((tn, d), lambda i, j: (j, 0)),
                  pl.BlockSpec((LANES, 3 * nsa_out), lambda i, j: (0, 0))],
        out_specs=pl.BlockSpec((tm, d), lambda i, j: (i, 0)),
        out_shape=jax.ShapeDtypeStruct((m, d), F32),
        scratch_shapes=[pltpu.VMEM((tm, nsa_out), BF16)],
        compiler_params=_cparams(("parallel", "arbitrary")),
        name="merge",
    )(x2d, o_mla, o_c, o_s, o_w, g_nsa, o_diff, g_merge, g_merge, g_merge,
      w_br_mla, w_br_nsa, w_br_diff, w_out, ex)


def _rope_tables(dim, seq, lead=0):
    inv = ROPE_THETA ** (-jnp.arange(0, dim, 2, dtype=F32) / dim)
    ang = jnp.arange(seq, dtype=F32)[:, None] * inv[None, :]
    cos, sin = jnp.cos(ang), jnp.sin(ang)
    cos = jnp.concatenate([jnp.ones((seq, lead), F32), cos, cos], axis=-1)
    sin = jnp.concatenate([jnp.zeros((seq, lead), F32), sin, sin], axis=-1)
    return cos, sin


def _heads(w, g, dh):
    return w.reshape(w.shape[0], g, dh).transpose(1, 0, 2)


def _rot_cols(w, lead=0):
    half = (w.shape[-1] - lead) // 2
    x1 = w[..., lead:lead + half]
    x2 = w[..., lead + half:]
    return jnp.concatenate([jnp.zeros_like(w[..., :lead]), -x2, x1], axis=-1)


def _col_offsets():
    sizes = (MLA_Q_LORA, MLA_KV_LORA, MLA_ROPE,
             NSA_HEADS * NSA_DK, NSA_DK, NSA_DV, NSA_DK, NSA_DV, NSA_DK, NSA_DV, NSA_HEADS * 3,
             DIFF_HEADS * 2 * DIFF_HD, DIFF_HEADS * 2 * DIFF_HD, DIFF_HEADS * DIFF_VD)
    names = ("c_q", "c_kv", "k_rope", "nsa_q", "nsa_kc", "nsa_vc", "nsa_ks", "nsa_vs", "nsa_kw", "nsa_vw",
             "nsa_g", "d_q", "d_k", "d_v")
    offs = {}
    o = 0
    for nme, sz in zip(names, sizes):
        offs[nme] = (o, o + sz)
        o += sz
    offs["merge"] = (o, None)
    return offs


def _mixers(n2d, batch, seq, layer, w_in, p):
    offs = _col_offsets()
    col = lambda name: w_in[:, offs[name][0]:offs[name][1]]
    bf = lambda a: a.astype(BF16)

    lat = proj_plain(n2d, bf(jnp.concatenate([col("c_q"), col("c_kv")], axis=1)), F32)
    n_q, n_kv = mla_norm(lat, p["mla_q_norm"], p["mla_kv_norm"])
    cos_m, sin_m = _rope_tables(MLA_ROPE, seq, lead=MLA_NOPE)
    w_kr = jnp.concatenate([jnp.zeros((w_in.shape[0], MLA_NOPE), F32), col("k_rope")], axis=1)[None]
    kpe = proj_heads(n2d, bf(w_kr), batch, seq, rope=(bf(_rot_cols(w_kr, MLA_NOPE)), cos_m, sin_m))
    w_uq = _heads(p["mla_w_uq"], MLA_HEADS, MLA_QK)
    q_mla = proj_heads(n_q, bf(w_uq), batch, seq, rope=(bf(_rot_cols(w_uq, MLA_NOPE)), cos_m, sin_m),
                       scale=MLA_QK ** -0.5)
    w_ukv = _heads(p["mla_w_ukv"], MLA_HEADS, MLA_NOPE + MLA_V)
    w_uk = jnp.concatenate([w_ukv[..., :MLA_NOPE], jnp.zeros(w_ukv.shape[:2] + (MLA_ROPE,), F32)], axis=-1)
    k_mla = proj_heads(n_kv, bf(w_uk), batch, seq, add=kpe)
    vt_mla = proj_heads_t(n_kv, bf(w_ukv[..., MLA_NOPE:]), batch, seq)
    o_mla = flash_causal(q_mla, k_mla, vt_mla)

    cos_d, sin_d = _rope_tables(DIFF_HD, seq)
    w_dq = _heads(col("d_q"), 2 * DIFF_HEADS, DIFF_HD)
    w_dk = _heads(col("d_k"), 2 * DIFF_HEADS, DIFF_HD)
    q_d = proj_heads(n2d, bf(w_dq), batch, seq, rope=(bf(_rot_cols(w_dq)), cos_d, sin_d), scale=DIFF_HD ** -0.5)
    k_d = proj_heads(n2d, bf(w_dk), batch, seq, rope=(bf(_rot_cols(w_dk)), cos_d, sin_d))
    vt_d = proj_heads_t(n2d, bf(_heads(col("d_v"), DIFF_HEADS, DIFF_VD)), batch, seq)
    lambda_init = 0.8 - 0.6 * math.exp(-0.3 * layer)
    o_diff = diff_attention(q_d, k_d, vt_d, p["diff_lam_q1"], p["diff_lam_k1"], p["diff_lam_q2"],
                            p["diff_lam_k2"], p["diff_subln"], lambda_init)

    cos_n, sin_n = _rope_tables(NSA_DK, seq)
    w_nq = _heads(col("nsa_q"), NSA_HEADS, NSA_DK)
    q_n = proj_heads(n2d, bf(w_nq), batch, seq, rope=(bf(_rot_cols(w_nq)), cos_n, sin_n), scale=NSA_DK ** -0.5)
    w_nk = jnp.stack([col("nsa_kc"), col("nsa_ks"), col("nsa_kw")], axis=0)
    k_n = proj_heads(n2d, bf(w_nk), batch, seq, rope=(bf(_rot_cols(w_nk)), cos_n, sin_n))
    v_n = proj_plain(n2d, bf(jnp.concatenate([col("nsa_vc"), col("nsa_vw")], axis=1)), BF16)
    v_n = v_n.reshape(batch, seq, 2 * NSA_DV)
    vt_slc = proj_heads_t(n2d, bf(col("nsa_vs"))[None], batch, seq)
    w_g = jnp.concatenate([col("nsa_g"), jnp.zeros((w_in.shape[0], LANES - NSA_HEADS * 3), F32)], axis=1)
    g_nsa = proj_plain(n2d, bf(w_g), F32, sigmoid=True)

    ng = seq // NSA_CMP_STRIDE
    kc = nsa_compress(k_n.reshape(batch, 3, ng, NSA_CMP_STRIDE * NSA_DK), 0,
                      p["nsa_cmp_k_pos"], p["nsa_cmp_k_w1"], p["nsa_cmp_k_w2"])
    vc_tok = v_n[:, :, :NSA_DV].reshape(batch, 1, ng, NSA_CMP_STRIDE * NSA_DV)
    vc = nsa_compress(vc_tok, 0, p["nsa_cmp_v_pos"], p["nsa_cmp_v_w1"], p["nsa_cmp_v_w2"])
    o_c, sel = nsa_cmp_topk(q_n, kc, vc)
    o_s = nsa_selected(q_n, k_n, 1, vt_slc, sel)
    o_w = nsa_window(q_n, k_n, 2, v_n, 1)

    g_merge = proj_plain(n2d, bf(w_in[:, offs["merge"][0]:]), BF16, sigmoid=True)
    m = batch * seq
    return (o_mla.reshape(m, -1), o_c.reshape(m, -1), o_s.reshape(m, -1), o_w.reshape(m, -1), g_nsa,
            o_diff, g_merge)


def kernel(x, ffn1_norm, ffn1_w_in, ffn1_w_out, mix_norm, w_in, mla_q_norm, mla_kv_norm, mla_w_uq, mla_w_ukv, nsa_cmp_k_pos, nsa_cmp_k_w1, nsa_cmp_k_w2, nsa_cmp_v_pos, nsa_cmp_v_w1, nsa_cmp_v_w2, diff_lam_q1, diff_lam_k1, diff_lam_q2, diff_lam_k2, diff_subln, w_br_mla, w_br_nsa, w_br_diff, w_out, ffn2_norm, ffn2_w_in, ffn2_w_out, final_norm):
    batch, seq, d = x.shape
    depth = w_in.shape[0]
    x2d = x.reshape(batch * seq, d)
    bf = lambda a: a.astype(BF16)
    for l in range(depth):
        p = {"mla_q_norm": mla_q_norm[l], "mla_kv_norm": mla_kv_norm[l], "mla_w_uq": mla_w_uq[l],
             "mla_w_ukv": mla_w_ukv[l], "nsa_cmp_k_pos": nsa_cmp_k_pos[l], "nsa_cmp_k_w1": nsa_cmp_k_w1[l],
             "nsa_cmp_k_w2": nsa_cmp_k_w2[l], "nsa_cmp_v_pos": nsa_cmp_v_pos[l], "nsa_cmp_v_w1": nsa_cmp_v_w1[l],
             "nsa_cmp_v_w2": nsa_cmp_v_w2[l], "diff_lam_q1": diff_lam_q1[l], "diff_lam_k1": diff_lam_k1[l],
             "diff_lam_q2": diff_lam_q2[l], "diff_lam_k2": diff_lam_k2[l], "diff_subln": diff_subln[l]}
        x2d = ffn(x2d, ffn1_norm[l], bf(ffn1_w_in[l]), bf(ffn1_w_out[l]))
        n2d = rmsnorm(x2d, mix_norm[l], BF16)
        o_mla, o_c, o_s, o_w, g_nsa, o_diff, g_merge = _mixers(n2d, batch, seq, l, w_in[l], p)
        x2d = merge(x2d, o_mla, o_c, o_s, o_w, g_nsa, o_diff, g_merge, bf(w_br_mla[l]), bf(w_br_nsa[l]),
                    bf(w_br_diff[l]).reshape(DIFF_HEADS, DIFF_VD, d), bf(w_out[l]), seq)
        x2d = ffn(x2d, ffn2_norm[l], bf(ffn2_w_in[l]), bf(ffn2_w_out[l]))
    return rmsnorm(x2d, final_norm, F32).reshape(batch, seq, d)
```

```python
import functools
import math

import numpy as np
import jax
import jax.numpy as jnp
from jax import lax
from jax.experimental import pallas as pl
from jax.experimental.pallas import tpu as pltpu

F32 = jnp.float32
BF16 = jnp.bfloat16

NORM_EPS = 1e-6
ROPE_THETA = 10000.0
NEG_INF = -1e30
REMOVED = -3e38
N_BRANCH = 3

MLA_HEADS = 6
MLA_Q_LORA = 768
MLA_KV_LORA = 512
MLA_NOPE = 128
MLA_ROPE = 64
MLA_V = 128
MLA_QK = MLA_NOPE + MLA_ROPE

NSA_HEADS = 4
NSA_DK = 192
NSA_DV = 128
NSA_CMP_LEN = 32
NSA_CMP_STRIDE = 16
NSA_CMP_HIDDEN = 256
NSA_SEL_LEN = 64
NSA_TOPK = 16
NSA_WINDOW = 512
NSA_FORCE_SCORE = 1e6

DIFF_HEADS = 4
DIFF_HD = 96
DIFF_VD = 2 * DIFF_HD

LANES = 128
ONES_PAD = 16
VMEM_LIMIT_MB = 56


def _cparams(dims, vmem_mb=VMEM_LIMIT_MB):
    return pltpu.CompilerParams(dimension_semantics=dims, vmem_limit_bytes=vmem_mb * 2**20)


def _sigmoid(x):
    return 1.0 / (1.0 + jnp.exp(-x))


def _dot(a, b):
    return jnp.dot(a, b, preferred_element_type=F32)


def _dot_nt(a, b):
    return lax.dot_general(a, b, (((1,), (1,)), ((), ())), preferred_element_type=F32)


def _tile(n, pref):
    t = min(n, pref)
    assert n % t == 0, (n, t)
    return t


def _rmsnorm_kernel(x_ref, w_ref, o_ref):
    x = x_ref[...].astype(F32)
    y = x * lax.rsqrt(jnp.mean(x * x, axis=-1, keepdims=True) + NORM_EPS)
    o_ref[...] = (y * w_ref[...]).astype(o_ref.dtype)


def rmsnorm(x2d, w, out_dtype):
    m, d = x2d.shape
    tm = _tile(m, 1024)
    return pl.pallas_call(
        _rmsnorm_kernel,
        grid=(m // tm,),
        in_specs=[pl.BlockSpec((tm, d), lambda i: (i, 0)), pl.BlockSpec((1, d), lambda i: (0, 0))],
        out_specs=pl.BlockSpec((tm, d), lambda i: (i, 0)),
        out_shape=jax.ShapeDtypeStruct((m, d), out_dtype),
        compiler_params=_cparams(("parallel",)),
        name="rmsnorm",
    )(x2d, w.reshape(1, d).astype(F32))


def _mla_norm_kernel(lat_ref, wq_ref, wkv_ref, nq_ref, nkv_ref):
    lat = lat_ref[...]
    cq = lat[:, :MLA_Q_LORA]
    ckv = lat[:, MLA_Q_LORA:]
    nq = cq * lax.rsqrt(jnp.mean(cq * cq, axis=-1, keepdims=True) + NORM_EPS)
    nkv = ckv * lax.rsqrt(jnp.mean(ckv * ckv, axis=-1, keepdims=True) + NORM_EPS)
    nq_ref[...] = (nq * wq_ref[...]).astype(BF16)
    nkv_ref[...] = (nkv * wkv_ref[...]).astype(BF16)


def mla_norm(lat, wq, wkv):
    m, d = lat.shape
    tm = _tile(m, 1024)
    return pl.pallas_call(
        _mla_norm_kernel,
        grid=(m // tm,),
        in_specs=[pl.BlockSpec((tm, d), lambda i: (i, 0)),
                  pl.BlockSpec((1, MLA_Q_LORA), lambda i: (0, 0)),
                  pl.BlockSpec((1, MLA_KV_LORA), lambda i: (0, 0))],
        out_specs=[pl.BlockSpec((tm, MLA_Q_LORA), lambda i: (i, 0)),
                   pl.BlockSpec((tm, MLA_KV_LORA), lambda i: (i, 0))],
        out_shape=[jax.ShapeDtypeStruct((m, MLA_Q_LORA), BF16), jax.ShapeDtypeStruct((m, MLA_KV_LORA), BF16)],
        compiler_params=_cparams(("parallel",)),
        name="mla_norm",
    )(lat, wq.reshape(1, -1).astype(F32), wkv.reshape(1, -1).astype(F32))


def _ffn_kernel(x_ref, nw_ref, wg_ref, wu_ref, wo_ref, o_ref, n_scr, *, n_f):
    f = pl.program_id(1)

    @pl.when(f == 0)
    def _():
        x = x_ref[...]
        y = x * lax.rsqrt(jnp.mean(x * x, axis=-1, keepdims=True) + NORM_EPS)
        n_scr[...] = (y * nw_ref[...]).astype(BF16)
        o_ref[...] = jnp.zeros_like(o_ref)

    n = n_scr[...]
    g = _dot(n, wg_ref[...])
    u = _dot(n, wu_ref[...])
    h = (g * _sigmoid(g) * u).astype(BF16)
    o_ref[...] += _dot(h, wo_ref[...])

    @pl.when(f == n_f - 1)
    def _():
        o_ref[...] = x_ref[...] + 0.5 * o_ref[...]


def ffn(x2d, norm_w, w_in, w_out):
    m, d = x2d.shape
    f_dim = w_out.shape[0]
    tm = _tile(m, 512)
    tf = _tile(f_dim, 512)
    n_f = f_dim // tf
    return pl.pallas_call(
        functools.partial(_ffn_kernel, n_f=n_f),
        grid=(m // tm, n_f),
        in_specs=[pl.BlockSpec((tm, d), lambda i, f: (i, 0)),
                  pl.BlockSpec((1, d), lambda i, f: (0, 0)),
                  pl.BlockSpec((d, tf), lambda i, f: (0, f)),
                  pl.BlockSpec((d, tf), lambda i, f: (0, f + n_f)),
                  pl.BlockSpec((tf, d), lambda i, f: (f, 0))],
        out_specs=pl.BlockSpec((tm, d), lambda i, f: (i, 0)),
        out_shape=jax.ShapeDtypeStruct((m, d), F32),
        scratch_shapes=[pltpu.VMEM((tm, d), BF16)],
        compiler_params=_cparams(("parallel", "arbitrary")),
        name="ffn",
    )(x2d, norm_w.reshape(1, d).astype(F32), w_in, w_in, w_out)


def _proj_plain_kernel(n_ref, w_ref, o_ref, *, sigmoid):
    y = _dot(n_ref[...], w_ref[...])
    if sigmoid:
        y = _sigmoid(y)
    o_ref[...] = y.astype(o_ref.dtype)


def proj_plain(n2d, w, out_dtype, sigmoid=False):
    m, k = n2d.shape
    n_out = w.shape[1]
    tm = _tile(m, 1024)
    tn = next(t for t in (512, 256, LANES) if n_out % t == 0)
    return pl.pallas_call(
        functools.partial(_proj_plain_kernel, sigmoid=sigmoid),
        grid=(m // tm, n_out // tn),
        in_specs=[pl.BlockSpec((tm, k), lambda i, j: (i, 0)),
                  pl.BlockSpec((k, tn), lambda i, j: (0, j))],
        out_specs=pl.BlockSpec((tm, tn), lambda i, j: (i, j)),
        out_shape=jax.ShapeDtypeStruct((m, n_out), out_dtype),
        compiler_params=_cparams(("parallel", "arbitrary")),
        name="proj_plain",
    )(n2d, w)


def _proj_heads_kernel(*refs, rope, has_add, has_masks, scale):
    n_ref, w_ref = refs[0], refs[1]
    pos = 2
    n = n_ref[...]
    y = _dot(n, w_ref[0])
    if rope == "weights":
        wr_ref, cos_ref, sin_ref = refs[pos:pos + 3]
        pos += 3
        y = y * cos_ref[...] + _dot(n, wr_ref[0]) * sin_ref[...]
    elif rope == "swap":
        cos_ref, sin_ref = refs[pos:pos + 2]
        pos += 2
        half = y.shape[1] // 2
        y = y * cos_ref[...] + jnp.concatenate([y[:, half:], y[:, :half]], axis=1) * sin_ref[...]
    if has_add:
        y = y + refs[pos][0, 0].astype(F32)
        pos += 1
    if scale != 1.0:
        y = y * scale
    if has_masks:
        mask_ref, o_ref = refs[pos], refs[pos + 1]
        for c in range(o_ref.shape[1]):
            o_ref[0, c] = (y * mask_ref[c:c + 1, :]).astype(o_ref.dtype)
    else:
        o_ref = refs[pos]
        o_ref[0, 0] = y.astype(o_ref.dtype)


def proj_heads(n2d, w, batch, seq, rope=None, add=None, scale=1.0, masks=None):
    m, k = n2d.shape
    g, _, dh = w.shape
    tm = _tile(seq, 1024)
    ns = seq // tm
    in_specs = [pl.BlockSpec((tm, k), lambda b, i, h: (b * ns + i, 0)),
                pl.BlockSpec((1, k, dh), lambda b, i, h: (h, 0, 0))]
    args = [n2d, w]
    table = pl.BlockSpec((tm, dh), lambda b, i, h: (i, 0))
    if rope is not None and rope[0] == "weights":
        in_specs += [pl.BlockSpec((1, k, dh), lambda b, i, h: (h, 0, 0)), table, table]
        args += list(rope[1:])
    elif rope is not None:
        assert rope[0] == "swap" and dh % (2 * LANES) == 0
        in_specs += [table, table]
        args += list(rope[1:])
    if add is not None:
        in_specs.append(pl.BlockSpec((1, 1, tm, dh), lambda b, i, h: (b, 0, i, 0)))
        args.append(add)
    n_copy = 1
    if masks is not None:
        n_copy = masks.shape[0]
        in_specs.append(pl.BlockSpec((n_copy, dh), lambda b, i, h: (0, 0)))
        args.append(masks)
    return pl.pallas_call(
        functools.partial(_proj_heads_kernel, rope=None if rope is None else rope[0], has_add=add is not None,
                          has_masks=masks is not None, scale=scale),
        grid=(batch, ns, g),
        in_specs=in_specs,
        out_specs=pl.BlockSpec((1, n_copy, tm, dh), lambda b, i, h: (b, h, i, 0)),
        out_shape=jax.ShapeDtypeStruct((batch, g * n_copy, seq, dh), BF16),
        compiler_params=_cparams(("parallel", "parallel", "arbitrary")),
        name="proj_heads",
    )(*args)


def _proj_heads_t_kernel(n_ref, wt_ref, o_ref):
    dh = wt_ref.shape[1]
    o_ref[0, 0, :dh, :] = _dot_nt(wt_ref[0], n_ref[...]).astype(o_ref.dtype)
    pad = o_ref.shape[2] - dh
    row = lax.broadcasted_iota(jnp.int32, (pad, o_ref.shape[3]), 0)
    o_ref[0, 0, dh:, :] = jnp.where(row == 0, 1.0, 0.0).astype(o_ref.dtype)


def proj_heads_t(n2d, w, batch, seq):
    m, k = n2d.shape
    g, _, dh = w.shape
    tm = _tile(seq, 1024)
    ns = seq // tm
    return pl.pallas_call(
        _proj_heads_t_kernel,
        grid=(batch, ns, g),
        in_specs=[pl.BlockSpec((tm, k), lambda b, i, h: (b * ns + i, 0)),
                  pl.BlockSpec((1, dh, k), lambda b, i, h: (h, 0, 0))],
        out_specs=pl.BlockSpec((1, 1, dh + ONES_PAD, tm), lambda b, i, h: (b, h, 0, i)),
        out_shape=jax.ShapeDtypeStruct((batch, g, dh + ONES_PAD, seq), BF16),
        compiler_params=_cparams(("parallel", "parallel", "arbitrary")),
        name="proj_heads_t",
    )(n2d, jnp.swapaxes(w, 1, 2))


def _softmax_step_t(st, vt, m_scr, acc_scr):
    m_prev = m_scr[...]
    m_new = jnp.maximum(m_prev, jnp.max(st, axis=0, keepdims=True))
    alpha = jnp.exp(m_prev - m_new)
    pt = jnp.exp((st - m_new).astype(BF16))
    acc_scr[...] = alpha * acc_scr[...] + _dot(vt, pt)
    m_scr[...] = m_new


def _diag_visibility(d, c, tq, tk):
    if d * tk >= (c + 1) * tq:
        return "none"
    if (d + 1) * tk - 1 <= c * tq:
        return "full"
    return "partial"


def _causal_where(st, d, c):
    tk, tq = st.shape
    key = lax.broadcasted_iota(jnp.int32, (tk, tq), 0) + d * tk
    qry = lax.broadcasted_iota(jnp.int32, (tk, tq), 1) + c * tq
    return jnp.where(key <= qry, st, NEG_INF)


def _flash_sweep(n, r, scores, vt_tile, m_scr, acc_scr, st_scr):
    n_chain = m_scr.shape[0]
    for c in range(n_chain):
        m_scr[c] = jnp.full(m_scr.shape[1:], NEG_INF, F32)
        acc_scr[c] = jnp.zeros(acc_scr.shape[1:], F32)

    @pl.when(n > 0)
    def _():
        for c, st in enumerate(scores(0, None)):
            st_scr[c] = st

    def body(j, carry):
        nxt = scores(jnp.minimum(j + 1, n - 1), None)
        vt = vt_tile(j)
        for c in range(n_chain):
            st = st_scr[c]
            st_scr[c] = nxt[c]
            _softmax_step_t(st, vt, m_scr.at[c], acc_scr.at[c])
        return carry

    lax.fori_loop(0, n, body, 0)
    for d in range(r):
        sts = scores(n + d, d)
        vt = vt_tile(n + d)
        for c in range(n_chain):
            if sts[c] is not None:
                _softmax_step_t(sts[c], vt, m_scr.at[c], acc_scr.at[c])


def _flash_kernel(q_ref, k_ref, vt_ref, o_ref, m_scr, acc_scr, st_scr, *, n_sub, tq, tk):
    i = pl.program_id(2)
    dv = o_ref.shape[2]
    r = n_sub * tq // tk

    def scores(j, d):
        k = k_ref[0, 0, pl.ds(pl.multiple_of(j * tk, tk), tk), :]
        out = []
        for c in range(n_sub):
            vis = "full" if d is None else _diag_visibility(d, c, tq, tk)
            if vis == "none":
                out.append(None)
                continue
            st = _dot_nt(k, q_ref[0, 0, c * tq:(c + 1) * tq, :])
            out.append(_causal_where(st, d, c) if vis == "partial" else st)
        return out

    def vt_tile(j):
        return vt_ref[0, 0, :, pl.ds(pl.multiple_of(j * tk, tk), tk)]

    _flash_sweep(i * r, r, scores, vt_tile, m_scr, acc_scr, st_scr)
    for c in range(n_sub):
        o_t = acc_scr[c, :dv, :] / acc_scr[c, dv:dv + 1, :]
        o_ref[0, c * tq:(c + 1) * tq, :] = o_t.T.astype(o_ref.dtype)


def _flash_tiles(s):
    tq = _tile(s, 512)
    n_sub = 2 if s % (2 * tq) == 0 else 1
    return n_sub, tq, tq


def flash_causal(q, k, vt):
    b, h, s, dk = q.shape
    dve = vt.shape[2]
    dv = dve - ONES_PAD
    n_sub, tq, tk = _flash_tiles(s)
    t = n_sub * tq
    return pl.pallas_call(
        functools.partial(_flash_kernel, n_sub=n_sub, tq=tq, tk=tk),
        grid=(b, h, s // t),
        in_specs=[pl.BlockSpec((1, 1, t, dk), lambda b_, h_, i: (b_, h_, i, 0)),
                  pl.BlockSpec((1, 1, s, dk), lambda b_, h_, i: (b_, h_, 0, 0)),
                  pl.BlockSpec((1, 1, dve, s), lambda b_, h_, i: (b_, h_, 0, 0))],
        out_specs=pl.BlockSpec((1, t, dv), lambda b_, h_, i: (b_, i, h_)),
        out_shape=jax.ShapeDtypeStruct((b, s, h * dv), BF16),
        scratch_shapes=[pltpu.VMEM((n_sub, 1, tq), F32), pltpu.VMEM((n_sub, dve, tq), F32),
                        pltpu.VMEM((n_sub, tk, tq), F32)],
        compiler_params=_cparams(("parallel", "parallel", "arbitrary")),
        name="mla_flash",
    )(q, k, vt)


def _diff_kernel(q_ref, k_ref, vt_ref, lq1_ref, lk1_ref, lq2_ref, lk2_ref, sub_ref, o_ref,
                 m_scr, acc_scr, st_scr, *, n_sub, tq, tk, lambda_init):
    i = pl.program_id(2)
    vd = o_ref.shape[3]
    r = n_sub * tq // tk

    def scores(j, d):
        off = pl.multiple_of(j * tk, tk)
        out = []
        k = k_ref[0, 0, pl.ds(off, tk), :]
        for comp in range(2):
            for c in range(n_sub):
                vis = "full" if d is None else _diag_visibility(d, c, tq, tk)
                if vis == "none":
                    out.append(None)
                    continue
                st = _dot_nt(k, q_ref[0, comp, c * tq:(c + 1) * tq, :])
                out.append(_causal_where(st, d, c) if vis == "partial" else st)
        return out

    def vt_tile(j):
        return vt_ref[0, 0, :, pl.ds(pl.multiple_of(j * tk, tk), tk)]

    _flash_sweep(i * r, r, scores, vt_tile, m_scr, acc_scr, st_scr)

    lam = (jnp.exp(jnp.sum(lq1_ref[...] * lk1_ref[...], axis=-1, keepdims=True))
           - jnp.exp(jnp.sum(lq2_ref[...] * lk2_ref[...], axis=-1, keepdims=True)) + lambda_init)
    lane_pad = (-vd) % LANES
    for c in range(n_sub):
        o_t = (acc_scr[c, :vd, :] / acc_scr[c, vd:vd + 1, :]
               - lam * (acc_scr[n_sub + c, :vd, :] / acc_scr[n_sub + c, vd:vd + 1, :]))
        o_t = o_t * lax.rsqrt(jnp.mean(o_t * o_t, axis=0, keepdims=True) + NORM_EPS)
        o_t = jnp.concatenate([o_t, jnp.zeros((lane_pad, tq), F32)], axis=0)
        o_ref[0, 0, c * tq:(c + 1) * tq, :] = (
            o_t.T[:, :vd] * sub_ref[...] * (1.0 - lambda_init)).astype(o_ref.dtype)


def diff_attention(q, k, vt, lq1, lk1, lq2, lk2, subln, lambda_init):
    b, h2, s, dk = q.shape
    h = h2 // 2
    hd = DIFF_HD
    vde = vt.shape[2]
    vd = vde - ONES_PAD
    n_sub, tq, tk = _flash_tiles(s)
    t = n_sub * tq
    vec = lambda a: a.reshape(1, -1).astype(F32)
    small = lambda n: pl.BlockSpec((1, n), lambda b_, h_, i: (0, 0))
    return pl.pallas_call(
        functools.partial(_diff_kernel, n_sub=n_sub, tq=tq, tk=tk, lambda_init=lambda_init),
        grid=(b, h, s // t),
        in_specs=[pl.BlockSpec((1, 2, t, dk), lambda b_, h_, i: (b_, h_, i, 0)),
                  pl.BlockSpec((1, 1, s, dk), lambda b_, h_, i: (b_, h_, 0, 0)),
                  pl.BlockSpec((1, 1, vde, s), lambda b_, h_, i: (b_, h_, 0, 0)),
                  small(hd), small(hd), small(hd), small(hd), small(vd)],
        out_specs=pl.BlockSpec((1, 1, t, vd), lambda b_, h_, i: (b_, h_, i, 0)),
        out_shape=jax.ShapeDtypeStruct((b, h, s, vd), BF16),
        scratch_shapes=[pltpu.VMEM((2 * n_sub, 1, tq), F32), pltpu.VMEM((2 * n_sub, vde, tq), F32),
                        pltpu.VMEM((2 * n_sub, tk, tq), F32)],
        compiler_params=_cparams(("parallel", "parallel", "arbitrary")),
        name="diff_flash",
    )(q, k, vt, vec(lq1), vec(lk1), vec(lq2), vec(lk2), vec(subln))


def _compress_kernel(a_ref, pa_ref, pb_ref, w1a_ref, w1b_ref, w2_ref, o_ref):
    ng = a_ref.shape[2]
    rows = min(ng, 256)
    us, vs = [], []
    for r0 in range(0, ng, rows):
        a = a_ref[0, 0, r0:r0 + rows, :].astype(F32)
        us.append(_dot((a + pa_ref[...]).astype(BF16), w1a_ref[...]))
        vs.append(_dot((a + pb_ref[...]).astype(BF16), w1b_ref[...]))
    u = jnp.concatenate(us, axis=0)
    v = jnp.concatenate(vs, axis=0)
    hdn = u + pltpu.roll(v, ng - 1, 0)
    hdn = hdn * _sigmoid(hdn)
    o_ref[0] = _dot(hdn.astype(BF16), w2_ref[...]).astype(o_ref.dtype)


def nsa_compress(tok, g_idx, pos, w1, w2):
    assert NSA_CMP_LEN == 2 * NSA_CMP_STRIDE
    b, _, ng, wd = tok.shape
    d = wd // NSA_CMP_STRIDE
    d_out = w2.shape[1]
    pos_flat = pos.astype(F32).reshape(1, NSA_CMP_LEN * d)
    w1 = w1.astype(BF16)
    full = lambda shape: pl.BlockSpec(shape, lambda b_: (0,) * len(shape))
    return pl.pallas_call(
        _compress_kernel,
        grid=(b,),
        in_specs=[pl.BlockSpec((1, 1, ng, wd), lambda b_: (b_, g_idx, 0, 0)),
                  full((1, wd)), full((1, wd)), full((wd, NSA_CMP_HIDDEN)), full((wd, NSA_CMP_HIDDEN)),
                  full((NSA_CMP_HIDDEN, d_out))],
        out_specs=pl.BlockSpec((1, ng, d_out), lambda b_: (b_, 0, 0)),
        out_shape=jax.ShapeDtypeStruct((b, ng, d_out), BF16),
        compiler_params=_cparams(("parallel",)),
        name="nsa_compress",
    )(tok, pos_flat[:, :wd], pos_flat[:, wd:], w1[:wd], w1[wd:], w2.astype(BF16))


def _split3(x):
    hi = x.astype(BF16)
    r = x - hi.astype(F32)
    mid = r.astype(BF16)
    lo = (r - mid.astype(F32)).astype(BF16)
    return hi, mid, lo


def _cmp_topk_kernel(q_ref, kc_ref, vc_ref, wsel_ref, oc_ref, sel_ref, *, tq, k_top):
    i = pl.program_id(1)
    qs = i * tq
    nc = kc_ref.shape[1]
    n_sel = wsel_ref.shape[1]
    kc = kc_ref[0]
    vc = vc_ref[0]
    qpos_c = qs + lax.broadcasted_iota(jnp.int32, (tq, nc), 0)
    cend = lax.broadcasted_iota(jnp.int32, (tq, nc), 1) * NSA_CMP_STRIDE + (NSA_CMP_LEN - 1)
    valid_c = cend <= qpos_c
    imp = jnp.zeros((tq, nc), F32)
    for h in range(NSA_HEADS):
        s = jnp.where(valid_c, _dot_nt(q_ref[0, h], kc), NEG_INF)
        e = jnp.exp(s - jnp.max(s, axis=-1, keepdims=True))
        p = jnp.where(valid_c, e / jnp.sum(e, axis=-1, keepdims=True), 0.0)
        oc_ref[0, :, h * NSA_DV:(h + 1) * NSA_DV] = _dot(p.astype(BF16), vc)
        imp = imp + p

    wsel = wsel_ref[...]
    hi, mid, lo = _split3(imp)
    imp_sel = _dot(hi, wsel) + _dot(mid, wsel) + _dot(lo, wsel)

    blk = lax.broadcasted_iota(jnp.int32, (tq, n_sel), 1)
    qpos = qs + lax.broadcasted_iota(jnp.int32, (tq, n_sel), 0)
    cur = lax.shift_right_arithmetic(qpos, int(math.log2(NSA_SEL_LEN)))
    forced = (blk == 0) | (blk == cur) | (blk == cur - 1)
    valid_s = blk * NSA_SEL_LEN <= qpos
    score = jnp.where(valid_s, jnp.where(forced, NSA_FORCE_SCORE, imp_sel), NEG_INF)
    blk_f = blk.astype(F32)
    sel = jnp.zeros((tq, n_sel), F32)
    for _ in range(k_top):
        mx = jnp.max(score, axis=-1, keepdims=True)
        first = jnp.min(jnp.where(score == mx, blk_f, float(n_sel)), axis=-1, keepdims=True)
        hit = blk_f == first
        sel = jnp.where(hit, 1.0, sel)
        score = jnp.where(hit, REMOVED, score)
    sel_ref[0] = sel.astype(sel_ref.dtype)


def _sel_weight_matrix(n_pad, n_sel):
    r_c = NSA_CMP_LEN // NSA_CMP_STRIDE
    ratio = NSA_SEL_LEN // NSA_CMP_STRIDE
    overlap_w = [max(0, min(o * NSA_CMP_STRIDE + NSA_CMP_LEN, NSA_SEL_LEN) - max(o * NSA_CMP_STRIDE, 0))
                 / NSA_CMP_STRIDE for o in range(-(r_c - 1), ratio)]
    w = np.zeros((n_pad, n_sel), np.float32)
    for n in range(n_sel):
        for u, w_u in enumerate(overlap_w):
            c = ratio * n + u - (r_c - 1)
            if 0 <= c < n_pad:
                w[c, n] = w_u
    return w


def nsa_cmp_topk(q, kc, vc):
    b, h, s, dk = q.shape
    nc = kc.shape[1]
    n_sel = s // NSA_SEL_LEN
    assert NSA_SEL_LEN & (NSA_SEL_LEN - 1) == 0
    tq = _tile(s, 256)
    wsel = jnp.asarray(_sel_weight_matrix(nc, n_sel), BF16)
    return pl.pallas_call(
        functools.partial(_cmp_topk_kernel, tq=tq, k_top=min(NSA_TOPK, n_sel)),
        grid=(b, s // tq),
        in_specs=[pl.BlockSpec((1, h, tq, dk), lambda b_, i: (b_, 0, i, 0)),
                  pl.BlockSpec((1, nc, dk), lambda b_, i: (b_, 0, 0)),
                  pl.BlockSpec((1, nc, NSA_DV), lambda b_, i: (b_, 0, 0)),
                  pl.BlockSpec((nc, n_sel), lambda b_, i: (0, 0))],
        out_specs=[pl.BlockSpec((1, tq, h * NSA_DV), lambda b_, i: (b_, i, 0)),
                   pl.BlockSpec((1, tq, n_sel), lambda b_, i: (b_, i, 0))],
        out_shape=[jax.ShapeDtypeStruct((b, s, h * NSA_DV), F32), jax.ShapeDtypeStruct((b, s, n_sel), BF16)],
        compiler_params=_cparams(("parallel", "parallel")),
        name="nsa_cmp_topk",
    )(q, kc, vc, wsel)


def _nsa_sel_kernel(q_ref, k_ref, vt_ref, sel_ref, o_ref, m_scr, acc_scr, st_scr, *, t):
    i = pl.program_id(1)
    n_sel = sel_ref.shape[2]
    bpt = t // NSA_SEL_LEN
    shift = int(math.log2(NSA_SEL_LEN))

    def scores(j, d):
        k = k_ref[0, 0, pl.ds(pl.multiple_of(j * t, t), t), :]
        blk_key = lax.shift_right_arithmetic(lax.broadcasted_iota(jnp.int32, (t, n_sel), 0), shift) + j * bpt
        blk_col = lax.broadcasted_iota(jnp.int32, (t, n_sel), 1)
        expand_t = jnp.where(blk_key == blk_col, 1.0, 0.0).astype(BF16)
        flags_t = _dot_nt(expand_t, sel_ref[0])
        if d is not None:
            flags_t = _causal_where(flags_t, 0, 0)
        keep_t = flags_t > 0.5
        return [jnp.where(keep_t, _dot_nt(k, q_ref[0, h]), NEG_INF) for h in range(NSA_HEADS)]

    def vt_tile(j):
        return vt_ref[0, 0, :, pl.ds(pl.multiple_of(j * t, t), t)]

    _flash_sweep(i, 1, scores, vt_tile, m_scr, acc_scr, st_scr)
    for h in range(NSA_HEADS):
        o_t = acc_scr[h, :NSA_DV, :] / acc_scr[h, NSA_DV:NSA_DV + 1, :]
        o_ref[0, :, h * NSA_DV:(h + 1) * NSA_DV] = o_t.T


def nsa_selected(q, k3, k_idx, vt, sel):
    b, h, s, dk = q.shape
    n_sel = sel.shape[-1]
    dve = vt.shape[2]
    t = _tile(s, 512)
    assert t % NSA_SEL_LEN == 0
    return pl.pallas_call(
        functools.partial(_nsa_sel_kernel, t=t),
        grid=(b, s // t),
        in_specs=[pl.BlockSpec((1, h, t, dk), lambda b_, i: (b_, 0, i, 0)),
                  pl.BlockSpec((1, 1, s, dk), lambda b_, i: (b_, k_idx, 0, 0)),
                  pl.BlockSpec((1, 1, dve, s), lambda b_, i: (b_, 0, 0, 0)),
                  pl.BlockSpec((1, t, n_sel), lambda b_, i: (b_, i, 0))],
        out_specs=pl.BlockSpec((1, t, h * NSA_DV), lambda b_, i: (b_, i, 0)),
        out_shape=jax.ShapeDtypeStruct((b, s, h * NSA_DV), F32),
        scratch_shapes=[pltpu.VMEM((h, 1, t), F32), pltpu.VMEM((h, dve, t), F32), pltpu.VMEM((h, t, t), F32)],
        compiler_params=_cparams(("parallel", "arbitrary")),
        name="nsa_selected",
    )(q, k3, vt, sel)


def _nsa_win_kernel(q_ref, kp_ref, kc_ref, vp_ref, vc_ref, o_ref, *, t):
    i = pl.program_id(1)
    row = lax.broadcasted_iota(jnp.int32, (t, t), 0)
    col = lax.broadcasted_iota(jnp.int32, (t, t), 1)
    keep_prev = (col > row) & (i > 0)
    keep_cur = col <= row
    kp, kc, vp, vc = kp_ref[0, 0], kc_ref[0, 0], vp_ref[0], vc_ref[0]
    for h in range(NSA_HEADS):
        q = q_ref[0, h]
        sp = jnp.where(keep_prev, _dot_nt(q, kp), NEG_INF)
        sc = jnp.where(keep_cur, _dot_nt(q, kc), NEG_INF)
        m = jnp.maximum(jnp.max(sp, axis=-1, keepdims=True), jnp.max(sc, axis=-1, keepdims=True))
        pp = jnp.exp(sp - m)
        pc = jnp.exp(sc - m)
        l = jnp.sum(pp, axis=-1, keepdims=True) + jnp.sum(pc, axis=-1, keepdims=True)
        o = _dot(pp.astype(BF16), vp) + _dot(pc.astype(BF16), vc)
        o_ref[0, :, h * NSA_DV:(h + 1) * NSA_DV] = o / l


def nsa_window(q, k3, k_idx, v3, v_idx):
    b, h, s, dk = q.shape
    t = _tile(s, NSA_WINDOW)
    assert t == NSA_WINDOW, "window kernel needs the query tile to equal the window"
    prev = lambda i: jnp.maximum(i - 1, 0)
    return pl.pallas_call(
        functools.partial(_nsa_win_kernel, t=t),
        grid=(b, s // t),
        in_specs=[pl.BlockSpec((1, h, t, dk), lambda b_, i: (b_, 0, i, 0)),
                  pl.BlockSpec((1, 1, t, dk), lambda b_, i: (b_, k_idx, prev(i), 0)),
                  pl.BlockSpec((1, 1, t, dk), lambda b_, i: (b_, k_idx, i, 0)),
                  pl.BlockSpec((1, t, NSA_DV), lambda b_, i: (b_, prev(i), v_idx)),
                  pl.BlockSpec((1, t, NSA_DV), lambda b_, i: (b_, i, v_idx))],
        out_specs=pl.BlockSpec((1, t, h * NSA_DV), lambda b_, i: (b_, i, 0)),
        out_shape=jax.ShapeDtypeStruct((b, s, h * NSA_DV), F32),
        compiler_params=_cparams(("parallel", "parallel")),
        name="nsa_window",
    )(q, k3, k3, v3, v3)


def _merge_kernel(x_ref, om_ref, oc_ref, os_ref, ow_ref, gn_ref, od_ref, gm_ref, gs_ref, gd_ref,
                  wm_ref, wn_ref, wd_ref, wo_ref, ex_ref, o_ref, onsa_scr, *, n_j):
    j = pl.program_id(1)
    nsa_out = NSA_HEADS * NSA_DV

    @pl.when(j == 0)
    def _():
        g = gn_ref[...]
        hi = g.astype(BF16)
        lo = (g - hi.astype(F32)).astype(BF16)
        ge = _dot(hi, ex_ref[...]) + _dot(lo, ex_ref[...])
        onsa = (ge[:, :nsa_out] * oc_ref[...] + ge[:, nsa_out:2 * nsa_out] * os_ref[...]
                + ge[:, 2 * nsa_out:] * ow_ref[...])
        onsa_scr[...] = onsa.astype(BF16)
        o_ref[...] = jnp.zeros_like(o_ref)

    ym = _dot(om_ref[...], wm_ref[...])
    yn = _dot(onsa_scr[...], wn_ref[...])
    yd = _dot(od_ref[0, 0], wd_ref[0])
    for h in range(1, DIFF_HEADS):
        yd = yd + _dot(od_ref[0, h], wd_ref[h])
    mixed = (gm_ref[...].astype(F32) * ym + gs_ref[...].astype(F32) * yn + gd_ref[...].astype(F32) * yd)
    o_ref[...] += _dot(mixed.astype(BF16), wo_ref[...])

    @pl.when(j == n_j - 1)
    def _():
        o_ref[...] = x_ref[...] + o_ref[...]


def _gate_expand_matrix():
    nsa_out = NSA_HEADS * NSA_DV
    e = np.zeros((LANES, 3 * nsa_out), np.float32)
    for h in range(NSA_HEADS):
        for c in range(3):
            e[h * 3 + c, c * nsa_out + h * NSA_DV:c * nsa_out + (h + 1) * NSA_DV] = 1.0
    return e


def merge(x2d, o_mla, o_c, o_s, o_w, g_nsa, o_diff, g_merge, w_br_mla, w_br_nsa, w_br_diff, w_out, seq):
    m, d = x2d.shape
    tm = _tile(seq, 512)
    ns = seq // tm
    tn = _tile(d, 512)
    n_j = d // tn
    nsa_out = NSA_HEADS * NSA_DV
    mla_out = o_mla.shape[-1]
    ex = jnp.asarray(_gate_expand_matrix(), BF16)
    row = lambda w: pl.BlockSpec((tm, w), lambda i, j: (i, 0))
    return pl.pallas_call(
        functools.partial(_merge_kernel, n_j=n_j),
        grid=(m // tm, n_j),
        in_specs=[row(d), row(mla_out), row(nsa_out), row(nsa_out), row(nsa_out), row(LANES),
                  pl.BlockSpec((1, DIFF_HEADS, tm, DIFF_VD), lambda i, j: (i // ns, 0, i % ns, 0)),
                  pl.BlockSpec((tm, tn), lambda i, j: (i, j)),
                  pl.BlockSpec((tm, tn), lambda i, j: (i, n_j + j)),
                  pl.BlockSpec((tm, tn), lambda i, j: (i, 2 * n_j + j)),
                  pl.BlockSpec((mla_out, tn), lambda i, j: (0, j)),
                  pl.BlockSpec((nsa_out, tn), lambda i, j: (0, j)),
                  pl.BlockSpec((DIFF_HEADS, DIFF_VD, tn), lambda i, j: (0, 0, j)),
                  pl.BlockSpec((tn, d), lambda i, j: (j, 0)),
                  pl.BlockSpec((LANES, 3 * nsa_out), lambda i, j: (0, 0))],
        out_specs=pl.BlockSpec((tm, d), lambda i, j: (i, 0)),
        out_shape=jax.ShapeDtypeStruct((m, d), F32),
        scratch_shapes=[pltpu.VMEM((tm, nsa_out), BF16)],
        compiler_params=_cparams(("parallel", "arbitrary")),
        name="merge",
    )(x2d, o_mla, o_c, o_s, o_w, g_nsa, o_diff, g_merge, g_merge, g_merge,
      w_br_mla, w_br_nsa, w_br_diff, w_out, ex)


def _rope_tables(dim, seq, lead=0):
    inv = ROPE_THETA ** (-jnp.arange(0, dim, 2, dtype=F32) / dim)
    ang = jnp.arange(seq, dtype=F32)[:, None] * inv[None, :]
    cos, sin = jnp.cos(ang), jnp.sin(ang)
    cos = jnp.concatenate([jnp.ones((seq, lead), F32), cos, cos], axis=-1)
    sin = jnp.concatenate([jnp.zeros((seq, lead), F32), sin, sin], axis=-1)
    return cos, sin


def _heads(w, g, dh):
    return w.reshape(w.shape[0], g, dh).transpose(1, 0, 2)


def _rot_cols(w, lead=0):
    half = (w.shape[-1] - lead) // 2
    x1 = w[..., lead:lead + half]
    x2 = w[..., lead + half:]
    return jnp.concatenate([jnp.zeros_like(w[..., :lead]), -x2, x1], axis=-1)


def _pad_halves(w, axis=-1):
    w = jnp.moveaxis(w, axis, -1)
    half = w.shape[-1] // 2
    z = jnp.zeros(w.shape[:-1] + ((-half) % LANES,), w.dtype)
    out = jnp.concatenate([w[..., :half], z, w[..., half:], z], axis=-1)
    return jnp.moveaxis(out, -1, axis)


def _swap_tables(dim, seq, copies=1):
    inv = ROPE_THETA ** (-jnp.arange(0, dim, 2, dtype=F32) / dim)
    ang = jnp.arange(seq, dtype=F32)[:, None] * inv[None, :]
    cos = jnp.tile(jnp.cos(ang), (1, copies))
    sin = jnp.tile(jnp.sin(ang), (1, copies))
    return _pad_halves(jnp.concatenate([cos, cos], axis=-1)), _pad_halves(jnp.concatenate([-sin, sin], axis=-1))


def _pair_heads(w):
    k = w.shape[0]
    half = DIFF_HD // 2
    w = w.reshape(k, DIFF_HEADS, 2, 2, half)
    w = w.transpose(1, 0, 3, 2, 4).reshape(DIFF_HEADS, k, 2 * DIFF_HD)
    return _pad_halves(w)


def _pair_masks():
    half = DIFF_HD // 2
    m = np.zeros((2, 2, 2, half), np.float32)
    for c in range(2):
        m[c, :, c, :] = 1.0
    return _pad_halves(jnp.asarray(m.reshape(2, 2 * DIFF_HD)))


def _col_offsets():
    sizes = (MLA_Q_LORA, MLA_KV_LORA, MLA_ROPE,
             NSA_HEADS * NSA_DK, NSA_DK, NSA_DV, NSA_DK, NSA_DV, NSA_DK, NSA_DV, NSA_HEADS * 3,
             DIFF_HEADS * 2 * DIFF_HD, DIFF_HEADS * 2 * DIFF_HD, DIFF_HEADS * DIFF_VD)
    names = ("c_q", "c_kv", "k_rope", "nsa_q", "nsa_kc", "nsa_vc", "nsa_ks", "nsa_vs", "nsa_kw", "nsa_vw",
             "nsa_g", "d_q", "d_k", "d_v")
    offs = {}
    o = 0
    for nme, sz in zip(names, sizes):
        offs[nme] = (o, o + sz)
        o += sz
    offs["merge"] = (o, None)
    return offs


def _mixers(n2d, batch, seq, layer, w_in, p):
    offs = _col_offsets()
    col = lambda name: w_in[:, offs[name][0]:offs[name][1]]
    bf = lambda a: a.astype(BF16)

    lat = proj_plain(n2d, bf(jnp.concatenate([col("c_q"), col("c_kv")], axis=1)), F32)
    n_q, n_kv = mla_norm(lat, p["mla_q_norm"], p["mla_kv_norm"])
    cos_m, sin_m = _rope_tables(MLA_ROPE, seq, lead=MLA_NOPE)
    w_kr = jnp.concatenate([jnp.zeros((w_in.shape[0], MLA_NOPE), F32), col("k_rope")], axis=1)[None]
    kpe = proj_heads(n2d, bf(w_kr), batch, seq, rope=("weights", bf(_rot_cols(w_kr, MLA_NOPE)), cos_m, sin_m))
    w_uq = _heads(p["mla_w_uq"], MLA_HEADS, MLA_QK)
    q_mla = proj_heads(n_q, bf(w_uq), batch, seq, rope=("weights", bf(_rot_cols(w_uq, MLA_NOPE)), cos_m, sin_m),
                       scale=MLA_QK ** -0.5)
    w_ukv = _heads(p["mla_w_ukv"], MLA_HEADS, MLA_NOPE + MLA_V)
    w_uk = jnp.concatenate([w_ukv[..., :MLA_NOPE], jnp.zeros(w_ukv.shape[:2] + (MLA_ROPE,), F32)], axis=-1)
    k_mla = proj_heads(n_kv, bf(w_uk), batch, seq, add=kpe)
    vt_mla = proj_heads_t(n_kv, bf(w_ukv[..., MLA_NOPE:]), batch, seq)
    o_mla = flash_causal(q_mla, k_mla, vt_mla)

    cos_d, sin_d = _swap_tables(DIFF_HD, seq, copies=2)
    q_d = proj_heads(n2d, bf(_pair_heads(col("d_q"))), batch, seq, rope=("swap", cos_d, sin_d),
                     scale=DIFF_HD ** -0.5, masks=_pair_masks())
    k_d = proj_heads(n2d, bf(_pair_heads(col("d_k"))), batch, seq, rope=("swap", cos_d, sin_d))
    vt_d = proj_heads_t(n2d, bf(_heads(col("d_v"), DIFF_HEADS, DIFF_VD)), batch, seq)
    lambda_init = 0.8 - 0.6 * math.exp(-0.3 * layer)
    o_diff = diff_attention(q_d, k_d, vt_d, p["diff_lam_q1"], p["diff_lam_k1"], p["diff_lam_q2"],
                            p["diff_lam_k2"], p["diff_subln"], lambda_init)

    cos_n, sin_n = _swap_tables(NSA_DK, seq)
    w_nq = _pad_halves(_heads(col("nsa_q"), NSA_HEADS, NSA_DK))
    q_n = proj_heads(n2d, bf(w_nq), batch, seq, rope=("swap", cos_n, sin_n), scale=NSA_DK ** -0.5)
    w_nk = _pad_halves(jnp.stack([col("nsa_kc"), col("nsa_ks"), col("nsa_kw")], axis=0))
    k_n = proj_heads(n2d, bf(w_nk), batch, seq, rope=("swap", cos_n, sin_n))
    dk_pad = w_nk.shape[-1]
    v_n = proj_plain(n2d, bf(jnp.concatenate([col("nsa_vc"), col("nsa_vw")], axis=1)), BF16)
    v_n = v_n.reshape(batch, seq, 2 * NSA_DV)
    vt_slc = proj_heads_t(n2d, bf(col("nsa_vs"))[None], batch, seq)
    w_g = jnp.concatenate([col("nsa_g"), jnp.zeros((w_in.shape[0], LANES - NSA_HEADS * 3), F32)], axis=1)
    g_nsa = proj_plain(n2d, bf(w_g), F32, sigmoid=True)

    ng = seq // NSA_CMP_STRIDE
    w1k = _pad_halves(p["nsa_cmp_k_w1"].reshape(NSA_CMP_LEN, NSA_DK, NSA_CMP_HIDDEN), axis=1)
    kc = nsa_compress(k_n.reshape(batch, 3, ng, NSA_CMP_STRIDE * dk_pad), 0,
                      _pad_halves(p["nsa_cmp_k_pos"]), w1k.reshape(NSA_CMP_LEN * dk_pad, NSA_CMP_HIDDEN),
                      _pad_halves(p["nsa_cmp_k_w2"]))
    vc_tok = v_n[:, :, :NSA_DV].reshape(batch, 1, ng, NSA_CMP_STRIDE * NSA_DV)
    vc = nsa_compress(vc_tok, 0, p["nsa_cmp_v_pos"], p["nsa_cmp_v_w1"], p["nsa_cmp_v_w2"])
    o_c, sel = nsa_cmp_topk(q_n, kc, vc)
    o_s = nsa_selected(q_n, k_n, 1, vt_slc, sel)
    o_w = nsa_window(q_n, k_n, 2, v_n, 1)

    g_merge = proj_plain(n2d, bf(w_in[:, offs["merge"][0]:]), BF16, sigmoid=True)
    m = batch * seq
    return (o_mla.reshape(m, -1), o_c.reshape(m, -1), o_s.reshape(m, -1), o_w.reshape(m, -1), g_nsa,
            o_diff, g_merge)


def kernel(x, ffn1_norm, ffn1_w_in, ffn1_w_out, mix_norm, w_in, mla_q_norm, mla_kv_norm, mla_w_uq, mla_w_ukv, nsa_cmp_k_pos, nsa_cmp_k_w1, nsa_cmp_k_w2, nsa_cmp_v_pos, nsa_cmp_v_w1, nsa_cmp_v_w2, diff_lam_q1, diff_lam_k1, diff_lam_q2, diff_lam_k2, diff_subln, w_br_mla, w_br_nsa, w_br_diff, w_out, ffn2_norm, ffn2_w_in, ffn2_w_out, final_norm):
    batch, seq, d = x.shape
    depth = w_in.shape[0]
    x2d = x.reshape(batch * seq, d)
    bf = lambda a: a.astype(BF16)
    for l in range(depth):
        p = {"mla_q_norm": mla_q_norm[l], "mla_kv_norm": mla_kv_norm[l], "mla_w_uq": mla_w_uq[l],
             "mla_w_ukv": mla_w_ukv[l], "nsa_cmp_k_pos": nsa_cmp_k_pos[l], "nsa_cmp_k_w1": nsa_cmp_k_w1[l],
             "nsa_cmp_k_w2": nsa_cmp_k_w2[l], "nsa_cmp_v_pos": nsa_cmp_v_pos[l], "nsa_cmp_v_w1": nsa_cmp_v_w1[l],
             "nsa_cmp_v_w2": nsa_cmp_v_w2[l], "diff_lam_q1": diff_lam_q1[l], "diff_lam_k1": diff_lam_k1[l],
             "diff_lam_q2": diff_lam_q2[l], "diff_lam_k2": diff_lam_k2[l], "diff_subln": diff_subln[l]}
        x2d = ffn(x2d, ffn1_norm[l], bf(ffn1_w_in[l]), bf(ffn1_w_out[l]))
        n2d = rmsnorm(x2d, mix_norm[l], BF16)
        o_mla, o_c, o_s, o_w, g_nsa, o_diff, g_merge = _mixers(n2d, batch, seq, l, w_in[l], p)
        x2d = merge(x2d, o_mla, o_c, o_s, o_w, g_nsa, o_diff, g_merge, bf(w_br_mla[l]), bf(w_br_nsa[l]),
                    bf(w_br_diff[l]).reshape(DIFF_HEADS, DIFF_VD, d), bf(w_out[l]), seq)
        x2d = ffn(x2d, ffn2_norm[l], bf(ffn2_w_in[l]), bf(ffn2_w_out[l]))
    return rmsnorm(x2d, final_norm, F32).reshape(batch, seq, d)
```

```python
import functools
import math

import numpy as np
import jax
import jax.numpy as jnp
from jax import lax
from jax.experimental import pallas as pl
from jax.experimental.pallas import tpu as pltpu

F32 = jnp.float32
BF16 = jnp.bfloat16

NORM_EPS = 1e-6
ROPE_THETA = 10000.0
NEG_INF = -1e30
REMOVED = -3e38
N_BRANCH = 3

MLA_HEADS = 6
MLA_Q_LORA = 768
MLA_KV_LORA = 512
MLA_NOPE = 128
MLA_ROPE = 64
MLA_V = 128
MLA_QK = MLA_NOPE + MLA_ROPE

NSA_HEADS = 4
NSA_DK = 192
NSA_DV = 128
NSA_CMP_LEN = 32
NSA_CMP_STRIDE = 16
NSA_CMP_HIDDEN = 256
NSA_SEL_LEN = 64
NSA_TOPK = 16
NSA_WINDOW = 512
NSA_FORCE_SCORE = 1e6

DIFF_HEADS = 4
DIFF_HD = 96
DIFF_VD = 2 * DIFF_HD

LANES = 128
ONES_PAD = 16
VMEM_LIMIT_MB = 56


def _cparams(dims, vmem_mb=VMEM_LIMIT_MB):
    return pltpu.CompilerParams(dimension_semantics=dims, vmem_limit_bytes=vmem_mb * 2**20)


def _sigmoid(x):
    return 1.0 / (1.0 + jnp.exp(-x))


def _dot(a, b):
    return jnp.dot(a, b, preferred_element_type=F32)


def _dot_nt(a, b):
    return lax.dot_general(a, b, (((1,), (1,)), ((), ())), preferred_element_type=F32)


def _tile(n, pref):
    t = min(n, pref)
    assert n % t == 0, (n, t)
    return t


def _rmsnorm_kernel(x_ref, w_ref, o_ref):
    x = x_ref[...].astype(F32)
    y = x * lax.rsqrt(jnp.mean(x * x, axis=-1, keepdims=True) + NORM_EPS)
    o_ref[...] = (y * w_ref[...]).astype(o_ref.dtype)


def rmsnorm(x2d, w, out_dtype):
    m, d = x2d.shape
    tm = _tile(m, 1024)
    return pl.pallas_call(
        _rmsnorm_kernel,
        grid=(m // tm,),
        in_specs=[pl.BlockSpec((tm, d), lambda i: (i, 0)), pl.BlockSpec((1, d), lambda i: (0, 0))],
        out_specs=pl.BlockSpec((tm, d), lambda i: (i, 0)),
        out_shape=jax.ShapeDtypeStruct((m, d), out_dtype),
        compiler_params=_cparams(("parallel",)),
        name="rmsnorm",
    )(x2d, w.reshape(1, d).astype(F32))


def _mla_norm_kernel(lat_ref, wq_ref, wkv_ref, nq_ref, nkv_ref):
    lat = lat_ref[...]
    cq = lat[:, :MLA_Q_LORA]
    ckv = lat[:, MLA_Q_LORA:]
    nq = cq * lax.rsqrt(jnp.mean(cq * cq, axis=-1, keepdims=True) + NORM_EPS)
    nkv = ckv * lax.rsqrt(jnp.mean(ckv * ckv, axis=-1, keepdims=True) + NORM_EPS)
    nq_ref[...] = (nq * wq_ref[...]).astype(BF16)
    nkv_ref[...] = (nkv * wkv_ref[...]).astype(BF16)


def mla_norm(lat, wq, wkv):
    m, d = lat.shape
    tm = _tile(m, 1024)
    return pl.pallas_call(
        _mla_norm_kernel,
        grid=(m // tm,),
        in_specs=[pl.BlockSpec((tm, d), lambda i: (i, 0)),
                  pl.BlockSpec((1, MLA_Q_LORA), lambda i: (0, 0)),
                  pl.BlockSpec((1, MLA_KV_LORA), lambda i: (0, 0))],
        out_specs=[pl.BlockSpec((tm, MLA_Q_LORA), lambda i: (i, 0)),
                   pl.BlockSpec((tm, MLA_KV_LORA), lambda i: (i, 0))],
        out_shape=[jax.ShapeDtypeStruct((m, MLA_Q_LORA), BF16), jax.ShapeDtypeStruct((m, MLA_KV_LORA), BF16)],
        compiler_params=_cparams(("parallel",)),
        name="mla_norm",
    )(lat, wq.reshape(1, -1).astype(F32), wkv.reshape(1, -1).astype(F32))


def _ffn_kernel(x_ref, nw_ref, wg_ref, wu_ref, wo_ref, o_ref, n_scr, *, n_f):
    f = pl.program_id(1)

    @pl.when(f == 0)
    def _():
        x = x_ref[...]
        y = x * lax.rsqrt(jnp.mean(x * x, axis=-1, keepdims=True) + NORM_EPS)
        n_scr[...] = (y * nw_ref[...]).astype(BF16)
        o_ref[...] = jnp.zeros_like(o_ref)

    n = n_scr[...]
    g = _dot(n, wg_ref[...])
    u = _dot(n, wu_ref[...])
    h = (g * _sigmoid(g) * u).astype(BF16)
    o_ref[...] += _dot(h, wo_ref[...])

    @pl.when(f == n_f - 1)
    def _():
        o_ref[...] = x_ref[...] + 0.5 * o_ref[...]


def ffn(x2d, norm_w, w_in, w_out):
    m, d = x2d.shape
    f_dim = w_out.shape[0]
    tm = _tile(m, 512)
    tf = _tile(f_dim, 512)
    n_f = f_dim // tf
    return pl.pallas_call(
        functools.partial(_ffn_kernel, n_f=n_f),
        grid=(m // tm, n_f),
        in_specs=[pl.BlockSpec((tm, d), lambda i, f: (i, 0)),
                  pl.BlockSpec((1, d), lambda i, f: (0, 0)),
                  pl.BlockSpec((d, tf), lambda i, f: (0, f)),
                  pl.BlockSpec((d, tf), lambda i, f: (0, f + n_f)),
                  pl.BlockSpec((tf, d), lambda i, f: (f, 0))],
        out_specs=pl.BlockSpec((tm, d), lambda i, f: (i, 0)),
        out_shape=jax.ShapeDtypeStruct((m, d), F32),
        scratch_shapes=[pltpu.VMEM((tm, d), BF16)],
        compiler_params=_cparams(("parallel", "arbitrary")),
        name="ffn",
    )(x2d, norm_w.reshape(1, d).astype(F32), w_in, w_in, w_out)


def _proj_plain_kernel(n_ref, w_ref, o_ref, *, sigmoid):
    y = _dot(n_ref[...], w_ref[...])
    if sigmoid:
        y = _sigmoid(y)
    o_ref[...] = y.astype(o_ref.dtype)


def proj_plain(n2d, w, out_dtype, sigmoid=False):
    m, k = n2d.shape
    n_out = w.shape[1]
    tm = _tile(m, 1024)
    tn = next(t for t in (512, 256, LANES) if n_out % t == 0)
    return pl.pallas_call(
        functools.partial(_proj_plain_kernel, sigmoid=sigmoid),
        grid=(m // tm, n_out // tn),
        in_specs=[pl.BlockSpec((tm, k), lambda i, j: (i, 0)),
                  pl.BlockSpec((k, tn), lambda i, j: (0, j))],
        out_specs=pl.BlockSpec((tm, tn), lambda i, j: (i, j)),
        out_shape=jax.ShapeDtypeStruct((m, n_out), out_dtype),
        compiler_params=_cparams(("parallel", "arbitrary")),
        name="proj_plain",
    )(n2d, w)


def _proj_heads_kernel(*refs, rope, has_add, has_masks, scale):
    n_ref, w_ref = refs[0], refs[1]
    o_ref = refs[-1]
    n = n_ref[...]
    n_copy = o_ref.shape[1] // w_ref.shape[0]
    for h in range(w_ref.shape[0]):
        pos = 2
        y = _dot(n, w_ref[h])
        if rope == "weights":
            wr_ref, cos_ref, sin_ref = refs[pos:pos + 3]
            pos += 3
            y = y * cos_ref[...] + _dot(n, wr_ref[h]) * sin_ref[...]
        elif rope == "swap":
            cos_ref, sin_ref = refs[pos:pos + 2]
            pos += 2
            half = y.shape[1] // 2
            y = y * cos_ref[...] + jnp.concatenate([y[:, half:], y[:, :half]], axis=1) * sin_ref[...]
        if has_add:
            y = y + refs[pos][0, 0].astype(F32)
            pos += 1
        if scale != 1.0:
            y = y * scale
        if has_masks:
            mask_ref = refs[pos]
            for c in range(n_copy):
                o_ref[0, h * n_copy + c] = (y * mask_ref[c:c + 1, :]).astype(o_ref.dtype)
        else:
            o_ref[0, h] = y.astype(o_ref.dtype)


def proj_heads(n2d, w, batch, seq, rope=None, add=None, scale=1.0, masks=None):
    m, k = n2d.shape
    g, _, dh = w.shape
    tm = _tile(seq, 1024)
    ns = seq // tm
    weights = pl.BlockSpec((g, k, dh), lambda b, i: (0, 0, 0))
    in_specs = [pl.BlockSpec((tm, k), lambda b, i: (b * ns + i, 0)), weights]
    args = [n2d, w]
    table = pl.BlockSpec((tm, dh), lambda b, i: (i, 0))
    if rope is not None and rope[0] == "weights":
        in_specs += [weights, table, table]
        args += list(rope[1:])
    elif rope is not None:
        assert rope[0] == "swap" and dh % (2 * LANES) == 0
        in_specs += [table, table]
        args += list(rope[1:])
    if add is not None:
        in_specs.append(pl.BlockSpec((1, 1, tm, dh), lambda b, i: (b, 0, i, 0)))
        args.append(add)
    n_copy = 1
    if masks is not None:
        n_copy = masks.shape[0]
        in_specs.append(pl.BlockSpec((n_copy, dh), lambda b, i: (0, 0)))
        args.append(masks)
    return pl.pallas_call(
        functools.partial(_proj_heads_kernel, rope=None if rope is None else rope[0], has_add=add is not None,
                          has_masks=masks is not None, scale=scale),
        grid=(batch, ns),
        in_specs=in_specs,
        out_specs=pl.BlockSpec((1, g * n_copy, tm, dh), lambda b, i: (b, 0, i, 0)),
        out_shape=jax.ShapeDtypeStruct((batch, g * n_copy, seq, dh), BF16),
        compiler_params=_cparams(("parallel", "parallel")),
        name="proj_heads",
    )(*args)


def _proj_heads_t_kernel(n_ref, wt_ref, o_ref):
    dh = wt_ref.shape[1]
    pad = o_ref.shape[2] - dh
    row = lax.broadcasted_iota(jnp.int32, (pad, o_ref.shape[3]), 0)
    ones_rows = jnp.where(row == 0, 1.0, 0.0).astype(o_ref.dtype)
    n = n_ref[...]
    for h in range(wt_ref.shape[0]):
        o_ref[0, h, :dh, :] = _dot_nt(wt_ref[h], n).astype(o_ref.dtype)
        o_ref[0, h, dh:, :] = ones_rows


def proj_heads_t(n2d, w, batch, seq):
    m, k = n2d.shape
    g, _, dh = w.shape
    tm = _tile(seq, 1024)
    ns = seq // tm
    return pl.pallas_call(
        _proj_heads_t_kernel,
        grid=(batch, ns),
        in_specs=[pl.BlockSpec((tm, k), lambda b, i: (b * ns + i, 0)),
                  pl.BlockSpec((g, dh, k), lambda b, i: (0, 0, 0))],
        out_specs=pl.BlockSpec((1, g, dh + ONES_PAD, tm), lambda b, i: (b, 0, 0, i)),
        out_shape=jax.ShapeDtypeStruct((batch, g, dh + ONES_PAD, seq), BF16),
        compiler_params=_cparams(("parallel", "parallel")),
        name="proj_heads_t",
    )(n2d, jnp.swapaxes(w, 1, 2))


def _softmax_step_t(st, vt, m_scr, acc_scr, exp_dtype):
    m_prev = m_scr[...]
    m_new = jnp.maximum(m_prev, jnp.max(st, axis=0, keepdims=True))
    alpha = jnp.exp(m_prev - m_new)
    pt = jnp.exp((st - m_new).astype(exp_dtype)).astype(BF16)
    acc_scr[...] = alpha * acc_scr[...] + _dot(vt, pt)
    m_scr[...] = m_new


def _diag_visibility(d, c, tq, tk):
    if d * tk >= (c + 1) * tq:
        return "none"
    if (d + 1) * tk - 1 <= c * tq:
        return "full"
    return "partial"


def _causal_where(st, d, c):
    tk, tq = st.shape
    key = lax.broadcasted_iota(jnp.int32, (tk, tq), 0) + d * tk
    qry = lax.broadcasted_iota(jnp.int32, (tk, tq), 1) + c * tq
    return jnp.where(key <= qry, st, NEG_INF)


def _flash_sweep(n, r, scores, vt_tile, m_scr, acc_scr, st_scr, exp_dtype=F32):
    n_chain = m_scr.shape[0]
    for c in range(n_chain):
        m_scr[c] = jnp.full(m_scr.shape[1:], NEG_INF, F32)
        acc_scr[c] = jnp.zeros(acc_scr.shape[1:], F32)

    @pl.when(n > 0)
    def _():
        for c, st in enumerate(scores(0, None)):
            st_scr[c] = st

    def body(j, carry):
        nxt = scores(jnp.minimum(j + 1, n - 1), None)
        vt = vt_tile(j)
        for c in range(n_chain):
            st = st_scr[c]
            st_scr[c] = nxt[c]
            _softmax_step_t(st, vt, m_scr.at[c], acc_scr.at[c], exp_dtype)
        return carry

    lax.fori_loop(0, n, body, 0)
    for d in range(r):
        sts = scores(n + d, d)
        vt = vt_tile(n + d)
        for c in range(n_chain):
            if sts[c] is not None:
                _softmax_step_t(sts[c], vt, m_scr.at[c], acc_scr.at[c], exp_dtype)


def _flash_kernel(q_ref, k_ref, vt_ref, o_ref, m_scr, acc_scr, st_scr, *, n_sub, tq, tk):
    i = pl.program_id(2)
    dv = o_ref.shape[2]
    r = n_sub * tq // tk

    def scores(j, d):
        k = k_ref[0, 0, pl.ds(pl.multiple_of(j * tk, tk), tk), :]
        out = []
        for c in range(n_sub):
            vis = "full" if d is None else _diag_visibility(d, c, tq, tk)
            if vis == "none":
                out.append(None)
                continue
            st = _dot_nt(k, q_ref[0, 0, c * tq:(c + 1) * tq, :])
            out.append(_causal_where(st, d, c) if vis == "partial" else st)
        return out

    def vt_tile(j):
        return vt_ref[0, 0, :, pl.ds(pl.multiple_of(j * tk, tk), tk)]

    _flash_sweep(i * r, r, scores, vt_tile, m_scr, acc_scr, st_scr)
    for c in range(n_sub):
        o_t = acc_scr[c, :dv, :] / acc_scr[c, dv:dv + 1, :]
        o_ref[0, c * tq:(c + 1) * tq, :] = o_t.T.astype(o_ref.dtype)


def _flash_tiles(s):
    tq = _tile(s, 512)
    n_sub = 2 if s % (2 * tq) == 0 else 1
    return n_sub, tq, tq


def flash_causal(q, k, vt):
    b, h, s, dk = q.shape
    dve = vt.shape[2]
    dv = dve - ONES_PAD
    n_sub, tq, tk = _flash_tiles(s)
    t = n_sub * tq
    return pl.pallas_call(
        functools.partial(_flash_kernel, n_sub=n_sub, tq=tq, tk=tk),
        grid=(b, h, s // t),
        in_specs=[pl.BlockSpec((1, 1, t, dk), lambda b_, h_, i: (b_, h_, i, 0)),
                  pl.BlockSpec((1, 1, s, dk), lambda b_, h_, i: (b_, h_, 0, 0)),
                  pl.BlockSpec((1, 1, dve, s), lambda b_, h_, i: (b_, h_, 0, 0))],
        out_specs=pl.BlockSpec((1, t, dv), lambda b_, h_, i: (b_, i, h_)),
        out_shape=jax.ShapeDtypeStruct((b, s, h * dv), BF16),
        scratch_shapes=[pltpu.VMEM((n_sub, 1, tq), F32), pltpu.VMEM((n_sub, dve, tq), F32),
                        pltpu.VMEM((n_sub, tk, tq), F32)],
        compiler_params=_cparams(("parallel", "parallel", "arbitrary")),
        name="mla_flash",
    )(q, k, vt)


def _diff_kernel(q_ref, k_ref, vt_ref, lq1_ref, lk1_ref, lq2_ref, lk2_ref, sub_ref, o_ref,
                 m_scr, acc_scr, st_scr, *, n_sub, tq, tk, lambda_init):
    i = pl.program_id(2)
    vd = o_ref.shape[3]
    r = n_sub * tq // tk

    def scores(j, d):
        off = pl.multiple_of(j * tk, tk)
        out = []
        k = k_ref[0, 0, pl.ds(off, tk), :]
        for comp in range(2):
            for c in range(n_sub):
                vis = "full" if d is None else _diag_visibility(d, c, tq, tk)
                if vis == "none":
                    out.append(None)
                    continue
                st = _dot_nt(k, q_ref[0, comp, c * tq:(c + 1) * tq, :])
                out.append(_causal_where(st, d, c) if vis == "partial" else st)
        return out

    def vt_tile(j):
        return vt_ref[0, 0, :, pl.ds(pl.multiple_of(j * tk, tk), tk)]

    _flash_sweep(i * r, r, scores, vt_tile, m_scr, acc_scr, st_scr, exp_dtype=BF16)

    lam = (jnp.exp(jnp.sum(lq1_ref[...] * lk1_ref[...], axis=-1, keepdims=True))
           - jnp.exp(jnp.sum(lq2_ref[...] * lk2_ref[...], axis=-1, keepdims=True)) + lambda_init)
    lane_pad = (-vd) % LANES
    for c in range(n_sub):
        o_t = (acc_scr[c, :vd, :] / acc_scr[c, vd:vd + 1, :]
               - lam * (acc_scr[n_sub + c, :vd, :] / acc_scr[n_sub + c, vd:vd + 1, :]))
        o_t = o_t * lax.rsqrt(jnp.mean(o_t * o_t, axis=0, keepdims=True) + NORM_EPS)
        o_t = jnp.concatenate([o_t, jnp.zeros((lane_pad, tq), F32)], axis=0)
        o_ref[0, 0, c * tq:(c + 1) * tq, :] = (
            o_t.T[:, :vd] * sub_ref[...] * (1.0 - lambda_init)).astype(o_ref.dtype)


def diff_attention(q, k, vt, lq1, lk1, lq2, lk2, subln, lambda_init):
    b, h2, s, dk = q.shape
    h = h2 // 2
    hd = DIFF_HD
    vde = vt.shape[2]
    vd = vde - ONES_PAD
    n_sub, tq, tk = _flash_tiles(s)
    t = n_sub * tq
    vec = lambda a: a.reshape(1, -1).astype(F32)
    small = lambda n: pl.BlockSpec((1, n), lambda b_, h_, i: (0, 0))
    return pl.pallas_call(
        functools.partial(_diff_kernel, n_sub=n_sub, tq=tq, tk=tk, lambda_init=lambda_init),
        grid=(b, h, s // t),
        in_specs=[pl.BlockSpec((1, 2, t, dk), lambda b_, h_, i: (b_, h_, i, 0)),
                  pl.BlockSpec((1, 1, s, dk), lambda b_, h_, i: (b_, h_, 0, 0)),
                  pl.BlockSpec((1, 1, vde, s), lambda b_, h_, i: (b_, h_, 0, 0)),
                  small(hd), small(hd), small(hd), small(hd), small(vd)],
        out_specs=pl.BlockSpec((1, 1, t, vd), lambda b_, h_, i: (b_, h_, i, 0)),
        out_shape=jax.ShapeDtypeStruct((b, h, s, vd), BF16),
        scratch_shapes=[pltpu.VMEM((2 * n_sub, 1, tq), F32), pltpu.VMEM((2 * n_sub, vde, tq), F32),
                        pltpu.VMEM((2 * n_sub, tk, tq), F32)],
        compiler_params=_cparams(("parallel", "parallel", "arbitrary")),
        name="diff_flash",
    )(q, k, vt, vec(lq1), vec(lk1), vec(lq2), vec(lk2), vec(subln))


def _compress_kernel(a_ref, pa_ref, pb_ref, w1a_ref, w1b_ref, w2_ref, o_ref):
    ng = a_ref.shape[2]
    rows = min(ng, 256)
    us, vs = [], []
    for r0 in range(0, ng, rows):
        a = a_ref[0, 0, r0:r0 + rows, :].astype(F32)
        us.append(_dot((a + pa_ref[...]).astype(BF16), w1a_ref[...]))
        vs.append(_dot((a + pb_ref[...]).astype(BF16), w1b_ref[...]))
    u = jnp.concatenate(us, axis=0)
    v = jnp.concatenate(vs, axis=0)
    hdn = u + pltpu.roll(v, ng - 1, 0)
    hdn = hdn * _sigmoid(hdn)
    o_ref[0] = _dot(hdn.astype(BF16), w2_ref[...]).astype(o_ref.dtype)


def nsa_compress(tok, g_idx, pos, w1, w2):
    assert NSA_CMP_LEN == 2 * NSA_CMP_STRIDE
    b, _, ng, wd = tok.shape
    d = wd // NSA_CMP_STRIDE
    d_out = w2.shape[1]
    pos_flat = pos.astype(F32).reshape(1, NSA_CMP_LEN * d)
    w1 = w1.astype(BF16)
    full = lambda shape: pl.BlockSpec(shape, lambda b_: (0,) * len(shape))
    return pl.pallas_call(
        _compress_kernel,
        grid=(b,),
        in_specs=[pl.BlockSpec((1, 1, ng, wd), lambda b_: (b_, g_idx, 0, 0)),
                  full((1, wd)), full((1, wd)), full((wd, NSA_CMP_HIDDEN)), full((wd, NSA_CMP_HIDDEN)),
                  full((NSA_CMP_HIDDEN, d_out))],
        out_specs=pl.BlockSpec((1, ng, d_out), lambda b_: (b_, 0, 0)),
        out_shape=jax.ShapeDtypeStruct((b, ng, d_out), BF16),
        compiler_params=_cparams(("parallel",)),
        name="nsa_compress",
    )(tok, pos_flat[:, :wd], pos_flat[:, wd:], w1[:wd], w1[wd:], w2.astype(BF16))


def _split3(x):
    hi = x.astype(BF16)
    r = x - hi.astype(F32)
    mid = r.astype(BF16)
    lo = (r - mid.astype(F32)).astype(BF16)
    return hi, mid, lo


def _cmp_topk_kernel(q_ref, kc_ref, vc_ref, wsel_ref, oc_ref, sel_ref, *, tq, k_top):
    i = pl.program_id(1)
    qs = i * tq
    nc = kc_ref.shape[1]
    n_sel = wsel_ref.shape[1]
    kc = kc_ref[0]
    vc = vc_ref[0]
    qpos_c = qs + lax.broadcasted_iota(jnp.int32, (tq, nc), 0)
    cend = lax.broadcasted_iota(jnp.int32, (tq, nc), 1) * NSA_CMP_STRIDE + (NSA_CMP_LEN - 1)
    valid_c = cend <= qpos_c
    imp = jnp.zeros((tq, nc), F32)
    for h in range(NSA_HEADS):
        s = jnp.where(valid_c, _dot_nt(q_ref[0, h], kc), NEG_INF)
        e = jnp.exp(s - jnp.max(s, axis=-1, keepdims=True))
        p = jnp.where(valid_c, e * (1.0 / jnp.sum(e, axis=-1, keepdims=True)), 0.0)
        oc_ref[0, :, h * NSA_DV:(h + 1) * NSA_DV] = _dot(p.astype(BF16), vc)
        imp = imp + p

    wsel = wsel_ref[...]
    hi, mid, lo = _split3(imp)
    imp_sel = _dot(hi, wsel) + _dot(mid, wsel) + _dot(lo, wsel)

    blk = lax.broadcasted_iota(jnp.int32, (tq, n_sel), 1)
    qpos = qs + lax.broadcasted_iota(jnp.int32, (tq, n_sel), 0)
    cur = lax.shift_right_arithmetic(qpos, int(math.log2(NSA_SEL_LEN)))
    forced = (blk == 0) | (blk == cur) | (blk == cur - 1)
    valid_s = blk * NSA_SEL_LEN <= qpos
    score = jnp.where(valid_s, jnp.where(forced, NSA_FORCE_SCORE, imp_sel), NEG_INF)
    blk_f = blk.astype(F32)
    sel = jnp.zeros((tq, n_sel), F32)
    for _ in range(k_top):
        mx = jnp.max(score, axis=-1, keepdims=True)
        first = jnp.min(jnp.where(score == mx, blk_f, float(n_sel)), axis=-1, keepdims=True)
        hit = blk_f == first
        sel = jnp.where(hit, 1.0, sel)
        score = jnp.where(hit, REMOVED, score)
    sel_ref[0] = sel.astype(sel_ref.dtype)


def _sel_weight_matrix(n_pad, n_sel):
    r_c = NSA_CMP_LEN // NSA_CMP_STRIDE
    ratio = NSA_SEL_LEN // NSA_CMP_STRIDE
    overlap_w = [max(0, min(o * NSA_CMP_STRIDE + NSA_CMP_LEN, NSA_SEL_LEN) - max(o * NSA_CMP_STRIDE, 0))
                 / NSA_CMP_STRIDE for o in range(-(r_c - 1), ratio)]
    w = np.zeros((n_pad, n_sel), np.float32)
    for n in range(n_sel):
        for u, w_u in enumerate(overlap_w):
            c = ratio * n + u - (r_c - 1)
            if 0 <= c < n_pad:
                w[c, n] = w_u
    return w


def nsa_cmp_topk(q, kc, vc):
    b, h, s, dk = q.shape
    nc = kc.shape[1]
    n_sel = s // NSA_SEL_LEN
    assert NSA_SEL_LEN & (NSA_SEL_LEN - 1) == 0
    tq = _tile(s, 256)
    wsel = jnp.asarray(_sel_weight_matrix(nc, n_sel), BF16)
    return pl.pallas_call(
        functools.partial(_cmp_topk_kernel, tq=tq, k_top=min(NSA_TOPK, n_sel)),
        grid=(b, s // tq),
        in_specs=[pl.BlockSpec((1, h, tq, dk), lambda b_, i: (b_, 0, i, 0)),
                  pl.BlockSpec((1, nc, dk), lambda b_, i: (b_, 0, 0)),
                  pl.BlockSpec((1, nc, NSA_DV), lambda b_, i: (b_, 0, 0)),
                  pl.BlockSpec((nc, n_sel), lambda b_, i: (0, 0))],
        out_specs=[pl.BlockSpec((1, tq, h * NSA_DV), lambda b_, i: (b_, i, 0)),
                   pl.BlockSpec((1, tq, n_sel), lambda b_, i: (b_, i, 0))],
        out_shape=[jax.ShapeDtypeStruct((b, s, h * NSA_DV), F32), jax.ShapeDtypeStruct((b, s, n_sel), BF16)],
        compiler_params=_cparams(("parallel", "parallel")),
        name="nsa_cmp_topk",
    )(q, kc, vc, wsel)


def _nsa_sel_kernel(q_ref, k_ref, vt_ref, sel_ref, o_ref, m_scr, acc_scr, st_scr, *, t):
    i = pl.program_id(1)
    n_sel = sel_ref.shape[2]
    bpt = t // NSA_SEL_LEN
    shift = int(math.log2(NSA_SEL_LEN))

    def scores(j, d):
        k = k_ref[0, 0, pl.ds(pl.multiple_of(j * t, t), t), :]
        blk_key = lax.shift_right_arithmetic(lax.broadcasted_iota(jnp.int32, (t, n_sel), 0), shift) + j * bpt
        blk_col = lax.broadcasted_iota(jnp.int32, (t, n_sel), 1)
        expand_t = jnp.where(blk_key == blk_col, 1.0, 0.0).astype(BF16)
        flags_t = _dot_nt(expand_t, sel_ref[0])
        if d is not None:
            flags_t = _causal_where(flags_t, 0, 0)
        keep_t = flags_t > 0.5
        return [jnp.where(keep_t, _dot_nt(k, q_ref[0, h]), NEG_INF) for h in range(NSA_HEADS)]

    def vt_tile(j):
        return vt_ref[0, 0, :, pl.ds(pl.multiple_of(j * t, t), t)]

    _flash_sweep(i, 1, scores, vt_tile, m_scr, acc_scr, st_scr, exp_dtype=BF16)
    for h in range(NSA_HEADS):
        o_t = acc_scr[h, :NSA_DV, :] / acc_scr[h, NSA_DV:NSA_DV + 1, :]
        o_ref[0, :, h * NSA_DV:(h + 1) * NSA_DV] = o_t.T


def nsa_selected(q, k3, k_idx, vt, sel):
    b, h, s, dk = q.shape
    n_sel = sel.shape[-1]
    dve = vt.shape[2]
    t = _tile(s, 512)
    assert t % NSA_SEL_LEN == 0
    return pl.pallas_call(
        functools.partial(_nsa_sel_kernel, t=t),
        grid=(b, s // t),
        in_specs=[pl.BlockSpec((1, h, t, dk), lambda b_, i: (b_, 0, i, 0)),
                  pl.BlockSpec((1, 1, s, dk), lambda b_, i: (b_, k_idx, 0, 0)),
                  pl.BlockSpec((1, 1, dve, s), lambda b_, i: (b_, 0, 0, 0)),
                  pl.BlockSpec((1, t, n_sel), lambda b_, i: (b_, i, 0))],
        out_specs=pl.BlockSpec((1, t, h * NSA_DV), lambda b_, i: (b_, i, 0)),
        out_shape=jax.ShapeDtypeStruct((b, s, h * NSA_DV), F32),
        scratch_shapes=[pltpu.VMEM((h, 1, t), F32), pltpu.VMEM((h, dve, t), F32), pltpu.VMEM((h, t, t), F32)],
        compiler_params=_cparams(("parallel", "arbitrary")),
        name="nsa_selected",
    )(q, k3, vt, sel)


def _nsa_win_kernel(q_ref, kp_ref, kc_ref, vp_ref, vc_ref, o_ref, *, t):
    i = pl.program_id(1)
    row = lax.broadcasted_iota(jnp.int32, (t, t), 0)
    col = lax.broadcasted_iota(jnp.int32, (t, t), 1)
    keep_prev = (col > row) & (i > 0)
    keep_cur = col <= row
    kp, kc, vp, vc = kp_ref[0, 0], kc_ref[0, 0], vp_ref[0], vc_ref[0]
    for h in range(NSA_HEADS):
        q = q_ref[0, h]
        sp = jnp.where(keep_prev, _dot_nt(q, kp), NEG_INF)
        sc = jnp.where(keep_cur, _dot_nt(q, kc), NEG_INF)
        m = jnp.maximum(jnp.max(sp, axis=-1, keepdims=True), jnp.max(sc, axis=-1, keepdims=True))
        pp = jnp.exp(sp - m)
        pc = jnp.exp(sc - m)
        l = jnp.sum(pp, axis=-1, keepdims=True) + jnp.sum(pc, axis=-1, keepdims=True)
        o = _dot(pp.astype(BF16), vp) + _dot(pc.astype(BF16), vc)
        o_ref[0, :, h * NSA_DV:(h + 1) * NSA_DV] = o / l


def nsa_window(q, k3, k_idx, v3, v_idx):
    b, h, s, dk = q.shape
    t = _tile(s, NSA_WINDOW)
    assert t == NSA_WINDOW, "window kernel needs the query tile to equal the window"
    prev = lambda i: jnp.maximum(i - 1, 0)
    return pl.pallas_call(
        functools.partial(_nsa_win_kernel, t=t),
        grid=(b, s // t),
        in_specs=[pl.BlockSpec((1, h, t, dk), lambda b_, i: (b_, 0, i, 0)),
                  pl.BlockSpec((1, 1, t, dk), lambda b_, i: (b_, k_idx, prev(i), 0)),
                  pl.BlockSpec((1, 1, t, dk), lambda b_, i: (b_, k_idx, i, 0)),
                  pl.BlockSpec((1, t, NSA_DV), lambda b_, i: (b_, prev(i), v_idx)),
                  pl.BlockSpec((1, t, NSA_DV), lambda b_, i: (b_, i, v_idx))],
        out_specs=pl.BlockSpec((1, t, h * NSA_DV), lambda b_, i: (b_, i, 0)),
        out_shape=jax.ShapeDtypeStruct((b, s, h * NSA_DV), F32),
        compiler_params=_cparams(("parallel", "parallel")),
        name="nsa_window",
    )(q, k3, k3, v3, v3)


def _merge_kernel(x_ref, om_ref, oc_ref, os_ref, ow_ref, gn_ref, od_ref, gm_ref, gs_ref, gd_ref,
                  wm_ref, wn_ref, wd_ref, wo_ref, ex_ref, o_ref, onsa_scr, *, n_j):
    j = pl.program_id(1)
    nsa_out = NSA_HEADS * NSA_DV

    @pl.when(j == 0)
    def _():
        g = gn_ref[...]
        hi = g.astype(BF16)
        lo = (g - hi.astype(F32)).astype(BF16)
        ge = _dot(hi, ex_ref[...]) + _dot(lo, ex_ref[...])
        onsa = (ge[:, :nsa_out] * oc_ref[...] + ge[:, nsa_out:2 * nsa_out] * os_ref[...]
                + ge[:, 2 * nsa_out:] * ow_ref[...])
        onsa_scr[...] = onsa.astype(BF16)
        o_ref[...] = jnp.zeros_like(o_ref)

    ym = _dot(om_ref[...], wm_ref[...])
    yn = _dot(onsa_scr[...], wn_ref[...])
    yd = _dot(od_ref[0, 0], wd_ref[0])
    for h in range(1, DIFF_HEADS):
        yd = yd + _dot(od_ref[0, h], wd_ref[h])
    mixed = (gm_ref[...].astype(F32) * ym + gs_ref[...].astype(F32) * yn + gd_ref[...].astype(F32) * yd)
    o_ref[...] += _dot(mixed.astype(BF16), wo_ref[...])

    @pl.when(j == n_j - 1)
    def _():
        o_ref[...] = x_ref[...] + o_ref[...]


def _gate_expand_matrix():
    nsa_out = NSA_HEADS * NSA_DV
    e = np.zeros((LANES, 3 * nsa_out), np.float32)
    for h in range(NSA_HEADS):
        for c in range(3):
            e[h * 3 + c, c * nsa_out + h * NSA_DV:c * nsa_out + (h + 1) * NSA_DV] = 1.0
    return e


def merge(x2d, o_mla, o_c, o_s, o_w, g_nsa, o_diff, g_merge, w_br_mla, w_br_nsa, w_br_diff, w_out, seq):
    m, d = x2d.shape
    tm = _tile(seq, 512)
    ns = seq // tm
    tn = _tile(d, 512)
    n_j = d // tn
    nsa_out = NSA_HEADS * NSA_DV
    mla_out = o_mla.shape[-1]
    ex = jnp.asarray(_gate_expand_matrix(), BF16)
    row = lambda w: pl.BlockSpec((tm, w), lambda i, j: (i, 0))
    return pl.pallas_call(
        functools.partial(_merge_kernel, n_j=n_j),
        grid=(m // tm, n_j),
        in_specs=[row(d), row(mla_out), row(nsa_out), row(nsa_out), row(nsa_out), row(LANES),
                  pl.BlockSpec((1, DIFF_HEADS, tm, DIFF_VD), lambda i, j: (i // ns, 0, i % ns, 0)),
                  pl.BlockSpec((tm, tn), lambda i, j: (i, j)),
                  pl.BlockSpec((tm, tn), lambda i, j: (i, n_j + j)),
                  pl.BlockSpec((tm, tn), lambda i, j: (i, 2 * n_j + j)),
                  pl.BlockSpec((mla_out, tn), lambda i, j: (0, j)),
                  pl.BlockSpec((nsa_out, tn), lambda i, j: (0, j)),
                  pl.BlockSpec((DIFF_HEADS, DIFF_VD, tn), lambda i, j: (0, 0, j)),
                  pl.BlockSpec((tn, d), lambda i, j: (j, 0)),
                  pl.BlockSpec((LANES, 3 * nsa_out), lambda i, j: (0, 0))],
        out_specs=pl.BlockSpec((tm, d), lambda i, j: (i, 0)),
        out_shape=jax.ShapeDtypeStruct((m, d), F32),
        scratch_shapes=[pltpu.VMEM((tm, nsa_out), BF16)],
        compiler_params=_cparams(("parallel", "arbitrary")),
        name="merge",
    )(x2d, o_mla, o_c, o_s, o_w, g_nsa, o_diff, g_merge, g_merge, g_merge,
      w_br_mla, w_br_nsa, w_br_diff, w_out, ex)


def _rope_tables(dim, seq, lead=0):
    inv = ROPE_THETA ** (-jnp.arange(0, dim, 2, dtype=F32) / dim)
    ang = jnp.arange(seq, dtype=F32)[:, None] * inv[None, :]
    cos, sin = jnp.cos(ang), jnp.sin(ang)
    cos = jnp.concatenate([jnp.ones((seq, lead), F32), cos, cos], axis=-1)
    sin = jnp.concatenate([jnp.zeros((seq, lead), F32), sin, sin], axis=-1)
    return cos, sin


def _heads(w, g, dh):
    return w.reshape(w.shape[0], g, dh).transpose(1, 0, 2)


def _rot_cols(w, lead=0):
    half = (w.shape[-1] - lead) // 2
    x1 = w[..., lead:lead + half]
    x2 = w[..., lead + half:]
    return jnp.concatenate([jnp.zeros_like(w[..., :lead]), -x2, x1], axis=-1)


def _pad_halves(w, axis=-1):
    w = jnp.moveaxis(w, axis, -1)
    half = w.shape[-1] // 2
    z = jnp.zeros(w.shape[:-1] + ((-half) % LANES,), w.dtype)
    out = jnp.concatenate([w[..., :half], z, w[..., half:], z], axis=-1)
    return jnp.moveaxis(out, -1, axis)


def _swap_tables(dim, seq, copies=1):
    inv = ROPE_THETA ** (-jnp.arange(0, dim, 2, dtype=F32) / dim)
    ang = jnp.arange(seq, dtype=F32)[:, None] * inv[None, :]
    cos = jnp.tile(jnp.cos(ang), (1, copies))
    sin = jnp.tile(jnp.sin(ang), (1, copies))
    return _pad_halves(jnp.concatenate([cos, cos], axis=-1)), _pad_halves(jnp.concatenate([-sin, sin], axis=-1))


def _pair_heads(w):
    k = w.shape[0]
    half = DIFF_HD // 2
    w = w.reshape(k, DIFF_HEADS, 2, 2, half)
    w = w.transpose(1, 0, 3, 2, 4).reshape(DIFF_HEADS, k, 2 * DIFF_HD)
    return _pad_halves(w)


def _pair_masks():
    half = DIFF_HD // 2
    m = np.zeros((2, 2, 2, half), np.float32)
    for c in range(2):
        m[c, :, c, :] = 1.0
    return _pad_halves(jnp.asarray(m.reshape(2, 2 * DIFF_HD)))


def _col_offsets():
    sizes = (MLA_Q_LORA, MLA_KV_LORA, MLA_ROPE,
             NSA_HEADS * NSA_DK, NSA_DK, NSA_DV, NSA_DK, NSA_DV, NSA_DK, NSA_DV, NSA_HEADS * 3,
             DIFF_HEADS * 2 * DIFF_HD, DIFF_HEADS * 2 * DIFF_HD, DIFF_HEADS * DIFF_VD)
    names = ("c_q", "c_kv", "k_rope", "nsa_q", "nsa_kc", "nsa_vc", "nsa_ks", "nsa_vs", "nsa_kw", "nsa_vw",
             "nsa_g", "d_q", "d_k", "d_v")
    offs = {}
    o = 0
    for nme, sz in zip(names, sizes):
        offs[nme] = (o, o + sz)
        o += sz
    offs["merge"] = (o, None)
    return offs


def _mixers(n2d, batch, seq, layer, w_in, p):
    offs = _col_offsets()
    col = lambda name: w_in[:, offs[name][0]:offs[name][1]]
    bf = lambda a: a.astype(BF16)

    lat = proj_plain(n2d, bf(jnp.concatenate([col("c_q"), col("c_kv")], axis=1)), F32)
    n_q, n_kv = mla_norm(lat, p["mla_q_norm"], p["mla_kv_norm"])
    cos_m, sin_m = _rope_tables(MLA_ROPE, seq, lead=MLA_NOPE)
    w_kr = jnp.concatenate([jnp.zeros((w_in.shape[0], MLA_NOPE), F32), col("k_rope")], axis=1)[None]
    kpe = proj_heads(n2d, bf(w_kr), batch, seq, rope=("weights", bf(_rot_cols(w_kr, MLA_NOPE)), cos_m, sin_m))
    w_uq = _heads(p["mla_w_uq"], MLA_HEADS, MLA_QK)
    q_mla = proj_heads(n_q, bf(w_uq), batch, seq, rope=("weights", bf(_rot_cols(w_uq, MLA_NOPE)), cos_m, sin_m),
                       scale=MLA_QK ** -0.5)
    w_ukv = _heads(p["mla_w_ukv"], MLA_HEADS, MLA_NOPE + MLA_V)
    w_uk = jnp.concatenate([w_ukv[..., :MLA_NOPE], jnp.zeros(w_ukv.shape[:2] + (MLA_ROPE,), F32)], axis=-1)
    k_mla = proj_heads(n_kv, bf(w_uk), batch, seq, add=kpe)
    vt_mla = proj_heads_t(n_kv, bf(w_ukv[..., MLA_NOPE:]), batch, seq)
    o_mla = flash_causal(q_mla, k_mla, vt_mla)

    cos_d, sin_d = _swap_tables(DIFF_HD, seq, copies=2)
    q_d = proj_heads(n2d, bf(_pair_heads(col("d_q"))), batch, seq, rope=("swap", cos_d, sin_d),
                     scale=DIFF_HD ** -0.5, masks=_pair_masks())
    k_d = proj_heads(n2d, bf(_pair_heads(col("d_k"))), batch, seq, rope=("swap", cos_d, sin_d))
    vt_d = proj_heads_t(n2d, bf(_heads(col("d_v"), DIFF_HEADS, DIFF_VD)), batch, seq)
    lambda_init = 0.8 - 0.6 * math.exp(-0.3 * layer)
    o_diff = diff_attention(q_d, k_d, vt_d, p["diff_lam_q1"], p["diff_lam_k1"], p["diff_lam_q2"],
                            p["diff_lam_k2"], p["diff_subln"], lambda_init)

    cos_n, sin_n = _swap_tables(NSA_DK, seq)
    w_nq = _pad_halves(_heads(col("nsa_q"), NSA_HEADS, NSA_DK))
    q_n = proj_heads(n2d, bf(w_nq), batch, seq, rope=("swap", cos_n, sin_n), scale=NSA_DK ** -0.5)
    w_nk = _pad_halves(jnp.stack([col("nsa_kc"), col("nsa_ks"), col("nsa_kw")], axis=0))
    k_n = proj_heads(n2d, bf(w_nk), batch, seq, rope=("swap", cos_n, sin_n))
    dk_pad = w_nk.shape[-1]
    v_n = proj_plain(n2d, bf(jnp.concatenate([col("nsa_vc"), col("nsa_vw")], axis=1)), BF16)
    v_n = v_n.reshape(batch, seq, 2 * NSA_DV)
    vt_slc = proj_heads_t(n2d, bf(col("nsa_vs"))[None], batch, seq)
    w_g = jnp.concatenate([col("nsa_g"), jnp.zeros((w_in.shape[0], LANES - NSA_HEADS * 3), F32)], axis=1)
    g_nsa = proj_plain(n2d, bf(w_g), F32, sigmoid=True)

    ng = seq // NSA_CMP_STRIDE
    w1k = _pad_halves(p["nsa_cmp_k_w1"].reshape(NSA_CMP_LEN, NSA_DK, NSA_CMP_HIDDEN), axis=1)
    kc = nsa_compress(k_n.reshape(batch, 3, ng, NSA_CMP_STRIDE * dk_pad), 0,
                      _pad_halves(p["nsa_cmp_k_pos"]), w1k.reshape(NSA_CMP_LEN * dk_pad, NSA_CMP_HIDDEN),
                      _pad_halves(p["nsa_cmp_k_w2"]))
    vc_tok = v_n[:, :, :NSA_DV].reshape(batch, 1, ng, NSA_CMP_STRIDE * NSA_DV)
    vc = nsa_compress(vc_tok, 0, p["nsa_cmp_v_pos"], p["nsa_cmp_v_w1"], p["nsa_cmp_v_w2"])
    o_c, sel = nsa_cmp_topk(q_n, kc, vc)
    o_s = nsa_selected(q_n, k_n, 1, vt_slc, sel)
    o_w = nsa_window(q_n, k_n, 2, v_n, 1)

    g_merge = proj_plain(n2d, bf(w_in[:, offs["merge"][0]:]), BF16, sigmoid=True)
    m = batch * seq
    return (o_mla.reshape(m, -1), o_c.reshape(m, -1), o_s.reshape(m, -1), o_w.reshape(m, -1), g_nsa,
            o_diff, g_merge)


def kernel(x, ffn1_norm, ffn1_w_in, ffn1_w_out, mix_norm, w_in, mla_q_norm, mla_kv_norm, mla_w_uq, mla_w_ukv, nsa_cmp_k_pos, nsa_cmp_k_w1, nsa_cmp_k_w2, nsa_cmp_v_pos, nsa_cmp_v_w1, nsa_cmp_v_w2, diff_lam_q1, diff_lam_k1, diff_lam_q2, diff_lam_k2, diff_subln, w_br_mla, w_br_nsa, w_br_diff, w_out, ffn2_norm, ffn2_w_in, ffn2_w_out, final_norm):
    batch, seq, d = x.shape
    depth = w_in.shape[0]
    x2d = x.reshape(batch * seq, d)
    bf = lambda a: a.astype(BF16)
    for l in range(depth):
        p = {"mla_q_norm": mla_q_norm[l], "mla_kv_norm": mla_kv_norm[l], "mla_w_uq": mla_w_uq[l],
             "mla_w_ukv": mla_w_ukv[l], "nsa_cmp_k_pos": nsa_cmp_k_pos[l], "nsa_cmp_k_w1": nsa_cmp_k_w1[l],
             "nsa_cmp_k_w2": nsa_cmp_k_w2[l], "nsa_cmp_v_pos": nsa_cmp_v_pos[l], "nsa_cmp_v_w1": nsa_cmp_v_w1[l],
             "nsa_cmp_v_w2": nsa_cmp_v_w2[l], "diff_lam_q1": diff_lam_q1[l], "diff_lam_k1": diff_lam_k1[l],
             "diff_lam_q2": diff_lam_q2[l], "diff_lam_k2": diff_lam_k2[l], "diff_subln": diff_subln[l]}
        x2d = ffn(x2d, ffn1_norm[l], bf(ffn1_w_in[l]), bf(ffn1_w_out[l]))
        n2d = rmsnorm(x2d, mix_norm[l], BF16)
        o_mla, o_c, o_s, o_w, g_nsa, o_diff, g_merge = _mixers(n2d, batch, seq, l, w_in[l], p)
        x2d = merge(x2d, o_mla, o_c, o_s, o_w, g_nsa, o_diff, g_merge, bf(w_br_mla[l]), bf(w_br_nsa[l]),
                    bf(w_br_diff[l]).reshape(DIFF_HEADS, DIFF_VD, d), bf(w_out[l]), seq)
        x2d = ffn(x2d, ffn2_norm[l], bf(ffn2_w_in[l]), bf(ffn2_w_out[l]))
    return rmsnorm(x2d, final_norm, F32).reshape(batch, seq, d)
```

```python
import functools
import math

import numpy as np
import jax
import jax.numpy as jnp
from jax import lax
from jax.experimental import pallas as pl
from jax.experimental.pallas import tpu as pltpu

F32 = jnp.float32
BF16 = jnp.bfloat16

NORM_EPS = 1e-6
ROPE_THETA = 10000.0
NEG_INF = -1e30
REMOVED = -3e38
N_BRANCH = 3

MLA_HEADS = 6
MLA_Q_LORA = 768
MLA_KV_LORA = 512
MLA_NOPE = 128
MLA_ROPE = 64
MLA_V = 128
MLA_QK = MLA_NOPE + MLA_ROPE

NSA_HEADS = 4
NSA_DK = 192
NSA_DV = 128
NSA_CMP_LEN = 32
NSA_CMP_STRIDE = 16
NSA_CMP_HIDDEN = 256
NSA_SEL_LEN = 64
NSA_TOPK = 16
NSA_WINDOW = 512
NSA_FORCE_SCORE = 1e6

DIFF_HEADS = 4
DIFF_HD = 96
DIFF_VD = 2 * DIFF_HD

LANES = 128
ONES_PAD = 16
VMEM_LIMIT_MB = 56


def _cparams(dims, vmem_mb=VMEM_LIMIT_MB):
    return pltpu.CompilerParams(dimension_semantics=dims, vmem_limit_bytes=vmem_mb * 2**20)


def _sigmoid(x):
    return 1.0 / (1.0 + jnp.exp(-x))


def _dot(a, b):
    return jnp.dot(a, b, preferred_element_type=F32)


def _dot_nt(a, b):
    return lax.dot_general(a, b, (((1,), (1,)), ((), ())), preferred_element_type=F32)


def _tile(n, pref):
    t = min(n, pref)
    assert n % t == 0, (n, t)
    return t


def _rmsnorm_kernel(x_ref, w_ref, o_ref):
    x = x_ref[...].astype(F32)
    y = x * lax.rsqrt(jnp.mean(x * x, axis=-1, keepdims=True) + NORM_EPS)
    o_ref[...] = (y * w_ref[...]).astype(o_ref.dtype)


def rmsnorm(x2d, w, out_dtype):
    m, d = x2d.shape
    tm = _tile(m, 1024)
    return pl.pallas_call(
        _rmsnorm_kernel,
        grid=(m // tm,),
        in_specs=[pl.BlockSpec((tm, d), lambda i: (i, 0)), pl.BlockSpec((1, d), lambda i: (0, 0))],
        out_specs=pl.BlockSpec((tm, d), lambda i: (i, 0)),
        out_shape=jax.ShapeDtypeStruct((m, d), out_dtype),
        compiler_params=_cparams(("parallel",)),
        name="rmsnorm",
    )(x2d, w.reshape(1, d).astype(F32))


def _mla_norm_kernel(lat_ref, wq_ref, wkv_ref, nq_ref, nkv_ref):
    lat = lat_ref[...]
    cq = lat[:, :MLA_Q_LORA]
    ckv = lat[:, MLA_Q_LORA:]
    nq = cq * lax.rsqrt(jnp.mean(cq * cq, axis=-1, keepdims=True) + NORM_EPS)
    nkv = ckv * lax.rsqrt(jnp.mean(ckv * ckv, axis=-1, keepdims=True) + NORM_EPS)
    nq_ref[...] = (nq * wq_ref[...]).astype(BF16)
    nkv_ref[...] = (nkv * wkv_ref[...]).astype(BF16)


def mla_norm(lat, wq, wkv):
    m, d = lat.shape
    tm = _tile(m, 1024)
    return pl.pallas_call(
        _mla_norm_kernel,
        grid=(m // tm,),
        in_specs=[pl.BlockSpec((tm, d), lambda i: (i, 0)),
                  pl.BlockSpec((1, MLA_Q_LORA), lambda i: (0, 0)),
                  pl.BlockSpec((1, MLA_KV_LORA), lambda i: (0, 0))],
        out_specs=[pl.BlockSpec((tm, MLA_Q_LORA), lambda i: (i, 0)),
                   pl.BlockSpec((tm, MLA_KV_LORA), lambda i: (i, 0))],
        out_shape=[jax.ShapeDtypeStruct((m, MLA_Q_LORA), BF16), jax.ShapeDtypeStruct((m, MLA_KV_LORA), BF16)],
        compiler_params=_cparams(("parallel",)),
        name="mla_norm",
    )(lat, wq.reshape(1, -1).astype(F32), wkv.reshape(1, -1).astype(F32))


def _ffn_kernel(x_ref, nw_ref, wg_ref, wu_ref, wo_ref, o_ref, n_scr, *, n_f):
    f = pl.program_id(1)

    @pl.when(f == 0)
    def _():
        x = x_ref[...]
        y = x * lax.rsqrt(jnp.mean(x * x, axis=-1, keepdims=True) + NORM_EPS)
        n_scr[...] = (y * nw_ref[...]).astype(BF16)
        o_ref[...] = jnp.zeros_like(o_ref)

    n = n_scr[...]
    g = _dot(n, wg_ref[...])
    u = _dot(n, wu_ref[...])
    h = (g * _sigmoid(g) * u).astype(BF16)
    o_ref[...] += _dot(h, wo_ref[...])

    @pl.when(f == n_f - 1)
    def _():
        o_ref[...] = x_ref[...] + 0.5 * o_ref[...]


def ffn(x2d, norm_w, w_in, w_out):
    m, d = x2d.shape
    f_dim = w_out.shape[0]
    tm = _tile(m, 512)
    tf = _tile(f_dim, 512)
    n_f = f_dim // tf
    return pl.pallas_call(
        functools.partial(_ffn_kernel, n_f=n_f),
        grid=(m // tm, n_f),
        in_specs=[pl.BlockSpec((tm, d), lambda i, f: (i, 0)),
                  pl.BlockSpec((1, d), lambda i, f: (0, 0)),
                  pl.BlockSpec((d, tf), lambda i, f: (0, f)),
                  pl.BlockSpec((d, tf), lambda i, f: (0, f + n_f)),
                  pl.BlockSpec((tf, d), lambda i, f: (f, 0))],
        out_specs=pl.BlockSpec((tm, d), lambda i, f: (i, 0)),
        out_shape=jax.ShapeDtypeStruct((m, d), F32),
        scratch_shapes=[pltpu.VMEM((tm, d), BF16)],
        compiler_params=_cparams(("parallel", "arbitrary")),
        name="ffn",
    )(x2d, norm_w.reshape(1, d).astype(F32), w_in, w_in, w_out)


def _proj_plain_kernel(n_ref, w_ref, o_ref, *, sigmoid):
    y = _dot(n_ref[...], w_ref[...])
    if sigmoid:
        y = _sigmoid(y)
    o_ref[...] = y.astype(o_ref.dtype)


def proj_plain(n2d, w, out_dtype, sigmoid=False):
    m, k = n2d.shape
    n_out = w.shape[1]
    tm = _tile(m, 1024)
    tn = next(t for t in (512, 256, LANES) if n_out % t == 0)
    return pl.pallas_call(
        functools.partial(_proj_plain_kernel, sigmoid=sigmoid),
        grid=(m // tm, n_out // tn),
        in_specs=[pl.BlockSpec((tm, k), lambda i, j: (i, 0)),
                  pl.BlockSpec((k, tn), lambda i, j: (0, j))],
        out_specs=pl.BlockSpec((tm, tn), lambda i, j: (i, j)),
        out_shape=jax.ShapeDtypeStruct((m, n_out), out_dtype),
        compiler_params=_cparams(("parallel", "arbitrary")),
        name="proj_plain",
    )(n2d, w)


def _proj_heads_kernel(*refs, rope, has_add, has_masks, scale):
    n_ref, w_ref = refs[0], refs[1]
    o_ref = refs[-1]
    n = n_ref[...]
    n_copy = o_ref.shape[1] // w_ref.shape[0]
    for h in range(w_ref.shape[0]):
        pos = 2
        y = _dot(n, w_ref[h])
        if rope == "weights":
            wr_ref, cos_ref, sin_ref = refs[pos:pos + 3]
            pos += 3
            y = y * cos_ref[...] + _dot(n, wr_ref[h]) * sin_ref[...]
        elif rope == "swap":
            cos_ref, sin_ref = refs[pos:pos + 2]
            pos += 2
            half = y.shape[1] // 2
            y = y * cos_ref[...] + jnp.concatenate([y[:, half:], y[:, :half]], axis=1) * sin_ref[...]
        if has_add:
            y = y + refs[pos][0, 0].astype(F32)
            pos += 1
        if scale != 1.0:
            y = y * scale
        if has_masks:
            mask_ref = refs[pos]
            for c in range(n_copy):
                o_ref[0, h * n_copy + c] = (y * mask_ref[c:c + 1, :]).astype(o_ref.dtype)
        else:
            o_ref[0, h] = y.astype(o_ref.dtype)


def proj_heads(n2d, w, batch, seq, rope=None, add=None, scale=1.0, masks=None):
    m, k = n2d.shape
    g, _, dh = w.shape
    tm = _tile(seq, 1024)
    ns = seq // tm
    weights = pl.BlockSpec((g, k, dh), lambda b, i: (0, 0, 0))
    in_specs = [pl.BlockSpec((tm, k), lambda b, i: (b * ns + i, 0)), weights]
    args = [n2d, w]
    table = pl.BlockSpec((tm, dh), lambda b, i: (i, 0))
    if rope is not None and rope[0] == "weights":
        in_specs += [weights, table, table]
        args += list(rope[1:])
    elif rope is not None:
        assert rope[0] == "swap" and dh % (2 * LANES) == 0
        in_specs += [table, table]
        args += list(rope[1:])
    if add is not None:
        in_specs.append(pl.BlockSpec((1, 1, tm, dh), lambda b, i: (b, 0, i, 0)))
        args.append(add)
    n_copy = 1
    if masks is not None:
        n_copy = masks.shape[0]
        in_specs.append(pl.BlockSpec((n_copy, dh), lambda b, i: (0, 0)))
        args.append(masks)
    return pl.pallas_call(
        functools.partial(_proj_heads_kernel, rope=None if rope is None else rope[0], has_add=add is not None,
                          has_masks=masks is not None, scale=scale),
        grid=(batch, ns),
        in_specs=in_specs,
        out_specs=pl.BlockSpec((1, g * n_copy, tm, dh), lambda b, i: (b, 0, i, 0)),
        out_shape=jax.ShapeDtypeStruct((batch, g * n_copy, seq, dh), BF16),
        compiler_params=_cparams(("parallel", "parallel")),
        name="proj_heads",
    )(*args)


def _proj_heads_t_kernel(n_ref, wt_ref, o_ref):
    dh = wt_ref.shape[1]
    pad = o_ref.shape[2] - dh
    row = lax.broadcasted_iota(jnp.int32, (pad, o_ref.shape[3]), 0)
    ones_rows = jnp.where(row == 0, 1.0, 0.0).astype(o_ref.dtype)
    n = n_ref[...]
    for h in range(wt_ref.shape[0]):
        o_ref[0, h, :dh, :] = _dot_nt(wt_ref[h], n).astype(o_ref.dtype)
        o_ref[0, h, dh:, :] = ones_rows


def proj_heads_t(n2d, w, batch, seq):
    m, k = n2d.shape
    g, _, dh = w.shape
    tm = _tile(seq, 1024)
    ns = seq // tm
    return pl.pallas_call(
        _proj_heads_t_kernel,
        grid=(batch, ns),
        in_specs=[pl.BlockSpec((tm, k), lambda b, i: (b * ns + i, 0)),
                  pl.BlockSpec((g, dh, k), lambda b, i: (0, 0, 0))],
        out_specs=pl.BlockSpec((1, g, dh + ONES_PAD, tm), lambda b, i: (b, 0, 0, i)),
        out_shape=jax.ShapeDtypeStruct((batch, g, dh + ONES_PAD, seq), BF16),
        compiler_params=_cparams(("parallel", "parallel")),
        name="proj_heads_t",
    )(n2d, jnp.swapaxes(w, 1, 2))


def _softmax_step_t(st, vt, m_scr, acc_scr, exp_dtype):
    m_prev = m_scr[...]
    m_new = jnp.maximum(m_prev, jnp.max(st, axis=0, keepdims=True))
    alpha = jnp.exp(m_prev - m_new)
    pt = jnp.exp((st - m_new).astype(exp_dtype)).astype(BF16)
    acc_scr[...] = alpha * acc_scr[...] + _dot(vt, pt)
    m_scr[...] = m_new


def _diag_visibility(d, c, tq, tk):
    if d * tk >= (c + 1) * tq:
        return "none"
    if (d + 1) * tk - 1 <= c * tq:
        return "full"
    return "partial"


def _causal_where(st, d, c):
    tk, tq = st.shape
    key = lax.broadcasted_iota(jnp.int32, (tk, tq), 0) + d * tk
    qry = lax.broadcasted_iota(jnp.int32, (tk, tq), 1) + c * tq
    return jnp.where(key <= qry, st, NEG_INF)


def _flash_sweep(n, r, sub_of, tq, tk, scores, vt_tile, m_scr, acc_scr, st_scr, exp_dtype=F32):
    n_chain = m_scr.shape[0]
    for c in range(n_chain):
        m_scr[c] = jnp.full(m_scr.shape[1:], NEG_INF, F32)
        acc_scr[c] = jnp.zeros(acc_scr.shape[1:], F32)
    n_slot = st_scr.shape[0]
    assert r % n_slot == 0
    everyone = (True,) * n_chain
    for c, st in enumerate(scores(0, everyone)):
        st_scr[0, c] = st

    def body(jj, carry):
        for slot in range(n_slot):
            j = jj * n_slot + slot
            nxt = scores(j + 1, everyone)
            vt = vt_tile(j)
            for c in range(n_chain):
                st = st_scr[slot, c]
                st_scr[(slot + 1) % n_slot, c] = nxt[c]
                _softmax_step_t(st, vt, m_scr.at[c], acc_scr.at[c], exp_dtype)
        return carry

    lax.fori_loop(0, n // n_slot, body, 0)
    for d in range(r):
        vis = [_diag_visibility(d, sub_of(c), tq, tk) for c in range(n_chain)]
        need = tuple(d + 1 < r and _diag_visibility(d + 1, sub_of(c), tq, tk) != "none" for c in range(n_chain))
        nxt = scores(n + d + 1, need) if any(need) else None
        vt = vt_tile(n + d)
        for c in range(n_chain):
            if vis[c] != "none":
                st = st_scr[d % n_slot, c]
                if vis[c] == "partial":
                    st = _causal_where(st, d, sub_of(c))
            if need[c]:
                st_scr[(d + 1) % n_slot, c] = nxt[c]
            if vis[c] != "none":
                _softmax_step_t(st, vt, m_scr.at[c], acc_scr.at[c], exp_dtype)


def _flash_kernel(q_ref, k_ref, vt_ref, o_ref, m_scr, acc_scr, st_scr, *, n_sub, tq, tk):
    i = pl.program_id(2)
    dv = o_ref.shape[2]
    r = n_sub * tq // tk

    def scores(j, need):
        k = k_ref[0, 0, pl.ds(pl.multiple_of(j * tk, tk), tk), :]
        return [_dot_nt(k, q_ref[0, 0, c * tq:(c + 1) * tq, :]) if need[c] else None for c in range(n_sub)]

    def vt_tile(j):
        return vt_ref[0, 0, :, pl.ds(pl.multiple_of(j * tk, tk), tk)]

    _flash_sweep(i * r, r, lambda c: c, tq, tk, scores, vt_tile, m_scr, acc_scr, st_scr)
    for c in range(n_sub):
        o_t = acc_scr[c, :dv, :] / acc_scr[c, dv:dv + 1, :]
        o_ref[0, c * tq:(c + 1) * tq, :] = o_t.T.astype(o_ref.dtype)


def _score_slots(r):
    return 2 if r % 2 == 0 else 1


def _flash_tiles(s):
    tq = _tile(s, 512)
    n_sub = 2 if s % (2 * tq) == 0 else 1
    return n_sub, tq, tq


def flash_causal(q, k, vt):
    b, h, s, dk = q.shape
    dve = vt.shape[2]
    dv = dve - ONES_PAD
    n_sub, tq, tk = _flash_tiles(s)
    t = n_sub * tq
    return pl.pallas_call(
        functools.partial(_flash_kernel, n_sub=n_sub, tq=tq, tk=tk),
        grid=(b, h, s // t),
        in_specs=[pl.BlockSpec((1, 1, t, dk), lambda b_, h_, i: (b_, h_, i, 0)),
                  pl.BlockSpec((1, 1, s, dk), lambda b_, h_, i: (b_, h_, 0, 0)),
                  pl.BlockSpec((1, 1, dve, s), lambda b_, h_, i: (b_, h_, 0, 0))],
        out_specs=pl.BlockSpec((1, t, dv), lambda b_, h_, i: (b_, i, h_)),
        out_shape=jax.ShapeDtypeStruct((b, s, h * dv), BF16),
        scratch_shapes=[pltpu.VMEM((n_sub, 1, tq), F32), pltpu.VMEM((n_sub, dve, tq), F32),
                        pltpu.VMEM((_score_slots(n_sub * tq // tk), n_sub, tk, tq), F32)],
        compiler_params=_cparams(("parallel", "parallel", "arbitrary")),
        name="mla_flash",
    )(q, k, vt)


def _diff_kernel(q_ref, k_ref, vt_ref, lq1_ref, lk1_ref, lq2_ref, lk2_ref, sub_ref, o_ref,
                 m_scr, acc_scr, st_scr, *, n_sub, tq, tk, lambda_init):
    i = pl.program_id(2)
    vd = o_ref.shape[3]
    r = n_sub * tq // tk

    def scores(j, need):
        k = k_ref[0, 0, pl.ds(pl.multiple_of(j * tk, tk), tk), :]
        return [_dot_nt(k, q_ref[0, ch // n_sub, (ch % n_sub) * tq:(ch % n_sub + 1) * tq, :]) if need[ch] else None
                for ch in range(2 * n_sub)]

    def vt_tile(j):
        return vt_ref[0, 0, :, pl.ds(pl.multiple_of(j * tk, tk), tk)]

    _flash_sweep(i * r, r, lambda ch: ch % n_sub, tq, tk, scores, vt_tile, m_scr, acc_scr, st_scr, exp_dtype=BF16)

    lam = (jnp.exp(jnp.sum(lq1_ref[...] * lk1_ref[...], axis=-1, keepdims=True))
           - jnp.exp(jnp.sum(lq2_ref[...] * lk2_ref[...], axis=-1, keepdims=True)) + lambda_init)
    lane_pad = (-vd) % LANES
    for c in range(n_sub):
        o_t = (acc_scr[c, :vd, :] / acc_scr[c, vd:vd + 1, :]
               - lam * (acc_scr[n_sub + c, :vd, :] / acc_scr[n_sub + c, vd:vd + 1, :]))
        o_t = o_t * lax.rsqrt(jnp.mean(o_t * o_t, axis=0, keepdims=True) + NORM_EPS)
        o_t = jnp.concatenate([o_t, jnp.zeros((lane_pad, tq), F32)], axis=0)
        o_ref[0, 0, c * tq:(c + 1) * tq, :] = (
            o_t.T[:, :vd] * sub_ref[...] * (1.0 - lambda_init)).astype(o_ref.dtype)


def diff_attention(q, k, vt, lq1, lk1, lq2, lk2, subln, lambda_init):
    b, h2, s, dk = q.shape
    h = h2 // 2
    hd = DIFF_HD
    vde = vt.shape[2]
    vd = vde - ONES_PAD
    n_sub, tq, tk = _flash_tiles(s)
    t = n_sub * tq
    vec = lambda a: a.reshape(1, -1).astype(F32)
    small = lambda n: pl.BlockSpec((1, n), lambda b_, h_, i: (0, 0))
    return pl.pallas_call(
        functools.partial(_diff_kernel, n_sub=n_sub, tq=tq, tk=tk, lambda_init=lambda_init),
        grid=(b, h, s // t),
        in_specs=[pl.BlockSpec((1, 2, t, dk), lambda b_, h_, i: (b_, h_, i, 0)),
                  pl.BlockSpec((1, 1, s, dk), lambda b_, h_, i: (b_, h_, 0, 0)),
                  pl.BlockSpec((1, 1, vde, s), lambda b_, h_, i: (b_, h_, 0, 0)),
                  small(hd), small(hd), small(hd), small(hd), small(vd)],
        out_specs=pl.BlockSpec((1, 1, t, vd), lambda b_, h_, i: (b_, h_, i, 0)),
        out_shape=jax.ShapeDtypeStruct((b, h, s, vd), BF16),
        scratch_shapes=[pltpu.VMEM((2 * n_sub, 1, tq), F32), pltpu.VMEM((2 * n_sub, vde, tq), F32),
                        pltpu.VMEM((_score_slots(n_sub * tq // tk), 2 * n_sub, tk, tq), F32)],
        compiler_params=_cparams(("parallel", "parallel", "arbitrary")),
        name="diff_flash",
    )(q, k, vt, vec(lq1), vec(lk1), vec(lq2), vec(lk2), vec(subln))


def _compress_kernel(a_ref, pa_ref, pb_ref, w1a_ref, w1b_ref, w2_ref, o_ref):
    ng = a_ref.shape[2]
    rows = min(ng, 256)
    us, vs = [], []
    for r0 in range(0, ng, rows):
        a = a_ref[0, 0, r0:r0 + rows, :].astype(F32)
        us.append(_dot((a + pa_ref[...]).astype(BF16), w1a_ref[...]))
        vs.append(_dot((a + pb_ref[...]).astype(BF16), w1b_ref[...]))
    u = jnp.concatenate(us, axis=0)
    v = jnp.concatenate(vs, axis=0)
    hdn = u + pltpu.roll(v, ng - 1, 0)
    hdn = hdn * _sigmoid(hdn)
    o_ref[0] = _dot(hdn.astype(BF16), w2_ref[...]).astype(o_ref.dtype)


def nsa_compress(tok, g_idx, pos, w1, w2):
    assert NSA_CMP_LEN == 2 * NSA_CMP_STRIDE
    b, _, ng, wd = tok.shape
    d = wd // NSA_CMP_STRIDE
    d_out = w2.shape[1]
    pos_flat = pos.astype(F32).reshape(1, NSA_CMP_LEN * d)
    w1 = w1.astype(BF16)
    full = lambda shape: pl.BlockSpec(shape, lambda b_: (0,) * len(shape))
    return pl.pallas_call(
        _compress_kernel,
        grid=(b,),
        in_specs=[pl.BlockSpec((1, 1, ng, wd), lambda b_: (b_, g_idx, 0, 0)),
                  full((1, wd)), full((1, wd)), full((wd, NSA_CMP_HIDDEN)), full((wd, NSA_CMP_HIDDEN)),
                  full((NSA_CMP_HIDDEN, d_out))],
        out_specs=pl.BlockSpec((1, ng, d_out), lambda b_: (b_, 0, 0)),
        out_shape=jax.ShapeDtypeStruct((b, ng, d_out), BF16),
        compiler_params=_cparams(("parallel",)),
        name="nsa_compress",
    )(tok, pos_flat[:, :wd], pos_flat[:, wd:], w1[:wd], w1[wd:], w2.astype(BF16))


def _split3(x):
    hi = x.astype(BF16)
    r = x - hi.astype(F32)
    mid = r.astype(BF16)
    lo = (r - mid.astype(F32)).astype(BF16)
    return hi, mid, lo


def _cmp_topk_kernel(q_ref, kc_ref, vc_ref, wsel_ref, oc_ref, sel_ref, *, tq, k_top):
    i = pl.program_id(1)
    qs = i * tq
    nc = kc_ref.shape[1]
    n_sel = wsel_ref.shape[1]
    kc = kc_ref[0]
    vc = vc_ref[0]
    qpos_c = qs + lax.broadcasted_iota(jnp.int32, (tq, nc), 0)
    cend = lax.broadcasted_iota(jnp.int32, (tq, nc), 1) * NSA_CMP_STRIDE + (NSA_CMP_LEN - 1)
    valid_c = cend <= qpos_c
    imp = jnp.zeros((tq, nc), F32)
    for h in range(NSA_HEADS):
        s = jnp.where(valid_c, _dot_nt(q_ref[0, h], kc), NEG_INF)
        e = jnp.exp(s - jnp.max(s, axis=-1, keepdims=True))
        p = jnp.where(valid_c, e * (1.0 / jnp.sum(e, axis=-1, keepdims=True)), 0.0)
        oc_ref[0, :, h * NSA_DV:(h + 1) * NSA_DV] = _dot(p.astype(BF16), vc)
        imp = imp + p

    wsel = wsel_ref[...]
    hi, mid, lo = _split3(imp)
    imp_sel = _dot(hi, wsel) + _dot(mid, wsel) + _dot(lo, wsel)

    blk = lax.broadcasted_iota(jnp.int32, (tq, n_sel), 1)
    qpos = qs + lax.broadcasted_iota(jnp.int32, (tq, n_sel), 0)
    cur = lax.shift_right_arithmetic(qpos, int(math.log2(NSA_SEL_LEN)))
    forced = (blk == 0) | (blk == cur) | (blk == cur - 1)
    valid_s = blk * NSA_SEL_LEN <= qpos
    score = jnp.where(valid_s, jnp.where(forced, NSA_FORCE_SCORE, imp_sel), NEG_INF)
    blk_f = blk.astype(F32)
    sel = jnp.zeros((tq, n_sel), F32)
    for _ in range(k_top):
        mx = jnp.max(score, axis=-1, keepdims=True)
        first = jnp.min(jnp.where(score == mx, blk_f, float(n_sel)), axis=-1, keepdims=True)
        hit = blk_f == first
        sel = jnp.where(hit, 1.0, sel)
        score = jnp.where(hit, REMOVED, score)
    sel_ref[0] = sel.astype(sel_ref.dtype)


def _sel_weight_matrix(n_pad, n_sel):
    r_c = NSA_CMP_LEN // NSA_CMP_STRIDE
    ratio = NSA_SEL_LEN // NSA_CMP_STRIDE
    overlap_w = [max(0, min(o * NSA_CMP_STRIDE + NSA_CMP_LEN, NSA_SEL_LEN) - max(o * NSA_CMP_STRIDE, 0))
                 / NSA_CMP_STRIDE for o in range(-(r_c - 1), ratio)]
    w = np.zeros((n_pad, n_sel), np.float32)
    for n in range(n_sel):
        for u, w_u in enumerate(overlap_w):
            c = ratio * n + u - (r_c - 1)
            if 0 <= c < n_pad:
                w[c, n] = w_u
    return w


def nsa_cmp_topk(q, kc, vc):
    b, h, s, dk = q.shape
    nc = kc.shape[1]
    n_sel = s // NSA_SEL_LEN
    assert NSA_SEL_LEN & (NSA_SEL_LEN - 1) == 0
    tq = _tile(s, 256)
    wsel = jnp.asarray(_sel_weight_matrix(nc, n_sel), BF16)
    return pl.pallas_call(
        functools.partial(_cmp_topk_kernel, tq=tq, k_top=min(NSA_TOPK, n_sel)),
        grid=(b, s // tq),
        in_specs=[pl.BlockSpec((1, h, tq, dk), lambda b_, i: (b_, 0, i, 0)),
                  pl.BlockSpec((1, nc, dk), lambda b_, i: (b_, 0, 0)),
                  pl.BlockSpec((1, nc, NSA_DV), lambda b_, i: (b_, 0, 0)),
                  pl.BlockSpec((nc, n_sel), lambda b_, i: (0, 0))],
        out_specs=[pl.BlockSpec((1, tq, h * NSA_DV), lambda b_, i: (b_, i, 0)),
                   pl.BlockSpec((1, tq, n_sel), lambda b_, i: (b_, i, 0))],
        out_shape=[jax.ShapeDtypeStruct((b, s, h * NSA_DV), F32), jax.ShapeDtypeStruct((b, s, n_sel), BF16)],
        compiler_params=_cparams(("parallel", "parallel")),
        name="nsa_cmp_topk",
    )(q, kc, vc, wsel)


def _nsa_sel_kernel(q_ref, k_ref, vt_ref, sel_ref, o_ref, m_scr, acc_scr, st_scr, *, t):
    i = pl.program_id(1)
    n_sel = sel_ref.shape[2]
    bpt = t // NSA_SEL_LEN
    shift = int(math.log2(NSA_SEL_LEN))

    def scores(j, need):
        k = k_ref[0, 0, pl.ds(pl.multiple_of(j * t, t), t), :]
        blk_key = lax.shift_right_arithmetic(lax.broadcasted_iota(jnp.int32, (t, n_sel), 0), shift) + j * bpt
        blk_col = lax.broadcasted_iota(jnp.int32, (t, n_sel), 1)
        expand_t = jnp.where(blk_key == blk_col, 1.0, 0.0).astype(BF16)
        keep_t = _dot_nt(expand_t, sel_ref[0]) > 0.5
        return [jnp.where(keep_t, _dot_nt(k, q_ref[0, h]), NEG_INF) if need[h] else None
                for h in range(NSA_HEADS)]

    def vt_tile(j):
        return vt_ref[0, 0, :, pl.ds(pl.multiple_of(j * t, t), t)]

    _flash_sweep(i, 1, lambda h: 0, t, t, scores, vt_tile, m_scr, acc_scr, st_scr, exp_dtype=BF16)
    for h in range(NSA_HEADS):
        o_t = acc_scr[h, :NSA_DV, :] / acc_scr[h, NSA_DV:NSA_DV + 1, :]
        o_ref[0, :, h * NSA_DV:(h + 1) * NSA_DV] = o_t.T


def nsa_selected(q, k3, k_idx, vt, sel):
    b, h, s, dk = q.shape
    n_sel = sel.shape[-1]
    dve = vt.shape[2]
    t = _tile(s, 512)
    assert t % NSA_SEL_LEN == 0
    return pl.pallas_call(
        functools.partial(_nsa_sel_kernel, t=t),
        grid=(b, s // t),
        in_specs=[pl.BlockSpec((1, h, t, dk), lambda b_, i: (b_, 0, i, 0)),
                  pl.BlockSpec((1, 1, s, dk), lambda b_, i: (b_, k_idx, 0, 0)),
                  pl.BlockSpec((1, 1, dve, s), lambda b_, i: (b_, 0, 0, 0)),
                  pl.BlockSpec((1, t, n_sel), lambda b_, i: (b_, i, 0))],
        out_specs=pl.BlockSpec((1, t, h * NSA_DV), lambda b_, i: (b_, i, 0)),
        out_shape=jax.ShapeDtypeStruct((b, s, h * NSA_DV), F32),
        scratch_shapes=[pltpu.VMEM((h, 1, t), F32), pltpu.VMEM((h, dve, t), F32),
                        pltpu.VMEM((_score_slots(1), h, t, t), F32)],
        compiler_params=_cparams(("parallel", "arbitrary")),
        name="nsa_selected",
    )(q, k3, vt, sel)


def _nsa_win_kernel(q_ref, kp_ref, kc_ref, vp_ref, vc_ref, o_ref, *, t):
    i = pl.program_id(1)
    row = lax.broadcasted_iota(jnp.int32, (t, t), 0)
    col = lax.broadcasted_iota(jnp.int32, (t, t), 1)
    keep_prev = (col > row) & (i > 0)
    keep_cur = col <= row
    kp, kc, vp, vc = kp_ref[0, 0], kc_ref[0, 0], vp_ref[0], vc_ref[0]
    for h in range(NSA_HEADS):
        q = q_ref[0, h]
        sp = jnp.where(keep_prev, _dot_nt(q, kp), NEG_INF)
        sc = jnp.where(keep_cur, _dot_nt(q, kc), NEG_INF)
        m = jnp.maximum(jnp.max(sp, axis=-1, keepdims=True), jnp.max(sc, axis=-1, keepdims=True))
        pp = jnp.exp(sp - m)
        pc = jnp.exp(sc - m)
        l = jnp.sum(pp, axis=-1, keepdims=True) + jnp.sum(pc, axis=-1, keepdims=True)
        o = _dot(pp.astype(BF16), vp) + _dot(pc.astype(BF16), vc)
        o_ref[0, :, h * NSA_DV:(h + 1) * NSA_DV] = o / l


def nsa_window(q, k3, k_idx, v3, v_idx):
    b, h, s, dk = q.shape
    t = _tile(s, NSA_WINDOW)
    assert t == NSA_WINDOW, "window kernel needs the query tile to equal the window"
    prev = lambda i: jnp.maximum(i - 1, 0)
    return pl.pallas_call(
        functools.partial(_nsa_win_kernel, t=t),
        grid=(b, s // t),
        in_specs=[pl.BlockSpec((1, h, t, dk), lambda b_, i: (b_, 0, i, 0)),
                  pl.BlockSpec((1, 1, t, dk), lambda b_, i: (b_, k_idx, prev(i), 0)),
                  pl.BlockSpec((1, 1, t, dk), lambda b_, i: (b_, k_idx, i, 0)),
                  pl.BlockSpec((1, t, NSA_DV), lambda b_, i: (b_, prev(i), v_idx)),
                  pl.BlockSpec((1, t, NSA_DV), lambda b_, i: (b_, i, v_idx))],
        out_specs=pl.BlockSpec((1, t, h * NSA_DV), lambda b_, i: (b_, i, 0)),
        out_shape=jax.ShapeDtypeStruct((b, s, h * NSA_DV), F32),
        compiler_params=_cparams(("parallel", "parallel")),
        name="nsa_window",
    )(q, k3, k3, v3, v3)


def _merge_kernel(x_ref, om_ref, oc_ref, os_ref, ow_ref, gn_ref, od_ref, gm_ref, gs_ref, gd_ref,
                  wm_ref, wn_ref, wd_ref, wo_ref, ex_ref, o_ref, onsa_scr, *, n_j):
    j = pl.program_id(1)
    nsa_out = NSA_HEADS * NSA_DV

    @pl.when(j == 0)
    def _():
        g = gn_ref[...]
        hi = g.astype(BF16)
        lo = (g - hi.astype(F32)).astype(BF16)
        ge = _dot(hi, ex_ref[...]) + _dot(lo, ex_ref[...])
        onsa = (ge[:, :nsa_out] * oc_ref[...] + ge[:, nsa_out:2 * nsa_out] * os_ref[...]
                + ge[:, 2 * nsa_out:] * ow_ref[...])
        onsa_scr[...] = onsa.astype(BF16)
        o_ref[...] = jnp.zeros_like(o_ref)

    ym = _dot(om_ref[...], wm_ref[...])
    yn = _dot(onsa_scr[...], wn_ref[...])
    yd = _dot(od_ref[0, 0], wd_ref[0])
    for h in range(1, DIFF_HEADS):
        yd = yd + _dot(od_ref[0, h], wd_ref[h])
    mixed = (gm_ref[...].astype(F32) * ym + gs_ref[...].astype(F32) * yn + gd_ref[...].astype(F32) * yd)
    o_ref[...] += _dot(mixed.astype(BF16), wo_ref[...])

    @pl.when(j == n_j - 1)
    def _():
        o_ref[...] = x_ref[...] + o_ref[...]


def _gate_expand_matrix():
    nsa_out = NSA_HEADS * NSA_DV
    e = np.zeros((LANES, 3 * nsa_out), np.float32)
    for h in range(NSA_HEADS):
        for c in range(3):
            e[h * 3 + c, c * nsa_out + h * NSA_DV:c * nsa_out + (h + 1) * NSA_DV] = 1.0
    return e


def merge(x2d, o_mla, o_c, o_s, o_w, g_nsa, o_diff, g_merge, w_br_mla, w_br_nsa, w_br_diff, w_out, seq):
    m, d = x2d.shape
    tm = _tile(seq, 512)
    ns = seq // tm
    tn = _tile(d, 512)
    n_j = d // tn
    nsa_out = NSA_HEADS * NSA_DV
    mla_out = o_mla.shape[-1]
    ex = jnp.asarray(_gate_expand_matrix(), BF16)
    row = lambda w: pl.BlockSpec((tm, w), lambda i, j: (i, 0))
    return pl.pallas_call(
        functools.partial(_merge_kernel, n_j=n_j),
        grid=(m // tm, n_j),
        in_specs=[row(d), row(mla_out), row(nsa_out), row(nsa_out), row(nsa_out), row(LANES),
                  pl.BlockSpec((1, DIFF_HEADS, tm, DIFF_VD), lambda i, j: (i // ns, 0, i % ns, 0)),
                  pl.BlockSpec((tm, tn), lambda i, j: (i, j)),
                  pl.BlockSpec((tm, tn), lambda i, j: (i, n_j + j)),
                  pl.BlockSpec((tm, tn), lambda i, j: (i, 2 * n_j + j)),
                  pl.BlockSpec((mla_out, tn), lambda i, j: (0, j)),
                  pl.BlockSpec((nsa_out, tn), lambda i, j: (0, j)),
                  pl.BlockSpec((DIFF_HEADS, DIFF_VD, tn), lambda i, j: (0, 0, j)),
                  pl.BlockSpec((tn, d), lambda i, j: (j, 0)),
                  pl.BlockSpec((LANES, 3 * nsa_out), lambda i, j: (0, 0))],
        out_specs=pl.BlockSpec((tm, d), lambda i, j: (i, 0)),
        out_shape=jax.ShapeDtypeStruct((m, d), F32),
        scratch_shapes=[pltpu.VMEM((tm, nsa_out), BF16)],
        compiler_params=_cparams(("parallel", "arbitrary")),
        name="merge",
    )(x2d, o_mla, o_c, o_s, o_w, g_nsa, o_diff, g_merge, g_merge, g_merge,
      w_br_mla, w_br_nsa, w_br_diff, w_out, ex)


def _rope_tables(dim, seq, lead=0):
    inv = ROPE_THETA ** (-jnp.arange(0, dim, 2, dtype=F32) / dim)
    ang = jnp.arange(seq, dtype=F32)[:, None] * inv[None, :]
    cos, sin = jnp.cos(ang), jnp.sin(ang)
    cos = jnp.concatenate([jnp.ones((seq, lead), F32), cos, cos], axis=-1)
    sin = jnp.concatenate([jnp.zeros((seq, lead), F32), sin, sin], axis=-1)
    return cos, sin


def _heads(w, g, dh):
    return w.reshape(w.shape[0], g, dh).transpose(1, 0, 2)


def _rot_cols(w, lead=0):
    half = (w.shape[-1] - lead) // 2
    x1 = w[..., lead:lead + half]
    x2 = w[..., lead + half:]
    return jnp.concatenate([jnp.zeros_like(w[..., :lead]), -x2, x1], axis=-1)


def _pad_halves(w, axis=-1):
    w = jnp.moveaxis(w, axis, -1)
    half = w.shape[-1] // 2
    z = jnp.zeros(w.shape[:-1] + ((-half) % LANES,), w.dtype)
    out = jnp.concatenate([w[..., :half], z, w[..., half:], z], axis=-1)
    return jnp.moveaxis(out, -1, axis)


def _swap_tables(dim, seq, copies=1):
    inv = ROPE_THETA ** (-jnp.arange(0, dim, 2, dtype=F32) / dim)
    ang = jnp.arange(seq, dtype=F32)[:, None] * inv[None, :]
    cos = jnp.tile(jnp.cos(ang), (1, copies))
    sin = jnp.tile(jnp.sin(ang), (1, copies))
    return _pad_halves(jnp.concatenate([cos, cos], axis=-1)), _pad_halves(jnp.concatenate([-sin, sin], axis=-1))


def _pair_heads(w):
    k = w.shape[0]
    half = DIFF_HD // 2
    w = w.reshape(k, DIFF_HEADS, 2, 2, half)
    w = w.transpose(1, 0, 3, 2, 4).reshape(DIFF_HEADS, k, 2 * DIFF_HD)
    return _pad_halves(w)


def _pair_masks():
    half = DIFF_HD // 2
    m = np.zeros((2, 2, 2, half), np.float32)
    for c in range(2):
        m[c, :, c, :] = 1.0
    return _pad_halves(jnp.asarray(m.reshape(2, 2 * DIFF_HD)))


def _col_offsets():
    sizes = (MLA_Q_LORA, MLA_KV_LORA, MLA_ROPE,
             NSA_HEADS * NSA_DK, NSA_DK, NSA_DV, NSA_DK, NSA_DV, NSA_DK, NSA_DV, NSA_HEADS * 3,
             DIFF_HEADS * 2 * DIFF_HD, DIFF_HEADS * 2 * DIFF_HD, DIFF_HEADS * DIFF_VD)
    names = ("c_q", "c_kv", "k_rope", "nsa_q", "nsa_kc", "nsa_vc", "nsa_ks", "nsa_vs", "nsa_kw", "nsa_vw",
             "nsa_g", "d_q", "d_k", "d_v")
    offs = {}
    o = 0
    for nme, sz in zip(names, sizes):
        offs[nme] = (o, o + sz)
        o += sz
    offs["merge"] = (o, None)
    return offs


def _mixers(n2d, batch, seq, layer, w_in, p):
    offs = _col_offsets()
    col = lambda name: w_in[:, offs[name][0]:offs[name][1]]
    bf = lambda a: a.astype(BF16)

    lat = proj_plain(n2d, bf(jnp.concatenate([col("c_q"), col("c_kv")], axis=1)), F32)
    n_q, n_kv = mla_norm(lat, p["mla_q_norm"], p["mla_kv_norm"])
    cos_m, sin_m = _rope_tables(MLA_ROPE, seq, lead=MLA_NOPE)
    w_kr = jnp.concatenate([jnp.zeros((w_in.shape[0], MLA_NOPE), F32), col("k_rope")], axis=1)[None]
    kpe = proj_heads(n2d, bf(w_kr), batch, seq, rope=("weights", bf(_rot_cols(w_kr, MLA_NOPE)), cos_m, sin_m))
    w_uq = _heads(p["mla_w_uq"], MLA_HEADS, MLA_QK)
    q_mla = proj_heads(n_q, bf(w_uq), batch, seq, rope=("weights", bf(_rot_cols(w_uq, MLA_NOPE)), cos_m, sin_m),
                       scale=MLA_QK ** -0.5)
    w_ukv = _heads(p["mla_w_ukv"], MLA_HEADS, MLA_NOPE + MLA_V)
    w_uk = jnp.concatenate([w_ukv[..., :MLA_NOPE], jnp.zeros(w_ukv.shape[:2] + (MLA_ROPE,), F32)], axis=-1)
    k_mla = proj_heads(n_kv, bf(w_uk), batch, seq, add=kpe)
    vt_mla = proj_heads_t(n_kv, bf(w_ukv[..., MLA_NOPE:]), batch, seq)
    o_mla = flash_causal(q_mla, k_mla, vt_mla)

    cos_d, sin_d = _swap_tables(DIFF_HD, seq, copies=2)
    q_d = proj_heads(n2d, bf(_pair_heads(col("d_q"))), batch, seq, rope=("swap", cos_d, sin_d),
                     scale=DIFF_HD ** -0.5, masks=_pair_masks())
    k_d = proj_heads(n2d, bf(_pair_heads(col("d_k"))), batch, seq, rope=("swap", cos_d, sin_d))
    vt_d = proj_heads_t(n2d, bf(_heads(col("d_v"), DIFF_HEADS, DIFF_VD)), batch, seq)
    lambda_init = 0.8 - 0.6 * math.exp(-0.3 * layer)
    o_diff = diff_attention(q_d, k_d, vt_d, p["diff_lam_q1"], p["diff_lam_k1"], p["diff_lam_q2"],
                            p["diff_lam_k2"], p["diff_subln"], lambda_init)

    cos_n, sin_n = _swap_tables(NSA_DK, seq)
    w_nq = _pad_halves(_heads(col("nsa_q"), NSA_HEADS, NSA_DK))
    q_n = proj_heads(n2d, bf(w_nq), batch, seq, rope=("swap", cos_n, sin_n), scale=NSA_DK ** -0.5)
    w_nk = _pad_halves(jnp.stack([col("nsa_kc"), col("nsa_ks"), col("nsa_kw")], axis=0))
    k_n = proj_heads(n2d, bf(w_nk), batch, seq, rope=("swap", cos_n, sin_n))
    dk_pad = w_nk.shape[-1]
    v_n = proj_plain(n2d, bf(jnp.concatenate([col("nsa_vc"), col("nsa_vw")], axis=1)), BF16)
    v_n = v_n.reshape(batch, seq, 2 * NSA_DV)
    vt_slc = proj_heads_t(n2d, bf(col("nsa_vs"))[None], batch, seq)
    w_g = jnp.concatenate([col("nsa_g"), jnp.zeros((w_in.shape[0], LANES - NSA_HEADS * 3), F32)], axis=1)
    g_nsa = proj_plain(n2d, bf(w_g), F32, sigmoid=True)

    ng = seq // NSA_CMP_STRIDE
    w1k = _pad_halves(p["nsa_cmp_k_w1"].reshape(NSA_CMP_LEN, NSA_DK, NSA_CMP_HIDDEN), axis=1)
    kc = nsa_compress(k_n.reshape(batch, 3, ng, NSA_CMP_STRIDE * dk_pad), 0,
                      _pad_halves(p["nsa_cmp_k_pos"]), w1k.reshape(NSA_CMP_LEN * dk_pad, NSA_CMP_HIDDEN),
                      _pad_halves(p["nsa_cmp_k_w2"]))
    vc_tok = v_n[:, :, :NSA_DV].reshape(batch, 1, ng, NSA_CMP_STRIDE * NSA_DV)
    vc = nsa_compress(vc_tok, 0, p["nsa_cmp_v_pos"], p["nsa_cmp_v_w1"], p["nsa_cmp_v_w2"])
    o_c, sel = nsa_cmp_topk(q_n, kc, vc)
    o_s = nsa_selected(q_n, k_n, 1, vt_slc, sel)
    o_w = nsa_window(q_n, k_n, 2, v_n, 1)

    g_merge = proj_plain(n2d, bf(w_in[:, offs["merge"][0]:]), BF16, sigmoid=True)
    m = batch * seq
    return (o_mla.reshape(m, -1), o_c.reshape(m, -1), o_s.reshape(m, -1), o_w.reshape(m, -1), g_nsa,
            o_diff, g_merge)


def kernel(x, ffn1_norm, ffn1_w_in, ffn1_w_out, mix_norm, w_in, mla_q_norm, mla_kv_norm, mla_w_uq, mla_w_ukv, nsa_cmp_k_pos, nsa_cmp_k_w1, nsa_cmp_k_w2, nsa_cmp_v_pos, nsa_cmp_v_w1, nsa_cmp_v_w2, diff_lam_q1, diff_lam_k1, diff_lam_q2, diff_lam_k2, diff_subln, w_br_mla, w_br_nsa, w_br_diff, w_out, ffn2_norm, ffn2_w_in, ffn2_w_out, final_norm):
    batch, seq, d = x.shape
    depth = w_in.shape[0]
    x2d = x.reshape(batch * seq, d)
    bf = lambda a: a.astype(BF16)
    for l in range(depth):
        p = {"mla_q_norm": mla_q_norm[l], "mla_kv_norm": mla_kv_norm[l], "mla_w_uq": mla_w_uq[l],
             "mla_w_ukv": mla_w_ukv[l], "nsa_cmp_k_pos": nsa_cmp_k_pos[l], "nsa_cmp_k_w1": nsa_cmp_k_w1[l],
             "nsa_cmp_k_w2": nsa_cmp_k_w2[l], "nsa_cmp_v_pos": nsa_cmp_v_pos[l], "nsa_cmp_v_w1": nsa_cmp_v_w1[l],
             "nsa_cmp_v_w2": nsa_cmp_v_w2[l], "diff_lam_q1": diff_lam_q1[l], "diff_lam_k1": diff_lam_k1[l],
             "diff_lam_q2": diff_lam_q2[l], "diff_lam_k2": diff_lam_k2[l], "diff_subln": diff_subln[l]}
        x2d = ffn(x2d, ffn1_norm[l], bf(ffn1_w_in[l]), bf(ffn1_w_out[l]))
        n2d = rmsnorm(x2d, mix_norm[l], BF16)
        o_mla, o_c, o_s, o_w, g_nsa, o_diff, g_merge = _mixers(n2d, batch, seq, l, w_in[l], p)
        x2d = merge(x2d, o_mla, o_c, o_s, o_w, g_nsa, o_diff, g_merge, bf(w_br_mla[l]), bf(w_br_nsa[l]),
                    bf(w_br_diff[l]).reshape(DIFF_HEADS, DIFF_VD, d), bf(w_out[l]), seq)
        x2d = ffn(x2d, ffn2_norm[l], bf(ffn2_w_in[l]), bf(ffn2_w_out[l]))
    return rmsnorm(x2d, final_norm, F32).reshape(batch, seq, d)
```

```python
import functools
import math

import numpy as np
import jax
import jax.numpy as jnp
from jax import lax
from jax.experimental import pallas as pl
from jax.experimental.pallas import tpu as pltpu

F32 = jnp.float32
BF16 = jnp.bfloat16

NORM_EPS = 1e-6
ROPE_THETA = 10000.0
NEG_INF = -1e30
REMOVED = -3e38
N_BRANCH = 3

MLA_HEADS = 6
MLA_Q_LORA = 768
MLA_KV_LORA = 512
MLA_NOPE = 128
MLA_ROPE = 64
MLA_V = 128
MLA_QK = MLA_NOPE + MLA_ROPE

NSA_HEADS = 4
NSA_DK = 192
NSA_DV = 128
NSA_CMP_LEN = 32
NSA_CMP_STRIDE = 16
NSA_CMP_HIDDEN = 256
NSA_SEL_LEN = 64
NSA_TOPK = 16
NSA_WINDOW = 512
NSA_FORCE_SCORE = 1e6

DIFF_HEADS = 4
DIFF_HD = 96
DIFF_VD = 2 * DIFF_HD

LANES = 128
ONES_PAD = 16
SCORE_DTYPE = BF16
VMEM_LIMIT_MB = 56


def _cparams(dims, vmem_mb=VMEM_LIMIT_MB):
    return pltpu.CompilerParams(dimension_semantics=dims, vmem_limit_bytes=vmem_mb * 2**20)


def _sigmoid(x):
    return 1.0 / (1.0 + jnp.exp(-x))


def _dot(a, b):
    return jnp.dot(a, b, preferred_element_type=F32)


def _dot_nt(a, b):
    return lax.dot_general(a, b, (((1,), (1,)), ((), ())), preferred_element_type=F32)


def _tile(n, pref):
    t = min(n, pref)
    assert n % t == 0, (n, t)
    return t


def _rmsnorm_kernel(x_ref, w_ref, o_ref):
    x = x_ref[...].astype(F32)
    y = x * lax.rsqrt(jnp.mean(x * x, axis=-1, keepdims=True) + NORM_EPS)
    o_ref[...] = (y * w_ref[...]).astype(o_ref.dtype)


def rmsnorm(x2d, w, out_dtype):
    m, d = x2d.shape
    tm = _tile(m, 1024)
    return pl.pallas_call(
        _rmsnorm_kernel,
        grid=(m // tm,),
        in_specs=[pl.BlockSpec((tm, d), lambda i: (i, 0)), pl.BlockSpec((1, d), lambda i: (0, 0))],
        out_specs=pl.BlockSpec((tm, d), lambda i: (i, 0)),
        out_shape=jax.ShapeDtypeStruct((m, d), out_dtype),
        compiler_params=_cparams(("parallel",)),
        name="rmsnorm",
    )(x2d, w.reshape(1, d).astype(F32))


def _mla_norm_kernel(lat_ref, wq_ref, wkv_ref, nq_ref, nkv_ref):
    lat = lat_ref[...]
    cq = lat[:, :MLA_Q_LORA]
    ckv = lat[:, MLA_Q_LORA:]
    nq = cq * lax.rsqrt(jnp.mean(cq * cq, axis=-1, keepdims=True) + NORM_EPS)
    nkv = ckv * lax.rsqrt(jnp.mean(ckv * ckv, axis=-1, keepdims=True) + NORM_EPS)
    nq_ref[...] = (nq * wq_ref[...]).astype(BF16)
    nkv_ref[...] = (nkv * wkv_ref[...]).astype(BF16)


def mla_norm(lat, wq, wkv):
    m, d = lat.shape
    tm = _tile(m, 1024)
    return pl.pallas_call(
        _mla_norm_kernel,
        grid=(m // tm,),
        in_specs=[pl.BlockSpec((tm, d), lambda i: (i, 0)),
                  pl.BlockSpec((1, MLA_Q_LORA), lambda i: (0, 0)),
                  pl.BlockSpec((1, MLA_KV_LORA), lambda i: (0, 0))],
        out_specs=[pl.BlockSpec((tm, MLA_Q_LORA), lambda i: (i, 0)),
                   pl.BlockSpec((tm, MLA_KV_LORA), lambda i: (i, 0))],
        out_shape=[jax.ShapeDtypeStruct((m, MLA_Q_LORA), BF16), jax.ShapeDtypeStruct((m, MLA_KV_LORA), BF16)],
        compiler_params=_cparams(("parallel",)),
        name="mla_norm",
    )(lat, wq.reshape(1, -1).astype(F32), wkv.reshape(1, -1).astype(F32))


def _ffn_kernel(x_ref, nw_ref, wg_ref, wu_ref, wo_ref, o_ref, n_scr, *, n_f):
    f = pl.program_id(1)

    @pl.when(f == 0)
    def _():
        x = x_ref[...]
        y = x * lax.rsqrt(jnp.mean(x * x, axis=-1, keepdims=True) + NORM_EPS)
        n_scr[...] = (y * nw_ref[...]).astype(BF16)
        o_ref[...] = jnp.zeros_like(o_ref)

    n = n_scr[...]
    g = _dot(n, wg_ref[...])
    u = _dot(n, wu_ref[...])
    h = (g * _sigmoid(g) * u).astype(BF16)
    o_ref[...] += _dot(h, wo_ref[...])

    @pl.when(f == n_f - 1)
    def _():
        o_ref[...] = x_ref[...] + 0.5 * o_ref[...]


def ffn(x2d, norm_w, w_in, w_out):
    m, d = x2d.shape
    f_dim = w_out.shape[0]
    tm = _tile(m, 512)
    tf = _tile(f_dim, 512)
    n_f = f_dim // tf
    return pl.pallas_call(
        functools.partial(_ffn_kernel, n_f=n_f),
        grid=(m // tm, n_f),
        in_specs=[pl.BlockSpec((tm, d), lambda i, f: (i, 0)),
                  pl.BlockSpec((1, d), lambda i, f: (0, 0)),
                  pl.BlockSpec((d, tf), lambda i, f: (0, f)),
                  pl.BlockSpec((d, tf), lambda i, f: (0, f + n_f)),
                  pl.BlockSpec((tf, d), lambda i, f: (f, 0))],
        out_specs=pl.BlockSpec((tm, d), lambda i, f: (i, 0)),
        out_shape=jax.ShapeDtypeStruct((m, d), F32),
        scratch_shapes=[pltpu.VMEM((tm, d), BF16)],
        compiler_params=_cparams(("parallel", "arbitrary")),
        name="ffn",
    )(x2d, norm_w.reshape(1, d).astype(F32), w_in, w_in, w_out)


def _proj_plain_kernel(n_ref, w_ref, o_ref, *, sigmoid):
    y = _dot(n_ref[...], w_ref[...])
    if sigmoid:
        y = _sigmoid(y)
    o_ref[...] = y.astype(o_ref.dtype)


def proj_plain(n2d, w, out_dtype, sigmoid=False):
    m, k = n2d.shape
    n_out = w.shape[1]
    tm = _tile(m, 1024)
    tn = next(t for t in (512, 256, LANES) if n_out % t == 0)
    return pl.pallas_call(
        functools.partial(_proj_plain_kernel, sigmoid=sigmoid),
        grid=(m // tm, n_out // tn),
        in_specs=[pl.BlockSpec((tm, k), lambda i, j: (i, 0)),
                  pl.BlockSpec((k, tn), lambda i, j: (0, j))],
        out_specs=pl.BlockSpec((tm, tn), lambda i, j: (i, j)),
        out_shape=jax.ShapeDtypeStruct((m, n_out), out_dtype),
        compiler_params=_cparams(("parallel", "arbitrary")),
        name="proj_plain",
    )(n2d, w)


def _proj_heads_kernel(*refs, rope, has_add, has_masks, scale):
    n_ref, w_ref = refs[0], refs[1]
    o_ref = refs[-1]
    n = n_ref[...]
    n_copy = o_ref.shape[1] // w_ref.shape[0]
    for h in range(w_ref.shape[0]):
        pos = 2
        y = _dot(n, w_ref[h])
        if rope == "weights":
            wr_ref, cos_ref, sin_ref = refs[pos:pos + 3]
            pos += 3
            y = y * cos_ref[...] + _dot(n, wr_ref[h]) * sin_ref[...]
        elif rope == "swap":
            cos_ref, sin_ref = refs[pos:pos + 2]
            pos += 2
            half = y.shape[1] // 2
            y = y * cos_ref[...] + jnp.concatenate([y[:, half:], y[:, :half]], axis=1) * sin_ref[...]
        if has_add:
            y = y + refs[pos][0, 0].astype(F32)
            pos += 1
        if scale != 1.0:
            y = y * scale
        if has_masks:
            mask_ref = refs[pos]
            for c in range(n_copy):
                o_ref[0, h * n_copy + c] = (y * mask_ref[c:c + 1, :]).astype(o_ref.dtype)
        else:
            o_ref[0, h] = y.astype(o_ref.dtype)


def proj_heads(n2d, w, batch, seq, rope=None, add=None, scale=1.0, masks=None):
    m, k = n2d.shape
    g, _, dh = w.shape
    tm = _tile(seq, 1024)
    ns = seq // tm
    weights = pl.BlockSpec((g, k, dh), lambda b, i: (0, 0, 0))
    in_specs = [pl.BlockSpec((tm, k), lambda b, i: (b * ns + i, 0)), weights]
    args = [n2d, w]
    table = pl.BlockSpec((tm, dh), lambda b, i: (i, 0))
    if rope is not None and rope[0] == "weights":
        in_specs += [weights, table, table]
        args += list(rope[1:])
    elif rope is not None:
        assert rope[0] == "swap" and dh % (2 * LANES) == 0
        in_specs += [table, table]
        args += list(rope[1:])
    if add is not None:
        in_specs.append(pl.BlockSpec((1, 1, tm, dh), lambda b, i: (b, 0, i, 0)))
        args.append(add)
    n_copy = 1
    if masks is not None:
        n_copy = masks.shape[0]
        in_specs.append(pl.BlockSpec((n_copy, dh), lambda b, i: (0, 0)))
        args.append(masks)
    return pl.pallas_call(
        functools.partial(_proj_heads_kernel, rope=None if rope is None else rope[0], has_add=add is not None,
                          has_masks=masks is not None, scale=scale),
        grid=(batch, ns),
        in_specs=in_specs,
        out_specs=pl.BlockSpec((1, g * n_copy, tm, dh), lambda b, i: (b, 0, i, 0)),
        out_shape=jax.ShapeDtypeStruct((batch, g * n_copy, seq, dh), BF16),
        compiler_params=_cparams(("parallel", "parallel")),
        name="proj_heads",
    )(*args)


def _proj_heads_t_kernel(n_ref, wt_ref, o_ref):
    dh = wt_ref.shape[1]
    pad = o_ref.shape[2] - dh
    row = lax.broadcasted_iota(jnp.int32, (pad, o_ref.shape[3]), 0)
    ones_rows = jnp.where(row == 0, 1.0, 0.0).astype(o_ref.dtype)
    n = n_ref[...]
    for h in range(wt_ref.shape[0]):
        o_ref[0, h, :dh, :] = _dot_nt(wt_ref[h], n).astype(o_ref.dtype)
        o_ref[0, h, dh:, :] = ones_rows


def proj_heads_t(n2d, w, batch, seq):
    m, k = n2d.shape
    g, _, dh = w.shape
    tm = _tile(seq, 1024)
    ns = seq // tm
    return pl.pallas_call(
        _proj_heads_t_kernel,
        grid=(batch, ns),
        in_specs=[pl.BlockSpec((tm, k), lambda b, i: (b * ns + i, 0)),
                  pl.BlockSpec((g, dh, k), lambda b, i: (0, 0, 0))],
        out_specs=pl.BlockSpec((1, g, dh + ONES_PAD, tm), lambda b, i: (b, 0, 0, i)),
        out_shape=jax.ShapeDtypeStruct((batch, g, dh + ONES_PAD, seq), BF16),
        compiler_params=_cparams(("parallel", "parallel")),
        name="proj_heads_t",
    )(n2d, jnp.swapaxes(w, 1, 2))


def _softmax_step_t(st, vt, m_scr, acc_scr, exp_dtype):
    m_prev = m_scr[...]
    m_new = jnp.maximum(m_prev, jnp.max(st, axis=0, keepdims=True).astype(F32))
    alpha = jnp.exp(m_prev - m_new)
    pt = jnp.exp((st - m_new.astype(st.dtype)).astype(exp_dtype)).astype(BF16)
    acc_scr[...] = alpha * acc_scr[...] + _dot(vt, pt)
    m_scr[...] = m_new


def _diag_visibility(d, c, tq, tk):
    if d * tk >= (c + 1) * tq:
        return "none"
    if (d + 1) * tk - 1 <= c * tq:
        return "full"
    return "partial"


def _causal_where(st, d, c):
    tk, tq = st.shape
    key = lax.broadcasted_iota(jnp.int32, (tk, tq), 0) + d * tk
    qry = lax.broadcasted_iota(jnp.int32, (tk, tq), 1) + c * tq
    return jnp.where(key <= qry, st, NEG_INF)


def _flash_sweep(n, r, sub_of, tq, tk, scores, vt_tile, m_scr, acc_scr, st_scr, exp_dtype=F32):
    n_chain = m_scr.shape[0]
    for c in range(n_chain):
        m_scr[c] = jnp.full(m_scr.shape[1:], NEG_INF, F32)
        acc_scr[c] = jnp.zeros(acc_scr.shape[1:], F32)
    n_slot = st_scr.shape[0]
    assert r % n_slot == 0
    everyone = (True,) * n_chain
    for c, st in enumerate(scores(0, everyone)):
        st_scr[0, c] = st.astype(st_scr.dtype)

    def body(jj, carry):
        for slot in range(n_slot):
            j = jj * n_slot + slot
            nxt = scores(j + 1, everyone)
            vt = vt_tile(j)
            for c in range(n_chain):
                st = st_scr[slot, c]
                st_scr[(slot + 1) % n_slot, c] = nxt[c].astype(st_scr.dtype)
                _softmax_step_t(st, vt, m_scr.at[c], acc_scr.at[c], exp_dtype)
        return carry

    lax.fori_loop(0, n // n_slot, body, 0)
    for d in range(r):
        vis = [_diag_visibility(d, sub_of(c), tq, tk) for c in range(n_chain)]
        need = tuple(d + 1 < r and _diag_visibility(d + 1, sub_of(c), tq, tk) != "none" for c in range(n_chain))
        nxt = scores(n + d + 1, need) if any(need) else None
        vt = vt_tile(n + d)
        for c in range(n_chain):
            if vis[c] != "none":
                st = st_scr[d % n_slot, c]
                if vis[c] == "partial":
                    st = _causal_where(st.astype(F32), d, sub_of(c)).astype(st_scr.dtype)
            if need[c]:
                st_scr[(d + 1) % n_slot, c] = nxt[c].astype(st_scr.dtype)
            if vis[c] != "none":
                _softmax_step_t(st, vt, m_scr.at[c], acc_scr.at[c], exp_dtype)


def _flash_kernel(q_ref, k_ref, vt_ref, o_ref, m_scr, acc_scr, st_scr, *, n_sub, tq, tk):
    i = pl.program_id(2)
    dv = o_ref.shape[2]
    r = n_sub * tq // tk

    def scores(j, need):
        k = k_ref[0, 0, pl.ds(pl.multiple_of(j * tk, tk), tk), :]
        return [_dot_nt(k, q_ref[0, 0, c * tq:(c + 1) * tq, :]) if need[c] else None for c in range(n_sub)]

    def vt_tile(j):
        return vt_ref[0, 0, :, pl.ds(pl.multiple_of(j * tk, tk), tk)]

    _flash_sweep(i * r, r, lambda c: c, tq, tk, scores, vt_tile, m_scr, acc_scr, st_scr, exp_dtype=BF16)
    for c in range(n_sub):
        o_t = acc_scr[c, :dv, :] / acc_scr[c, dv:dv + 1, :]
        o_ref[0, c * tq:(c + 1) * tq, :] = o_t.T.astype(o_ref.dtype)


def _score_slots(r):
    return 2 if r % 2 == 0 else 1


def _flash_tiles(s):
    tq = _tile(s, 512)
    n_sub = 2 if s % (2 * tq) == 0 else 1
    return n_sub, tq, tq


def flash_causal(q, k, vt):
    b, h, s, dk = q.shape
    dve = vt.shape[2]
    dv = dve - ONES_PAD
    n_sub, tq, tk = _flash_tiles(s)
    t = n_sub * tq
    return pl.pallas_call(
        functools.partial(_flash_kernel, n_sub=n_sub, tq=tq, tk=tk),
        grid=(b, h, s // t),
        in_specs=[pl.BlockSpec((1, 1, t, dk), lambda b_, h_, i: (b_, h_, i, 0)),
                  pl.BlockSpec((1, 1, s, dk), lambda b_, h_, i: (b_, h_, 0, 0)),
                  pl.BlockSpec((1, 1, dve, s), lambda b_, h_, i: (b_, h_, 0, 0))],
        out_specs=pl.BlockSpec((1, t, dv), lambda b_, h_, i: (b_, i, h_)),
        out_shape=jax.ShapeDtypeStruct((b, s, h * dv), BF16),
        scratch_shapes=[pltpu.VMEM((n_sub, 1, tq), F32), pltpu.VMEM((n_sub, dve, tq), F32),
                        pltpu.VMEM((_score_slots(n_sub * tq // tk), n_sub, tk, tq), SCORE_DTYPE)],
        compiler_params=_cparams(("parallel", "parallel", "arbitrary")),
        name="mla_flash",
    )(q, k, vt)


def _diff_kernel(q_ref, k_ref, vt_ref, lq1_ref, lk1_ref, lq2_ref, lk2_ref, sub_ref, o_ref,
                 m_scr, acc_scr, st_scr, *, n_sub, tq, tk, lambda_init):
    i = pl.program_id(2)
    vd = o_ref.shape[3]
    r = n_sub * tq // tk

    def scores(j, need):
        k = k_ref[0, 0, pl.ds(pl.multiple_of(j * tk, tk), tk), :]
        return [_dot_nt(k, q_ref[0, ch // n_sub, (ch % n_sub) * tq:(ch % n_sub + 1) * tq, :]) if need[ch] else None
                for ch in range(2 * n_sub)]

    def vt_tile(j):
        return vt_ref[0, 0, :, pl.ds(pl.multiple_of(j * tk, tk), tk)]

    _flash_sweep(i * r, r, lambda ch: ch % n_sub, tq, tk, scores, vt_tile, m_scr, acc_scr, st_scr, exp_dtype=BF16)

    lam = (jnp.exp(jnp.sum(lq1_ref[...] * lk1_ref[...], axis=-1, keepdims=True))
           - jnp.exp(jnp.sum(lq2_ref[...] * lk2_ref[...], axis=-1, keepdims=True)) + lambda_init)
    lane_pad = (-vd) % LANES
    for c in range(n_sub):
        o_t = (acc_scr[c, :vd, :] / acc_scr[c, vd:vd + 1, :]
               - lam * (acc_scr[n_sub + c, :vd, :] / acc_scr[n_sub + c, vd:vd + 1, :]))
        o_t = o_t * lax.rsqrt(jnp.mean(o_t * o_t, axis=0, keepdims=True) + NORM_EPS)
        o_t = jnp.concatenate([o_t, jnp.zeros((lane_pad, tq), F32)], axis=0)
        o_ref[0, 0, c * tq:(c + 1) * tq, :] = (
            o_t.T[:, :vd] * sub_ref[...] * (1.0 - lambda_init)).astype(o_ref.dtype)


def diff_attention(q, k, vt, lq1, lk1, lq2, lk2, subln, lambda_init):
    b, h2, s, dk = q.shape
    h = h2 // 2
    hd = DIFF_HD
    vde = vt.shape[2]
    vd = vde - ONES_PAD
    n_sub, tq, tk = _flash_tiles(s)
    t = n_sub * tq
    vec = lambda a: a.reshape(1, -1).astype(F32)
    small = lambda n: pl.BlockSpec((1, n), lambda b_, h_, i: (0, 0))
    return pl.pallas_call(
        functools.partial(_diff_kernel, n_sub=n_sub, tq=tq, tk=tk, lambda_init=lambda_init),
        grid=(b, h, s // t),
        in_specs=[pl.BlockSpec((1, 2, t, dk), lambda b_, h_, i: (b_, h_, i, 0)),
                  pl.BlockSpec((1, 1, s, dk), lambda b_, h_, i: (b_, h_, 0, 0)),
                  pl.BlockSpec((1, 1, vde, s), lambda b_, h_, i: (b_, h_, 0, 0)),
                  small(hd), small(hd), small(hd), small(hd), small(vd)],
        out_specs=pl.BlockSpec((1, 1, t, vd), lambda b_, h_, i: (b_, h_, i, 0)),
        out_shape=jax.ShapeDtypeStruct((b, h, s, vd), BF16),
        scratch_shapes=[pltpu.VMEM((2 * n_sub, 1, tq), F32), pltpu.VMEM((2 * n_sub, vde, tq), F32),
                        pltpu.VMEM((_score_slots(n_sub * tq // tk), 2 * n_sub, tk, tq), SCORE_DTYPE)],
        compiler_params=_cparams(("parallel", "parallel", "arbitrary")),
        name="diff_flash",
    )(q, k, vt, vec(lq1), vec(lk1), vec(lq2), vec(lk2), vec(subln))


def _compress_kernel(a_ref, pa_ref, pb_ref, w1a_ref, w1b_ref, w2_ref, o_ref):
    ng = a_ref.shape[2]
    rows = min(ng, 256)
    us, vs = [], []
    for r0 in range(0, ng, rows):
        a = a_ref[0, 0, r0:r0 + rows, :].astype(F32)
        us.append(_dot((a + pa_ref[...]).astype(BF16), w1a_ref[...]))
        vs.append(_dot((a + pb_ref[...]).astype(BF16), w1b_ref[...]))
    u = jnp.concatenate(us, axis=0)
    v = jnp.concatenate(vs, axis=0)
    hdn = u + pltpu.roll(v, ng - 1, 0)
    hdn = hdn * _sigmoid(hdn)
    o_ref[0] = _dot(hdn.astype(BF16), w2_ref[...]).astype(o_ref.dtype)


def nsa_compress(tok, g_idx, pos, w1, w2):
    assert NSA_CMP_LEN == 2 * NSA_CMP_STRIDE
    b, _, ng, wd = tok.shape
    d = wd // NSA_CMP_STRIDE
    d_out = w2.shape[1]
    pos_flat = pos.astype(F32).reshape(1, NSA_CMP_LEN * d)
    w1 = w1.astype(BF16)
    full = lambda shape: pl.BlockSpec(shape, lambda b_: (0,) * len(shape))
    return pl.pallas_call(
        _compress_kernel,
        grid=(b,),
        in_specs=[pl.BlockSpec((1, 1, ng, wd), lambda b_: (b_, g_idx, 0, 0)),
                  full((1, wd)), full((1, wd)), full((wd, NSA_CMP_HIDDEN)), full((wd, NSA_CMP_HIDDEN)),
                  full((NSA_CMP_HIDDEN, d_out))],
        out_specs=pl.BlockSpec((1, ng, d_out), lambda b_: (b_, 0, 0)),
        out_shape=jax.ShapeDtypeStruct((b, ng, d_out), BF16),
        compiler_params=_cparams(("parallel",)),
        name="nsa_compress",
    )(tok, pos_flat[:, :wd], pos_flat[:, wd:], w1[:wd], w1[wd:], w2.astype(BF16))


def _split3(x):
    hi = x.astype(BF16)
    r = x - hi.astype(F32)
    mid = r.astype(BF16)
    lo = (r - mid.astype(F32)).astype(BF16)
    return hi, mid, lo


def _cmp_topk_kernel(q_ref, kc_ref, vc_ref, wsel_ref, oc_ref, sel_ref, *, tq, k_top):
    i = pl.program_id(1)
    qs = i * tq
    nc = kc_ref.shape[1]
    n_sel = wsel_ref.shape[1]
    kc = kc_ref[0]
    vc = vc_ref[0]
    qpos_c = qs + lax.broadcasted_iota(jnp.int32, (tq, nc), 0)
    cend = lax.broadcasted_iota(jnp.int32, (tq, nc), 1) * NSA_CMP_STRIDE + (NSA_CMP_LEN - 1)
    valid_c = cend <= qpos_c
    imp = jnp.zeros((tq, nc), F32)
    for h in range(NSA_HEADS):
        s = jnp.where(valid_c, _dot_nt(q_ref[0, h], kc), NEG_INF)
        e = jnp.exp(s - jnp.max(s, axis=-1, keepdims=True))
        p = jnp.where(valid_c, e * (1.0 / jnp.sum(e, axis=-1, keepdims=True)), 0.0)
        oc_ref[0, :, h * NSA_DV:(h + 1) * NSA_DV] = _dot(p.astype(BF16), vc)
        imp = imp + p

    wsel = wsel_ref[...]
    hi, mid, lo = _split3(imp)
    imp_sel = _dot(hi, wsel) + _dot(mid, wsel) + _dot(lo, wsel)

    blk = lax.broadcasted_iota(jnp.int32, (tq, n_sel), 1)
    qpos = qs + lax.broadcasted_iota(jnp.int32, (tq, n_sel), 0)
    cur = lax.shift_right_arithmetic(qpos, int(math.log2(NSA_SEL_LEN)))
    forced = (blk == 0) | (blk == cur) | (blk == cur - 1)
    valid_s = blk * NSA_SEL_LEN <= qpos
    score = jnp.where(valid_s, jnp.where(forced, NSA_FORCE_SCORE, imp_sel), NEG_INF)
    blk_f = blk.astype(F32)
    sel = jnp.zeros((tq, n_sel), F32)
    for _ in range(k_top):
        mx = jnp.max(score, axis=-1, keepdims=True)
        first = jnp.min(jnp.where(score == mx, blk_f, float(n_sel)), axis=-1, keepdims=True)
        hit = blk_f == first
        sel = jnp.where(hit, 1.0, sel)
        score = jnp.where(hit, REMOVED, score)
    sel_ref[0] = sel.astype(sel_ref.dtype)


def _sel_weight_matrix(n_pad, n_sel):
    r_c = NSA_CMP_LEN // NSA_CMP_STRIDE
    ratio = NSA_SEL_LEN // NSA_CMP_STRIDE
    overlap_w = [max(0, min(o * NSA_CMP_STRIDE + NSA_CMP_LEN, NSA_SEL_LEN) - max(o * NSA_CMP_STRIDE, 0))
                 / NSA_CMP_STRIDE for o in range(-(r_c - 1), ratio)]
    w = np.zeros((n_pad, n_sel), np.float32)
    for n in range(n_sel):
        for u, w_u in enumerate(overlap_w):
            c = ratio * n + u - (r_c - 1)
            if 0 <= c < n_pad:
                w[c, n] = w_u
    return w


def nsa_cmp_topk(q, kc, vc):
    b, h, s, dk = q.shape
    nc = kc.shape[1]
    n_sel = s // NSA_SEL_LEN
    assert NSA_SEL_LEN & (NSA_SEL_LEN - 1) == 0
    tq = _tile(s, 256)
    wsel = jnp.asarray(_sel_weight_matrix(nc, n_sel), BF16)
    return pl.pallas_call(
        functools.partial(_cmp_topk_kernel, tq=tq, k_top=min(NSA_TOPK, n_sel)),
        grid=(b, s // tq),
        in_specs=[pl.BlockSpec((1, h, tq, dk), lambda b_, i: (b_, 0, i, 0)),
                  pl.BlockSpec((1, nc, dk), lambda b_, i: (b_, 0, 0)),
                  pl.BlockSpec((1, nc, NSA_DV), lambda b_, i: (b_, 0, 0)),
                  pl.BlockSpec((nc, n_sel), lambda b_, i: (0, 0))],
        out_specs=[pl.BlockSpec((1, tq, h * NSA_DV), lambda b_, i: (b_, i, 0)),
                   pl.BlockSpec((1, tq, n_sel), lambda b_, i: (b_, i, 0))],
        out_shape=[jax.ShapeDtypeStruct((b, s, h * NSA_DV), F32), jax.ShapeDtypeStruct((b, s, n_sel), BF16)],
        compiler_params=_cparams(("parallel", "parallel")),
        name="nsa_cmp_topk",
    )(q, kc, vc, wsel)


def _nsa_sel_kernel(q_ref, k_ref, vt_ref, sel_ref, o_ref, m_scr, acc_scr, st_scr, *, n_sub, tq, tk):
    i = pl.program_id(1)
    n_sel = sel_ref.shape[2]
    bpt = tk // NSA_SEL_LEN
    shift = int(math.log2(NSA_SEL_LEN))
    r = n_sub * tq // tk

    def scores(j, need):
        k = k_ref[0, 0, pl.ds(pl.multiple_of(j * tk, tk), tk), :]
        blk_key = lax.shift_right_arithmetic(lax.broadcasted_iota(jnp.int32, (tk, n_sel), 0), shift) + j * bpt
        blk_col = lax.broadcasted_iota(jnp.int32, (tk, n_sel), 1)
        expand_t = jnp.where(blk_key == blk_col, 1.0, 0.0).astype(BF16)
        out = []
        for c in range(n_sub):
            rows = slice(c * tq, (c + 1) * tq)
            if not any(need[c * NSA_HEADS:(c + 1) * NSA_HEADS]):
                out += [None] * NSA_HEADS
                continue
            keep_t = _dot_nt(expand_t, sel_ref[0, rows, :]) > 0.5
            out += [jnp.where(keep_t, _dot_nt(k, q_ref[0, h, rows, :]), NEG_INF) if need[c * NSA_HEADS + h] else None
                    for h in range(NSA_HEADS)]
        return out

    def vt_tile(j):
        return vt_ref[0, 0, :, pl.ds(pl.multiple_of(j * tk, tk), tk)]

    _flash_sweep(i * r, r, lambda ch: ch // NSA_HEADS, tq, tk, scores, vt_tile, m_scr, acc_scr, st_scr,
                 exp_dtype=BF16)
    for ch in range(n_sub * NSA_HEADS):
        c, h = divmod(ch, NSA_HEADS)
        o_t = acc_scr[ch, :NSA_DV, :] / acc_scr[ch, NSA_DV:NSA_DV + 1, :]
        o_ref[0, c * tq:(c + 1) * tq, h * NSA_DV:(h + 1) * NSA_DV] = o_t.T


def _nsa_sel_tiles(s):
    t = _tile(s, 512)
    return 1, t, t


def nsa_selected(q, k3, k_idx, vt, sel):
    b, h, s, dk = q.shape
    n_sel = sel.shape[-1]
    dve = vt.shape[2]
    n_sub, tq, tk = _nsa_sel_tiles(s)
    t = n_sub * tq
    n_chain = n_sub * h
    assert tk % NSA_SEL_LEN == 0 and t % tk == 0
    return pl.pallas_call(
        functools.partial(_nsa_sel_kernel, n_sub=n_sub, tq=tq, tk=tk),
        grid=(b, s // t),
        in_specs=[pl.BlockSpec((1, h, t, dk), lambda b_, i: (b_, 0, i, 0)),
                  pl.BlockSpec((1, 1, s, dk), lambda b_, i: (b_, k_idx, 0, 0)),
                  pl.BlockSpec((1, 1, dve, s), lambda b_, i: (b_, 0, 0, 0)),
                  pl.BlockSpec((1, t, n_sel), lambda b_, i: (b_, i, 0))],
        out_specs=pl.BlockSpec((1, t, h * NSA_DV), lambda b_, i: (b_, i, 0)),
        out_shape=jax.ShapeDtypeStruct((b, s, h * NSA_DV), F32),
        scratch_shapes=[pltpu.VMEM((n_chain, 1, tq), F32), pltpu.VMEM((n_chain, dve, tq), F32),
                        pltpu.VMEM((_score_slots(t // tk), n_chain, tk, tq), SCORE_DTYPE)],
        compiler_params=_cparams(("parallel", "arbitrary")),
        name="nsa_selected",
    )(q, k3, vt, sel)


def _nsa_win_kernel(q_ref, kp_ref, kc_ref, vp_ref, vc_ref, o_ref, *, t):
    i = pl.program_id(1)
    row = lax.broadcasted_iota(jnp.int32, (t, t), 0)
    col = lax.broadcasted_iota(jnp.int32, (t, t), 1)
    keep_prev = (col > row) & (i > 0)
    keep_cur = col <= row
    kp, kc, vp, vc = kp_ref[0, 0], kc_ref[0, 0], vp_ref[0], vc_ref[0]
    for h in range(NSA_HEADS):
        q = q_ref[0, h]
        sp = jnp.where(keep_prev, _dot_nt(q, kp), NEG_INF)
        sc = jnp.where(keep_cur, _dot_nt(q, kc), NEG_INF)
        m = jnp.maximum(jnp.max(sp, axis=-1, keepdims=True), jnp.max(sc, axis=-1, keepdims=True))
        pp = jnp.exp(sp - m)
        pc = jnp.exp(sc - m)
        l = jnp.sum(pp, axis=-1, keepdims=True) + jnp.sum(pc, axis=-1, keepdims=True)
        o = _dot(pp.astype(BF16), vp) + _dot(pc.astype(BF16), vc)
        o_ref[0, :, h * NSA_DV:(h + 1) * NSA_DV] = o / l


def nsa_window(q, k3, k_idx, v3, v_idx):
    b, h, s, dk = q.shape
    t = _tile(s, NSA_WINDOW)
    assert t == NSA_WINDOW, "window kernel needs the query tile to equal the window"
    prev = lambda i: jnp.maximum(i - 1, 0)
    return pl.pallas_call(
        functools.partial(_nsa_win_kernel, t=t),
        grid=(b, s // t),
        in_specs=[pl.BlockSpec((1, h, t, dk), lambda b_, i: (b_, 0, i, 0)),
                  pl.BlockSpec((1, 1, t, dk), lambda b_, i: (b_, k_idx, prev(i), 0)),
                  pl.BlockSpec((1, 1, t, dk), lambda b_, i: (b_, k_idx, i, 0)),
                  pl.BlockSpec((1, t, NSA_DV), lambda b_, i: (b_, prev(i), v_idx)),
                  pl.BlockSpec((1, t, NSA_DV), lambda b_, i: (b_, i, v_idx))],
        out_specs=pl.BlockSpec((1, t, h * NSA_DV), lambda b_, i: (b_, i, 0)),
        out_shape=jax.ShapeDtypeStruct((b, s, h * NSA_DV), F32),
        compiler_params=_cparams(("parallel", "parallel")),
        name="nsa_window",
    )(q, k3, k3, v3, v3)


def _merge_kernel(x_ref, om_ref, oc_ref, os_ref, ow_ref, gn_ref, od_ref, gm_ref, gs_ref, gd_ref,
                  wm_ref, wn_ref, wd_ref, wo_ref, ex_ref, o_ref, onsa_scr, *, n_j):
    j = pl.program_id(1)
    nsa_out = NSA_HEADS * NSA_DV

    @pl.when(j == 0)
    def _():
        g = gn_ref[...]
        hi = g.astype(BF16)
        lo = (g - hi.astype(F32)).astype(BF16)
        ge = _dot(hi, ex_ref[...]) + _dot(lo, ex_ref[...])
        onsa = (ge[:, :nsa_out] * oc_ref[...] + ge[:, nsa_out:2 * nsa_out] * os_ref[...]
                + ge[:, 2 * nsa_out:] * ow_ref[...])
        onsa_scr[...] = onsa.astype(BF16)
        o_ref[...] = jnp.zeros_like(o_ref)

    ym = _dot(om_ref[...], wm_ref[...])
    yn = _dot(onsa_scr[...], wn_ref[...])
    yd = _dot(od_ref[0, 0], wd_ref[0])
    for h in range(1, DIFF_HEADS):
        yd = yd + _dot(od_ref[0, h], wd_ref[h])
    mixed = (gm_ref[...].astype(F32) * ym + gs_ref[...].astype(F32) * yn + gd_ref[...].astype(F32) * yd)
    o_ref[...] += _dot(mixed.astype(BF16), wo_ref[...])

    @pl.when(j == n_j - 1)
    def _():
        o_ref[...] = x_ref[...] + o_ref[...]


def _gate_expand_matrix():
    nsa_out = NSA_HEADS * NSA_DV
    e = np.zeros((LANES, 3 * nsa_out), np.float32)
    for h in range(NSA_HEADS):
        for c in range(3):
            e[h * 3 + c, c * nsa_out + h * NSA_DV:c * nsa_out + (h + 1) * NSA_DV] = 1.0
    return e


def merge(x2d, o_mla, o_c, o_s, o_w, g_nsa, o_diff, g_merge, w_br_mla, w_br_nsa, w_br_diff, w_out, seq):
    m, d = x2d.shape
    tm = _tile(seq, 512)
    ns = seq // tm
    tn = _tile(d, 512)
    n_j = d // tn
    nsa_out = NSA_HEADS * NSA_DV
    mla_out = o_mla.shape[-1]
    ex = jnp.asarray(_gate_expand_matrix(), BF16)
    row = lambda w: pl.BlockSpec((tm, w), lambda i, j: (i, 0))
    return pl.pallas_call(
        functools.partial(_merge_kernel, n_j=n_j),
        grid=(m // tm, n_j),
        in_specs=[row(d), row(mla_out), row(nsa_out), row(nsa_out), row(nsa_out), row(LANES),
                  pl.BlockSpec((1, DIFF_HEADS, tm, DIFF_VD), lambda i, j: (i // ns, 0, i % ns, 0)),
                  pl.BlockSpec((tm, tn), lambda i, j: (i, j)),
                  pl.BlockSpec((tm, tn), lambda i, j: (i, n_j + j)),
                  pl.BlockSpec((tm, tn), lambda i, j: (i, 2 * n_j + j)),
                  pl.BlockSpec((mla_out, tn), lambda i, j: (0, j)),
                  pl.BlockSpec((nsa_out, tn), lambda i, j: (0, j)),
                  pl.BlockSpec((DIFF_HEADS, DIFF_VD, tn), lambda i, j: (0, 0, j)),
                  pl.BlockSpec((tn, d), lambda i, j: (j, 0)),
                  pl.BlockSpec((LANES, 3 * nsa_out), lambda i, j: (0, 0))],
        out_specs=pl.BlockSpec((tm, d), lambda i, j: (i, 0)),
        out_shape=jax.ShapeDtypeStruct((m, d), F32),
        scratch_shapes=[pltpu.VMEM((tm, nsa_out), BF16)],
        compiler_params=_cparams(("parallel", "arbitrary")),
        name="merge",
    )(x2d, o_mla, o_c, o_s, o_w, g_nsa, o_diff, g_merge, g_merge, g_merge,
      w_br_mla, w_br_nsa, w_br_diff, w_out, ex)


def _rope_tables(dim, seq, lead=0):
    inv = ROPE_THETA ** (-jnp.arange(0, dim, 2, dtype=F32) / dim)
    ang = jnp.arange(seq, dtype=F32)[:, None] * inv[None, :]
    cos, sin = jnp.cos(ang), jnp.sin(ang)
    cos = jnp.concatenate([jnp.ones((seq, lead), F32), cos, cos], axis=-1)
    sin = jnp.concatenate([jnp.zeros((seq, lead), F32), sin, sin], axis=-1)
    return cos, sin


def _heads(w, g, dh):
    return w.reshape(w.shape[0], g, dh).transpose(1, 0, 2)


def _rot_cols(w, lead=0):
    half = (w.shape[-1] - lead) // 2
    x1 = w[..., lead:lead + half]
    x2 = w[..., lead + half:]
    return jnp.concatenate([jnp.zeros_like(w[..., :lead]), -x2, x1], axis=-1)


def _pad_halves(w, axis=-1):
    w = jnp.moveaxis(w, axis, -1)
    half = w.shape[-1] // 2
    z = jnp.zeros(w.shape[:-1] + ((-half) % LANES,), w.dtype)
    out = jnp.concatenate([w[..., :half], z, w[..., half:], z], axis=-1)
    return jnp.moveaxis(out, -1, axis)


def _swap_tables(dim, seq, copies=1):
    inv = ROPE_THETA ** (-jnp.arange(0, dim, 2, dtype=F32) / dim)
    ang = jnp.arange(seq, dtype=F32)[:, None] * inv[None, :]
    cos = jnp.tile(jnp.cos(ang), (1, copies))
    sin = jnp.tile(jnp.sin(ang), (1, copies))
    return _pad_halves(jnp.concatenate([cos, cos], axis=-1)), _pad_halves(jnp.concatenate([-sin, sin], axis=-1))


def _pair_heads(w):
    k = w.shape[0]
    half = DIFF_HD // 2
    w = w.reshape(k, DIFF_HEADS, 2, 2, half)
    w = w.transpose(1, 0, 3, 2, 4).reshape(DIFF_HEADS, k, 2 * DIFF_HD)
    return _pad_halves(w)


def _pair_masks():
    half = DIFF_HD // 2
    m = np.zeros((2, 2, 2, half), np.float32)
    for c in range(2):
        m[c, :, c, :] = 1.0
    return _pad_halves(jnp.asarray(m.reshape(2, 2 * DIFF_HD)))


def _col_offsets():
    sizes = (MLA_Q_LORA, MLA_KV_LORA, MLA_ROPE,
             NSA_HEADS * NSA_DK, NSA_DK, NSA_DV, NSA_DK, NSA_DV, NSA_DK, NSA_DV, NSA_HEADS * 3,
             DIFF_HEADS * 2 * DIFF_HD, DIFF_HEADS * 2 * DIFF_HD, DIFF_HEADS * DIFF_VD)
    names = ("c_q", "c_kv", "k_rope", "nsa_q", "nsa_kc", "nsa_vc", "nsa_ks", "nsa_vs", "nsa_kw", "nsa_vw",
             "nsa_g", "d_q", "d_k", "d_v")
    offs = {}
    o = 0
    for nme, sz in zip(names, sizes):
        offs[nme] = (o, o + sz)
        o += sz
    offs["merge"] = (o, None)
    return offs


def _mixers(n2d, batch, seq, layer, w_in, p):
    offs = _col_offsets()
    col = lambda name: w_in[:, offs[name][0]:offs[name][1]]
    bf = lambda a: a.astype(BF16)

    lat = proj_plain(n2d, bf(jnp.concatenate([col("c_q"), col("c_kv")], axis=1)), F32)
    n_q, n_kv = mla_norm(lat, p["mla_q_norm"], p["mla_kv_norm"])
    cos_m, sin_m = _rope_tables(MLA_ROPE, seq, lead=MLA_NOPE)
    w_kr = jnp.concatenate([jnp.zeros((w_in.shape[0], MLA_NOPE), F32), col("k_rope")], axis=1)[None]
    kpe = proj_heads(n2d, bf(w_kr), batch, seq, rope=("weights", bf(_rot_cols(w_kr, MLA_NOPE)), cos_m, sin_m))
    w_uq = _heads(p["mla_w_uq"], MLA_HEADS, MLA_QK)
    q_mla = proj_heads(n_q, bf(w_uq), batch, seq, rope=("weights", bf(_rot_cols(w_uq, MLA_NOPE)), cos_m, sin_m),
                       scale=MLA_QK ** -0.5)
    w_ukv = _heads(p["mla_w_ukv"], MLA_HEADS, MLA_NOPE + MLA_V)
    w_uk = jnp.concatenate([w_ukv[..., :MLA_NOPE], jnp.zeros(w_ukv.shape[:2] + (MLA_ROPE,), F32)], axis=-1)
    k_mla = proj_heads(n_kv, bf(w_uk), batch, seq, add=kpe)
    vt_mla = proj_heads_t(n_kv, bf(w_ukv[..., MLA_NOPE:]), batch, seq)
    o_mla = flash_causal(q_mla, k_mla, vt_mla)

    cos_d, sin_d = _swap_tables(DIFF_HD, seq, copies=2)
    q_d = proj_heads(n2d, bf(_pair_heads(col("d_q"))), batch, seq, rope=("swap", cos_d, sin_d),
                     scale=DIFF_HD ** -0.5, masks=_pair_masks())
    k_d = proj_heads(n2d, bf(_pair_heads(col("d_k"))), batch, seq, rope=("swap", cos_d, sin_d))
    vt_d = proj_heads_t(n2d, bf(_heads(col("d_v"), DIFF_HEADS, DIFF_VD)), batch, seq)
    lambda_init = 0.8 - 0.6 * math.exp(-0.3 * layer)
    o_diff = diff_attention(q_d, k_d, vt_d, p["diff_lam_q1"], p["diff_lam_k1"], p["diff_lam_q2"],
                            p["diff_lam_k2"], p["diff_subln"], lambda_init)

    cos_n, sin_n = _swap_tables(NSA_DK, seq)
    w_nq = _pad_halves(_heads(col("nsa_q"), NSA_HEADS, NSA_DK))
    q_n = proj_heads(n2d, bf(w_nq), batch, seq, rope=("swap", cos_n, sin_n), scale=NSA_DK ** -0.5)
    w_nk = _pad_halves(jnp.stack([col("nsa_kc"), col("nsa_ks"), col("nsa_kw")], axis=0))
    k_n = proj_heads(n2d, bf(w_nk), batch, seq, rope=("swap", cos_n, sin_n))
    dk_pad = w_nk.shape[-1]
    v_n = proj_plain(n2d, bf(jnp.concatenate([col("nsa_vc"), col("nsa_vw")], axis=1)), BF16)
    v_n = v_n.reshape(batch, seq, 2 * NSA_DV)
    vt_slc = proj_heads_t(n2d, bf(col("nsa_vs"))[None], batch, seq)
    w_g = jnp.concatenate([col("nsa_g"), jnp.zeros((w_in.shape[0], LANES - NSA_HEADS * 3), F32)], axis=1)
    g_nsa = proj_plain(n2d, bf(w_g), F32, sigmoid=True)

    ng = seq // NSA_CMP_STRIDE
    w1k = _pad_halves(p["nsa_cmp_k_w1"].reshape(NSA_CMP_LEN, NSA_DK, NSA_CMP_HIDDEN), axis=1)
    kc = nsa_compress(k_n.reshape(batch, 3, ng, NSA_CMP_STRIDE * dk_pad), 0,
                      _pad_halves(p["nsa_cmp_k_pos"]), w1k.reshape(NSA_CMP_LEN * dk_pad, NSA_CMP_HIDDEN),
                      _pad_halves(p["nsa_cmp_k_w2"]))
    vc_tok = v_n[:, :, :NSA_DV].reshape(batch, 1, ng, NSA_CMP_STRIDE * NSA_DV)
    vc = nsa_compress(vc_tok, 0, p["nsa_cmp_v_pos"], p["nsa_cmp_v_w1"], p["nsa_cmp_v_w2"])
    o_c, sel = nsa_cmp_topk(q_n, kc, vc)
    o_s = nsa_selected(q_n, k_n, 1, vt_slc, sel)
    o_w = nsa_window(q_n, k_n, 2, v_n, 1)

    g_merge = proj_plain(n2d, bf(w_in[:, offs["merge"][0]:]), BF16, sigmoid=True)
    m = batch * seq
    return (o_mla.reshape(m, -1), o_c.reshape(m, -1), o_s.reshape(m, -1), o_w.reshape(m, -1), g_nsa,
            o_diff, g_merge)


def kernel(x, ffn1_norm, ffn1_w_in, ffn1_w_out, mix_norm, w_in, mla_q_norm, mla_kv_norm, mla_w_uq, mla_w_ukv, nsa_cmp_k_pos, nsa_cmp_k_w1, nsa_cmp_k_w2, nsa_cmp_v_pos, nsa_cmp_v_w1, nsa_cmp_v_w2, diff_lam_q1, diff_lam_k1, diff_lam_q2, diff_lam_k2, diff_subln, w_br_mla, w_br_nsa, w_br_diff, w_out, ffn2_norm, ffn2_w_in, ffn2_w_out, final_norm):
    batch, seq, d = x.shape
    depth = w_in.shape[0]
    x2d = x.reshape(batch * seq, d)
    bf = lambda a: a.astype(BF16)
    for l in range(depth):
        p = {"mla_q_norm": mla_q_norm[l], "mla_kv_norm": mla_kv_norm[l], "mla_w_uq": mla_w_uq[l],
             "mla_w_ukv": mla_w_ukv[l], "nsa_cmp_k_pos": nsa_cmp_k_pos[l], "nsa_cmp_k_w1": nsa_cmp_k_w1[l],
             "nsa_cmp_k_w2": nsa_cmp_k_w2[l], "nsa_cmp_v_pos": nsa_cmp_v_pos[l], "nsa_cmp_v_w1": nsa_cmp_v_w1[l],
             "nsa_cmp_v_w2": nsa_cmp_v_w2[l], "diff_lam_q1": diff_lam_q1[l], "diff_lam_k1": diff_lam_k1[l],
             "diff_lam_q2": diff_lam_q2[l], "diff_lam_k2": diff_lam_k2[l], "diff_subln": diff_subln[l]}
        x2d = ffn(x2d, ffn1_norm[l], bf(ffn1_w_in[l]), bf(ffn1_w_out[l]))
        n2d = rmsnorm(x2d, mix_norm[l], BF16)
        o_mla, o_c, o_s, o_w, g_nsa, o_diff, g_merge = _mixers(n2d, batch, seq, l, w_in[l], p)
        x2d = merge(x2d, o_mla, o_c, o_s, o_w, g_nsa, o_diff, g_merge, bf(w_br_mla[l]), bf(w_br_nsa[l]),
                    bf(w_br_diff[l]).reshape(DIFF_HEADS, DIFF_VD, d), bf(w_out[l]), seq)
        x2d = ffn(x2d, ffn2_norm[l], bf(ffn2_w_in[l]), bf(ffn2_w_out[l]))
    return rmsnorm(x2d, final_norm, F32).reshape(batch, seq, d)
```

```python
import functools
import math

import numpy as np
import jax
import jax.numpy as jnp
from jax import lax
from jax.experimental import pallas as pl
from jax.experimental.pallas import tpu as pltpu

F32 = jnp.float32
BF16 = jnp.bfloat16

NORM_EPS = 1e-6
ROPE_THETA = 10000.0
NEG_INF = -1e30
REMOVED = -3e38
N_BRANCH = 3

MLA_HEADS = 6
MLA_Q_LORA = 768
MLA_KV_LORA = 512
MLA_NOPE = 128
MLA_ROPE = 64
MLA_V = 128
MLA_QK = MLA_NOPE + MLA_ROPE

NSA_HEADS = 4
NSA_DK = 192
NSA_DV = 128
NSA_CMP_LEN = 32
NSA_CMP_STRIDE = 16
NSA_CMP_HIDDEN = 256
NSA_SEL_LEN = 64
NSA_TOPK = 16
NSA_WINDOW = 512
NSA_FORCE_SCORE = 1e6

DIFF_HEADS = 4
DIFF_HD = 96
DIFF_VD = 2 * DIFF_HD

LANES = 128
ONES_PAD = 16
SCORE_DTYPE = BF16
VMEM_LIMIT_MB = 56


def _cparams(dims, vmem_mb=VMEM_LIMIT_MB):
    return pltpu.CompilerParams(dimension_semantics=dims, vmem_limit_bytes=vmem_mb * 2**20)


def _sigmoid(x):
    return 1.0 / (1.0 + jnp.exp(-x))


def _dot(a, b):
    return jnp.dot(a, b, preferred_element_type=F32)


def _dot_nt(a, b):
    return lax.dot_general(a, b, (((1,), (1,)), ((), ())), preferred_element_type=F32)


def _tile(n, pref):
    t = min(n, pref)
    assert n % t == 0, (n, t)
    return t


def _rmsnorm_kernel(x_ref, w_ref, o_ref):
    x = x_ref[...].astype(F32)
    y = x * lax.rsqrt(jnp.mean(x * x, axis=-1, keepdims=True) + NORM_EPS)
    o_ref[...] = (y * w_ref[...]).astype(o_ref.dtype)


def rmsnorm(x2d, w, out_dtype):
    m, d = x2d.shape
    tm = _tile(m, 1024)
    return pl.pallas_call(
        _rmsnorm_kernel,
        grid=(m // tm,),
        in_specs=[pl.BlockSpec((tm, d), lambda i: (i, 0)), pl.BlockSpec((1, d), lambda i: (0, 0))],
        out_specs=pl.BlockSpec((tm, d), lambda i: (i, 0)),
        out_shape=jax.ShapeDtypeStruct((m, d), out_dtype),
        compiler_params=_cparams(("parallel",)),
        name="rmsnorm",
    )(x2d, w.reshape(1, d).astype(F32))


def _mla_norm_kernel(lat_ref, wq_ref, wkv_ref, nq_ref, nkv_ref):
    lat = lat_ref[...]
    cq = lat[:, :MLA_Q_LORA]
    ckv = lat[:, MLA_Q_LORA:]
    nq = cq * lax.rsqrt(jnp.mean(cq * cq, axis=-1, keepdims=True) + NORM_EPS)
    nkv = ckv * lax.rsqrt(jnp.mean(ckv * ckv, axis=-1, keepdims=True) + NORM_EPS)
    nq_ref[...] = (nq * wq_ref[...]).astype(BF16)
    nkv_ref[...] = (nkv * wkv_ref[...]).astype(BF16)


def mla_norm(lat, wq, wkv):
    m, d = lat.shape
    tm = _tile(m, 1024)
    return pl.pallas_call(
        _mla_norm_kernel,
        grid=(m // tm,),
        in_specs=[pl.BlockSpec((tm, d), lambda i: (i, 0)),
                  pl.BlockSpec((1, MLA_Q_LORA), lambda i: (0, 0)),
                  pl.BlockSpec((1, MLA_KV_LORA), lambda i: (0, 0))],
        out_specs=[pl.BlockSpec((tm, MLA_Q_LORA), lambda i: (i, 0)),
                   pl.BlockSpec((tm, MLA_KV_LORA), lambda i: (i, 0))],
        out_shape=[jax.ShapeDtypeStruct((m, MLA_Q_LORA), BF16), jax.ShapeDtypeStruct((m, MLA_KV_LORA), BF16)],
        compiler_params=_cparams(("parallel",)),
        name="mla_norm",
    )(lat, wq.reshape(1, -1).astype(F32), wkv.reshape(1, -1).astype(F32))


def _ffn_kernel(x_ref, nw_ref, wg_ref, wu_ref, wo_ref, o_ref, n_scr, *, n_f):
    f = pl.program_id(1)

    @pl.when(f == 0)
    def _():
        x = x_ref[...]
        y = x * lax.rsqrt(jnp.mean(x * x, axis=-1, keepdims=True) + NORM_EPS)
        n_scr[...] = (y * nw_ref[...]).astype(BF16)
        o_ref[...] = jnp.zeros_like(o_ref)

    n = n_scr[...]
    g = _dot(n, wg_ref[...])
    u = _dot(n, wu_ref[...])
    h = (g * _sigmoid(g) * u).astype(BF16)
    o_ref[...] += _dot(h, wo_ref[...])

    @pl.when(f == n_f - 1)
    def _():
        o_ref[...] = x_ref[...] + 0.5 * o_ref[...]


def ffn(x2d, norm_w, w_in, w_out):
    m, d = x2d.shape
    f_dim = w_out.shape[0]
    tm = _tile(m, 512)
    tf = _tile(f_dim, 512)
    n_f = f_dim // tf
    return pl.pallas_call(
        functools.partial(_ffn_kernel, n_f=n_f),
        grid=(m // tm, n_f),
        in_specs=[pl.BlockSpec((tm, d), lambda i, f: (i, 0)),
                  pl.BlockSpec((1, d), lambda i, f: (0, 0)),
                  pl.BlockSpec((d, tf), lambda i, f: (0, f)),
                  pl.BlockSpec((d, tf), lambda i, f: (0, f + n_f)),
                  pl.BlockSpec((tf, d), lambda i, f: (f, 0))],
        out_specs=pl.BlockSpec((tm, d), lambda i, f: (i, 0)),
        out_shape=jax.ShapeDtypeStruct((m, d), F32),
        scratch_shapes=[pltpu.VMEM((tm, d), BF16)],
        compiler_params=_cparams(("parallel", "arbitrary")),
        name="ffn",
    )(x2d, norm_w.reshape(1, d).astype(F32), w_in, w_in, w_out)


def _proj_plain_kernel(n_ref, w_ref, o_ref, *, sigmoid):
    y = _dot(n_ref[...], w_ref[...])
    if sigmoid:
        y = _sigmoid(y)
    o_ref[...] = y.astype(o_ref.dtype)


def proj_plain(n2d, w, out_dtype, sigmoid=False):
    m, k = n2d.shape
    n_out = w.shape[1]
    tm = _tile(m, 1024)
    tn = next(t for t in (512, 256, LANES) if n_out % t == 0)
    return pl.pallas_call(
        functools.partial(_proj_plain_kernel, sigmoid=sigmoid),
        grid=(m // tm, n_out // tn),
        in_specs=[pl.BlockSpec((tm, k), lambda i, j: (i, 0)),
                  pl.BlockSpec((k, tn), lambda i, j: (0, j))],
        out_specs=pl.BlockSpec((tm, tn), lambda i, j: (i, j)),
        out_shape=jax.ShapeDtypeStruct((m, n_out), out_dtype),
        compiler_params=_cparams(("parallel", "arbitrary")),
        name="proj_plain",
    )(n2d, w)


def _proj_heads_kernel(*refs, rope, has_add, has_masks, scale):
    n_ref, w_ref = refs[0], refs[1]
    o_ref = refs[-1]
    n = n_ref[...]
    n_copy = o_ref.shape[1] // w_ref.shape[0]
    for h in range(w_ref.shape[0]):
        pos = 2
        y = _dot(n, w_ref[h])
        if rope == "weights":
            wr_ref, cos_ref, sin_ref = refs[pos:pos + 3]
            pos += 3
            y = y * cos_ref[...] + _dot(n, wr_ref[h]) * sin_ref[...]
        elif rope == "swap":
            cos_ref, sin_ref = refs[pos:pos + 2]
            pos += 2
            half = y.shape[1] // 2
            y = y * cos_ref[...] + jnp.concatenate([y[:, half:], y[:, :half]], axis=1) * sin_ref[...]
        if has_add:
            y = y + refs[pos][0, 0].astype(F32)
            pos += 1
        if scale != 1.0:
            y = y * scale
        if has_masks:
            mask_ref = refs[pos]
            for c in range(n_copy):
                o_ref[0, h * n_copy + c] = (y * mask_ref[c:c + 1, :]).astype(o_ref.dtype)
        else:
            o_ref[0, h] = y.astype(o_ref.dtype)


def proj_heads(n2d, w, batch, seq, rope=None, add=None, scale=1.0, masks=None):
    m, k = n2d.shape
    g, _, dh = w.shape
    tm = _tile(seq, 1024)
    ns = seq // tm
    weights = pl.BlockSpec((g, k, dh), lambda b, i: (0, 0, 0))
    in_specs = [pl.BlockSpec((tm, k), lambda b, i: (b * ns + i, 0)), weights]
    args = [n2d, w]
    table = pl.BlockSpec((tm, dh), lambda b, i: (i, 0))
    if rope is not None and rope[0] == "weights":
        in_specs += [weights, table, table]
        args += list(rope[1:])
    elif rope is not None:
        assert rope[0] == "swap" and dh % (2 * LANES) == 0
        in_specs += [table, table]
        args += list(rope[1:])
    if add is not None:
        in_specs.append(pl.BlockSpec((1, 1, tm, dh), lambda b, i: (b, 0, i, 0)))
        args.append(add)
    n_copy = 1
    if masks is not None:
        n_copy = masks.shape[0]
        in_specs.append(pl.BlockSpec((n_copy, dh), lambda b, i: (0, 0)))
        args.append(masks)
    return pl.pallas_call(
        functools.partial(_proj_heads_kernel, rope=None if rope is None else rope[0], has_add=add is not None,
                          has_masks=masks is not None, scale=scale),
        grid=(batch, ns),
        in_specs=in_specs,
        out_specs=pl.BlockSpec((1, g * n_copy, tm, dh), lambda b, i: (b, 0, i, 0)),
        out_shape=jax.ShapeDtypeStruct((batch, g * n_copy, seq, dh), BF16),
        compiler_params=_cparams(("parallel", "parallel")),
        name="proj_heads",
    )(*args)


def _proj_heads_t_kernel(n_ref, wt_ref, o_ref):
    dh = wt_ref.shape[1]
    pad = o_ref.shape[2] - dh
    row = lax.broadcasted_iota(jnp.int32, (pad, o_ref.shape[3]), 0)
    ones_rows = jnp.where(row == 0, 1.0, 0.0).astype(o_ref.dtype)
    n = n_ref[...]
    for h in range(wt_ref.shape[0]):
        o_ref[0, h, :dh, :] = _dot_nt(wt_ref[h], n).astype(o_ref.dtype)
        o_ref[0, h, dh:, :] = ones_rows


def proj_heads_t(n2d, w, batch, seq):
    m, k = n2d.shape
    g, _, dh = w.shape
    tm = _tile(seq, 1024)
    ns = seq // tm
    return pl.pallas_call(
        _proj_heads_t_kernel,
        grid=(batch, ns),
        in_specs=[pl.BlockSpec((tm, k), lambda b, i: (b * ns + i, 0)),
                  pl.BlockSpec((g, dh, k), lambda b, i: (0, 0, 0))],
        out_specs=pl.BlockSpec((1, g, dh + ONES_PAD, tm), lambda b, i: (b, 0, 0, i)),
        out_shape=jax.ShapeDtypeStruct((batch, g, dh + ONES_PAD, seq), BF16),
        compiler_params=_cparams(("parallel", "parallel")),
        name="proj_heads_t",
    )(n2d, jnp.swapaxes(w, 1, 2))


def _softmax_step_t(st, vt, m_scr, acc_scr, exp_dtype):
    m_prev = m_scr[...]
    m_new = jnp.maximum(m_prev, jnp.max(st, axis=0, keepdims=True).astype(F32))
    alpha = jnp.exp(m_prev - m_new)
    pt = jnp.exp((st - m_new.astype(st.dtype)).astype(exp_dtype)).astype(BF16)
    acc_scr[...] = alpha * acc_scr[...] + _dot(vt, pt)
    m_scr[...] = m_new


def _diag_visibility(d, c, tq, tk):
    if d * tk >= (c + 1) * tq:
        return "none"
    if (d + 1) * tk - 1 <= c * tq:
        return "full"
    return "partial"


def _causal_where(st, d, c):
    tk, tq = st.shape
    key = lax.broadcasted_iota(jnp.int32, (tk, tq), 0) + d * tk
    qry = lax.broadcasted_iota(jnp.int32, (tk, tq), 1) + c * tq
    return jnp.where(key <= qry, st, NEG_INF)


def _flash_sweep(n, r, sub_of, tq, tk, scores, vt_tile, m_scr, acc_scr, st_scr, exp_dtype=F32):
    n_chain = m_scr.shape[0]
    for c in range(n_chain):
        m_scr[c] = jnp.full(m_scr.shape[1:], NEG_INF, F32)
        acc_scr[c] = jnp.zeros(acc_scr.shape[1:], F32)
    n_slot = st_scr.shape[0]
    assert r % n_slot == 0
    everyone = (True,) * n_chain
    for c, st in enumerate(scores(0, everyone)):
        st_scr[0, c] = st.astype(st_scr.dtype)

    def body(jj, carry):
        for slot in range(n_slot):
            j = jj * n_slot + slot
            nxt = scores(j + 1, everyone)
            vt = vt_tile(j)
            for c in range(n_chain):
                st = st_scr[slot, c]
                st_scr[(slot + 1) % n_slot, c] = nxt[c].astype(st_scr.dtype)
                _softmax_step_t(st, vt, m_scr.at[c], acc_scr.at[c], exp_dtype)
        return carry

    lax.fori_loop(0, n // n_slot, body, 0)
    for d in range(r):
        vis = [_diag_visibility(d, sub_of(c), tq, tk) for c in range(n_chain)]
        need = tuple(d + 1 < r and _diag_visibility(d + 1, sub_of(c), tq, tk) != "none" for c in range(n_chain))
        nxt = scores(n + d + 1, need) if any(need) else None
        vt = vt_tile(n + d)
        for c in range(n_chain):
            if vis[c] != "none":
                st = st_scr[d % n_slot, c]
                if vis[c] == "partial":
                    st = _causal_where(st.astype(F32), d, sub_of(c)).astype(st_scr.dtype)
            if need[c]:
                st_scr[(d + 1) % n_slot, c] = nxt[c].astype(st_scr.dtype)
            if vis[c] != "none":
                _softmax_step_t(st, vt, m_scr.at[c], acc_scr.at[c], exp_dtype)


def _flash_kernel(q_ref, k_ref, vt_ref, o_ref, m_scr, acc_scr, st_scr, *, n_sub, tq, tk):
    i = pl.program_id(2)
    dv = o_ref.shape[2]
    r = n_sub * tq // tk

    def scores(j, need):
        k = k_ref[0, 0, pl.ds(pl.multiple_of(j * tk, tk), tk), :]
        return [_dot_nt(k, q_ref[0, 0, c * tq:(c + 1) * tq, :]) if need[c] else None for c in range(n_sub)]

    def vt_tile(j):
        return vt_ref[0, 0, :, pl.ds(pl.multiple_of(j * tk, tk), tk)]

    _flash_sweep(i * r, r, lambda c: c, tq, tk, scores, vt_tile, m_scr, acc_scr, st_scr, exp_dtype=BF16)
    for c in range(n_sub):
        o_t = acc_scr[c, :dv, :] / acc_scr[c, dv:dv + 1, :]
        o_ref[0, c * tq:(c + 1) * tq, :] = o_t.T.astype(o_ref.dtype)


def _score_slots(r):
    return 2 if r % 2 == 0 else 1


def _flash_tiles(s, max_sub):
    tq = _tile(s, 512)
    n_sub = next(n for n in (4, 2, 1) if n <= max_sub and s % (n * tq) == 0)
    return n_sub, tq, tq


def flash_causal(q, k, vt):
    b, h, s, dk = q.shape
    dve = vt.shape[2]
    dv = dve - ONES_PAD
    n_sub, tq, tk = _flash_tiles(s, max_sub=4)
    t = n_sub * tq
    return pl.pallas_call(
        functools.partial(_flash_kernel, n_sub=n_sub, tq=tq, tk=tk),
        grid=(b, h, s // t),
        in_specs=[pl.BlockSpec((1, 1, t, dk), lambda b_, h_, i: (b_, h_, i, 0)),
                  pl.BlockSpec((1, 1, s, dk), lambda b_, h_, i: (b_, h_, 0, 0)),
                  pl.BlockSpec((1, 1, dve, s), lambda b_, h_, i: (b_, h_, 0, 0))],
        out_specs=pl.BlockSpec((1, t, dv), lambda b_, h_, i: (b_, i, h_)),
        out_shape=jax.ShapeDtypeStruct((b, s, h * dv), BF16),
        scratch_shapes=[pltpu.VMEM((n_sub, 1, tq), F32), pltpu.VMEM((n_sub, dve, tq), F32),
                        pltpu.VMEM((_score_slots(n_sub * tq // tk), n_sub, tk, tq), SCORE_DTYPE)],
        compiler_params=_cparams(("parallel", "parallel", "arbitrary")),
        name="mla_flash",
    )(q, k, vt)


def _diff_kernel(q_ref, k_ref, vt_ref, lq1_ref, lk1_ref, lq2_ref, lk2_ref, sub_ref, o_ref,
                 m_scr, acc_scr, st_scr, *, n_sub, tq, tk, lambda_init):
    i = pl.program_id(2)
    vd = o_ref.shape[3]
    r = n_sub * tq // tk

    def scores(j, need):
        k = k_ref[0, 0, pl.ds(pl.multiple_of(j * tk, tk), tk), :]
        return [_dot_nt(k, q_ref[0, ch // n_sub, (ch % n_sub) * tq:(ch % n_sub + 1) * tq, :]) if need[ch] else None
                for ch in range(2 * n_sub)]

    def vt_tile(j):
        return vt_ref[0, 0, :, pl.ds(pl.multiple_of(j * tk, tk), tk)]

    _flash_sweep(i * r, r, lambda ch: ch % n_sub, tq, tk, scores, vt_tile, m_scr, acc_scr, st_scr, exp_dtype=BF16)

    lam = (jnp.exp(jnp.sum(lq1_ref[...] * lk1_ref[...], axis=-1, keepdims=True))
           - jnp.exp(jnp.sum(lq2_ref[...] * lk2_ref[...], axis=-1, keepdims=True)) + lambda_init)
    lane_pad = (-vd) % LANES
    for c in range(n_sub):
        o_t = (acc_scr[c, :vd, :] / acc_scr[c, vd:vd + 1, :]
               - lam * (acc_scr[n_sub + c, :vd, :] / acc_scr[n_sub + c, vd:vd + 1, :]))
        o_t = o_t * lax.rsqrt(jnp.mean(o_t * o_t, axis=0, keepdims=True) + NORM_EPS)
        o_t = jnp.concatenate([o_t, jnp.zeros((lane_pad, tq), F32)], axis=0)
        o_ref[0, 0, c * tq:(c + 1) * tq, :] = (
            o_t.T[:, :vd] * sub_ref[...] * (1.0 - lambda_init)).astype(o_ref.dtype)


def diff_attention(q, k, vt, lq1, lk1, lq2, lk2, subln, lambda_init):
    b, h2, s, dk = q.shape
    h = h2 // 2
    hd = DIFF_HD
    vde = vt.shape[2]
    vd = vde - ONES_PAD
    n_sub, tq, tk = _flash_tiles(s, max_sub=2)
    t = n_sub * tq
    vec = lambda a: a.reshape(1, -1).astype(F32)
    small = lambda n: pl.BlockSpec((1, n), lambda b_, h_, i: (0, 0))
    return pl.pallas_call(
        functools.partial(_diff_kernel, n_sub=n_sub, tq=tq, tk=tk, lambda_init=lambda_init),
        grid=(b, h, s // t),
        in_specs=[pl.BlockSpec((1, 2, t, dk), lambda b_, h_, i: (b_, h_, i, 0)),
                  pl.BlockSpec((1, 1, s, dk), lambda b_, h_, i: (b_, h_, 0, 0)),
                  pl.BlockSpec((1, 1, vde, s), lambda b_, h_, i: (b_, h_, 0, 0)),
                  small(hd), small(hd), small(hd), small(hd), small(vd)],
        out_specs=pl.BlockSpec((1, 1, t, vd), lambda b_, h_, i: (b_, h_, i, 0)),
        out_shape=jax.ShapeDtypeStruct((b, h, s, vd), BF16),
        scratch_shapes=[pltpu.VMEM((2 * n_sub, 1, tq), F32), pltpu.VMEM((2 * n_sub, vde, tq), F32),
                        pltpu.VMEM((_score_slots(n_sub * tq // tk), 2 * n_sub, tk, tq), SCORE_DTYPE)],
        compiler_params=_cparams(("parallel", "parallel", "arbitrary")),
        name="diff_flash",
    )(q, k, vt, vec(lq1), vec(lk1), vec(lq2), vec(lk2), vec(subln))


def _compress_kernel(a_ref, pa_ref, pb_ref, w1a_ref, w1b_ref, w2_ref, o_ref, *, transposed):
    ng = a_ref.shape[2]
    rows = min(ng, 256)
    us, vs = [], []
    for r0 in range(0, ng, rows):
        a = a_ref[0, 0, r0:r0 + rows, :].astype(F32)
        us.append(_dot((a + pa_ref[...]).astype(BF16), w1a_ref[...]))
        vs.append(_dot((a + pb_ref[...]).astype(BF16), w1b_ref[...]))
    u = jnp.concatenate(us, axis=0)
    v = jnp.concatenate(vs, axis=0)
    hdn = u + pltpu.roll(v, ng - 1, 0)
    hdn = (hdn * _sigmoid(hdn)).astype(BF16)
    if transposed:
        d_out = w2_ref.shape[0]
        o_ref[0, :d_out, :] = _dot_nt(w2_ref[...], hdn).astype(o_ref.dtype)
        row = lax.broadcasted_iota(jnp.int32, (o_ref.shape[1] - d_out, ng), 0)
        o_ref[0, d_out:, :] = jnp.where(row == 0, 1.0, 0.0).astype(o_ref.dtype)
    else:
        o_ref[0] = _dot(hdn, w2_ref[...]).astype(o_ref.dtype)


def nsa_compress(tok, g_idx, pos, w1, w2, transposed=False):
    assert NSA_CMP_LEN == 2 * NSA_CMP_STRIDE
    b, _, ng, wd = tok.shape
    d = wd // NSA_CMP_STRIDE
    d_out = w2.shape[1]
    pos_flat = pos.astype(F32).reshape(1, NSA_CMP_LEN * d)
    w1 = w1.astype(BF16)
    w2 = w2.astype(BF16).T if transposed else w2.astype(BF16)
    out_block = (1, d_out + ONES_PAD, ng) if transposed else (1, ng, d_out)
    full = lambda shape: pl.BlockSpec(shape, lambda b_: (0,) * len(shape))
    return pl.pallas_call(
        functools.partial(_compress_kernel, transposed=transposed),
        grid=(b,),
        in_specs=[pl.BlockSpec((1, 1, ng, wd), lambda b_: (b_, g_idx, 0, 0)),
                  full((1, wd)), full((1, wd)), full((wd, NSA_CMP_HIDDEN)), full((wd, NSA_CMP_HIDDEN)),
                  full(w2.shape)],
        out_specs=pl.BlockSpec(out_block, lambda b_: (b_, 0, 0)),
        out_shape=jax.ShapeDtypeStruct((b,) + out_block[1:], BF16),
        compiler_params=_cparams(("parallel",)),
        name="nsa_compress",
    )(tok, pos_flat[:, :wd], pos_flat[:, wd:], w1[:wd], w1[wd:], w2)


def _split3(x):
    hi = x.astype(BF16)
    r = x - hi.astype(F32)
    mid = r.astype(BF16)
    lo = (r - mid.astype(F32)).astype(BF16)
    return hi, mid, lo


def _cmp_topk_kernel(q_ref, kc_ref, vct_ref, wselt_ref, oc_ref, selt_ref, *, tq, k_top):
    i = pl.program_id(1)
    qs = i * tq
    nc = kc_ref.shape[1]
    n_sel = wselt_ref.shape[0]
    kc = kc_ref[0]
    vct = vct_ref[0]
    qpos_c = qs + lax.broadcasted_iota(jnp.int32, (nc, tq), 1)
    cend = lax.broadcasted_iota(jnp.int32, (nc, tq), 0) * NSA_CMP_STRIDE + (NSA_CMP_LEN - 1)
    valid_c = cend <= qpos_c
    any_valid = qs + lax.broadcasted_iota(jnp.int32, (1, tq), 1) >= NSA_CMP_LEN - 1
    imp_t = jnp.zeros((nc, tq), F32)
    for h in range(NSA_HEADS):
        st = jnp.where(valid_c, _dot_nt(kc, q_ref[0, h]), NEG_INF)
        e = jnp.exp(st - jnp.max(st, axis=0, keepdims=True))
        inv = jnp.where(any_valid, 1.0 / jnp.sum(e, axis=0, keepdims=True), 0.0)
        o_t = _dot(vct, e.astype(BF16))[:NSA_DV, :] * inv
        oc_ref[0, :, h * NSA_DV:(h + 1) * NSA_DV] = o_t.T
        imp_t = imp_t + e * inv

    wselt = wselt_ref[...]
    hi, mid, lo = _split3(imp_t)
    imp_sel = _dot(wselt, hi) + _dot(wselt, mid) + _dot(wselt, lo)

    blk = lax.broadcasted_iota(jnp.int32, (n_sel, tq), 0)
    qpos = qs + lax.broadcasted_iota(jnp.int32, (n_sel, tq), 1)
    cur = lax.shift_right_arithmetic(qpos, int(math.log2(NSA_SEL_LEN)))
    forced = (blk == 0) | (blk == cur) | (blk == cur - 1)
    valid_s = blk * NSA_SEL_LEN <= qpos
    score = jnp.where(valid_s, jnp.where(forced, NSA_FORCE_SCORE, imp_sel), NEG_INF)
    blk_f = blk.astype(F32)
    sel = jnp.zeros((n_sel, tq), F32)
    for _ in range(k_top):
        mx = jnp.max(score, axis=0, keepdims=True)
        first = jnp.min(jnp.where(score == mx, blk_f, float(n_sel)), axis=0, keepdims=True)
        hit = blk_f == first
        sel = jnp.where(hit, 1.0, sel)
        score = jnp.where(hit, REMOVED, score)
    selt_ref[0] = sel.astype(selt_ref.dtype)


def _sel_weight_matrix(n_pad, n_sel):
    r_c = NSA_CMP_LEN // NSA_CMP_STRIDE
    ratio = NSA_SEL_LEN // NSA_CMP_STRIDE
    overlap_w = [max(0, min(o * NSA_CMP_STRIDE + NSA_CMP_LEN, NSA_SEL_LEN) - max(o * NSA_CMP_STRIDE, 0))
                 / NSA_CMP_STRIDE for o in range(-(r_c - 1), ratio)]
    w = np.zeros((n_pad, n_sel), np.float32)
    for n in range(n_sel):
        for u, w_u in enumerate(overlap_w):
            c = ratio * n + u - (r_c - 1)
            if 0 <= c < n_pad:
                w[c, n] = w_u
    return w


def nsa_cmp_topk(q, kc, vct):
    b, h, s, dk = q.shape
    nc = kc.shape[1]
    dve = vct.shape[1]
    n_sel = s // NSA_SEL_LEN
    assert NSA_SEL_LEN & (NSA_SEL_LEN - 1) == 0
    tq = _tile(s, 512)
    wselt = jnp.asarray(_sel_weight_matrix(nc, n_sel).T, BF16)
    return pl.pallas_call(
        functools.partial(_cmp_topk_kernel, tq=tq, k_top=min(NSA_TOPK, n_sel)),
        grid=(b, s // tq),
        in_specs=[pl.BlockSpec((1, h, tq, dk), lambda b_, i: (b_, 0, i, 0)),
                  pl.BlockSpec((1, nc, dk), lambda b_, i: (b_, 0, 0)),
                  pl.BlockSpec((1, dve, nc), lambda b_, i: (b_, 0, 0)),
                  pl.BlockSpec((n_sel, nc), lambda b_, i: (0, 0))],
        out_specs=[pl.BlockSpec((1, tq, h * NSA_DV), lambda b_, i: (b_, i, 0)),
                   pl.BlockSpec((1, n_sel, tq), lambda b_, i: (b_, 0, i))],
        out_shape=[jax.ShapeDtypeStruct((b, s, h * NSA_DV), F32), jax.ShapeDtypeStruct((b, n_sel, s), BF16)],
        compiler_params=_cparams(("parallel", "parallel")),
        name="nsa_cmp_topk",
    )(q, kc, vct, wselt)


def _nsa_sel_kernel(q_ref, k_ref, vt_ref, sel_ref, o_ref, m_scr, acc_scr, st_scr, *, n_sub, tq, tk):
    i = pl.program_id(1)
    n_sel = sel_ref.shape[1]
    bpt = tk // NSA_SEL_LEN
    shift = int(math.log2(NSA_SEL_LEN))
    r = n_sub * tq // tk

    def scores(j, need):
        k = k_ref[0, 0, pl.ds(pl.multiple_of(j * tk, tk), tk), :]
        blk_key = lax.shift_right_arithmetic(lax.broadcasted_iota(jnp.int32, (tk, n_sel), 0), shift) + j * bpt
        blk_col = lax.broadcasted_iota(jnp.int32, (tk, n_sel), 1)
        expand_t = jnp.where(blk_key == blk_col, 1.0, 0.0).astype(BF16)
        out = []
        for c in range(n_sub):
            rows = slice(c * tq, (c + 1) * tq)
            if not any(need[c * NSA_HEADS:(c + 1) * NSA_HEADS]):
                out += [None] * NSA_HEADS
                continue
            keep_t = _dot(expand_t, sel_ref[0, :, rows]) > 0.5
            out += [jnp.where(keep_t, _dot_nt(k, q_ref[0, h, rows, :]), NEG_INF) if need[c * NSA_HEADS + h] else None
                    for h in range(NSA_HEADS)]
        return out

    def vt_tile(j):
        return vt_ref[0, 0, :, pl.ds(pl.multiple_of(j * tk, tk), tk)]

    _flash_sweep(i * r, r, lambda ch: ch // NSA_HEADS, tq, tk, scores, vt_tile, m_scr, acc_scr, st_scr,
                 exp_dtype=BF16)
    for ch in range(n_sub * NSA_HEADS):
        c, h = divmod(ch, NSA_HEADS)
        o_t = acc_scr[ch, :NSA_DV, :] / acc_scr[ch, NSA_DV:NSA_DV + 1, :]
        o_ref[0, c * tq:(c + 1) * tq, h * NSA_DV:(h + 1) * NSA_DV] = o_t.T


def _nsa_sel_tiles(s):
    return _flash_tiles(s, max_sub=2)


def nsa_selected(q, k3, k_idx, vt, sel):
    b, h, s, dk = q.shape
    n_sel = sel.shape[1]
    dve = vt.shape[2]
    n_sub, tq, tk = _nsa_sel_tiles(s)
    t = n_sub * tq
    n_chain = n_sub * h
    assert tk % NSA_SEL_LEN == 0 and t % tk == 0
    return pl.pallas_call(
        functools.partial(_nsa_sel_kernel, n_sub=n_sub, tq=tq, tk=tk),
        grid=(b, s // t),
        in_specs=[pl.BlockSpec((1, h, t, dk), lambda b_, i: (b_, 0, i, 0)),
                  pl.BlockSpec((1, 1, s, dk), lambda b_, i: (b_, k_idx, 0, 0)),
                  pl.BlockSpec((1, 1, dve, s), lambda b_, i: (b_, 0, 0, 0)),
                  pl.BlockSpec((1, n_sel, t), lambda b_, i: (b_, 0, i))],
        out_specs=pl.BlockSpec((1, t, h * NSA_DV), lambda b_, i: (b_, i, 0)),
        out_shape=jax.ShapeDtypeStruct((b, s, h * NSA_DV), F32),
        scratch_shapes=[pltpu.VMEM((n_chain, 1, tq), F32), pltpu.VMEM((n_chain, dve, tq), F32),
                        pltpu.VMEM((_score_slots(t // tk), n_chain, tk, tq), SCORE_DTYPE)],
        compiler_params=_cparams(("parallel", "arbitrary")),
        name="nsa_selected",
    )(q, k3, vt, sel)


def _nsa_win_kernel(q_ref, kp_ref, kc_ref, vp_ref, vc_ref, o_ref, *, t):
    i = pl.program_id(1)
    row = lax.broadcasted_iota(jnp.int32, (t, t), 0)
    col = lax.broadcasted_iota(jnp.int32, (t, t), 1)
    keep_prev = (col > row) & (i > 0)
    keep_cur = col <= row
    kp, kc, vp, vc = kp_ref[0, 0], kc_ref[0, 0], vp_ref[0], vc_ref[0]
    for h in range(NSA_HEADS):
        q = q_ref[0, h]
        sp = jnp.where(keep_prev, _dot_nt(q, kp), NEG_INF)
        sc = jnp.where(keep_cur, _dot_nt(q, kc), NEG_INF)
        m = jnp.maximum(jnp.max(sp, axis=-1, keepdims=True), jnp.max(sc, axis=-1, keepdims=True))
        pp = jnp.exp(sp - m)
        pc = jnp.exp(sc - m)
        l = jnp.sum(pp, axis=-1, keepdims=True) + jnp.sum(pc, axis=-1, keepdims=True)
        o = _dot(pp.astype(BF16), vp) + _dot(pc.astype(BF16), vc)
        o_ref[0, :, h * NSA_DV:(h + 1) * NSA_DV] = o / l


def nsa_window(q, k3, k_idx, v3, v_idx):
    b, h, s, dk = q.shape
    t = _tile(s, NSA_WINDOW)
    assert t == NSA_WINDOW, "window kernel needs the query tile to equal the window"
    prev = lambda i: jnp.maximum(i - 1, 0)
    return pl.pallas_call(
        functools.partial(_nsa_win_kernel, t=t),
        grid=(b, s // t),
        in_specs=[pl.BlockSpec((1, h, t, dk), lambda b_, i: (b_, 0, i, 0)),
                  pl.BlockSpec((1, 1, t, dk), lambda b_, i: (b_, k_idx, prev(i), 0)),
                  pl.BlockSpec((1, 1, t, dk), lambda b_, i: (b_, k_idx, i, 0)),
                  pl.BlockSpec((1, t, NSA_DV), lambda b_, i: (b_, prev(i), v_idx)),
                  pl.BlockSpec((1, t, NSA_DV), lambda b_, i: (b_, i, v_idx))],
        out_specs=pl.BlockSpec((1, t, h * NSA_DV), lambda b_, i: (b_, i, 0)),
        out_shape=jax.ShapeDtypeStruct((b, s, h * NSA_DV), F32),
        compiler_params=_cparams(("parallel", "parallel")),
        name="nsa_window",
    )(q, k3, k3, v3, v3)


def _merge_kernel(x_ref, om_ref, oc_ref, os_ref, ow_ref, gn_ref, od_ref, gm_ref, gs_ref, gd_ref,
                  wm_ref, wn_ref, wd_ref, wo_ref, ex_ref, o_ref, onsa_scr, *, n_j):
    j = pl.program_id(1)
    nsa_out = NSA_HEADS * NSA_DV

    @pl.when(j == 0)
    def _():
        g = gn_ref[...]
        hi = g.astype(BF16)
        lo = (g - hi.astype(F32)).astype(BF16)
        ge = _dot(hi, ex_ref[...]) + _dot(lo, ex_ref[...])
        onsa = (ge[:, :nsa_out] * oc_ref[...] + ge[:, nsa_out:2 * nsa_out] * os_ref[...]
                + ge[:, 2 * nsa_out:] * ow_ref[...])
        onsa_scr[...] = onsa.astype(BF16)
        o_ref[...] = jnp.zeros_like(o_ref)

    ym = _dot(om_ref[...], wm_ref[...])
    yn = _dot(onsa_scr[...], wn_ref[...])
    yd = _dot(od_ref[0, 0], wd_ref[0])
    for h in range(1, DIFF_HEADS):
        yd = yd + _dot(od_ref[0, h], wd_ref[h])
    mixed = (gm_ref[...].astype(F32) * ym + gs_ref[...].astype(F32) * yn + gd_ref[...].astype(F32) * yd)
    o_ref[...] += _dot(mixed.astype(BF16), wo_ref[...])

    @pl.when(j == n_j - 1)
    def _():
        o_ref[...] = x_ref[...] + o_ref[...]


def _gate_expand_matrix():
    nsa_out = NSA_HEADS * NSA_DV
    e = np.zeros((LANES, 3 * nsa_out), np.float32)
    for h in range(NSA_HEADS):
        for c in range(3):
            e[h * 3 + c, c * nsa_out + h * NSA_DV:c * nsa_out + (h + 1) * NSA_DV] = 1.0
    return e


def merge(x2d, o_mla, o_c, o_s, o_w, g_nsa, o_diff, g_merge, w_br_mla, w_br_nsa, w_br_diff, w_out, seq):
    m, d = x2d.shape
    tm = _tile(seq, 512)
    ns = seq // tm
    tn = _tile(d, 512)
    n_j = d // tn
    nsa_out = NSA_HEADS * NSA_DV
    mla_out = o_mla.shape[-1]
    ex = jnp.asarray(_gate_expand_matrix(), BF16)
    row = lambda w: pl.BlockSpec((tm, w), lambda i, j: (i, 0))
    return pl.pallas_call(
        functools.partial(_merge_kernel, n_j=n_j),
        grid=(m // tm, n_j),
        in_specs=[row(d), row(mla_out), row(nsa_out), row(nsa_out), row(nsa_out), row(LANES),
                  pl.BlockSpec((1, DIFF_HEADS, tm, DIFF_VD), lambda i, j: (i // ns, 0, i % ns, 0)),
                  pl.BlockSpec((tm, tn), lambda i, j: (i, j)),
                  pl.BlockSpec((tm, tn), lambda i, j: (i, n_j + j)),
                  pl.BlockSpec((tm, tn), lambda i, j: (i, 2 * n_j + j)),
                  pl.BlockSpec((mla_out, tn), lambda i, j: (0, j)),
                  pl.BlockSpec((nsa_out, tn), lambda i, j: (0, j)),
                  pl.BlockSpec((DIFF_HEADS, DIFF_VD, tn), lambda i, j: (0, 0, j)),
                  pl.BlockSpec((tn, d), lambda i, j: (j, 0)),
                  pl.BlockSpec((LANES, 3 * nsa_out), lambda i, j: (0, 0))],
        out_specs=pl.BlockSpec((tm, d), lambda i, j: (i, 0)),
        out_shape=jax.ShapeDtypeStruct((m, d), F32),
        scratch_shapes=[pltpu.VMEM((tm, nsa_out), BF16)],
        compiler_params=_cparams(("parallel", "arbitrary")),
        name="merge",
    )(x2d, o_mla, o_c, o_s, o_w, g_nsa, o_diff, g_merge, g_merge, g_merge,
      w_br_mla, w_br_nsa, w_br_diff, w_out, ex)


def _rope_tables(dim, seq, lead=0):
    inv = ROPE_THETA ** (-jnp.arange(0, dim, 2, dtype=F32) / dim)
    ang = jnp.arange(seq, dtype=F32)[:, None] * inv[None, :]
    cos, sin = jnp.cos(ang), jnp.sin(ang)
    cos = jnp.concatenate([jnp.ones((seq, lead), F32), cos, cos], axis=-1)
    sin = jnp.concatenate([jnp.zeros((seq, lead), F32), sin, sin], axis=-1)
    return cos, sin


def _heads(w, g, dh):
    return w.reshape(w.shape[0], g, dh).transpose(1, 0, 2)


def _rot_cols(w, lead=0):
    half = (w.shape[-1] - lead) // 2
    x1 = w[..., lead:lead + half]
    x2 = w[..., lead + half:]
    return jnp.concatenate([jnp.zeros_like(w[..., :lead]), -x2, x1], axis=-1)


def _pad_halves(w, axis=-1):
    w = jnp.moveaxis(w, axis, -1)
    half = w.shape[-1] // 2
    z = jnp.zeros(w.shape[:-1] + ((-half) % LANES,), w.dtype)
    out = jnp.concatenate([w[..., :half], z, w[..., half:], z], axis=-1)
    return jnp.moveaxis(out, -1, axis)


def _swap_tables(dim, seq, copies=1):
    inv = ROPE_THETA ** (-jnp.arange(0, dim, 2, dtype=F32) / dim)
    ang = jnp.arange(seq, dtype=F32)[:, None] * inv[None, :]
    cos = jnp.tile(jnp.cos(ang), (1, copies))
    sin = jnp.tile(jnp.sin(ang), (1, copies))
    return _pad_halves(jnp.concatenate([cos, cos], axis=-1)), _pad_halves(jnp.concatenate([-sin, sin], axis=-1))


def _pair_heads(w):
    k = w.shape[0]
    half = DIFF_HD // 2
    w = w.reshape(k, DIFF_HEADS, 2, 2, half)
    w = w.transpose(1, 0, 3, 2, 4).reshape(DIFF_HEADS, k, 2 * DIFF_HD)
    return _pad_halves(w)


def _pair_masks():
    half = DIFF_HD // 2
    m = np.zeros((2, 2, 2, half), np.float32)
    for c in range(2):
        m[c, :, c, :] = 1.0
    return _pad_halves(jnp.asarray(m.reshape(2, 2 * DIFF_HD)))


def _col_offsets():
    sizes = (MLA_Q_LORA, MLA_KV_LORA, MLA_ROPE,
             NSA_HEADS * NSA_DK, NSA_DK, NSA_DV, NSA_DK, NSA_DV, NSA_DK, NSA_DV, NSA_HEADS * 3,
             DIFF_HEADS * 2 * DIFF_HD, DIFF_HEADS * 2 * DIFF_HD, DIFF_HEADS * DIFF_VD)
    names = ("c_q", "c_kv", "k_rope", "nsa_q", "nsa_kc", "nsa_vc", "nsa_ks", "nsa_vs", "nsa_kw", "nsa_vw",
             "nsa_g", "d_q", "d_k", "d_v")
    offs = {}
    o = 0
    for nme, sz in zip(names, sizes):
        offs[nme] = (o, o + sz)
        o += sz
    offs["merge"] = (o, None)
    return offs


def _mixers(n2d, batch, seq, layer, w_in, p):
    offs = _col_offsets()
    col = lambda name: w_in[:, offs[name][0]:offs[name][1]]
    bf = lambda a: a.astype(BF16)

    lat = proj_plain(n2d, bf(jnp.concatenate([col("c_q"), col("c_kv")], axis=1)), F32)
    n_q, n_kv = mla_norm(lat, p["mla_q_norm"], p["mla_kv_norm"])
    cos_m, sin_m = _rope_tables(MLA_ROPE, seq, lead=MLA_NOPE)
    w_kr = jnp.concatenate([jnp.zeros((w_in.shape[0], MLA_NOPE), F32), col("k_rope")], axis=1)[None]
    kpe = proj_heads(n2d, bf(w_kr), batch, seq, rope=("weights", bf(_rot_cols(w_kr, MLA_NOPE)), cos_m, sin_m))
    w_uq = _heads(p["mla_w_uq"], MLA_HEADS, MLA_QK)
    q_mla = proj_heads(n_q, bf(w_uq), batch, seq, rope=("weights", bf(_rot_cols(w_uq, MLA_NOPE)), cos_m, sin_m),
                       scale=MLA_QK ** -0.5)
    w_ukv = _heads(p["mla_w_ukv"], MLA_HEADS, MLA_NOPE + MLA_V)
    w_uk = jnp.concatenate([w_ukv[..., :MLA_NOPE], jnp.zeros(w_ukv.shape[:2] + (MLA_ROPE,), F32)], axis=-1)
    k_mla = proj_heads(n_kv, bf(w_uk), batch, seq, add=kpe)
    vt_mla = proj_heads_t(n_kv, bf(w_ukv[..., MLA_NOPE:]), batch, seq)
    o_mla = flash_causal(q_mla, k_mla, vt_mla)

    cos_d, sin_d = _swap_tables(DIFF_HD, seq, copies=2)
    q_d = proj_heads(n2d, bf(_pair_heads(col("d_q"))), batch, seq, rope=("swap", cos_d, sin_d),
                     scale=DIFF_HD ** -0.5, masks=_pair_masks())
    k_d = proj_heads(n2d, bf(_pair_heads(col("d_k"))), batch, seq, rope=("swap", cos_d, sin_d))
    vt_d = proj_heads_t(n2d, bf(_heads(col("d_v"), DIFF_HEADS, DIFF_VD)), batch, seq)
    lambda_init = 0.8 - 0.6 * math.exp(-0.3 * layer)
    o_diff = diff_attention(q_d, k_d, vt_d, p["diff_lam_q1"], p["diff_lam_k1"], p["diff_lam_q2"],
                            p["diff_lam_k2"], p["diff_subln"], lambda_init)

    cos_n, sin_n = _swap_tables(NSA_DK, seq)
    w_nq = _pad_halves(_heads(col("nsa_q"), NSA_HEADS, NSA_DK))
    q_n = proj_heads(n2d, bf(w_nq), batch, seq, rope=("swap", cos_n, sin_n), scale=NSA_DK ** -0.5)
    w_nk = _pad_halves(jnp.stack([col("nsa_kc"), col("nsa_ks"), col("nsa_kw")], axis=0))
    k_n = proj_heads(n2d, bf(w_nk), batch, seq, rope=("swap", cos_n, sin_n))
    dk_pad = w_nk.shape[-1]
    v_n = proj_plain(n2d, bf(jnp.concatenate([col("nsa_vc"), col("nsa_vw")], axis=1)), BF16)
    v_n = v_n.reshape(batch, seq, 2 * NSA_DV)
    vt_slc = proj_heads_t(n2d, bf(col("nsa_vs"))[None], batch, seq)
    w_g = jnp.concatenate([col("nsa_g"), jnp.zeros((w_in.shape[0], LANES - NSA_HEADS * 3), F32)], axis=1)
    g_nsa = proj_plain(n2d, bf(w_g), F32, sigmoid=True)

    ng = seq // NSA_CMP_STRIDE
    w1k = _pad_halves(p["nsa_cmp_k_w1"].reshape(NSA_CMP_LEN, NSA_DK, NSA_CMP_HIDDEN), axis=1)
    kc = nsa_compress(k_n.reshape(batch, 3, ng, NSA_CMP_STRIDE * dk_pad), 0,
                      _pad_halves(p["nsa_cmp_k_pos"]), w1k.reshape(NSA_CMP_LEN * dk_pad, NSA_CMP_HIDDEN),
                      _pad_halves(p["nsa_cmp_k_w2"]))
    vc_tok = v_n[:, :, :NSA_DV].reshape(batch, 1, ng, NSA_CMP_STRIDE * NSA_DV)
    vct = nsa_compress(vc_tok, 0, p["nsa_cmp_v_pos"], p["nsa_cmp_v_w1"], p["nsa_cmp_v_w2"], transposed=True)
    o_c, sel = nsa_cmp_topk(q_n, kc, vct)
    o_s = nsa_selected(q_n, k_n, 1, vt_slc, sel)
    o_w = nsa_window(q_n, k_n, 2, v_n, 1)

    g_merge = proj_plain(n2d, bf(w_in[:, offs["merge"][0]:]), BF16, sigmoid=True)
    m = batch * seq
    return (o_mla.reshape(m, -1), o_c.reshape(m, -1), o_s.reshape(m, -1), o_w.reshape(m, -1), g_nsa,
            o_diff, g_merge)


def kernel(x, ffn1_norm, ffn1_w_in, ffn1_w_out, mix_norm, w_in, mla_q_norm, mla_kv_norm, mla_w_uq, mla_w_ukv, nsa_cmp_k_pos, nsa_cmp_k_w1, nsa_cmp_k_w2, nsa_cmp_v_pos, nsa_cmp_v_w1, nsa_cmp_v_w2, diff_lam_q1, diff_lam_k1, diff_lam_q2, diff_lam_k2, diff_subln, w_br_mla, w_br_nsa, w_br_diff, w_out, ffn2_norm, ffn2_w_in, ffn2_w_out, final_norm):
    batch, seq, d = x.shape
    depth = w_in.shape[0]
    x2d = x.reshape(batch * seq, d)
    bf = lambda a: a.astype(BF16)
    for l in range(depth):
        p = {"mla_q_norm": mla_q_norm[l], "mla_kv_norm": mla_kv_norm[l], "mla_w_uq": mla_w_uq[l],
             "mla_w_ukv": mla_w_ukv[l], "nsa_cmp_k_pos": nsa_cmp_k_pos[l], "nsa_cmp_k_w1": nsa_cmp_k_w1[l],
             "nsa_cmp_k_w2": nsa_cmp_k_w2[l], "nsa_cmp_v_pos": nsa_cmp_v_pos[l], "nsa_cmp_v_w1": nsa_cmp_v_w1[l],
             "nsa_cmp_v_w2": nsa_cmp_v_w2[l], "diff_lam_q1": diff_lam_q1[l], "diff_lam_k1": diff_lam_k1[l],
             "diff_lam_q2": diff_lam_q2[l], "diff_lam_k2": diff_lam_k2[l], "diff_subln": diff_subln[l]}
        x2d = ffn(x2d, ffn1_norm[l], bf(ffn1_w_in[l]), bf(ffn1_w_out[l]))
        n2d = rmsnorm(x2d, mix_norm[l], BF16)
        o_mla, o_c, o_s, o_w, g_nsa, o_diff, g_merge = _mixers(n2d, batch, seq, l, w_in[l], p)
        x2d = merge(x2d, o_mla, o_c, o_s, o_w, g_nsa, o_diff, g_merge, bf(w_br_mla[l]), bf(w_br_nsa[l]),
                    bf(w_br_diff[l]).reshape(DIFF_HEADS, DIFF_VD, d), bf(w_out[l]), seq)
        x2d = ffn(x2d, ffn2_norm[l], bf(ffn2_w_in[l]), bf(ffn2_w_out[l]))
    return rmsnorm(x2d, final_norm, F32).reshape(batch, seq, d)
```

```python
import functools
import math

import numpy as np
import jax
import jax.numpy as jnp
from jax import lax
from jax.experimental import pallas as pl
from jax.experimental.pallas import tpu as pltpu

F32 = jnp.float32
BF16 = jnp.bfloat16

NORM_EPS = 1e-6
ROPE_THETA = 10000.0
NEG_INF = -1e30
REMOVED = -3e38
N_BRANCH = 3

MLA_HEADS = 6
MLA_Q_LORA = 768
MLA_KV_LORA = 512
MLA_NOPE = 128
MLA_ROPE = 64
MLA_V = 128
MLA_QK = MLA_NOPE + MLA_ROPE

NSA_HEADS = 4
NSA_DK = 192
NSA_DV = 128
NSA_CMP_LEN = 32
NSA_CMP_STRIDE = 16
NSA_CMP_HIDDEN = 256
NSA_SEL_LEN = 64
NSA_TOPK = 16
NSA_WINDOW = 512
NSA_FORCE_SCORE = 1e6

DIFF_HEADS = 4
DIFF_HD = 96
DIFF_VD = 2 * DIFF_HD

LANES = 128
ONES_PAD = 16
SCORE_DTYPE = BF16
VMEM_LIMIT_MB = 56


def _cparams(dims, vmem_mb=VMEM_LIMIT_MB):
    return pltpu.CompilerParams(dimension_semantics=dims, vmem_limit_bytes=vmem_mb * 2**20)


def _sigmoid(x):
    return 1.0 / (1.0 + jnp.exp(-x))


def _dot(a, b):
    return jnp.dot(a, b, preferred_element_type=F32)


def _dot_nt(a, b):
    return lax.dot_general(a, b, (((1,), (1,)), ((), ())), preferred_element_type=F32)


def _tile(n, pref):
    t = min(n, pref)
    assert n % t == 0, (n, t)
    return t


def _rmsnorm_kernel(x_ref, w_ref, o_ref):
    x = x_ref[...].astype(F32)
    y = x * lax.rsqrt(jnp.mean(x * x, axis=-1, keepdims=True) + NORM_EPS)
    o_ref[...] = (y * w_ref[...]).astype(o_ref.dtype)


def rmsnorm(x2d, w, out_dtype):
    m, d = x2d.shape
    tm = _tile(m, 1024)
    return pl.pallas_call(
        _rmsnorm_kernel,
        grid=(m // tm,),
        in_specs=[pl.BlockSpec((tm, d), lambda i: (i, 0)), pl.BlockSpec((1, d), lambda i: (0, 0))],
        out_specs=pl.BlockSpec((tm, d), lambda i: (i, 0)),
        out_shape=jax.ShapeDtypeStruct((m, d), out_dtype),
        compiler_params=_cparams(("parallel",)),
        name="rmsnorm",
    )(x2d, w.reshape(1, d).astype(F32))


def _mla_norm_kernel(lat_ref, wq_ref, wkv_ref, nq_ref, nkv_ref):
    lat = lat_ref[...]
    cq = lat[:, :MLA_Q_LORA]
    ckv = lat[:, MLA_Q_LORA:]
    nq = cq * lax.rsqrt(jnp.mean(cq * cq, axis=-1, keepdims=True) + NORM_EPS)
    nkv = ckv * lax.rsqrt(jnp.mean(ckv * ckv, axis=-1, keepdims=True) + NORM_EPS)
    nq_ref[...] = (nq * wq_ref[...]).astype(BF16)
    nkv_ref[...] = (nkv * wkv_ref[...]).astype(BF16)


def mla_norm(lat, wq, wkv):
    m, d = lat.shape
    tm = _tile(m, 1024)
    return pl.pallas_call(
        _mla_norm_kernel,
        grid=(m // tm,),
        in_specs=[pl.BlockSpec((tm, d), lambda i: (i, 0)),
                  pl.BlockSpec((1, MLA_Q_LORA), lambda i: (0, 0)),
                  pl.BlockSpec((1, MLA_KV_LORA), lambda i: (0, 0))],
        out_specs=[pl.BlockSpec((tm, MLA_Q_LORA), lambda i: (i, 0)),
                   pl.BlockSpec((tm, MLA_KV_LORA), lambda i: (i, 0))],
        out_shape=[jax.ShapeDtypeStruct((m, MLA_Q_LORA), BF16), jax.ShapeDtypeStruct((m, MLA_KV_LORA), BF16)],
        compiler_params=_cparams(("parallel",)),
        name="mla_norm",
    )(lat, wq.reshape(1, -1).astype(F32), wkv.reshape(1, -1).astype(F32))


def _ffn_kernel(x_ref, nw_ref, wg_ref, wu_ref, wo_ref, o_ref, n_scr, *, n_f):
    f = pl.program_id(1)

    @pl.when(f == 0)
    def _():
        x = x_ref[...]
        y = x * lax.rsqrt(jnp.mean(x * x, axis=-1, keepdims=True) + NORM_EPS)
        n_scr[...] = (y * nw_ref[...]).astype(BF16)
        o_ref[...] = jnp.zeros_like(o_ref)

    n = n_scr[...]
    g = _dot(n, wg_ref[...])
    u = _dot(n, wu_ref[...])
    h = (g * _sigmoid(g) * u).astype(BF16)
    o_ref[...] += _dot(h, wo_ref[...])

    @pl.when(f == n_f - 1)
    def _():
        o_ref[...] = x_ref[...] + 0.5 * o_ref[...]


def ffn(x2d, norm_w, w_in, w_out):
    m, d = x2d.shape
    f_dim = w_out.shape[0]
    tm = _tile(m, 512)
    tf = _tile(f_dim, 512)
    n_f = f_dim // tf
    return pl.pallas_call(
        functools.partial(_ffn_kernel, n_f=n_f),
        grid=(m // tm, n_f),
        in_specs=[pl.BlockSpec((tm, d), lambda i, f: (i, 0)),
                  pl.BlockSpec((1, d), lambda i, f: (0, 0)),
                  pl.BlockSpec((d, tf), lambda i, f: (0, f)),
                  pl.BlockSpec((d, tf), lambda i, f: (0, f + n_f)),
                  pl.BlockSpec((tf, d), lambda i, f: (f, 0))],
        out_specs=pl.BlockSpec((tm, d), lambda i, f: (i, 0)),
        out_shape=jax.ShapeDtypeStruct((m, d), F32),
        scratch_shapes=[pltpu.VMEM((tm, d), BF16)],
        compiler_params=_cparams(("parallel", "arbitrary")),
        name="ffn",
    )(x2d, norm_w.reshape(1, d).astype(F32), w_in, w_in, w_out)


def _proj_plain_kernel(n_ref, w_ref, o_ref, *, sigmoid):
    y = _dot(n_ref[...], w_ref[...])
    if sigmoid:
        y = _sigmoid(y)
    o_ref[...] = y.astype(o_ref.dtype)


def proj_plain(n2d, w, out_dtype, sigmoid=False):
    m, k = n2d.shape
    n_out = w.shape[1]
    tm = _tile(m, 1024)
    tn = next(t for t in (512, 256, LANES) if n_out % t == 0)
    return pl.pallas_call(
        functools.partial(_proj_plain_kernel, sigmoid=sigmoid),
        grid=(m // tm, n_out // tn),
        in_specs=[pl.BlockSpec((tm, k), lambda i, j: (i, 0)),
                  pl.BlockSpec((k, tn), lambda i, j: (0, j))],
        out_specs=pl.BlockSpec((tm, tn), lambda i, j: (i, j)),
        out_shape=jax.ShapeDtypeStruct((m, n_out), out_dtype),
        compiler_params=_cparams(("parallel", "arbitrary")),
        name="proj_plain",
    )(n2d, w)


def _proj_heads_kernel(*refs, rope, has_add, has_masks, scale):
    n_ref, w_ref = refs[0], refs[1]
    o_ref = refs[-1]
    n = n_ref[...]
    n_copy = o_ref.shape[1] // w_ref.shape[0]
    for h in range(w_ref.shape[0]):
        pos = 2
        y = _dot(n, w_ref[h])
        if rope == "weights":
            wr_ref, cos_ref, sin_ref = refs[pos:pos + 3]
            pos += 3
            y = y * cos_ref[...] + _dot(n, wr_ref[h]) * sin_ref[...]
        elif rope == "swap":
            cos_ref, sin_ref = refs[pos:pos + 2]
            pos += 2
            half = y.shape[1] // 2
            y = y * cos_ref[...] + jnp.concatenate([y[:, half:], y[:, :half]], axis=1) * sin_ref[...]
        if has_add:
            y = y + refs[pos][0, 0].astype(F32)
            pos += 1
        if scale != 1.0:
            y = y * scale
        if has_masks:
            mask_ref = refs[pos]
            for c in range(n_copy):
                o_ref[0, h * n_copy + c] = (y * mask_ref[c:c + 1, :]).astype(o_ref.dtype)
        else:
            o_ref[0, h] = y.astype(o_ref.dtype)


def proj_heads(n2d, w, batch, seq, rope=None, add=None, scale=1.0, masks=None):
    m, k = n2d.shape
    g, _, dh = w.shape
    tm = _tile(seq, 1024)
    ns = seq // tm
    weights = pl.BlockSpec((g, k, dh), lambda b, i: (0, 0, 0))
    in_specs = [pl.BlockSpec((tm, k), lambda b, i: (b * ns + i, 0)), weights]
    args = [n2d, w]
    table = pl.BlockSpec((tm, dh), lambda b, i: (i, 0))
    if rope is not None and rope[0] == "weights":
        in_specs += [weights, table, table]
        args += list(rope[1:])
    elif rope is not None:
        assert rope[0] == "swap" and dh % (2 * LANES) == 0
        in_specs += [table, table]
        args += list(rope[1:])
    if add is not None:
        in_specs.append(pl.BlockSpec((1, 1, tm, dh), lambda b, i: (b, 0, i, 0)))
        args.append(add)
    n_copy = 1
    if masks is not None:
        n_copy = masks.shape[0]
        in_specs.append(pl.BlockSpec((n_copy, dh), lambda b, i: (0, 0)))
        args.append(masks)
    return pl.pallas_call(
        functools.partial(_proj_heads_kernel, rope=None if rope is None else rope[0], has_add=add is not None,
                          has_masks=masks is not None, scale=scale),
        grid=(batch, ns),
        in_specs=in_specs,
        out_specs=pl.BlockSpec((1, g * n_copy, tm, dh), lambda b, i: (b, 0, i, 0)),
        out_shape=jax.ShapeDtypeStruct((batch, g * n_copy, seq, dh), BF16),
        compiler_params=_cparams(("parallel", "parallel")),
        name="proj_heads",
    )(*args)


def _proj_heads_t_kernel(n_ref, wt_ref, o_ref):
    dh = wt_ref.shape[1]
    pad = o_ref.shape[2] - dh
    row = lax.broadcasted_iota(jnp.int32, (pad, o_ref.shape[3]), 0)
    ones_rows = jnp.where(row == 0, 1.0, 0.0).astype(o_ref.dtype)
    n = n_ref[...]
    for h in range(wt_ref.shape[0]):
        o_ref[0, h, :dh, :] = _dot_nt(wt_ref[h], n).astype(o_ref.dtype)
        o_ref[0, h, dh:, :] = ones_rows


def proj_heads_t(n2d, w, batch, seq):
    m, k = n2d.shape
    g, _, dh = w.shape
    tm = _tile(seq, 1024)
    ns = seq // tm
    return pl.pallas_call(
        _proj_heads_t_kernel,
        grid=(batch, ns),
        in_specs=[pl.BlockSpec((tm, k), lambda b, i: (b * ns + i, 0)),
                  pl.BlockSpec((g, dh, k), lambda b, i: (0, 0, 0))],
        out_specs=pl.BlockSpec((1, g, dh + ONES_PAD, tm), lambda b, i: (b, 0, 0, i)),
        out_shape=jax.ShapeDtypeStruct((batch, g, dh + ONES_PAD, seq), BF16),
        compiler_params=_cparams(("parallel", "parallel")),
        name="proj_heads_t",
    )(n2d, jnp.swapaxes(w, 1, 2))


def _softmax_step_t(st, vt, m_scr, acc_scr, exp_dtype):
    m_prev = m_scr[...]
    m_new = jnp.maximum(m_prev, jnp.max(st, axis=0, keepdims=True).astype(F32))
    alpha = jnp.exp(m_prev - m_new)
    pt = jnp.exp((st - m_new.astype(st.dtype)).astype(exp_dtype)).astype(BF16)
    acc_scr[...] = alpha * acc_scr[...] + _dot(vt, pt)
    m_scr[...] = m_new


def _diag_visibility(d, c, tq, tk):
    if d * tk >= (c + 1) * tq:
        return "none"
    if (d + 1) * tk - 1 <= c * tq:
        return "full"
    return "partial"


def _causal_where(st, d, c):
    tk, tq = st.shape
    key = lax.broadcasted_iota(jnp.int32, (tk, tq), 0) + d * tk
    qry = lax.broadcasted_iota(jnp.int32, (tk, tq), 1) + c * tq
    return jnp.where(key <= qry, st, NEG_INF)


def _flash_sweep(n, r, sub_of, tq, tk, scores, vt_tile, m_scr, acc_scr, st_scr, exp_dtype=F32):
    n_chain = m_scr.shape[0]
    for c in range(n_chain):
        m_scr[c] = jnp.full(m_scr.shape[1:], NEG_INF, F32)
        acc_scr[c] = jnp.zeros(acc_scr.shape[1:], F32)
    n_slot = st_scr.shape[0]
    assert r % n_slot == 0
    everyone = (True,) * n_chain
    for c, st in enumerate(scores(0, everyone)):
        st_scr[0, c] = st.astype(st_scr.dtype)

    def body(jj, carry):
        for slot in range(n_slot):
            j = jj * n_slot + slot
            nxt = scores(j + 1, everyone)
            vt = vt_tile(j)
            for c in range(n_chain):
                st = st_scr[slot, c]
                st_scr[(slot + 1) % n_slot, c] = nxt[c].astype(st_scr.dtype)
                _softmax_step_t(st, vt, m_scr.at[c], acc_scr.at[c], exp_dtype)
        return carry

    lax.fori_loop(0, n // n_slot, body, 0)
    for d in range(r):
        vis = [_diag_visibility(d, sub_of(c), tq, tk) for c in range(n_chain)]
        need = tuple(d + 1 < r and _diag_visibility(d + 1, sub_of(c), tq, tk) != "none" for c in range(n_chain))
        nxt = scores(n + d + 1, need) if any(need) else None
        vt = vt_tile(n + d)
        for c in range(n_chain):
            if vis[c] != "none":
                st = st_scr[d % n_slot, c]
                if vis[c] == "partial":
                    st = _causal_where(st.astype(F32), d, sub_of(c)).astype(st_scr.dtype)
            if need[c]:
                st_scr[(d + 1) % n_slot, c] = nxt[c].astype(st_scr.dtype)
            if vis[c] != "none":
                _softmax_step_t(st, vt, m_scr.at[c], acc_scr.at[c], exp_dtype)


def _flash_kernel(q_ref, k_ref, vt_ref, o_ref, m_scr, acc_scr, st_scr, *, n_sub, tq, tk):
    i = pl.program_id(2)
    dv = o_ref.shape[2]
    r = n_sub * tq // tk

    def scores(j, need):
        k = k_ref[0, 0, pl.ds(pl.multiple_of(j * tk, tk), tk), :]
        return [_dot_nt(k, q_ref[0, 0, c * tq:(c + 1) * tq, :]) if need[c] else None for c in range(n_sub)]

    def vt_tile(j):
        return vt_ref[0, 0, :, pl.ds(pl.multiple_of(j * tk, tk), tk)]

    _flash_sweep(i * r, r, lambda c: c, tq, tk, scores, vt_tile, m_scr, acc_scr, st_scr, exp_dtype=BF16)
    for c in range(n_sub):
        o_t = acc_scr[c, :dv, :] / acc_scr[c, dv:dv + 1, :]
        o_ref[0, c * tq:(c + 1) * tq, :] = o_t.T.astype(o_ref.dtype)


def _score_slots(r):
    return 2 if r % 2 == 0 else 1


def _flash_tiles(s, max_sub):
    tq = _tile(s, 512)
    n_sub = next(n for n in (4, 2, 1) if n <= max_sub and s % (n * tq) == 0)
    return n_sub, tq, tq


def flash_causal(q, k, vt):
    b, h, s, dk = q.shape
    dve = vt.shape[2]
    dv = dve - ONES_PAD
    n_sub, tq, tk = _flash_tiles(s, max_sub=4)
    t = n_sub * tq
    return pl.pallas_call(
        functools.partial(_flash_kernel, n_sub=n_sub, tq=tq, tk=tk),
        grid=(b, h, s // t),
        in_specs=[pl.BlockSpec((1, 1, t, dk), lambda b_, h_, i: (b_, h_, i, 0)),
                  pl.BlockSpec((1, 1, s, dk), lambda b_, h_, i: (b_, h_, 0, 0)),
                  pl.BlockSpec((1, 1, dve, s), lambda b_, h_, i: (b_, h_, 0, 0))],
        out_specs=pl.BlockSpec((1, t, dv), lambda b_, h_, i: (b_, i, h_)),
        out_shape=jax.ShapeDtypeStruct((b, s, h * dv), BF16),
        scratch_shapes=[pltpu.VMEM((n_sub, 1, tq), F32), pltpu.VMEM((n_sub, dve, tq), F32),
                        pltpu.VMEM((_score_slots(n_sub * tq // tk), n_sub, tk, tq), SCORE_DTYPE)],
        compiler_params=_cparams(("parallel", "parallel", "arbitrary")),
        name="mla_flash",
    )(q, k, vt)


def _diff_kernel(q_ref, k_ref, vt_ref, lq1_ref, lk1_ref, lq2_ref, lk2_ref, sub_ref, o_ref,
                 m_scr, acc_scr, st_scr, *, n_sub, tq, tk, lambda_init):
    i = pl.program_id(2)
    vd = o_ref.shape[3]
    r = n_sub * tq // tk

    def scores(j, need):
        k = k_ref[0, 0, pl.ds(pl.multiple_of(j * tk, tk), tk), :]
        return [_dot_nt(k, q_ref[0, ch // n_sub, (ch % n_sub) * tq:(ch % n_sub + 1) * tq, :]) if need[ch] else None
                for ch in range(2 * n_sub)]

    def vt_tile(j):
        return vt_ref[0, 0, :, pl.ds(pl.multiple_of(j * tk, tk), tk)]

    _flash_sweep(i * r, r, lambda ch: ch % n_sub, tq, tk, scores, vt_tile, m_scr, acc_scr, st_scr, exp_dtype=BF16)

    lam = (jnp.exp(jnp.sum(lq1_ref[...] * lk1_ref[...], axis=-1, keepdims=True))
           - jnp.exp(jnp.sum(lq2_ref[...] * lk2_ref[...], axis=-1, keepdims=True)) + lambda_init)
    lane_pad = (-vd) % LANES
    for c in range(n_sub):
        o_t = (acc_scr[c, :vd, :] / acc_scr[c, vd:vd + 1, :]
               - lam * (acc_scr[n_sub + c, :vd, :] / acc_scr[n_sub + c, vd:vd + 1, :]))
        o_t = o_t * lax.rsqrt(jnp.mean(o_t * o_t, axis=0, keepdims=True) + NORM_EPS)
        o_t = jnp.concatenate([o_t, jnp.zeros((lane_pad, tq), F32)], axis=0)
        o_ref[0, 0, c * tq:(c + 1) * tq, :] = (
            o_t.T[:, :vd] * sub_ref[...] * (1.0 - lambda_init)).astype(o_ref.dtype)


def diff_attention(q, k, vt, lq1, lk1, lq2, lk2, subln, lambda_init):
    b, h2, s, dk = q.shape
    h = h2 // 2
    hd = DIFF_HD
    vde = vt.shape[2]
    vd = vde - ONES_PAD
    n_sub, tq, tk = _flash_tiles(s, max_sub=4)
    t = n_sub * tq
    vec = lambda a: a.reshape(1, -1).astype(F32)
    small = lambda n: pl.BlockSpec((1, n), lambda b_, h_, i: (0, 0))
    return pl.pallas_call(
        functools.partial(_diff_kernel, n_sub=n_sub, tq=tq, tk=tk, lambda_init=lambda_init),
        grid=(b, h, s // t),
        in_specs=[pl.BlockSpec((1, 2, t, dk), lambda b_, h_, i: (b_, h_, i, 0)),
                  pl.BlockSpec((1, 1, s, dk), lambda b_, h_, i: (b_, h_, 0, 0)),
                  pl.BlockSpec((1, 1, vde, s), lambda b_, h_, i: (b_, h_, 0, 0)),
                  small(hd), small(hd), small(hd), small(hd), small(vd)],
        out_specs=pl.BlockSpec((1, 1, t, vd), lambda b_, h_, i: (b_, h_, i, 0)),
        out_shape=jax.ShapeDtypeStruct((b, h, s, vd), BF16),
        scratch_shapes=[pltpu.VMEM((2 * n_sub, 1, tq), F32), pltpu.VMEM((2 * n_sub, vde, tq), F32),
                        pltpu.VMEM((_score_slots(n_sub * tq // tk), 2 * n_sub, tk, tq), SCORE_DTYPE)],
        compiler_params=_cparams(("parallel", "parallel", "arbitrary")),
        name="diff_flash",
    )(q, k, vt, vec(lq1), vec(lk1), vec(lq2), vec(lk2), vec(subln))


def _compress_kernel(a_ref, pa_ref, pb_ref, w1a_ref, w1b_ref, w2_ref, o_ref, *, transposed):
    ng = a_ref.shape[2]
    rows = min(ng, 256)
    us, vs = [], []
    for r0 in range(0, ng, rows):
        a = a_ref[0, 0, r0:r0 + rows, :].astype(F32)
        us.append(_dot((a + pa_ref[...]).astype(BF16), w1a_ref[...]))
        vs.append(_dot((a + pb_ref[...]).astype(BF16), w1b_ref[...]))
    u = jnp.concatenate(us, axis=0)
    v = jnp.concatenate(vs, axis=0)
    hdn = u + pltpu.roll(v, ng - 1, 0)
    hdn = (hdn * _sigmoid(hdn)).astype(BF16)
    if transposed:
        d_out = w2_ref.shape[0]
        o_ref[0, :d_out, :] = _dot_nt(w2_ref[...], hdn).astype(o_ref.dtype)
        row = lax.broadcasted_iota(jnp.int32, (o_ref.shape[1] - d_out, ng), 0)
        o_ref[0, d_out:, :] = jnp.where(row == 0, 1.0, 0.0).astype(o_ref.dtype)
    else:
        o_ref[0] = _dot(hdn, w2_ref[...]).astype(o_ref.dtype)


def nsa_compress(tok, g_idx, pos, w1, w2, transposed=False):
    assert NSA_CMP_LEN == 2 * NSA_CMP_STRIDE
    b, _, ng, wd = tok.shape
    d = wd // NSA_CMP_STRIDE
    d_out = w2.shape[1]
    pos_flat = pos.astype(F32).reshape(1, NSA_CMP_LEN * d)
    w1 = w1.astype(BF16)
    w2 = w2.astype(BF16).T if transposed else w2.astype(BF16)
    out_block = (1, d_out + ONES_PAD, ng) if transposed else (1, ng, d_out)
    full = lambda shape: pl.BlockSpec(shape, lambda b_: (0,) * len(shape))
    return pl.pallas_call(
        functools.partial(_compress_kernel, transposed=transposed),
        grid=(b,),
        in_specs=[pl.BlockSpec((1, 1, ng, wd), lambda b_: (b_, g_idx, 0, 0)),
                  full((1, wd)), full((1, wd)), full((wd, NSA_CMP_HIDDEN)), full((wd, NSA_CMP_HIDDEN)),
                  full(w2.shape)],
        out_specs=pl.BlockSpec(out_block, lambda b_: (b_, 0, 0)),
        out_shape=jax.ShapeDtypeStruct((b,) + out_block[1:], BF16),
        compiler_params=_cparams(("parallel",)),
        name="nsa_compress",
    )(tok, pos_flat[:, :wd], pos_flat[:, wd:], w1[:wd], w1[wd:], w2)


def _split3(x):
    hi = x.astype(BF16)
    r = x - hi.astype(F32)
    mid = r.astype(BF16)
    lo = (r - mid.astype(F32)).astype(BF16)
    return hi, mid, lo


def _cmp_topk_kernel(q_ref, kc_ref, vct_ref, wselt_ref, oc_ref, selt_ref, *, tq, k_top):
    i = pl.program_id(1)
    qs = i * tq
    nc = kc_ref.shape[1]
    n_sel = wselt_ref.shape[0]
    kc = kc_ref[0]
    vct = vct_ref[0]
    qpos_c = qs + lax.broadcasted_iota(jnp.int32, (nc, tq), 1)
    cend = lax.broadcasted_iota(jnp.int32, (nc, tq), 0) * NSA_CMP_STRIDE + (NSA_CMP_LEN - 1)
    valid_c = cend <= qpos_c
    any_valid = qs + lax.broadcasted_iota(jnp.int32, (1, tq), 1) >= NSA_CMP_LEN - 1
    imp_t = jnp.zeros((nc, tq), F32)
    for h in range(NSA_HEADS):
        st = jnp.where(valid_c, _dot_nt(kc, q_ref[0, h]), NEG_INF)
        e = jnp.exp(st - jnp.max(st, axis=0, keepdims=True))
        inv = jnp.where(any_valid, 1.0 / jnp.sum(e, axis=0, keepdims=True), 0.0)
        o_t = _dot(vct, e.astype(BF16))[:NSA_DV, :] * inv
        oc_ref[0, :, h * NSA_DV:(h + 1) * NSA_DV] = o_t.T
        imp_t = imp_t + e * inv

    wselt = wselt_ref[...]
    hi, mid, lo = _split3(imp_t)
    imp_sel = _dot(wselt, hi) + _dot(wselt, mid) + _dot(wselt, lo)

    blk = lax.broadcasted_iota(jnp.int32, (n_sel, tq), 0)
    qpos = qs + lax.broadcasted_iota(jnp.int32, (n_sel, tq), 1)
    cur = lax.shift_right_arithmetic(qpos, int(math.log2(NSA_SEL_LEN)))
    forced = (blk == 0) | (blk == cur) | (blk == cur - 1)
    valid_s = blk * NSA_SEL_LEN <= qpos
    score = jnp.where(valid_s, jnp.where(forced, NSA_FORCE_SCORE, imp_sel), NEG_INF)
    blk_f = blk.astype(F32)
    sel = jnp.zeros((n_sel, tq), F32)
    for _ in range(k_top):
        mx = jnp.max(score, axis=0, keepdims=True)
        first = jnp.min(jnp.where(score == mx, blk_f, float(n_sel)), axis=0, keepdims=True)
        hit = blk_f == first
        sel = jnp.where(hit, 1.0, sel)
        score = jnp.where(hit, REMOVED, score)
    selt_ref[0] = sel.astype(selt_ref.dtype)


def _sel_weight_matrix(n_pad, n_sel):
    r_c = NSA_CMP_LEN // NSA_CMP_STRIDE
    ratio = NSA_SEL_LEN // NSA_CMP_STRIDE
    overlap_w = [max(0, min(o * NSA_CMP_STRIDE + NSA_CMP_LEN, NSA_SEL_LEN) - max(o * NSA_CMP_STRIDE, 0))
                 / NSA_CMP_STRIDE for o in range(-(r_c - 1), ratio)]
    w = np.zeros((n_pad, n_sel), np.float32)
    for n in range(n_sel):
        for u, w_u in enumerate(overlap_w):
            c = ratio * n + u - (r_c - 1)
            if 0 <= c < n_pad:
                w[c, n] = w_u
    return w


def nsa_cmp_topk(q, kc, vct):
    b, h, s, dk = q.shape
    nc = kc.shape[1]
    dve = vct.shape[1]
    n_sel = s // NSA_SEL_LEN
    assert NSA_SEL_LEN & (NSA_SEL_LEN - 1) == 0
    tq = _tile(s, 512)
    wselt = jnp.asarray(_sel_weight_matrix(nc, n_sel).T, BF16)
    return pl.pallas_call(
        functools.partial(_cmp_topk_kernel, tq=tq, k_top=min(NSA_TOPK, n_sel)),
        grid=(b, s // tq),
        in_specs=[pl.BlockSpec((1, h, tq, dk), lambda b_, i: (b_, 0, i, 0)),
                  pl.BlockSpec((1, nc, dk), lambda b_, i: (b_, 0, 0)),
                  pl.BlockSpec((1, dve, nc), lambda b_, i: (b_, 0, 0)),
                  pl.BlockSpec((n_sel, nc), lambda b_, i: (0, 0))],
        out_specs=[pl.BlockSpec((1, tq, h * NSA_DV), lambda b_, i: (b_, i, 0)),
                   pl.BlockSpec((1, n_sel, tq), lambda b_, i: (b_, 0, i))],
        out_shape=[jax.ShapeDtypeStruct((b, s, h * NSA_DV), F32), jax.ShapeDtypeStruct((b, n_sel, s), BF16)],
        compiler_params=_cparams(("parallel", "parallel")),
        name="nsa_cmp_topk",
    )(q, kc, vct, wselt)


def _nsa_sel_kernel(q_ref, k_ref, vt_ref, sel_ref, o_ref, m_scr, acc_scr, st_scr, *, n_sub, tq, tk):
    i = pl.program_id(1)
    n_sel = sel_ref.shape[1]
    bpt = tk // NSA_SEL_LEN
    shift = int(math.log2(NSA_SEL_LEN))
    r = n_sub * tq // tk

    def scores(j, need):
        k = k_ref[0, 0, pl.ds(pl.multiple_of(j * tk, tk), tk), :]
        blk_key = lax.shift_right_arithmetic(lax.broadcasted_iota(jnp.int32, (tk, n_sel), 0), shift) + j * bpt
        blk_col = lax.broadcasted_iota(jnp.int32, (tk, n_sel), 1)
        expand_t = jnp.where(blk_key == blk_col, 1.0, 0.0).astype(BF16)
        out = []
        for c in range(n_sub):
            rows = slice(c * tq, (c + 1) * tq)
            if not any(need[c * NSA_HEADS:(c + 1) * NSA_HEADS]):
                out += [None] * NSA_HEADS
                continue
            keep_t = _dot(expand_t, sel_ref[0, :, rows]) > 0.5
            out += [jnp.where(keep_t, _dot_nt(k, q_ref[0, h, rows, :]), NEG_INF) if need[c * NSA_HEADS + h] else None
                    for h in range(NSA_HEADS)]
        return out

    def vt_tile(j):
        return vt_ref[0, 0, :, pl.ds(pl.multiple_of(j * tk, tk), tk)]

    _flash_sweep(i * r, r, lambda ch: ch // NSA_HEADS, tq, tk, scores, vt_tile, m_scr, acc_scr, st_scr,
                 exp_dtype=BF16)
    for ch in range(n_sub * NSA_HEADS):
        c, h = divmod(ch, NSA_HEADS)
        o_t = acc_scr[ch, :NSA_DV, :] / acc_scr[ch, NSA_DV:NSA_DV + 1, :]
        o_ref[0, c * tq:(c + 1) * tq, h * NSA_DV:(h + 1) * NSA_DV] = o_t.T


def _nsa_sel_tiles(s):
    return _flash_tiles(s, max_sub=2)


def nsa_selected(q, k3, k_idx, vt, sel):
    b, h, s, dk = q.shape
    n_sel = sel.shape[1]
    dve = vt.shape[2]
    n_sub, tq, tk = _nsa_sel_tiles(s)
    t = n_sub * tq
    n_chain = n_sub * h
    assert tk % NSA_SEL_LEN == 0 and t % tk == 0
    return pl.pallas_call(
        functools.partial(_nsa_sel_kernel, n_sub=n_sub, tq=tq, tk=tk),
        grid=(b, s // t),
        in_specs=[pl.BlockSpec((1, h, t, dk), lambda b_, i: (b_, 0, i, 0)),
                  pl.BlockSpec((1, 1, s, dk), lambda b_, i: (b_, k_idx, 0, 0)),
                  pl.BlockSpec((1, 1, dve, s), lambda b_, i: (b_, 0, 0, 0)),
                  pl.BlockSpec((1, n_sel, t), lambda b_, i: (b_, 0, i))],
        out_specs=pl.BlockSpec((1, t, h * NSA_DV), lambda b_, i: (b_, i, 0)),
        out_shape=jax.ShapeDtypeStruct((b, s, h * NSA_DV), F32),
        scratch_shapes=[pltpu.VMEM((n_chain, 1, tq), F32), pltpu.VMEM((n_chain, dve, tq), F32),
                        pltpu.VMEM((_score_slots(t // tk), n_chain, tk, tq), SCORE_DTYPE)],
        compiler_params=_cparams(("parallel", "arbitrary")),
        name="nsa_selected",
    )(q, k3, vt, sel)


def _nsa_win_kernel(q_ref, kp_ref, kc_ref, vp_ref, vc_ref, o_ref, *, t):
    i = pl.program_id(1)
    row = lax.broadcasted_iota(jnp.int32, (t, t), 0)
    col = lax.broadcasted_iota(jnp.int32, (t, t), 1)
    keep_prev = (col > row) & (i > 0)
    keep_cur = col <= row
    kp, kc, vp, vc = kp_ref[0, 0], kc_ref[0, 0], vp_ref[0], vc_ref[0]
    for h in range(NSA_HEADS):
        q = q_ref[0, h]
        sp = jnp.where(keep_prev, _dot_nt(q, kp), NEG_INF)
        sc = jnp.where(keep_cur, _dot_nt(q, kc), NEG_INF)
        m = jnp.maximum(jnp.max(sp, axis=-1, keepdims=True), jnp.max(sc, axis=-1, keepdims=True))
        pp = jnp.exp(sp - m)
        pc = jnp.exp(sc - m)
        l = jnp.sum(pp, axis=-1, keepdims=True) + jnp.sum(pc, axis=-1, keepdims=True)
        o = _dot(pp.astype(BF16), vp) + _dot(pc.astype(BF16), vc)
        o_ref[0, :, h * NSA_DV:(h + 1) * NSA_DV] = o / l


def nsa_window(q, k3, k_idx, v3, v_idx):
    b, h, s, dk = q.shape
    t = _tile(s, NSA_WINDOW)
    assert t == NSA_WINDOW, "window kernel needs the query tile to equal the window"
    prev = lambda i: jnp.maximum(i - 1, 0)
    return pl.pallas_call(
        functools.partial(_nsa_win_kernel, t=t),
        grid=(b, s // t),
        in_specs=[pl.BlockSpec((1, h, t, dk), lambda b_, i: (b_, 0, i, 0)),
                  pl.BlockSpec((1, 1, t, dk), lambda b_, i: (b_, k_idx, prev(i), 0)),
                  pl.BlockSpec((1, 1, t, dk), lambda b_, i: (b_, k_idx, i, 0)),
                  pl.BlockSpec((1, t, NSA_DV), lambda b_, i: (b_, prev(i), v_idx)),
                  pl.BlockSpec((1, t, NSA_DV), lambda b_, i: (b_, i, v_idx))],
        out_specs=pl.BlockSpec((1, t, h * NSA_DV), lambda b_, i: (b_, i, 0)),
        out_shape=jax.ShapeDtypeStruct((b, s, h * NSA_DV), F32),
        compiler_params=_cparams(("parallel", "parallel")),
        name="nsa_window",
    )(q, k3, k3, v3, v3)


def _merge_kernel(x_ref, om_ref, oc_ref, os_ref, ow_ref, gn_ref, od_ref, gm_ref, gs_ref, gd_ref,
                  wm_ref, wn_ref, wd_ref, wo_ref, ex_ref, o_ref, onsa_scr, mixed_scr, *, n_j):
    j = pl.program_id(1)
    nsa_out = NSA_HEADS * NSA_DV
    tn = mixed_scr.shape[2]

    @pl.when(j == 0)
    def _():
        g = gn_ref[...]
        hi = g.astype(BF16)
        lo = (g - hi.astype(F32)).astype(BF16)
        ge = _dot(hi, ex_ref[...]) + _dot(lo, ex_ref[...])
        onsa = (ge[:, :nsa_out] * oc_ref[...] + ge[:, nsa_out:2 * nsa_out] * os_ref[...]
                + ge[:, 2 * nsa_out:] * ow_ref[...])
        onsa_scr[...] = onsa.astype(BF16)

    @pl.when(j < n_j)
    def _():
        ym = _dot(om_ref[...], wm_ref[...])
        yn = _dot(onsa_scr[...], wn_ref[...])
        yd = _dot(od_ref[0, 0], wd_ref[0])
        for h in range(1, DIFF_HEADS):
            yd = yd + _dot(od_ref[0, h], wd_ref[h])
        mixed = (gm_ref[...].astype(F32) * ym + gs_ref[...].astype(F32) * yn + gd_ref[...].astype(F32) * yd)
        mixed_scr[j] = mixed.astype(BF16)

    @pl.when(j >= n_j)
    def _():
        y = _dot(mixed_scr[0], wo_ref[:tn, :])
        for c in range(1, n_j):
            y = y + _dot(mixed_scr[c], wo_ref[c * tn:(c + 1) * tn, :])
        o_ref[...] = x_ref[...] + y


def _gate_expand_matrix():
    nsa_out = NSA_HEADS * NSA_DV
    e = np.zeros((LANES, 3 * nsa_out), np.float32)
    for h in range(NSA_HEADS):
        for c in range(3):
            e[h * 3 + c, c * nsa_out + h * NSA_DV:c * nsa_out + (h + 1) * NSA_DV] = 1.0
    return e


def merge(x2d, o_mla, o_c, o_s, o_w, g_nsa, o_diff, g_merge, w_br_mla, w_br_nsa, w_br_diff, w_out, seq):
    m, d = x2d.shape
    tm = _tile(seq, 512)
    ns = seq // tm
    tn = _tile(d, 512)
    n_j = d // tn
    nsa_out = NSA_HEADS * NSA_DV
    mla_out = o_mla.shape[-1]
    ex = jnp.asarray(_gate_expand_matrix(), BF16)
    row = lambda w: pl.BlockSpec((tm, w), lambda i, j: (i, 0))
    mix = lambda j: jnp.minimum(j, n_j - 1)
    out = lambda j: jnp.maximum(j - n_j, 0)
    return pl.pallas_call(
        functools.partial(_merge_kernel, n_j=n_j),
        grid=(m // tm, 2 * n_j),
        in_specs=[pl.BlockSpec((tm, tn), lambda i, j: (i, out(j))),
                  row(mla_out), row(nsa_out), row(nsa_out), row(nsa_out), row(LANES),
                  pl.BlockSpec((1, DIFF_HEADS, tm, DIFF_VD), lambda i, j: (i // ns, 0, i % ns, 0)),
                  pl.BlockSpec((tm, tn), lambda i, j: (i, mix(j))),
                  pl.BlockSpec((tm, tn), lambda i, j: (i, n_j + mix(j))),
                  pl.BlockSpec((tm, tn), lambda i, j: (i, 2 * n_j + mix(j))),
                  pl.BlockSpec((mla_out, tn), lambda i, j: (0, mix(j))),
                  pl.BlockSpec((nsa_out, tn), lambda i, j: (0, mix(j))),
                  pl.BlockSpec((DIFF_HEADS, DIFF_VD, tn), lambda i, j: (0, 0, mix(j))),
                  pl.BlockSpec((d, tn), lambda i, j: (0, out(j))),
                  pl.BlockSpec((LANES, 3 * nsa_out), lambda i, j: (0, 0))],
        out_specs=pl.BlockSpec((tm, tn), lambda i, j: (i, out(j))),
        out_shape=jax.ShapeDtypeStruct((m, d), F32),
        scratch_shapes=[pltpu.VMEM((tm, nsa_out), BF16), pltpu.VMEM((n_j, tm, tn), BF16)],
        compiler_params=_cparams(("parallel", "arbitrary")),
        name="merge",
    )(x2d, o_mla, o_c, o_s, o_w, g_nsa, o_diff, g_merge, g_merge, g_merge,
      w_br_mla, w_br_nsa, w_br_diff, w_out, ex)


def _rope_tables(dim, seq, lead=0):
    inv = ROPE_THETA ** (-jnp.arange(0, dim, 2, dtype=F32) / dim)
    ang = jnp.arange(seq, dtype=F32)[:, None] * inv[None, :]
    cos, sin = jnp.cos(ang), jnp.sin(ang)
    cos = jnp.concatenate([jnp.ones((seq, lead), F32), cos, cos], axis=-1)
    sin = jnp.concatenate([jnp.zeros((seq, lead), F32), sin, sin], axis=-1)
    return cos, sin


def _heads(w, g, dh):
    return w.reshape(w.shape[0], g, dh).transpose(1, 0, 2)


def _rot_cols(w, lead=0):
    half = (w.shape[-1] - lead) // 2
    x1 = w[..., lead:lead + half]
    x2 = w[..., lead + half:]
    return jnp.concatenate([jnp.zeros_like(w[..., :lead]), -x2, x1], axis=-1)


def _pad_halves(w, axis=-1):
    w = jnp.moveaxis(w, axis, -1)
    half = w.shape[-1] // 2
    z = jnp.zeros(w.shape[:-1] + ((-half) % LANES,), w.dtype)
    out = jnp.concatenate([w[..., :half], z, w[..., half:], z], axis=-1)
    return jnp.moveaxis(out, -1, axis)


def _swap_tables(dim, seq, copies=1):
    inv = ROPE_THETA ** (-jnp.arange(0, dim, 2, dtype=F32) / dim)
    ang = jnp.arange(seq, dtype=F32)[:, None] * inv[None, :]
    cos = jnp.tile(jnp.cos(ang), (1, copies))
    sin = jnp.tile(jnp.sin(ang), (1, copies))
    return _pad_halves(jnp.concatenate([cos, cos], axis=-1)), _pad_halves(jnp.concatenate([-sin, sin], axis=-1))


def _pair_heads(w):
    k = w.shape[0]
    half = DIFF_HD // 2
    w = w.reshape(k, DIFF_HEADS, 2, 2, half)
    w = w.transpose(1, 0, 3, 2, 4).reshape(DIFF_HEADS, k, 2 * DIFF_HD)
    return _pad_halves(w)


def _pair_masks():
    half = DIFF_HD // 2
    m = np.zeros((2, 2, 2, half), np.float32)
    for c in range(2):
        m[c, :, c, :] = 1.0
    return _pad_halves(jnp.asarray(m.reshape(2, 2 * DIFF_HD)))


def _col_offsets():
    sizes = (MLA_Q_LORA, MLA_KV_LORA, MLA_ROPE,
             NSA_HEADS * NSA_DK, NSA_DK, NSA_DV, NSA_DK, NSA_DV, NSA_DK, NSA_DV, NSA_HEADS * 3,
             DIFF_HEADS * 2 * DIFF_HD, DIFF_HEADS * 2 * DIFF_HD, DIFF_HEADS * DIFF_VD)
    names = ("c_q", "c_kv", "k_rope", "nsa_q", "nsa_kc", "nsa_vc", "nsa_ks", "nsa_vs", "nsa_kw", "nsa_vw",
             "nsa_g", "d_q", "d_k", "d_v")
    offs = {}
    o = 0
    for nme, sz in zip(names, sizes):
        offs[nme] = (o, o + sz)
        o += sz
    offs["merge"] = (o, None)
    return offs


def _mixers(n2d, batch, seq, layer, w_in, p):
    offs = _col_offsets()
    col = lambda name: w_in[:, offs[name][0]:offs[name][1]]
    bf = lambda a: a.astype(BF16)

    lat = proj_plain(n2d, bf(jnp.concatenate([col("c_q"), col("c_kv")], axis=1)), F32)
    n_q, n_kv = mla_norm(lat, p["mla_q_norm"], p["mla_kv_norm"])
    cos_m, sin_m = _rope_tables(MLA_ROPE, seq, lead=MLA_NOPE)
    w_kr = jnp.concatenate([jnp.zeros((w_in.shape[0], MLA_NOPE), F32), col("k_rope")], axis=1)[None]
    kpe = proj_heads(n2d, bf(w_kr), batch, seq, rope=("weights", bf(_rot_cols(w_kr, MLA_NOPE)), cos_m, sin_m))
    w_uq = _heads(p["mla_w_uq"], MLA_HEADS, MLA_QK)
    q_mla = proj_heads(n_q, bf(w_uq), batch, seq, rope=("weights", bf(_rot_cols(w_uq, MLA_NOPE)), cos_m, sin_m),
                       scale=MLA_QK ** -0.5)
    w_ukv = _heads(p["mla_w_ukv"], MLA_HEADS, MLA_NOPE + MLA_V)
    w_uk = jnp.concatenate([w_ukv[..., :MLA_NOPE], jnp.zeros(w_ukv.shape[:2] + (MLA_ROPE,), F32)], axis=-1)
    k_mla = proj_heads(n_kv, bf(w_uk), batch, seq, add=kpe)
    vt_mla = proj_heads_t(n_kv, bf(w_ukv[..., MLA_NOPE:]), batch, seq)
    o_mla = flash_causal(q_mla, k_mla, vt_mla)

    cos_d, sin_d = _swap_tables(DIFF_HD, seq, copies=2)
    q_d = proj_heads(n2d, bf(_pair_heads(col("d_q"))), batch, seq, rope=("swap", cos_d, sin_d),
                     scale=DIFF_HD ** -0.5, masks=_pair_masks())
    k_d = proj_heads(n2d, bf(_pair_heads(col("d_k"))), batch, seq, rope=("swap", cos_d, sin_d))
    vt_d = proj_heads_t(n2d, bf(_heads(col("d_v"), DIFF_HEADS, DIFF_VD)), batch, seq)
    lambda_init = 0.8 - 0.6 * math.exp(-0.3 * layer)
    o_diff = diff_attention(q_d, k_d, vt_d, p["diff_lam_q1"], p["diff_lam_k1"], p["diff_lam_q2"],
                            p["diff_lam_k2"], p["diff_subln"], lambda_init)

    cos_n, sin_n = _swap_tables(NSA_DK, seq)
    w_nq = _pad_halves(_heads(col("nsa_q"), NSA_HEADS, NSA_DK))
    q_n = proj_heads(n2d, bf(w_nq), batch, seq, rope=("swap", cos_n, sin_n), scale=NSA_DK ** -0.5)
    w_nk = _pad_halves(jnp.stack([col("nsa_kc"), col("nsa_ks"), col("nsa_kw")], axis=0))
    k_n = proj_heads(n2d, bf(w_nk), batch, seq, rope=("swap", cos_n, sin_n))
    dk_pad = w_nk.shape[-1]
    v_n = proj_plain(n2d, bf(jnp.concatenate([col("nsa_vc"), col("nsa_vw")], axis=1)), BF16)
    v_n = v_n.reshape(batch, seq, 2 * NSA_DV)
    vt_slc = proj_heads_t(n2d, bf(col("nsa_vs"))[None], batch, seq)
    w_g = jnp.concatenate([col("nsa_g"), jnp.zeros((w_in.shape[0], LANES - NSA_HEADS * 3), F32)], axis=1)
    g_nsa = proj_plain(n2d, bf(w_g), F32, sigmoid=True)

    ng = seq // NSA_CMP_STRIDE
    w1k = _pad_halves(p["nsa_cmp_k_w1"].reshape(NSA_CMP_LEN, NSA_DK, NSA_CMP_HIDDEN), axis=1)
    kc = nsa_compress(k_n.reshape(batch, 3, ng, NSA_CMP_STRIDE * dk_pad), 0,
                      _pad_halves(p["nsa_cmp_k_pos"]), w1k.reshape(NSA_CMP_LEN * dk_pad, NSA_CMP_HIDDEN),
                      _pad_halves(p["nsa_cmp_k_w2"]))
    vc_tok = v_n[:, :, :NSA_DV].reshape(batch, 1, ng, NSA_CMP_STRIDE * NSA_DV)
    vct = nsa_compress(vc_tok, 0, p["nsa_cmp_v_pos"], p["nsa_cmp_v_w1"], p["nsa_cmp_v_w2"], transposed=True)
    o_c, sel = nsa_cmp_topk(q_n, kc, vct)
    o_s = nsa_selected(q_n, k_n, 1, vt_slc, sel)
    o_w = nsa_window(q_n, k_n, 2, v_n, 1)

    g_merge = proj_plain(n2d, bf(w_in[:, offs["merge"][0]:]), BF16, sigmoid=True)
    m = batch * seq
    return (o_mla.reshape(m, -1), o_c.reshape(m, -1), o_s.reshape(m, -1), o_w.reshape(m, -1), g_nsa,
            o_diff, g_merge)


def kernel(x, ffn1_norm, ffn1_w_in, ffn1_w_out, mix_norm, w_in, mla_q_norm, mla_kv_norm, mla_w_uq, mla_w_ukv, nsa_cmp_k_pos, nsa_cmp_k_w1, nsa_cmp_k_w2, nsa_cmp_v_pos, nsa_cmp_v_w1, nsa_cmp_v_w2, diff_lam_q1, diff_lam_k1, diff_lam_q2, diff_lam_k2, diff_subln, w_br_mla, w_br_nsa, w_br_diff, w_out, ffn2_norm, ffn2_w_in, ffn2_w_out, final_norm):
    batch, seq, d = x.shape
    depth = w_in.shape[0]
    x2d = x.reshape(batch * seq, d)
    bf = lambda a: a.astype(BF16)
    for l in range(depth):
        p = {"mla_q_norm": mla_q_norm[l], "mla_kv_norm": mla_kv_norm[l], "mla_w_uq": mla_w_uq[l],
             "mla_w_ukv": mla_w_ukv[l], "nsa_cmp_k_pos": nsa_cmp_k_pos[l], "nsa_cmp_k_w1": nsa_cmp_k_w1[l],
             "nsa_cmp_k_w2": nsa_cmp_k_w2[l], "nsa_cmp_v_pos": nsa_cmp_v_pos[l], "nsa_cmp_v_w1": nsa_cmp_v_w1[l],
             "nsa_cmp_v_w2": nsa_cmp_v_w2[l], "diff_lam_q1": diff_lam_q1[l], "diff_lam_k1": diff_lam_k1[l],
             "diff_lam_q2": diff_lam_q2[l], "diff_lam_k2": diff_lam_k2[l], "diff_subln": diff_subln[l]}
        x2d = ffn(x2d, ffn1_norm[l], bf(ffn1_w_in[l]), bf(ffn1_w_out[l]))
        n2d = rmsnorm(x2d, mix_norm[l], BF16)
        o_mla, o_c, o_s, o_w, g_nsa, o_diff, g_merge = _mixers(n2d, batch, seq, l, w_in[l], p)
        x2d = merge(x2d, o_mla, o_c, o_s, o_w, g_nsa, o_diff, g_merge, bf(w_br_mla[l]), bf(w_br_nsa[l]),
                    bf(w_br_diff[l]).reshape(DIFF_HEADS, DIFF_VD, d), bf(w_out[l]), seq)
        x2d = ffn(x2d, ffn2_norm[l], bf(ffn2_w_in[l]), bf(ffn2_w_out[l]))
    return rmsnorm(x2d, final_norm, F32).reshape(batch, seq, d)
```

```python
import functools
import math

import numpy as np
import jax
import jax.numpy as jnp
from jax import lax
from jax.experimental import pallas as pl
from jax.experimental.pallas import tpu as pltpu

F32 = jnp.float32
BF16 = jnp.bfloat16

NORM_EPS = 1e-6
ROPE_THETA = 10000.0
NEG_INF = -1e30
REMOVED = -3e38
N_BRANCH = 3

MLA_HEADS = 6
MLA_Q_LORA = 768
MLA_KV_LORA = 512
MLA_NOPE = 128
MLA_ROPE = 64
MLA_V = 128
MLA_QK = MLA_NOPE + MLA_ROPE

NSA_HEADS = 4
NSA_DK = 192
NSA_DV = 128
NSA_CMP_LEN = 32
NSA_CMP_STRIDE = 16
NSA_CMP_HIDDEN = 256
NSA_SEL_LEN = 64
NSA_TOPK = 16
NSA_WINDOW = 512
NSA_FORCE_SCORE = 1e6

DIFF_HEADS = 4
DIFF_HD = 96
DIFF_VD = 2 * DIFF_HD

LANES = 128
ONES_PAD = 16
SCORE_DTYPE = BF16
VMEM_LIMIT_MB = 56


def _cparams(dims, vmem_mb=VMEM_LIMIT_MB):
    return pltpu.CompilerParams(dimension_semantics=dims, vmem_limit_bytes=vmem_mb * 2**20)


def _sigmoid(x):
    return 1.0 / (1.0 + jnp.exp(-x))


def _dot(a, b):
    return jnp.dot(a, b, preferred_element_type=F32)


def _dot_nt(a, b):
    return lax.dot_general(a, b, (((1,), (1,)), ((), ())), preferred_element_type=F32)


def _tile(n, pref):
    t = min(n, pref)
    assert n % t == 0, (n, t)
    return t


def _mla_norm_kernel(lat_ref, wq_ref, wkv_ref, nq_ref, nkv_ref):
    lat = lat_ref[...]
    cq = lat[:, :MLA_Q_LORA]
    ckv = lat[:, MLA_Q_LORA:]
    nq = cq * lax.rsqrt(jnp.mean(cq * cq, axis=-1, keepdims=True) + NORM_EPS)
    nkv = ckv * lax.rsqrt(jnp.mean(ckv * ckv, axis=-1, keepdims=True) + NORM_EPS)
    nq_ref[...] = (nq * wq_ref[...]).astype(BF16)
    nkv_ref[...] = (nkv * wkv_ref[...]).astype(BF16)


def mla_norm(lat, wq, wkv):
    m, d = lat.shape
    tm = _tile(m, 1024)
    return pl.pallas_call(
        _mla_norm_kernel,
        grid=(m // tm,),
        in_specs=[pl.BlockSpec((tm, d), lambda i: (i, 0)),
                  pl.BlockSpec((1, MLA_Q_LORA), lambda i: (0, 0)),
                  pl.BlockSpec((1, MLA_KV_LORA), lambda i: (0, 0))],
        out_specs=[pl.BlockSpec((tm, MLA_Q_LORA), lambda i: (i, 0)),
                   pl.BlockSpec((tm, MLA_KV_LORA), lambda i: (i, 0))],
        out_shape=[jax.ShapeDtypeStruct((m, MLA_Q_LORA), BF16), jax.ShapeDtypeStruct((m, MLA_KV_LORA), BF16)],
        compiler_params=_cparams(("parallel",)),
        name="mla_norm",
    )(lat, wq.reshape(1, -1).astype(F32), wkv.reshape(1, -1).astype(F32))


def _ffn_kernel(x_ref, nw_ref, wg_ref, wu_ref, wo_ref, *rest, n_f, post_norm):
    if post_norm:
        pw_ref, o_ref, p_ref, n_scr = rest
    else:
        o_ref, n_scr = rest
    f = pl.program_id(1)

    @pl.when(f == 0)
    def _():
        x = x_ref[...]
        y = x * lax.rsqrt(jnp.mean(x * x, axis=-1, keepdims=True) + NORM_EPS)
        n_scr[...] = (y * nw_ref[...]).astype(BF16)
        o_ref[...] = jnp.zeros_like(o_ref)

    n = n_scr[...]
    g = _dot(n, wg_ref[...])
    u = _dot(n, wu_ref[...])
    h = (g * _sigmoid(g) * u).astype(BF16)
    o_ref[...] += _dot(h, wo_ref[...])

    @pl.when(f == n_f - 1)
    def _():
        y = x_ref[...] + 0.5 * o_ref[...]
        o_ref[...] = y
        if post_norm:
            yn = y * lax.rsqrt(jnp.mean(y * y, axis=-1, keepdims=True) + NORM_EPS)
            p_ref[...] = (yn * pw_ref[...]).astype(p_ref.dtype)


def ffn(x2d, norm_w, w_in, w_out, post_norm=None):
    m, d = x2d.shape
    f_dim = w_out.shape[0]
    tm = _tile(m, 512)
    tf = _tile(f_dim, 512)
    n_f = f_dim // tf
    row = pl.BlockSpec((tm, d), lambda i, f: (i, 0))
    vec = pl.BlockSpec((1, d), lambda i, f: (0, 0))
    in_specs = [row, vec,
                pl.BlockSpec((d, tf), lambda i, f: (0, f)),
                pl.BlockSpec((d, tf), lambda i, f: (0, f + n_f)),
                pl.BlockSpec((tf, d), lambda i, f: (f, 0))]
    args = [x2d, norm_w.reshape(1, d).astype(F32), w_in, w_in, w_out]
    out_specs, out_shape = row, jax.ShapeDtypeStruct((m, d), F32)
    if post_norm is not None:
        in_specs.append(vec)
        args.append(post_norm[0].reshape(1, d).astype(F32))
        out_specs, out_shape = [row, row], [out_shape, jax.ShapeDtypeStruct((m, d), post_norm[1])]
    return pl.pallas_call(
        functools.partial(_ffn_kernel, n_f=n_f, post_norm=post_norm is not None),
        grid=(m // tm, n_f),
        in_specs=in_specs,
        out_specs=out_specs,
        out_shape=out_shape,
        scratch_shapes=[pltpu.VMEM((tm, d), BF16)],
        compiler_params=_cparams(("parallel", "arbitrary")),
        name="ffn",
    )(*args)


def _proj_plain_kernel(n_ref, w_ref, o_ref, *, sigmoid):
    y = _dot(n_ref[...], w_ref[...])
    if sigmoid:
        y = _sigmoid(y)
    o_ref[...] = y.astype(o_ref.dtype)


def proj_plain(n2d, w, out_dtype, sigmoid=False):
    m, k = n2d.shape
    n_out = w.shape[1]
    tm = _tile(m, 1024)
    tn = next(t for t in (512, 256, LANES) if n_out % t == 0)
    return pl.pallas_call(
        functools.partial(_proj_plain_kernel, sigmoid=sigmoid),
        grid=(m // tm, n_out // tn),
        in_specs=[pl.BlockSpec((tm, k), lambda i, j: (i, 0)),
                  pl.BlockSpec((k, tn), lambda i, j: (0, j))],
        out_specs=pl.BlockSpec((tm, tn), lambda i, j: (i, j)),
        out_shape=jax.ShapeDtypeStruct((m, n_out), out_dtype),
        compiler_params=_cparams(("parallel", "arbitrary")),
        name="proj_plain",
    )(n2d, w)


def _proj_heads_kernel(*refs, rope, has_add, has_masks, scale):
    n_ref, w_ref = refs[0], refs[1]
    o_ref = refs[-1]
    n = n_ref[...]
    n_copy = o_ref.shape[1] // w_ref.shape[0]
    for h in range(w_ref.shape[0]):
        pos = 2
        y = _dot(n, w_ref[h])
        if rope == "weights":
            wr_ref, cos_ref, sin_ref = refs[pos:pos + 3]
            pos += 3
            y = y * cos_ref[...] + _dot(n, wr_ref[h]) * sin_ref[...]
        elif rope == "swap":
            cos_ref, sin_ref = refs[pos:pos + 2]
            pos += 2
            half = y.shape[1] // 2
            y = y * cos_ref[...] + jnp.concatenate([y[:, half:], y[:, :half]], axis=1) * sin_ref[...]
        if has_add:
            y = y + refs[pos][0, 0].astype(F32)
            pos += 1
        if scale != 1.0:
            y = y * scale
        if has_masks:
            mask_ref = refs[pos]
            for c in range(n_copy):
                o_ref[0, h * n_copy + c] = (y * mask_ref[c:c + 1, :]).astype(o_ref.dtype)
        else:
            o_ref[0, h] = y.astype(o_ref.dtype)


def proj_heads(n2d, w, batch, seq, rope=None, add=None, scale=1.0, masks=None):
    m, k = n2d.shape
    g, _, dh = w.shape
    tm = _tile(seq, 1024)
    ns = seq // tm
    weights = pl.BlockSpec((g, k, dh), lambda b, i: (0, 0, 0))
    in_specs = [pl.BlockSpec((tm, k), lambda b, i: (b * ns + i, 0)), weights]
    args = [n2d, w]
    table = pl.BlockSpec((tm, dh), lambda b, i: (i, 0))
    if rope is not None and rope[0] == "weights":
        in_specs += [weights, table, table]
        args += list(rope[1:])
    elif rope is not None:
        assert rope[0] == "swap" and dh % (2 * LANES) == 0
        in_specs += [table, table]
        args += list(rope[1:])
    if add is not None:
        in_specs.append(pl.BlockSpec((1, 1, tm, dh), lambda b, i: (b, 0, i, 0)))
        args.append(add)
    n_copy = 1
    if masks is not None:
        n_copy = masks.shape[0]
        in_specs.append(pl.BlockSpec((n_copy, dh), lambda b, i: (0, 0)))
        args.append(masks)
    return pl.pallas_call(
        functools.partial(_proj_heads_kernel, rope=None if rope is None else rope[0], has_add=add is not None,
                          has_masks=masks is not None, scale=scale),
        grid=(batch, ns),
        in_specs=in_specs,
        out_specs=pl.BlockSpec((1, g * n_copy, tm, dh), lambda b, i: (b, 0, i, 0)),
        out_shape=jax.ShapeDtypeStruct((batch, g * n_copy, seq, dh), BF16),
        compiler_params=_cparams(("parallel", "parallel")),
        name="proj_heads",
    )(*args)


def _proj_heads_t_kernel(n_ref, wt_ref, o_ref):
    dh = wt_ref.shape[1]
    pad = o_ref.shape[2] - dh
    row = lax.broadcasted_iota(jnp.int32, (pad, o_ref.shape[3]), 0)
    ones_rows = jnp.where(row == 0, 1.0, 0.0).astype(o_ref.dtype)
    n = n_ref[...]
    for h in range(wt_ref.shape[0]):
        o_ref[0, h, :dh, :] = _dot_nt(wt_ref[h], n).astype(o_ref.dtype)
        o_ref[0, h, dh:, :] = ones_rows


def proj_heads_t(n2d, w, batch, seq):
    m, k = n2d.shape
    g, _, dh = w.shape
    tm = _tile(seq, 1024)
    ns = seq // tm
    return pl.pallas_call(
        _proj_heads_t_kernel,
        grid=(batch, ns),
        in_specs=[pl.BlockSpec((tm, k), lambda b, i: (b * ns + i, 0)),
                  pl.BlockSpec((g, dh, k), lambda b, i: (0, 0, 0))],
        out_specs=pl.BlockSpec((1, g, dh + ONES_PAD, tm), lambda b, i: (b, 0, 0, i)),
        out_shape=jax.ShapeDtypeStruct((batch, g, dh + ONES_PAD, seq), BF16),
        compiler_params=_cparams(("parallel", "parallel")),
        name="proj_heads_t",
    )(n2d, jnp.swapaxes(w, 1, 2))


def _softmax_step_t(st, vt, m_scr, acc_scr, exp_dtype):
    m_prev = m_scr[...]
    m_new = jnp.maximum(m_prev, jnp.max(st, axis=0, keepdims=True).astype(F32))
    alpha = jnp.exp(m_prev - m_new)
    pt = jnp.exp((st - m_new.astype(st.dtype)).astype(exp_dtype)).astype(BF16)
    acc_scr[...] = alpha * acc_scr[...] + _dot(vt, pt)
    m_scr[...] = m_new


def _diag_visibility(d, c, tq, tk):
    if d * tk >= (c + 1) * tq:
        return "none"
    if (d + 1) * tk - 1 <= c * tq:
        return "full"
    return "partial"


def _causal_where(st, d, c):
    tk, tq = st.shape
    key = lax.broadcasted_iota(jnp.int32, (tk, tq), 0) + d * tk
    qry = lax.broadcasted_iota(jnp.int32, (tk, tq), 1) + c * tq
    return jnp.where(key <= qry, st, NEG_INF)


def _flash_sweep(n, r, sub_of, tq, tk, scores, vt_tile, m_scr, acc_scr, st_scr, exp_dtype=F32):
    n_chain = m_scr.shape[0]
    for c in range(n_chain):
        m_scr[c] = jnp.full(m_scr.shape[1:], NEG_INF, F32)
        acc_scr[c] = jnp.zeros(acc_scr.shape[1:], F32)
    n_slot = st_scr.shape[0]
    assert r % n_slot == 0
    everyone = (True,) * n_chain
    for c, st in enumerate(scores(0, everyone)):
        st_scr[0, c] = st.astype(st_scr.dtype)

    def body(jj, carry):
        for slot in range(n_slot):
            j = jj * n_slot + slot
            nxt = scores(j + 1, everyone)
            vt = vt_tile(j)
            for c in range(n_chain):
                st = st_scr[slot, c]
                st_scr[(slot + 1) % n_slot, c] = nxt[c].astype(st_scr.dtype)
                _softmax_step_t(st, vt, m_scr.at[c], acc_scr.at[c], exp_dtype)
        return carry

    lax.fori_loop(0, n // n_slot, body, 0)
    for d in range(r):
        vis = [_diag_visibility(d, sub_of(c), tq, tk) for c in range(n_chain)]
        need = tuple(d + 1 < r and _diag_visibility(d + 1, sub_of(c), tq, tk) != "none" for c in range(n_chain))
        nxt = scores(n + d + 1, need) if any(need) else None
        vt = vt_tile(n + d)
        for c in range(n_chain):
            if vis[c] != "none":
                st = st_scr[d % n_slot, c]
                if vis[c] == "partial":
                    st = _causal_where(st.astype(F32), d, sub_of(c)).astype(st_scr.dtype)
            if need[c]:
                st_scr[(d + 1) % n_slot, c] = nxt[c].astype(st_scr.dtype)
            if vis[c] != "none":
                _softmax_step_t(st, vt, m_scr.at[c], acc_scr.at[c], exp_dtype)


def _flash_kernel(q_ref, k_ref, vt_ref, o_ref, m_scr, acc_scr, st_scr, *, n_sub, tq, tk):
    i = pl.program_id(2)
    dv = o_ref.shape[2]
    r = n_sub * tq // tk

    def scores(j, need):
        k = k_ref[0, 0, pl.ds(pl.multiple_of(j * tk, tk), tk), :]
        return [_dot_nt(k, q_ref[0, 0, c * tq:(c + 1) * tq, :]) if need[c] else None for c in range(n_sub)]

    def vt_tile(j):
        return vt_ref[0, 0, :, pl.ds(pl.multiple_of(j * tk, tk), tk)]

    _flash_sweep(i * r, r, lambda c: c, tq, tk, scores, vt_tile, m_scr, acc_scr, st_scr, exp_dtype=BF16)
    for c in range(n_sub):
        o_t = acc_scr[c, :dv, :] / acc_scr[c, dv:dv + 1, :]
        o_ref[0, c * tq:(c + 1) * tq, :] = o_t.T.astype(o_ref.dtype)


def _score_slots(r):
    return 2 if r % 2 == 0 else 1


def _flash_tiles(s, max_sub):
    tq = _tile(s, 512)
    n_sub = next(n for n in (4, 2, 1) if n <= max_sub and s % (n * tq) == 0)
    return n_sub, tq, tq


def flash_causal(q, k, vt):
    b, h, s, dk = q.shape
    dve = vt.shape[2]
    dv = dve - ONES_PAD
    n_sub, tq, tk = _flash_tiles(s, max_sub=4)
    t = n_sub * tq
    return pl.pallas_call(
        functools.partial(_flash_kernel, n_sub=n_sub, tq=tq, tk=tk),
        grid=(b, h, s // t),
        in_specs=[pl.BlockSpec((1, 1, t, dk), lambda b_, h_, i: (b_, h_, i, 0)),
                  pl.BlockSpec((1, 1, s, dk), lambda b_, h_, i: (b_, h_, 0, 0)),
                  pl.BlockSpec((1, 1, dve, s), lambda b_, h_, i: (b_, h_, 0, 0))],
        out_specs=pl.BlockSpec((1, t, dv), lambda b_, h_, i: (b_, i, h_)),
        out_shape=jax.ShapeDtypeStruct((b, s, h * dv), BF16),
        scratch_shapes=[pltpu.VMEM((n_sub, 1, tq), F32), pltpu.VMEM((n_sub, dve, tq), F32),
                        pltpu.VMEM((_score_slots(n_sub * tq // tk), n_sub, tk, tq), SCORE_DTYPE)],
        compiler_params=_cparams(("parallel", "parallel", "arbitrary")),
        name="mla_flash",
    )(q, k, vt)


def _diff_kernel(q_ref, k_ref, vt_ref, lq1_ref, lk1_ref, lq2_ref, lk2_ref, sub_ref, o_ref,
                 m_scr, acc_scr, st_scr, *, n_sub, tq, tk, lambda_init):
    i = pl.program_id(2)
    vd = o_ref.shape[3]
    r = n_sub * tq // tk

    def scores(j, need):
        k = k_ref[0, 0, pl.ds(pl.multiple_of(j * tk, tk), tk), :]
        return [_dot_nt(k, q_ref[0, ch // n_sub, (ch % n_sub) * tq:(ch % n_sub + 1) * tq, :]) if need[ch] else None
                for ch in range(2 * n_sub)]

    def vt_tile(j):
        return vt_ref[0, 0, :, pl.ds(pl.multiple_of(j * tk, tk), tk)]

    _flash_sweep(i * r, r, lambda ch: ch % n_sub, tq, tk, scores, vt_tile, m_scr, acc_scr, st_scr, exp_dtype=BF16)

    lam = (jnp.exp(jnp.sum(lq1_ref[...] * lk1_ref[...], axis=-1, keepdims=True))
           - jnp.exp(jnp.sum(lq2_ref[...] * lk2_ref[...], axis=-1, keepdims=True)) + lambda_init)
    lane_pad = (-vd) % LANES
    for c in range(n_sub):
        o_t = (acc_scr[c, :vd, :] / acc_scr[c, vd:vd + 1, :]
               - lam * (acc_scr[n_sub + c, :vd, :] / acc_scr[n_sub + c, vd:vd + 1, :]))
        o_t = o_t * lax.rsqrt(jnp.mean(o_t * o_t, axis=0, keepdims=True) + NORM_EPS)
        o_t = jnp.concatenate([o_t, jnp.zeros((lane_pad, tq), F32)], axis=0)
        o_ref[0, 0, c * tq:(c + 1) * tq, :] = (
            o_t.T[:, :vd] * sub_ref[...] * (1.0 - lambda_init)).astype(o_ref.dtype)


def diff_attention(q, k, vt, lq1, lk1, lq2, lk2, subln, lambda_init):
    b, h2, s, dk = q.shape
    h = h2 // 2
    hd = DIFF_HD
    vde = vt.shape[2]
    vd = vde - ONES_PAD
    n_sub, tq, tk = _flash_tiles(s, max_sub=4)
    t = n_sub * tq
    vec = lambda a: a.reshape(1, -1).astype(F32)
    small = lambda n: pl.BlockSpec((1, n), lambda b_, h_, i: (0, 0))
    return pl.pallas_call(
        functools.partial(_diff_kernel, n_sub=n_sub, tq=tq, tk=tk, lambda_init=lambda_init),
        grid=(b, h, s // t),
        in_specs=[pl.BlockSpec((1, 2, t, dk), lambda b_, h_, i: (b_, h_, i, 0)),
                  pl.BlockSpec((1, 1, s, dk), lambda b_, h_, i: (b_, h_, 0, 0)),
                  pl.BlockSpec((1, 1, vde, s), lambda b_, h_, i: (b_, h_, 0, 0)),
                  small(hd), small(hd), small(hd), small(hd), small(vd)],
        out_specs=pl.BlockSpec((1, 1, t, vd), lambda b_, h_, i: (b_, h_, i, 0)),
        out_shape=jax.ShapeDtypeStruct((b, h, s, vd), BF16),
        scratch_shapes=[pltpu.VMEM((2 * n_sub, 1, tq), F32), pltpu.VMEM((2 * n_sub, vde, tq), F32),
                        pltpu.VMEM((_score_slots(n_sub * tq // tk), 2 * n_sub, tk, tq), SCORE_DTYPE)],
        compiler_params=_cparams(("parallel", "parallel", "arbitrary")),
        name="diff_flash",
    )(q, k, vt, vec(lq1), vec(lk1), vec(lq2), vec(lk2), vec(subln))


def _compress_kernel(a_ref, pa_ref, pb_ref, w1a_ref, w1b_ref, w2_ref, o_ref, *, transposed):
    ng = a_ref.shape[2]
    rows = min(ng, 256)
    us, vs = [], []
    for r0 in range(0, ng, rows):
        a = a_ref[0, 0, r0:r0 + rows, :].astype(F32)
        us.append(_dot((a + pa_ref[...]).astype(BF16), w1a_ref[...]))
        vs.append(_dot((a + pb_ref[...]).astype(BF16), w1b_ref[...]))
    u = jnp.concatenate(us, axis=0)
    v = jnp.concatenate(vs, axis=0)
    hdn = u + pltpu.roll(v, ng - 1, 0)
    hdn = (hdn * _sigmoid(hdn)).astype(BF16)
    if transposed:
        d_out = w2_ref.shape[0]
        o_ref[0, :d_out, :] = _dot_nt(w2_ref[...], hdn).astype(o_ref.dtype)
        row = lax.broadcasted_iota(jnp.int32, (o_ref.shape[1] - d_out, ng), 0)
        o_ref[0, d_out:, :] = jnp.where(row == 0, 1.0, 0.0).astype(o_ref.dtype)
    else:
        o_ref[0] = _dot(hdn, w2_ref[...]).astype(o_ref.dtype)


def nsa_compress(tok, g_idx, pos, w1, w2, transposed=False):
    assert NSA_CMP_LEN == 2 * NSA_CMP_STRIDE
    b, _, ng, wd = tok.shape
    d = wd // NSA_CMP_STRIDE
    d_out = w2.shape[1]
    pos_flat = pos.astype(F32).reshape(1, NSA_CMP_LEN * d)
    w1 = w1.astype(BF16)
    w2 = w2.astype(BF16).T if transposed else w2.astype(BF16)
    out_block = (1, d_out + ONES_PAD, ng) if transposed else (1, ng, d_out)
    full = lambda shape: pl.BlockSpec(shape, lambda b_: (0,) * len(shape))
    return pl.pallas_call(
        functools.partial(_compress_kernel, transposed=transposed),
        grid=(b,),
        in_specs=[pl.BlockSpec((1, 1, ng, wd), lambda b_: (b_, g_idx, 0, 0)),
                  full((1, wd)), full((1, wd)), full((wd, NSA_CMP_HIDDEN)), full((wd, NSA_CMP_HIDDEN)),
                  full(w2.shape)],
        out_specs=pl.BlockSpec(out_block, lambda b_: (b_, 0, 0)),
        out_shape=jax.ShapeDtypeStruct((b,) + out_block[1:], BF16),
        compiler_params=_cparams(("parallel",)),
        name="nsa_compress",
    )(tok, pos_flat[:, :wd], pos_flat[:, wd:], w1[:wd], w1[wd:], w2)


def _split3(x):
    hi = x.astype(BF16)
    r = x - hi.astype(F32)
    mid = r.astype(BF16)
    lo = (r - mid.astype(F32)).astype(BF16)
    return hi, mid, lo


def _cmp_topk_kernel(q_ref, kc_ref, vct_ref, wselt_ref, oc_ref, selt_ref, *, tq, k_top):
    i = pl.program_id(1)
    qs = i * tq
    nc = kc_ref.shape[1]
    n_sel = wselt_ref.shape[0]
    kc = kc_ref[0]
    vct = vct_ref[0]
    qpos_c = qs + lax.broadcasted_iota(jnp.int32, (nc, tq), 1)
    cend = lax.broadcasted_iota(jnp.int32, (nc, tq), 0) * NSA_CMP_STRIDE + (NSA_CMP_LEN - 1)
    valid_c = cend <= qpos_c
    any_valid = qs + lax.broadcasted_iota(jnp.int32, (1, tq), 1) >= NSA_CMP_LEN - 1
    imp_t = jnp.zeros((nc, tq), F32)
    for h in range(NSA_HEADS):
        st = jnp.where(valid_c, _dot_nt(kc, q_ref[0, h]), NEG_INF)
        e = jnp.exp(st - jnp.max(st, axis=0, keepdims=True))
        inv = jnp.where(any_valid, 1.0 / jnp.sum(e, axis=0, keepdims=True), 0.0)
        o_t = _dot(vct, e.astype(BF16))[:NSA_DV, :] * inv
        oc_ref[0, :, h * NSA_DV:(h + 1) * NSA_DV] = o_t.T
        imp_t = imp_t + e * inv

    wselt = wselt_ref[...]
    hi, mid, lo = _split3(imp_t)
    imp_sel = _dot(wselt, hi) + _dot(wselt, mid) + _dot(wselt, lo)

    blk = lax.broadcasted_iota(jnp.int32, (n_sel, tq), 0)
    qpos = qs + lax.broadcasted_iota(jnp.int32, (n_sel, tq), 1)
    cur = lax.shift_right_arithmetic(qpos, int(math.log2(NSA_SEL_LEN)))
    forced = (blk == 0) | (blk == cur) | (blk == cur - 1)
    valid_s = blk * NSA_SEL_LEN <= qpos
    score = jnp.where(valid_s, jnp.where(forced, NSA_FORCE_SCORE, imp_sel), NEG_INF)
    blk_f = blk.astype(F32)
    sel = jnp.zeros((n_sel, tq), F32)
    for _ in range(k_top):
        mx = jnp.max(score, axis=0, keepdims=True)
        first = jnp.min(jnp.where(score == mx, blk_f, float(n_sel)), axis=0, keepdims=True)
        hit = blk_f == first
        sel = jnp.where(hit, 1.0, sel)
        score = jnp.where(hit, REMOVED, score)
    selt_ref[0] = sel.astype(selt_ref.dtype)


def _sel_weight_matrix(n_pad, n_sel):
    r_c = NSA_CMP_LEN // NSA_CMP_STRIDE
    ratio = NSA_SEL_LEN // NSA_CMP_STRIDE
    overlap_w = [max(0, min(o * NSA_CMP_STRIDE + NSA_CMP_LEN, NSA_SEL_LEN) - max(o * NSA_CMP_STRIDE, 0))
                 / NSA_CMP_STRIDE for o in range(-(r_c - 1), ratio)]
    w = np.zeros((n_pad, n_sel), np.float32)
    for n in range(n_sel):
        for u, w_u in enumerate(overlap_w):
            c = ratio * n + u - (r_c - 1)
            if 0 <= c < n_pad:
                w[c, n] = w_u
    return w


def nsa_cmp_topk(q, kc, vct):
    b, h, s, dk = q.shape
    nc = kc.shape[1]
    dve = vct.shape[1]
    n_sel = s // NSA_SEL_LEN
    assert NSA_SEL_LEN & (NSA_SEL_LEN - 1) == 0
    tq = _tile(s, 512)
    wselt = jnp.asarray(_sel_weight_matrix(nc, n_sel).T, BF16)
    return pl.pallas_call(
        functools.partial(_cmp_topk_kernel, tq=tq, k_top=min(NSA_TOPK, n_sel)),
        grid=(b, s // tq),
        in_specs=[pl.BlockSpec((1, h, tq, dk), lambda b_, i: (b_, 0, i, 0)),
                  pl.BlockSpec((1, nc, dk), lambda b_, i: (b_, 0, 0)),
                  pl.BlockSpec((1, dve, nc), lambda b_, i: (b_, 0, 0)),
                  pl.BlockSpec((n_sel, nc), lambda b_, i: (0, 0))],
        out_specs=[pl.BlockSpec((1, tq, h * NSA_DV), lambda b_, i: (b_, i, 0)),
                   pl.BlockSpec((1, n_sel, tq), lambda b_, i: (b_, 0, i))],
        out_shape=[jax.ShapeDtypeStruct((b, s, h * NSA_DV), F32), jax.ShapeDtypeStruct((b, n_sel, s), BF16)],
        compiler_params=_cparams(("parallel", "parallel")),
        name="nsa_cmp_topk",
    )(q, kc, vct, wselt)


def _nsa_sel_kernel(q_ref, k_ref, vt_ref, sel_ref, o_ref, m_scr, acc_scr, st_scr, *, n_sub, tq, tk):
    i = pl.program_id(1)
    n_sel = sel_ref.shape[1]
    bpt = tk // NSA_SEL_LEN
    shift = int(math.log2(NSA_SEL_LEN))
    r = n_sub * tq // tk

    def scores(j, need):
        k = k_ref[0, 0, pl.ds(pl.multiple_of(j * tk, tk), tk), :]
        blk_key = lax.shift_right_arithmetic(lax.broadcasted_iota(jnp.int32, (tk, n_sel), 0), shift) + j * bpt
        blk_col = lax.broadcasted_iota(jnp.int32, (tk, n_sel), 1)
        expand_t = jnp.where(blk_key == blk_col, 1.0, 0.0).astype(BF16)
        out = []
        for c in range(n_sub):
            rows = slice(c * tq, (c + 1) * tq)
            if not any(need[c * NSA_HEADS:(c + 1) * NSA_HEADS]):
                out += [None] * NSA_HEADS
                continue
            keep_t = _dot(expand_t, sel_ref[0, :, rows]) > 0.5
            out += [jnp.where(keep_t, _dot_nt(k, q_ref[0, h, rows, :]), NEG_INF) if need[c * NSA_HEADS + h] else None
                    for h in range(NSA_HEADS)]
        return out

    def vt_tile(j):
        return vt_ref[0, 0, :, pl.ds(pl.multiple_of(j * tk, tk), tk)]

    _flash_sweep(i * r, r, lambda ch: ch // NSA_HEADS, tq, tk, scores, vt_tile, m_scr, acc_scr, st_scr,
                 exp_dtype=BF16)
    for ch in range(n_sub * NSA_HEADS):
        c, h = divmod(ch, NSA_HEADS)
        o_t = acc_scr[ch, :NSA_DV, :] / acc_scr[ch, NSA_DV:NSA_DV + 1, :]
        o_ref[0, c * tq:(c + 1) * tq, h * NSA_DV:(h + 1) * NSA_DV] = o_t.T


def _nsa_sel_tiles(s):
    return _flash_tiles(s, max_sub=2)


def nsa_selected(q, k3, k_idx, vt, sel):
    b, h, s, dk = q.shape
    n_sel = sel.shape[1]
    dve = vt.shape[2]
    n_sub, tq, tk = _nsa_sel_tiles(s)
    t = n_sub * tq
    n_chain = n_sub * h
    assert tk % NSA_SEL_LEN == 0 and t % tk == 0
    return pl.pallas_call(
        functools.partial(_nsa_sel_kernel, n_sub=n_sub, tq=tq, tk=tk),
        grid=(b, s // t),
        in_specs=[pl.BlockSpec((1, h, t, dk), lambda b_, i: (b_, 0, i, 0)),
                  pl.BlockSpec((1, 1, s, dk), lambda b_, i: (b_, k_idx, 0, 0)),
                  pl.BlockSpec((1, 1, dve, s), lambda b_, i: (b_, 0, 0, 0)),
                  pl.BlockSpec((1, n_sel, t), lambda b_, i: (b_, 0, i))],
        out_specs=pl.BlockSpec((1, t, h * NSA_DV), lambda b_, i: (b_, i, 0)),
        out_shape=jax.ShapeDtypeStruct((b, s, h * NSA_DV), F32),
        scratch_shapes=[pltpu.VMEM((n_chain, 1, tq), F32), pltpu.VMEM((n_chain, dve, tq), F32),
                        pltpu.VMEM((_score_slots(t // tk), n_chain, tk, tq), SCORE_DTYPE)],
        compiler_params=_cparams(("parallel", "arbitrary")),
        name="nsa_selected",
    )(q, k3, vt, sel)


def _nsa_win_kernel(q_ref, kp_ref, kc_ref, vp_ref, vc_ref, o_ref, *, t):
    i = pl.program_id(1)
    row = lax.broadcasted_iota(jnp.int32, (t, t), 0)
    col = lax.broadcasted_iota(jnp.int32, (t, t), 1)
    keep_prev = (col > row) & (i > 0)
    keep_cur = col <= row
    kp, kc, vp, vc = kp_ref[0, 0], kc_ref[0, 0], vp_ref[0], vc_ref[0]
    for h in range(NSA_HEADS):
        q = q_ref[0, h]
        sp = jnp.where(keep_prev, _dot_nt(q, kp), NEG_INF)
        sc = jnp.where(keep_cur, _dot_nt(q, kc), NEG_INF)
        m = jnp.maximum(jnp.max(sp, axis=-1, keepdims=True), jnp.max(sc, axis=-1, keepdims=True))
        pp = jnp.exp(sp - m)
        pc = jnp.exp(sc - m)
        l = jnp.sum(pp, axis=-1, keepdims=True) + jnp.sum(pc, axis=-1, keepdims=True)
        o = _dot(pp.astype(BF16), vp) + _dot(pc.astype(BF16), vc)
        o_ref[0, :, h * NSA_DV:(h + 1) * NSA_DV] = o / l


def nsa_window(q, k3, k_idx, v3, v_idx):
    b, h, s, dk = q.shape
    t = _tile(s, NSA_WINDOW)
    assert t == NSA_WINDOW, "window kernel needs the query tile to equal the window"
    prev = lambda i: jnp.maximum(i - 1, 0)
    return pl.pallas_call(
        functools.partial(_nsa_win_kernel, t=t),
        grid=(b, s // t),
        in_specs=[pl.BlockSpec((1, h, t, dk), lambda b_, i: (b_, 0, i, 0)),
                  pl.BlockSpec((1, 1, t, dk), lambda b_, i: (b_, k_idx, prev(i), 0)),
                  pl.BlockSpec((1, 1, t, dk), lambda b_, i: (b_, k_idx, i, 0)),
                  pl.BlockSpec((1, t, NSA_DV), lambda b_, i: (b_, prev(i), v_idx)),
                  pl.BlockSpec((1, t, NSA_DV), lambda b_, i: (b_, i, v_idx))],
        out_specs=pl.BlockSpec((1, t, h * NSA_DV), lambda b_, i: (b_, i, 0)),
        out_shape=jax.ShapeDtypeStruct((b, s, h * NSA_DV), F32),
        compiler_params=_cparams(("parallel", "parallel")),
        name="nsa_window",
    )(q, k3, k3, v3, v3)


def _merge_kernel(x_ref, om_ref, oc_ref, os_ref, ow_ref, gn_ref, od_ref, gm_ref, gs_ref, gd_ref,
                  wm_ref, wn_ref, wd_ref, wo_ref, ex_ref, o_ref, onsa_scr, *, n_j):
    j = pl.program_id(1)
    nsa_out = NSA_HEADS * NSA_DV

    @pl.when(j == 0)
    def _():
        g = gn_ref[...]
        hi = g.astype(BF16)
        lo = (g - hi.astype(F32)).astype(BF16)
        ge = _dot(hi, ex_ref[...]) + _dot(lo, ex_ref[...])
        onsa = (ge[:, :nsa_out] * oc_ref[...] + ge[:, nsa_out:2 * nsa_out] * os_ref[...]
                + ge[:, 2 * nsa_out:] * ow_ref[...])
        onsa_scr[...] = onsa.astype(BF16)
        o_ref[...] = jnp.zeros_like(o_ref)

    ym = _dot(om_ref[...], wm_ref[...])
    yn = _dot(onsa_scr[...], wn_ref[...])
    yd = _dot(od_ref[0, 0], wd_ref[0])
    for h in range(1, DIFF_HEADS):
        yd = yd + _dot(od_ref[0, h], wd_ref[h])
    mixed = (gm_ref[...].astype(F32) * ym + gs_ref[...].astype(F32) * yn + gd_ref[...].astype(F32) * yd)
    o_ref[...] += _dot(mixed.astype(BF16), wo_ref[...])

    @pl.when(j == n_j - 1)
    def _():
        o_ref[...] = x_ref[...] + o_ref[...]


def _gate_expand_matrix():
    nsa_out = NSA_HEADS * NSA_DV
    e = np.zeros((LANES, 3 * nsa_out), np.float32)
    for h in range(NSA_HEADS):
        for c in range(3):
            e[h * 3 + c, c * nsa_out + h * NSA_DV:c * nsa_out + (h + 1) * NSA_DV] = 1.0
    return e


def merge(x2d, o_mla, o_c, o_s, o_w, g_nsa, o_diff, g_merge, w_br_mla, w_br_nsa, w_br_diff, w_out, seq):
    m, d = x2d.shape
    tm = _tile(seq, 512)
    ns = seq // tm
    tn = _tile(d, 512)
    n_j = d // tn
    nsa_out = NSA_HEADS * NSA_DV
    mla_out = o_mla.shape[-1]
    ex = jnp.asarray(_gate_expand_matrix(), BF16)
    row = lambda w: pl.BlockSpec((tm, w), lambda i, j: (i, 0))
    return pl.pallas_call(
        functools.partial(_merge_kernel, n_j=n_j),
        grid=(m // tm, n_j),
        in_specs=[row(d), row(mla_out), row(nsa_out), row(nsa_out), row(nsa_out), row(LANES),
                  pl.BlockSpec((1, DIFF_HEADS, tm, DIFF_VD), lambda i, j: (i // ns, 0, i % ns, 0)),
                  pl.BlockSpec((tm, tn), lambda i, j: (i, j)),
                  pl.BlockSpec((tm, tn), lambda i, j: (i, n_j + j)),
                  pl.BlockSpec((tm, tn), lambda i, j: (i, 2 * n_j + j)),
                  pl.BlockSpec((mla_out, tn), lambda i, j: (0, j)),
                  pl.BlockSpec((nsa_out, tn), lambda i, j: (0, j)),
                  pl.BlockSpec((DIFF_HEADS, DIFF_VD, tn), lambda i, j: (0, 0, j)),
                  pl.BlockSpec((tn, d), lambda i, j: (j, 0)),
                  pl.BlockSpec((LANES, 3 * nsa_out), lambda i, j: (0, 0))],
        out_specs=pl.BlockSpec((tm, d), lambda i, j: (i, 0)),
        out_shape=jax.ShapeDtypeStruct((m, d), F32),
        scratch_shapes=[pltpu.VMEM((tm, nsa_out), BF16)],
        compiler_params=_cparams(("parallel", "arbitrary")),
        name="merge",
    )(x2d, o_mla, o_c, o_s, o_w, g_nsa, o_diff, g_merge, g_merge, g_merge,
      w_br_mla, w_br_nsa, w_br_diff, w_out, ex)


def _rope_tables(dim, seq, lead=0):
    inv = ROPE_THETA ** (-jnp.arange(0, dim, 2, dtype=F32) / dim)
    ang = jnp.arange(seq, dtype=F32)[:, None] * inv[None, :]
    cos, sin = jnp.cos(ang), jnp.sin(ang)
    cos = jnp.concatenate([jnp.ones((seq, lead), F32), cos, cos], axis=-1)
    sin = jnp.concatenate([jnp.zeros((seq, lead), F32), sin, sin], axis=-1)
    return cos, sin


def _heads(w, g, dh):
    return w.reshape(w.shape[0], g, dh).transpose(1, 0, 2)


def _rot_cols(w, lead=0):
    half = (w.shape[-1] - lead) // 2
    x1 = w[..., lead:lead + half]
    x2 = w[..., lead + half:]
    return jnp.concatenate([jnp.zeros_like(w[..., :lead]), -x2, x1], axis=-1)


def _pad_halves(w, axis=-1):
    w = jnp.moveaxis(w, axis, -1)
    half = w.shape[-1] // 2
    z = jnp.zeros(w.shape[:-1] + ((-half) % LANES,), w.dtype)
    out = jnp.concatenate([w[..., :half], z, w[..., half:], z], axis=-1)
    return jnp.moveaxis(out, -1, axis)


def _swap_tables(dim, seq, copies=1):
    inv = ROPE_THETA ** (-jnp.arange(0, dim, 2, dtype=F32) / dim)
    ang = jnp.arange(seq, dtype=F32)[:, None] * inv[None, :]
    cos = jnp.tile(jnp.cos(ang), (1, copies))
    sin = jnp.tile(jnp.sin(ang), (1, copies))
    return _pad_halves(jnp.concatenate([cos, cos], axis=-1)), _pad_halves(jnp.concatenate([-sin, sin], axis=-1))


def _pair_heads(w):
    k = w.shape[0]
    half = DIFF_HD // 2
    w = w.reshape(k, DIFF_HEADS, 2, 2, half)
    w = w.transpose(1, 0, 3, 2, 4).reshape(DIFF_HEADS, k, 2 * DIFF_HD)
    return _pad_halves(w)


def _pair_masks():
    half = DIFF_HD // 2
    m = np.zeros((2, 2, 2, half), np.float32)
    for c in range(2):
        m[c, :, c, :] = 1.0
    return _pad_halves(jnp.asarray(m.reshape(2, 2 * DIFF_HD)))


def _col_offsets():
    sizes = (MLA_Q_LORA, MLA_KV_LORA, MLA_ROPE,
             NSA_HEADS * NSA_DK, NSA_DK, NSA_DV, NSA_DK, NSA_DV, NSA_DK, NSA_DV, NSA_HEADS * 3,
             DIFF_HEADS * 2 * DIFF_HD, DIFF_HEADS * 2 * DIFF_HD, DIFF_HEADS * DIFF_VD)
    names = ("c_q", "c_kv", "k_rope", "nsa_q", "nsa_kc", "nsa_vc", "nsa_ks", "nsa_vs", "nsa_kw", "nsa_vw",
             "nsa_g", "d_q", "d_k", "d_v")
    offs = {}
    o = 0
    for nme, sz in zip(names, sizes):
        offs[nme] = (o, o + sz)
        o += sz
    offs["merge"] = (o, None)
    return offs


def _mixers(n2d, batch, seq, layer, w_in, p):
    offs = _col_offsets()
    col = lambda name: w_in[:, offs[name][0]:offs[name][1]]
    bf = lambda a: a.astype(BF16)

    lat = proj_plain(n2d, bf(jnp.concatenate([col("c_q"), col("c_kv")], axis=1)), F32)
    n_q, n_kv = mla_norm(lat, p["mla_q_norm"], p["mla_kv_norm"])
    cos_m, sin_m = _rope_tables(MLA_ROPE, seq, lead=MLA_NOPE)
    w_kr = jnp.concatenate([jnp.zeros((w_in.shape[0], MLA_NOPE), F32), col("k_rope")], axis=1)[None]
    kpe = proj_heads(n2d, bf(w_kr), batch, seq, rope=("weights", bf(_rot_cols(w_kr, MLA_NOPE)), cos_m, sin_m))
    w_uq = _heads(p["mla_w_uq"], MLA_HEADS, MLA_QK)
    q_mla = proj_heads(n_q, bf(w_uq), batch, seq, rope=("weights", bf(_rot_cols(w_uq, MLA_NOPE)), cos_m, sin_m),
                       scale=MLA_QK ** -0.5)
    w_ukv = _heads(p["mla_w_ukv"], MLA_HEADS, MLA_NOPE + MLA_V)
    w_uk = jnp.concatenate([w_ukv[..., :MLA_NOPE], jnp.zeros(w_ukv.shape[:2] + (MLA_ROPE,), F32)], axis=-1)
    k_mla = proj_heads(n_kv, bf(w_uk), batch, seq, add=kpe)
    vt_mla = proj_heads_t(n_kv, bf(w_ukv[..., MLA_NOPE:]), batch, seq)
    o_mla = flash_causal(q_mla, k_mla, vt_mla)

    cos_d, sin_d = _swap_tables(DIFF_HD, seq, copies=2)
    q_d = proj_heads(n2d, bf(_pair_heads(col("d_q"))), batch, seq, rope=("swap", cos_d, sin_d),
                     scale=DIFF_HD ** -0.5, masks=_pair_masks())
    k_d = proj_heads(n2d, bf(_pair_heads(col("d_k"))), batch, seq, rope=("swap", cos_d, sin_d))
    vt_d = proj_heads_t(n2d, bf(_heads(col("d_v"), DIFF_HEADS, DIFF_VD)), batch, seq)
    lambda_init = 0.8 - 0.6 * math.exp(-0.3 * layer)
    o_diff = diff_attention(q_d, k_d, vt_d, p["diff_lam_q1"], p["diff_lam_k1"], p["diff_lam_q2"],
                            p["diff_lam_k2"], p["diff_subln"], lambda_init)

    cos_n, sin_n = _swap_tables(NSA_DK, seq)
    w_nq = _pad_halves(_heads(col("nsa_q"), NSA_HEADS, NSA_DK))
    q_n = proj_heads(n2d, bf(w_nq), batch, seq, rope=("swap", cos_n, sin_n), scale=NSA_DK ** -0.5)
    w_nk = _pad_halves(jnp.stack([col("nsa_kc"), col("nsa_ks"), col("nsa_kw")], axis=0))
    k_n = proj_heads(n2d, bf(w_nk), batch, seq, rope=("swap", cos_n, sin_n))
    dk_pad = w_nk.shape[-1]
    v_n = proj_plain(n2d, bf(jnp.concatenate([col("nsa_vc"), col("nsa_vw")], axis=1)), BF16)
    v_n = v_n.reshape(batch, seq, 2 * NSA_DV)
    vt_slc = proj_heads_t(n2d, bf(col("nsa_vs"))[None], batch, seq)
    w_g = jnp.concatenate([col("nsa_g"), jnp.zeros((w_in.shape[0], LANES - NSA_HEADS * 3), F32)], axis=1)
    g_nsa = proj_plain(n2d, bf(w_g), F32, sigmoid=True)

    ng = seq // NSA_CMP_STRIDE
    w1k = _pad_halves(p["nsa_cmp_k_w1"].reshape(NSA_CMP_LEN, NSA_DK, NSA_CMP_HIDDEN), axis=1)
    kc = nsa_compress(k_n.reshape(batch, 3, ng, NSA_CMP_STRIDE * dk_pad), 0,
                      _pad_halves(p["nsa_cmp_k_pos"]), w1k.reshape(NSA_CMP_LEN * dk_pad, NSA_CMP_HIDDEN),
                      _pad_halves(p["nsa_cmp_k_w2"]))
    vc_tok = v_n[:, :, :NSA_DV].reshape(batch, 1, ng, NSA_CMP_STRIDE * NSA_DV)
    vct = nsa_compress(vc_tok, 0, p["nsa_cmp_v_pos"], p["nsa_cmp_v_w1"], p["nsa_cmp_v_w2"], transposed=True)
    o_c, sel = nsa_cmp_topk(q_n, kc, vct)
    o_s = nsa_selected(q_n, k_n, 1, vt_slc, sel)
    o_w = nsa_window(q_n, k_n, 2, v_n, 1)

    g_merge = proj_plain(n2d, bf(w_in[:, offs["merge"][0]:]), BF16, sigmoid=True)
    m = batch * seq
    return (o_mla.reshape(m, -1), o_c.reshape(m, -1), o_s.reshape(m, -1), o_w.reshape(m, -1), g_nsa,
            o_diff, g_merge)


def kernel(x, ffn1_norm, ffn1_w_in, ffn1_w_out, mix_norm, w_in, mla_q_norm, mla_kv_norm, mla_w_uq, mla_w_ukv, nsa_cmp_k_pos, nsa_cmp_k_w1, nsa_cmp_k_w2, nsa_cmp_v_pos, nsa_cmp_v_w1, nsa_cmp_v_w2, diff_lam_q1, diff_lam_k1, diff_lam_q2, diff_lam_k2, diff_subln, w_br_mla, w_br_nsa, w_br_diff, w_out, ffn2_norm, ffn2_w_in, ffn2_w_out, final_norm):
    batch, seq, d = x.shape
    depth = w_in.shape[0]
    x2d = x.reshape(batch * seq, d)
    bf = lambda a: a.astype(BF16)
    for l in range(depth):
        p = {"mla_q_norm": mla_q_norm[l], "mla_kv_norm": mla_kv_norm[l], "mla_w_uq": mla_w_uq[l],
             "mla_w_ukv": mla_w_ukv[l], "nsa_cmp_k_pos": nsa_cmp_k_pos[l], "nsa_cmp_k_w1": nsa_cmp_k_w1[l],
             "nsa_cmp_k_w2": nsa_cmp_k_w2[l], "nsa_cmp_v_pos": nsa_cmp_v_pos[l], "nsa_cmp_v_w1": nsa_cmp_v_w1[l],
             "nsa_cmp_v_w2": nsa_cmp_v_w2[l], "diff_lam_q1": diff_lam_q1[l], "diff_lam_k1": diff_lam_k1[l],
             "diff_lam_q2": diff_lam_q2[l], "diff_lam_k2": diff_lam_k2[l], "diff_subln": diff_subln[l]}
        x2d, n2d = ffn(x2d, ffn1_norm[l], bf(ffn1_w_in[l]), bf(ffn1_w_out[l]), post_norm=(mix_norm[l], BF16))
        o_mla, o_c, o_s, o_w, g_nsa, o_diff, g_merge = _mixers(n2d, batch, seq, l, w_in[l], p)
        x2d = merge(x2d, o_mla, o_c, o_s, o_w, g_nsa, o_diff, g_merge, bf(w_br_mla[l]), bf(w_br_nsa[l]),
                    bf(w_br_diff[l]).reshape(DIFF_HEADS, DIFF_VD, d), bf(w_out[l]), seq)
        if l + 1 < depth:
            x2d = ffn(x2d, ffn2_norm[l], bf(ffn2_w_in[l]), bf(ffn2_w_out[l]))
        else:
            _, out = ffn(x2d, ffn2_norm[l], bf(ffn2_w_in[l]), bf(ffn2_w_out[l]), post_norm=(final_norm, F32))
    return out.reshape(batch, seq, d)
```

```python
import functools
import math

import numpy as np
import jax
import jax.numpy as jnp
from jax import lax
from jax.experimental import pallas as pl
from jax.experimental.pallas import tpu as pltpu

F32 = jnp.float32
BF16 = jnp.bfloat16

NORM_EPS = 1e-6
ROPE_THETA = 10000.0
NEG_INF = -1e30
REMOVED = -3e38
N_BRANCH = 3

MLA_HEADS = 6
MLA_Q_LORA = 768
MLA_KV_LORA = 512
MLA_NOPE = 128
MLA_ROPE = 64
MLA_V = 128
MLA_QK = MLA_NOPE + MLA_ROPE

NSA_HEADS = 4
NSA_DK = 192
NSA_DV = 128
NSA_CMP_LEN = 32
NSA_CMP_STRIDE = 16
NSA_CMP_HIDDEN = 256
NSA_SEL_LEN = 64
NSA_TOPK = 16
NSA_WINDOW = 512
NSA_FORCE_SCORE = 1e6

DIFF_HEADS = 4
DIFF_HD = 96
DIFF_VD = 2 * DIFF_HD

LANES = 128
ONES_PAD = 16
SCORE_DTYPE = BF16
VMEM_LIMIT_MB = 56


def _cparams(dims, vmem_mb=VMEM_LIMIT_MB):
    return pltpu.CompilerParams(dimension_semantics=dims, vmem_limit_bytes=vmem_mb * 2**20)


def _sigmoid(x):
    return 1.0 / (1.0 + jnp.exp(-x))


def _dot(a, b):
    return jnp.dot(a, b, preferred_element_type=F32)


def _dot_nt(a, b):
    return lax.dot_general(a, b, (((1,), (1,)), ((), ())), preferred_element_type=F32)


def _tile(n, pref):
    t = min(n, pref)
    assert n % t == 0, (n, t)
    return t


def _mla_norm_kernel(lat_ref, wq_ref, wkv_ref, nq_ref, nkv_ref):
    lat = lat_ref[...]
    cq = lat[:, :MLA_Q_LORA]
    ckv = lat[:, MLA_Q_LORA:]
    nq = cq * lax.rsqrt(jnp.mean(cq * cq, axis=-1, keepdims=True) + NORM_EPS)
    nkv = ckv * lax.rsqrt(jnp.mean(ckv * ckv, axis=-1, keepdims=True) + NORM_EPS)
    nq_ref[...] = (nq * wq_ref[...]).astype(BF16)
    nkv_ref[...] = (nkv * wkv_ref[...]).astype(BF16)


def mla_norm(lat, wq, wkv):
    m, d = lat.shape
    tm = _tile(m, 1024)
    return pl.pallas_call(
        _mla_norm_kernel,
        grid=(m // tm,),
        in_specs=[pl.BlockSpec((tm, d), lambda i: (i, 0)),
                  pl.BlockSpec((1, MLA_Q_LORA), lambda i: (0, 0)),
                  pl.BlockSpec((1, MLA_KV_LORA), lambda i: (0, 0))],
        out_specs=[pl.BlockSpec((tm, MLA_Q_LORA), lambda i: (i, 0)),
                   pl.BlockSpec((tm, MLA_KV_LORA), lambda i: (i, 0))],
        out_shape=[jax.ShapeDtypeStruct((m, MLA_Q_LORA), BF16), jax.ShapeDtypeStruct((m, MLA_KV_LORA), BF16)],
        compiler_params=_cparams(("parallel",)),
        name="mla_norm",
    )(lat, wq.reshape(1, -1).astype(F32), wkv.reshape(1, -1).astype(F32))


def _ffn_kernel(x_ref, nw_ref, wg_ref, wu_ref, wo_ref, *rest, n_f, post_norm):
    if post_norm:
        pw_ref, o_ref, p_ref, n_scr = rest
    else:
        o_ref, n_scr = rest
    f = pl.program_id(1)

    @pl.when(f == 0)
    def _():
        x = x_ref[...]
        y = x * lax.rsqrt(jnp.mean(x * x, axis=-1, keepdims=True) + NORM_EPS)
        n_scr[...] = (y * nw_ref[...]).astype(BF16)
        o_ref[...] = jnp.zeros_like(o_ref)

    n = n_scr[...]
    g = _dot(n, wg_ref[...])
    u = _dot(n, wu_ref[...])
    h = (g * _sigmoid(g) * u).astype(BF16)
    o_ref[...] += _dot(h, wo_ref[...])

    @pl.when(f == n_f - 1)
    def _():
        y = x_ref[...] + 0.5 * o_ref[...]
        o_ref[...] = y
        if post_norm:
            yn = y * lax.rsqrt(jnp.mean(y * y, axis=-1, keepdims=True) + NORM_EPS)
            p_ref[...] = (yn * pw_ref[...]).astype(p_ref.dtype)


def ffn(x2d, norm_w, w_in, w_out, post_norm=None):
    m, d = x2d.shape
    f_dim = w_out.shape[0]
    tm = _tile(m, 512)
    tf = _tile(f_dim, 512)
    n_f = f_dim // tf
    row = pl.BlockSpec((tm, d), lambda i, f: (i, 0))
    vec = pl.BlockSpec((1, d), lambda i, f: (0, 0))
    in_specs = [row, vec,
                pl.BlockSpec((d, tf), lambda i, f: (0, f)),
                pl.BlockSpec((d, tf), lambda i, f: (0, f + n_f)),
                pl.BlockSpec((tf, d), lambda i, f: (f, 0))]
    args = [x2d, norm_w.reshape(1, d).astype(F32), w_in, w_in, w_out]
    out_specs, out_shape = row, jax.ShapeDtypeStruct((m, d), F32)
    if post_norm is not None:
        in_specs.append(vec)
        args.append(post_norm[0].reshape(1, d).astype(F32))
        out_specs, out_shape = [row, row], [out_shape, jax.ShapeDtypeStruct((m, d), post_norm[1])]
    return pl.pallas_call(
        functools.partial(_ffn_kernel, n_f=n_f, post_norm=post_norm is not None),
        grid=(m // tm, n_f),
        in_specs=in_specs,
        out_specs=out_specs,
        out_shape=out_shape,
        scratch_shapes=[pltpu.VMEM((tm, d), BF16)],
        compiler_params=_cparams(("parallel", "arbitrary")),
        name="ffn",
    )(*args)


def _proj_plain_kernel(n_ref, w_ref, o_ref, *, sigmoid):
    y = _dot(n_ref[...], w_ref[...])
    if sigmoid:
        y = _sigmoid(y)
    o_ref[...] = y.astype(o_ref.dtype)


def proj_plain(n2d, w, out_dtype, sigmoid=False):
    m, k = n2d.shape
    n_out = w.shape[1]
    tm = _tile(m, 1024)
    tn = next(t for t in (512, 256, LANES) if n_out % t == 0)
    return pl.pallas_call(
        functools.partial(_proj_plain_kernel, sigmoid=sigmoid),
        grid=(m // tm, n_out // tn),
        in_specs=[pl.BlockSpec((tm, k), lambda i, j: (i, 0)),
                  pl.BlockSpec((k, tn), lambda i, j: (0, j))],
        out_specs=pl.BlockSpec((tm, tn), lambda i, j: (i, j)),
        out_shape=jax.ShapeDtypeStruct((m, n_out), out_dtype),
        compiler_params=_cparams(("parallel", "arbitrary")),
        name="proj_plain",
    )(n2d, w)


def _proj_heads_kernel(*refs, rope, has_add, has_masks, scale):
    n_ref, w_ref = refs[0], refs[1]
    o_ref = refs[-1]
    n = n_ref[...]
    n_copy = o_ref.shape[1] // w_ref.shape[0]
    for h in range(w_ref.shape[0]):
        pos = 2
        y = _dot(n, w_ref[h])
        if rope == "weights":
            wr_ref, cos_ref, sin_ref = refs[pos:pos + 3]
            pos += 3
            y = y * cos_ref[...] + _dot(n, wr_ref[h]) * sin_ref[...]
        elif rope == "swap":
            cos_ref, sin_ref = refs[pos:pos + 2]
            pos += 2
            half = y.shape[1] // 2
            y = y * cos_ref[...] + jnp.concatenate([y[:, half:], y[:, :half]], axis=1) * sin_ref[...]
        if has_add:
            y = y + refs[pos][0, 0].astype(F32)
            pos += 1
        if scale != 1.0:
            y = y * scale
        if has_masks:
            mask_ref = refs[pos]
            for c in range(n_copy):
                o_ref[0, h * n_copy + c] = (y * mask_ref[c:c + 1, :]).astype(o_ref.dtype)
        else:
            o_ref[0, h] = y.astype(o_ref.dtype)


def proj_heads(n2d, w, batch, seq, rope=None, add=None, scale=1.0, masks=None):
    m, k = n2d.shape
    g, _, dh = w.shape
    tm = _tile(seq, 1024)
    ns = seq // tm
    weights = pl.BlockSpec((g, k, dh), lambda b, i: (0, 0, 0))
    in_specs = [pl.BlockSpec((tm, k), lambda b, i: (b * ns + i, 0)), weights]
    args = [n2d, w]
    table = pl.BlockSpec((tm, dh), lambda b, i: (i, 0))
    if rope is not None and rope[0] == "weights":
        in_specs += [weights, table, table]
        args += list(rope[1:])
    elif rope is not None:
        assert rope[0] == "swap" and dh % (2 * LANES) == 0
        in_specs += [table, table]
        args += list(rope[1:])
    if add is not None:
        in_specs.append(pl.BlockSpec((1, 1, tm, dh), lambda b, i: (b, 0, i, 0)))
        args.append(add)
    n_copy = 1
    if masks is not None:
        n_copy = masks.shape[0]
        in_specs.append(pl.BlockSpec((n_copy, dh), lambda b, i: (0, 0)))
        args.append(masks)
    return pl.pallas_call(
        functools.partial(_proj_heads_kernel, rope=None if rope is None else rope[0], has_add=add is not None,
                          has_masks=masks is not None, scale=scale),
        grid=(batch, ns),
        in_specs=in_specs,
        out_specs=pl.BlockSpec((1, g * n_copy, tm, dh), lambda b, i: (b, 0, i, 0)),
        out_shape=jax.ShapeDtypeStruct((batch, g * n_copy, seq, dh), BF16),
        compiler_params=_cparams(("parallel", "parallel")),
        name="proj_heads",
    )(*args)


def _proj_heads_t_kernel(n_ref, wt_ref, o_ref):
    dh = wt_ref.shape[1]
    pad = o_ref.shape[2] - dh
    row = lax.broadcasted_iota(jnp.int32, (pad, o_ref.shape[3]), 0)
    ones_rows = jnp.where(row == 0, 1.0, 0.0).astype(o_ref.dtype)
    n = n_ref[...]
    for h in range(wt_ref.shape[0]):
        o_ref[0, h, :dh, :] = _dot_nt(wt_ref[h], n).astype(o_ref.dtype)
        o_ref[0, h, dh:, :] = ones_rows


def proj_heads_t(n2d, w, batch, seq):
    m, k = n2d.shape
    g, _, dh = w.shape
    tm = _tile(seq, 1024)
    ns = seq // tm
    return pl.pallas_call(
        _proj_heads_t_kernel,
        grid=(batch, ns),
        in_specs=[pl.BlockSpec((tm, k), lambda b, i: (b * ns + i, 0)),
                  pl.BlockSpec((g, dh, k), lambda b, i: (0, 0, 0))],
        out_specs=pl.BlockSpec((1, g, dh + ONES_PAD, tm), lambda b, i: (b, 0, 0, i)),
        out_shape=jax.ShapeDtypeStruct((batch, g, dh + ONES_PAD, seq), BF16),
        compiler_params=_cparams(("parallel", "parallel")),
        name="proj_heads_t",
    )(n2d, jnp.swapaxes(w, 1, 2))


def _softmax_step_t(st, vt, m_scr, acc_scr, exp_dtype):
    m_prev = m_scr[...]
    m_new = jnp.maximum(m_prev, jnp.max(st, axis=0, keepdims=True).astype(F32))
    alpha = jnp.exp(m_prev - m_new)
    pt = jnp.exp((st - m_new.astype(st.dtype)).astype(exp_dtype)).astype(BF16)
    acc_scr[...] = alpha * acc_scr[...] + _dot(vt, pt)
    m_scr[...] = m_new


def _diag_visibility(d, c, tq, tk):
    if d * tk >= (c + 1) * tq:
        return "none"
    if (d + 1) * tk - 1 <= c * tq:
        return "full"
    return "partial"


def _causal_where(st, d, c):
    tk, tq = st.shape
    key = lax.broadcasted_iota(jnp.int32, (tk, tq), 0) + d * tk
    qry = lax.broadcasted_iota(jnp.int32, (tk, tq), 1) + c * tq
    return jnp.where(key <= qry, st, NEG_INF)


def _flash_sweep(n, r, sub_of, tq, tk, scores, vt_tile, m_scr, acc_scr, st_scr, exp_dtype=F32):
    n_chain = m_scr.shape[0]
    for c in range(n_chain):
        m_scr[c] = jnp.full(m_scr.shape[1:], NEG_INF, F32)
        acc_scr[c] = jnp.zeros(acc_scr.shape[1:], F32)
    n_slot = st_scr.shape[0]
    assert r % n_slot == 0
    everyone = (True,) * n_chain
    for c, st in enumerate(scores(0, everyone)):
        st_scr[0, c] = st.astype(st_scr.dtype)

    def body(jj, carry):
        for slot in range(n_slot):
            j = jj * n_slot + slot
            nxt = scores(j + 1, everyone)
            vt = vt_tile(j)
            for c in range(n_chain):
                st = st_scr[slot, c]
                st_scr[(slot + 1) % n_slot, c] = nxt[c].astype(st_scr.dtype)
                _softmax_step_t(st, vt, m_scr.at[c], acc_scr.at[c], exp_dtype)
        return carry

    lax.fori_loop(0, n // n_slot, body, 0)
    for d in range(r):
        vis = [_diag_visibility(d, sub_of(c), tq, tk) for c in range(n_chain)]
        need = tuple(d + 1 < r and _diag_visibility(d + 1, sub_of(c), tq, tk) != "none" for c in range(n_chain))
        nxt = scores(n + d + 1, need) if any(need) else None
        vt = vt_tile(n + d)
        for c in range(n_chain):
            if vis[c] != "none":
                st = st_scr[d % n_slot, c]
                if vis[c] == "partial":
                    st = _causal_where(st.astype(F32), d, sub_of(c)).astype(st_scr.dtype)
            if need[c]:
                st_scr[(d + 1) % n_slot, c] = nxt[c].astype(st_scr.dtype)
            if vis[c] != "none":
                _softmax_step_t(st, vt, m_scr.at[c], acc_scr.at[c], exp_dtype)


def _flash_kernel(q_ref, k_ref, vt_ref, o_ref, m_scr, acc_scr, st_scr, *, n_sub, tq, tk):
    i = pl.program_id(2)
    dv = o_ref.shape[2]
    r = n_sub * tq // tk

    def scores(j, need):
        k = k_ref[0, 0, pl.ds(pl.multiple_of(j * tk, tk), tk), :]
        return [_dot_nt(k, q_ref[0, 0, c * tq:(c + 1) * tq, :]) if need[c] else None for c in range(n_sub)]

    def vt_tile(j):
        return vt_ref[0, 0, :, pl.ds(pl.multiple_of(j * tk, tk), tk)]

    _flash_sweep(i * r, r, lambda c: c, tq, tk, scores, vt_tile, m_scr, acc_scr, st_scr, exp_dtype=BF16)
    for c in range(n_sub):
        o_t = acc_scr[c, :dv, :] / acc_scr[c, dv:dv + 1, :]
        o_ref[0, c * tq:(c + 1) * tq, :] = o_t.T.astype(o_ref.dtype)


def _score_slots(r):
    return 2 if r % 2 == 0 else 1


def _flash_tiles(s, max_sub):
    tq = _tile(s, 512)
    n_sub = next(n for n in (4, 2, 1) if n <= max_sub and s % (n * tq) == 0)
    return n_sub, tq, tq


def flash_causal(q, k, vt):
    b, h, s, dk = q.shape
    dve = vt.shape[2]
    dv = dve - ONES_PAD
    n_sub, tq, tk = _flash_tiles(s, max_sub=4)
    t = n_sub * tq
    return pl.pallas_call(
        functools.partial(_flash_kernel, n_sub=n_sub, tq=tq, tk=tk),
        grid=(b, h, s // t),
        in_specs=[pl.BlockSpec((1, 1, t, dk), lambda b_, h_, i: (b_, h_, i, 0)),
                  pl.BlockSpec((1, 1, s, dk), lambda b_, h_, i: (b_, h_, 0, 0)),
                  pl.BlockSpec((1, 1, dve, s), lambda b_, h_, i: (b_, h_, 0, 0))],
        out_specs=pl.BlockSpec((1, t, dv), lambda b_, h_, i: (b_, i, h_)),
        out_shape=jax.ShapeDtypeStruct((b, s, h * dv), BF16),
        scratch_shapes=[pltpu.VMEM((n_sub, 1, tq), F32), pltpu.VMEM((n_sub, dve, tq), F32),
                        pltpu.VMEM((_score_slots(n_sub * tq // tk), n_sub, tk, tq), SCORE_DTYPE)],
        compiler_params=_cparams(("parallel", "parallel", "arbitrary")),
        name="mla_flash",
    )(q, k, vt)


def _diff_kernel(q_ref, k_ref, vt_ref, lq1_ref, lk1_ref, lq2_ref, lk2_ref, sub_ref, o_ref,
                 m_scr, acc_scr, st_scr, *, n_sub, tq, tk, lambda_init):
    i = pl.program_id(2)
    vd = o_ref.shape[3]
    r = n_sub * tq // tk

    def scores(j, need):
        k = k_ref[0, 0, pl.ds(pl.multiple_of(j * tk, tk), tk), :]
        return [_dot_nt(k, q_ref[0, ch // n_sub, (ch % n_sub) * tq:(ch % n_sub + 1) * tq, :]) if need[ch] else None
                for ch in range(2 * n_sub)]

    def vt_tile(j):
        return vt_ref[0, 0, :, pl.ds(pl.multiple_of(j * tk, tk), tk)]

    _flash_sweep(i * r, r, lambda ch: ch % n_sub, tq, tk, scores, vt_tile, m_scr, acc_scr, st_scr, exp_dtype=BF16)

    lam = (jnp.exp(jnp.sum(lq1_ref[...] * lk1_ref[...], axis=-1, keepdims=True))
           - jnp.exp(jnp.sum(lq2_ref[...] * lk2_ref[...], axis=-1, keepdims=True)) + lambda_init)
    lane_pad = (-vd) % LANES
    for c in range(n_sub):
        o_t = (acc_scr[c, :vd, :] / acc_scr[c, vd:vd + 1, :]
               - lam * (acc_scr[n_sub + c, :vd, :] / acc_scr[n_sub + c, vd:vd + 1, :]))
        o_t = o_t * lax.rsqrt(jnp.mean(o_t * o_t, axis=0, keepdims=True) + NORM_EPS)
        o_t = jnp.concatenate([o_t, jnp.zeros((lane_pad, tq), F32)], axis=0)
        o_ref[0, 0, c * tq:(c + 1) * tq, :] = (
            o_t.T[:, :vd] * sub_ref[...] * (1.0 - lambda_init)).astype(o_ref.dtype)


def diff_attention(q, k, vt, lq1, lk1, lq2, lk2, subln, lambda_init):
    b, h2, s, dk = q.shape
    h = h2 // 2
    hd = DIFF_HD
    vde = vt.shape[2]
    vd = vde - ONES_PAD
    n_sub, tq, tk = _flash_tiles(s, max_sub=4)
    t = n_sub * tq
    vec = lambda a: a.reshape(1, -1).astype(F32)
    small = lambda n: pl.BlockSpec((1, n), lambda b_, h_, i: (0, 0))
    return pl.pallas_call(
        functools.partial(_diff_kernel, n_sub=n_sub, tq=tq, tk=tk, lambda_init=lambda_init),
        grid=(b, h, s // t),
        in_specs=[pl.BlockSpec((1, 2, t, dk), lambda b_, h_, i: (b_, h_, i, 0)),
                  pl.BlockSpec((1, 1, s, dk), lambda b_, h_, i: (b_, h_, 0, 0)),
                  pl.BlockSpec((1, 1, vde, s), lambda b_, h_, i: (b_, h_, 0, 0)),
                  small(hd), small(hd), small(hd), small(hd), small(vd)],
        out_specs=pl.BlockSpec((1, 1, t, vd), lambda b_, h_, i: (b_, h_, i, 0)),
        out_shape=jax.ShapeDtypeStruct((b, h, s, vd), BF16),
        scratch_shapes=[pltpu.VMEM((2 * n_sub, 1, tq), F32), pltpu.VMEM((2 * n_sub, vde, tq), F32),
                        pltpu.VMEM((_score_slots(n_sub * tq // tk), 2 * n_sub, tk, tq), SCORE_DTYPE)],
        compiler_params=_cparams(("parallel", "parallel", "arbitrary")),
        name="diff_flash",
    )(q, k, vt, vec(lq1), vec(lk1), vec(lq2), vec(lk2), vec(subln))


def _compress_kernel(a_ref, pa_ref, pb_ref, w1a_ref, w1b_ref, w2_ref, o_ref, *, transposed):
    ng = a_ref.shape[2]
    rows = min(ng, 256)
    us, vs = [], []
    for r0 in range(0, ng, rows):
        a = a_ref[0, 0, r0:r0 + rows, :].astype(F32)
        us.append(_dot((a + pa_ref[...]).astype(BF16), w1a_ref[...]))
        vs.append(_dot((a + pb_ref[...]).astype(BF16), w1b_ref[...]))
    u = jnp.concatenate(us, axis=0)
    v = jnp.concatenate(vs, axis=0)
    hdn = u + pltpu.roll(v, ng - 1, 0)
    hdn = (hdn * _sigmoid(hdn)).astype(BF16)
    if transposed:
        d_out = w2_ref.shape[0]
        o_ref[0, :d_out, :] = _dot_nt(w2_ref[...], hdn).astype(o_ref.dtype)
        row = lax.broadcasted_iota(jnp.int32, (o_ref.shape[1] - d_out, ng), 0)
        o_ref[0, d_out:, :] = jnp.where(row == 0, 1.0, 0.0).astype(o_ref.dtype)
    else:
        o_ref[0] = _dot(hdn, w2_ref[...]).astype(o_ref.dtype)


def nsa_compress(tok, g_idx, pos, w1, w2, transposed=False):
    assert NSA_CMP_LEN == 2 * NSA_CMP_STRIDE
    b, _, ng, wd = tok.shape
    d = wd // NSA_CMP_STRIDE
    d_out = w2.shape[1]
    pos_flat = pos.astype(F32).reshape(1, NSA_CMP_LEN * d)
    w1 = w1.astype(BF16)
    w2 = w2.astype(BF16).T if transposed else w2.astype(BF16)
    out_block = (1, d_out + ONES_PAD, ng) if transposed else (1, ng, d_out)
    full = lambda shape: pl.BlockSpec(shape, lambda b_: (0,) * len(shape))
    return pl.pallas_call(
        functools.partial(_compress_kernel, transposed=transposed),
        grid=(b,),
        in_specs=[pl.BlockSpec((1, 1, ng, wd), lambda b_: (b_, g_idx, 0, 0)),
                  full((1, wd)), full((1, wd)), full((wd, NSA_CMP_HIDDEN)), full((wd, NSA_CMP_HIDDEN)),
                  full(w2.shape)],
        out_specs=pl.BlockSpec(out_block, lambda b_: (b_, 0, 0)),
        out_shape=jax.ShapeDtypeStruct((b,) + out_block[1:], BF16),
        compiler_params=_cparams(("parallel",)),
        name="nsa_compress",
    )(tok, pos_flat[:, :wd], pos_flat[:, wd:], w1[:wd], w1[wd:], w2)


def _split3(x):
    hi = x.astype(BF16)
    r = x - hi.astype(F32)
    mid = r.astype(BF16)
    lo = (r - mid.astype(F32)).astype(BF16)
    return hi, mid, lo


def _cmp_topk_kernel(q_ref, kc_ref, vct_ref, wselt_ref, oc_ref, selt_ref, *, tq, k_top):
    i = pl.program_id(1)
    qs = i * tq
    nc = kc_ref.shape[1]
    n_sel = wselt_ref.shape[0]
    kc = kc_ref[0]
    vct = vct_ref[0]
    qpos_c = qs + lax.broadcasted_iota(jnp.int32, (nc, tq), 1)
    cend = lax.broadcasted_iota(jnp.int32, (nc, tq), 0) * NSA_CMP_STRIDE + (NSA_CMP_LEN - 1)
    valid_c = cend <= qpos_c
    any_valid = qs + lax.broadcasted_iota(jnp.int32, (1, tq), 1) >= NSA_CMP_LEN - 1
    imp_t = jnp.zeros((nc, tq), F32)
    for h in range(NSA_HEADS):
        st = jnp.where(valid_c, _dot_nt(kc, q_ref[0, h]), NEG_INF)
        e = jnp.exp(st - jnp.max(st, axis=0, keepdims=True))
        inv = jnp.where(any_valid, 1.0 / jnp.sum(e, axis=0, keepdims=True), 0.0)
        o_t = _dot(vct, e.astype(BF16))[:NSA_DV, :] * inv
        oc_ref[0, :, h * NSA_DV:(h + 1) * NSA_DV] = o_t.T
        imp_t = imp_t + e * inv

    wselt = wselt_ref[...]
    hi, mid, lo = _split3(imp_t)
    imp_sel = _dot(wselt, hi) + _dot(wselt, mid) + _dot(wselt, lo)

    blk = lax.broadcasted_iota(jnp.int32, (n_sel, tq), 0)
    qpos = qs + lax.broadcasted_iota(jnp.int32, (n_sel, tq), 1)
    cur = lax.shift_right_arithmetic(qpos, int(math.log2(NSA_SEL_LEN)))
    forced = (blk == 0) | (blk == cur) | (blk == cur - 1)
    valid_s = blk * NSA_SEL_LEN <= qpos
    score = jnp.where(valid_s, jnp.where(forced, NSA_FORCE_SCORE, imp_sel), NEG_INF)
    blk_f = blk.astype(F32)
    sel = jnp.zeros((n_sel, tq), F32)
    for _ in range(k_top):
        mx = jnp.max(score, axis=0, keepdims=True)
        first = jnp.min(jnp.where(score == mx, blk_f, float(n_sel)), axis=0, keepdims=True)
        hit = blk_f == first
        sel = jnp.where(hit, 1.0, sel)
        score = jnp.where(hit, REMOVED, score)
    selt_ref[0] = sel.astype(selt_ref.dtype)


def _sel_weight_matrix(n_pad, n_sel):
    r_c = NSA_CMP_LEN // NSA_CMP_STRIDE
    ratio = NSA_SEL_LEN // NSA_CMP_STRIDE
    overlap_w = [max(0, min(o * NSA_CMP_STRIDE + NSA_CMP_LEN, NSA_SEL_LEN) - max(o * NSA_CMP_STRIDE, 0))
                 / NSA_CMP_STRIDE for o in range(-(r_c - 1), ratio)]
    w = np.zeros((n_pad, n_sel), np.float32)
    for n in range(n_sel):
        for u, w_u in enumerate(overlap_w):
            c = ratio * n + u - (r_c - 1)
            if 0 <= c < n_pad:
                w[c, n] = w_u
    return w


def nsa_cmp_topk(q, kc, vct):
    b, h, s, dk = q.shape
    nc = kc.shape[1]
    dve = vct.shape[1]
    n_sel = s // NSA_SEL_LEN
    assert NSA_SEL_LEN & (NSA_SEL_LEN - 1) == 0
    tq = _tile(s, 512)
    wselt = jnp.asarray(_sel_weight_matrix(nc, n_sel).T, BF16)
    return pl.pallas_call(
        functools.partial(_cmp_topk_kernel, tq=tq, k_top=min(NSA_TOPK, n_sel)),
        grid=(b, s // tq),
        in_specs=[pl.BlockSpec((1, h, tq, dk), lambda b_, i: (b_, 0, i, 0)),
                  pl.BlockSpec((1, nc, dk), lambda b_, i: (b_, 0, 0)),
                  pl.BlockSpec((1, dve, nc), lambda b_, i: (b_, 0, 0)),
                  pl.BlockSpec((n_sel, nc), lambda b_, i: (0, 0))],
        out_specs=[pl.BlockSpec((1, tq, h * NSA_DV), lambda b_, i: (b_, i, 0)),
                   pl.BlockSpec((1, n_sel, tq), lambda b_, i: (b_, 0, i))],
        out_shape=[jax.ShapeDtypeStruct((b, s, h * NSA_DV), F32), jax.ShapeDtypeStruct((b, n_sel, s), F32)],
        compiler_params=_cparams(("parallel", "parallel")),
        name="nsa_cmp_topk",
    )(q, kc, vct, wselt)


def _nsa_sel_kernel(q_ref, k_ref, vt_ref, sel_ref, o_ref, m_scr, acc_scr, st_scr, *, n_sub, tq, tk):
    i = pl.program_id(1)
    bpt = tk // NSA_SEL_LEN
    r = n_sub * tq // tk

    def scores(j, need):
        k = k_ref[0, 0, pl.ds(pl.multiple_of(j * tk, tk), tk), :]
        out = []
        for c in range(n_sub):
            rows = slice(c * tq, (c + 1) * tq)
            if not any(need[c * NSA_HEADS:(c + 1) * NSA_HEADS]):
                out += [None] * NSA_HEADS
                continue
            flags = sel_ref[0, pl.ds(pl.multiple_of(j * bpt, bpt), bpt), rows]
            keep_t = jnp.broadcast_to(flags[:, None, :], (bpt, NSA_SEL_LEN, tq)).reshape(tk, tq) > 0.5
            out += [jnp.where(keep_t, _dot_nt(k, q_ref[0, h, rows, :]), NEG_INF) if need[c * NSA_HEADS + h] else None
                    for h in range(NSA_HEADS)]
        return out

    def vt_tile(j):
        return vt_ref[0, 0, :, pl.ds(pl.multiple_of(j * tk, tk), tk)]

    _flash_sweep(i * r, r, lambda ch: ch // NSA_HEADS, tq, tk, scores, vt_tile, m_scr, acc_scr, st_scr,
                 exp_dtype=BF16)
    for ch in range(n_sub * NSA_HEADS):
        c, h = divmod(ch, NSA_HEADS)
        o_t = acc_scr[ch, :NSA_DV, :] / acc_scr[ch, NSA_DV:NSA_DV + 1, :]
        o_ref[0, c * tq:(c + 1) * tq, h * NSA_DV:(h + 1) * NSA_DV] = o_t.T


def _nsa_sel_tiles(s):
    return _flash_tiles(s, max_sub=2)


def nsa_selected(q, k3, k_idx, vt, sel):
    b, h, s, dk = q.shape
    n_sel = sel.shape[1]
    dve = vt.shape[2]
    n_sub, tq, tk = _nsa_sel_tiles(s)
    t = n_sub * tq
    n_chain = n_sub * h
    assert tk % NSA_SEL_LEN == 0 and t % tk == 0
    return pl.pallas_call(
        functools.partial(_nsa_sel_kernel, n_sub=n_sub, tq=tq, tk=tk),
        grid=(b, s // t),
        in_specs=[pl.BlockSpec((1, h, t, dk), lambda b_, i: (b_, 0, i, 0)),
                  pl.BlockSpec((1, 1, s, dk), lambda b_, i: (b_, k_idx, 0, 0)),
                  pl.BlockSpec((1, 1, dve, s), lambda b_, i: (b_, 0, 0, 0)),
                  pl.BlockSpec((1, n_sel, t), lambda b_, i: (b_, 0, i))],
        out_specs=pl.BlockSpec((1, t, h * NSA_DV), lambda b_, i: (b_, i, 0)),
        out_shape=jax.ShapeDtypeStruct((b, s, h * NSA_DV), F32),
        scratch_shapes=[pltpu.VMEM((n_chain, 1, tq), F32), pltpu.VMEM((n_chain, dve, tq), F32),
                        pltpu.VMEM((_score_slots(t // tk), n_chain, tk, tq), SCORE_DTYPE)],
        compiler_params=_cparams(("parallel", "arbitrary")),
        name="nsa_selected",
    )(q, k3, vt, sel)


def _nsa_win_kernel(q_ref, kp_ref, kc_ref, vtp_ref, vtc_ref, o_ref, *, t):
    i = pl.program_id(1)
    key = lax.broadcasted_iota(jnp.int32, (t, t), 0)
    qry = lax.broadcasted_iota(jnp.int32, (t, t), 1)
    keep_prev = (key > qry) & (i > 0)
    keep_cur = key <= qry
    kp, kc, vtp, vtc = kp_ref[0, 0], kc_ref[0, 0], vtp_ref[0, 0], vtc_ref[0, 0]
    for h in range(NSA_HEADS):
        q = q_ref[0, h]
        sp = jnp.where(keep_prev, _dot_nt(kp, q), NEG_INF)
        sc = jnp.where(keep_cur, _dot_nt(kc, q), NEG_INF)
        m = jnp.maximum(jnp.max(sp, axis=0, keepdims=True), jnp.max(sc, axis=0, keepdims=True))
        pp = jnp.exp((sp - m).astype(SCORE_DTYPE)).astype(BF16)
        pc = jnp.exp((sc - m).astype(SCORE_DTYPE)).astype(BF16)
        acc = _dot(vtp, pp) + _dot(vtc, pc)
        o_t = acc[:NSA_DV, :] / acc[NSA_DV:NSA_DV + 1, :]
        o_ref[0, :, h * NSA_DV:(h + 1) * NSA_DV] = o_t.T


def nsa_window(q, k3, k_idx, vt, v_idx):
    b, h, s, dk = q.shape
    dve = vt.shape[2]
    t = _tile(s, NSA_WINDOW)
    assert t == NSA_WINDOW, "window kernel needs the query tile to equal the window"
    prev = lambda i: jnp.maximum(i - 1, 0)
    return pl.pallas_call(
        functools.partial(_nsa_win_kernel, t=t),
        grid=(b, s // t),
        in_specs=[pl.BlockSpec((1, h, t, dk), lambda b_, i: (b_, 0, i, 0)),
                  pl.BlockSpec((1, 1, t, dk), lambda b_, i: (b_, k_idx, prev(i), 0)),
                  pl.BlockSpec((1, 1, t, dk), lambda b_, i: (b_, k_idx, i, 0)),
                  pl.BlockSpec((1, 1, dve, t), lambda b_, i: (b_, v_idx, 0, prev(i))),
                  pl.BlockSpec((1, 1, dve, t), lambda b_, i: (b_, v_idx, 0, i))],
        out_specs=pl.BlockSpec((1, t, h * NSA_DV), lambda b_, i: (b_, i, 0)),
        out_shape=jax.ShapeDtypeStruct((b, s, h * NSA_DV), F32),
        compiler_params=_cparams(("parallel", "parallel")),
        name="nsa_window",
    )(q, k3, k3, vt, vt)


def _merge_kernel(x_ref, om_ref, oc_ref, os_ref, ow_ref, gn_ref, od_ref, gm_ref, gs_ref, gd_ref,
                  wm_ref, wn_ref, wd_ref, wo_ref, ex_ref, o_ref, onsa_scr, *, n_j):
    j = pl.program_id(1)
    nsa_out = NSA_HEADS * NSA_DV

    @pl.when(j == 0)
    def _():
        g = gn_ref[...]
        hi = g.astype(BF16)
        lo = (g - hi.astype(F32)).astype(BF16)
        ge = _dot(hi, ex_ref[...]) + _dot(lo, ex_ref[...])
        onsa = (ge[:, :nsa_out] * oc_ref[...] + ge[:, nsa_out:2 * nsa_out] * os_ref[...]
                + ge[:, 2 * nsa_out:] * ow_ref[...])
        onsa_scr[...] = onsa.astype(BF16)
        o_ref[...] = jnp.zeros_like(o_ref)

    ym = _dot(om_ref[...], wm_ref[...])
    yn = _dot(onsa_scr[...], wn_ref[...])
    yd = _dot(od_ref[0, 0], wd_ref[0])
    for h in range(1, DIFF_HEADS):
        yd = yd + _dot(od_ref[0, h], wd_ref[h])
    mixed = (gm_ref[...].astype(F32) * ym + gs_ref[...].astype(F32) * yn + gd_ref[...].astype(F32) * yd)
    o_ref[...] += _dot(mixed.astype(BF16), wo_ref[...])

    @pl.when(j == n_j - 1)
    def _():
        o_ref[...] = x_ref[...] + o_ref[...]


def _gate_expand_matrix():
    nsa_out = NSA_HEADS * NSA_DV
    e = np.zeros((LANES, 3 * nsa_out), np.float32)
    for h in range(NSA_HEADS):
        for c in range(3):
            e[h * 3 + c, c * nsa_out + h * NSA_DV:c * nsa_out + (h + 1) * NSA_DV] = 1.0
    return e


def merge(x2d, o_mla, o_c, o_s, o_w, g_nsa, o_diff, g_merge, w_br_mla, w_br_nsa, w_br_diff, w_out, seq):
    m, d = x2d.shape
    tm = _tile(seq, 512)
    ns = seq // tm
    tn = _tile(d, 512)
    n_j = d // tn
    nsa_out = NSA_HEADS * NSA_DV
    mla_out = o_mla.shape[-1]
    ex = jnp.asarray(_gate_expand_matrix(), BF16)
    row = lambda w: pl.BlockSpec((tm, w), lambda i, j: (i, 0))
    return pl.pallas_call(
        functools.partial(_merge_kernel, n_j=n_j),
        grid=(m // tm, n_j),
        in_specs=[row(d), row(mla_out), row(nsa_out), row(nsa_out), row(nsa_out), row(LANES),
                  pl.BlockSpec((1, DIFF_HEADS, tm, DIFF_VD), lambda i, j: (i // ns, 0, i % ns, 0)),
                  pl.BlockSpec((tm, tn), lambda i, j: (i, j)),
                  pl.BlockSpec((tm, tn), lambda i, j: (i, n_j + j)),
                  pl.BlockSpec((tm, tn), lambda i, j: (i, 2 * n_j + j)),
                  pl.BlockSpec((mla_out, tn), lambda i, j: (0, j)),
                  pl.BlockSpec((nsa_out, tn), lambda i, j: (0, j)),
                  pl.BlockSpec((DIFF_HEADS, DIFF_VD, tn), lambda i, j: (0, 0, j)),
                  pl.BlockSpec((tn, d), lambda i, j: (j, 0)),
                  pl.BlockSpec((LANES, 3 * nsa_out), lambda i, j: (0, 0))],
        out_specs=pl.BlockSpec((tm, d), lambda i, j: (i, 0)),
        out_shape=jax.ShapeDtypeStruct((m, d), F32),
        scratch_shapes=[pltpu.VMEM((tm, nsa_out), BF16)],
        compiler_params=_cparams(("parallel", "arbitrary")),
        name="merge",
    )(x2d, o_mla, o_c, o_s, o_w, g_nsa, o_diff, g_merge, g_merge, g_merge,
      w_br_mla, w_br_nsa, w_br_diff, w_out, ex)


def _rope_tables(dim, seq, lead=0):
    inv = ROPE_THETA ** (-jnp.arange(0, dim, 2, dtype=F32) / dim)
    ang = jnp.arange(seq, dtype=F32)[:, None] * inv[None, :]
    cos, sin = jnp.cos(ang), jnp.sin(ang)
    cos = jnp.concatenate([jnp.ones((seq, lead), F32), cos, cos], axis=-1)
    sin = jnp.concatenate([jnp.zeros((seq, lead), F32), sin, sin], axis=-1)
    return cos, sin


def _heads(w, g, dh):
    return w.reshape(w.shape[0], g, dh).transpose(1, 0, 2)


def _rot_cols(w, lead=0):
    half = (w.shape[-1] - lead) // 2
    x1 = w[..., lead:lead + half]
    x2 = w[..., lead + half:]
    return jnp.concatenate([jnp.zeros_like(w[..., :lead]), -x2, x1], axis=-1)


def _pad_halves(w, axis=-1):
    w = jnp.moveaxis(w, axis, -1)
    half = w.shape[-1] // 2
    z = jnp.zeros(w.shape[:-1] + ((-half) % LANES,), w.dtype)
    out = jnp.concatenate([w[..., :half], z, w[..., half:], z], axis=-1)
    return jnp.moveaxis(out, -1, axis)


def _swap_tables(dim, seq, copies=1):
    inv = ROPE_THETA ** (-jnp.arange(0, dim, 2, dtype=F32) / dim)
    ang = jnp.arange(seq, dtype=F32)[:, None] * inv[None, :]
    cos = jnp.tile(jnp.cos(ang), (1, copies))
    sin = jnp.tile(jnp.sin(ang), (1, copies))
    return _pad_halves(jnp.concatenate([cos, cos], axis=-1)), _pad_halves(jnp.concatenate([-sin, sin], axis=-1))


def _pair_heads(w):
    k = w.shape[0]
    half = DIFF_HD // 2
    w = w.reshape(k, DIFF_HEADS, 2, 2, half)
    w = w.transpose(1, 0, 3, 2, 4).reshape(DIFF_HEADS, k, 2 * DIFF_HD)
    return _pad_halves(w)


def _pair_masks():
    half = DIFF_HD // 2
    m = np.zeros((2, 2, 2, half), np.float32)
    for c in range(2):
        m[c, :, c, :] = 1.0
    return _pad_halves(jnp.asarray(m.reshape(2, 2 * DIFF_HD)))


def _col_offsets():
    sizes = (MLA_Q_LORA, MLA_KV_LORA, MLA_ROPE,
             NSA_HEADS * NSA_DK, NSA_DK, NSA_DV, NSA_DK, NSA_DV, NSA_DK, NSA_DV, NSA_HEADS * 3,
             DIFF_HEADS * 2 * DIFF_HD, DIFF_HEADS * 2 * DIFF_HD, DIFF_HEADS * DIFF_VD)
    names = ("c_q", "c_kv", "k_rope", "nsa_q", "nsa_kc", "nsa_vc", "nsa_ks", "nsa_vs", "nsa_kw", "nsa_vw",
             "nsa_g", "d_q", "d_k", "d_v")
    offs = {}
    o = 0
    for nme, sz in zip(names, sizes):
        offs[nme] = (o, o + sz)
        o += sz
    offs["merge"] = (o, None)
    return offs


def _mixers(n2d, batch, seq, layer, w_in, p):
    offs = _col_offsets()
    col = lambda name: w_in[:, offs[name][0]:offs[name][1]]
    bf = lambda a: a.astype(BF16)

    lat = proj_plain(n2d, bf(jnp.concatenate([col("c_q"), col("c_kv")], axis=1)), F32)
    n_q, n_kv = mla_norm(lat, p["mla_q_norm"], p["mla_kv_norm"])
    cos_m, sin_m = _rope_tables(MLA_ROPE, seq, lead=MLA_NOPE)
    w_kr = jnp.concatenate([jnp.zeros((w_in.shape[0], MLA_NOPE), F32), col("k_rope")], axis=1)[None]
    kpe = proj_heads(n2d, bf(w_kr), batch, seq, rope=("weights", bf(_rot_cols(w_kr, MLA_NOPE)), cos_m, sin_m))
    w_uq = _heads(p["mla_w_uq"], MLA_HEADS, MLA_QK)
    q_mla = proj_heads(n_q, bf(w_uq), batch, seq, rope=("weights", bf(_rot_cols(w_uq, MLA_NOPE)), cos_m, sin_m),
                       scale=MLA_QK ** -0.5)
    w_ukv = _heads(p["mla_w_ukv"], MLA_HEADS, MLA_NOPE + MLA_V)
    w_uk = jnp.concatenate([w_ukv[..., :MLA_NOPE], jnp.zeros(w_ukv.shape[:2] + (MLA_ROPE,), F32)], axis=-1)
    k_mla = proj_heads(n_kv, bf(w_uk), batch, seq, add=kpe)
    vt_mla = proj_heads_t(n_kv, bf(w_ukv[..., MLA_NOPE:]), batch, seq)
    o_mla = flash_causal(q_mla, k_mla, vt_mla)

    cos_d, sin_d = _swap_tables(DIFF_HD, seq, copies=2)
    q_d = proj_heads(n2d, bf(_pair_heads(col("d_q"))), batch, seq, rope=("swap", cos_d, sin_d),
                     scale=DIFF_HD ** -0.5, masks=_pair_masks())
    k_d = proj_heads(n2d, bf(_pair_heads(col("d_k"))), batch, seq, rope=("swap", cos_d, sin_d))
    vt_d = proj_heads_t(n2d, bf(_heads(col("d_v"), DIFF_HEADS, DIFF_VD)), batch, seq)
    lambda_init = 0.8 - 0.6 * math.exp(-0.3 * layer)
    o_diff = diff_attention(q_d, k_d, vt_d, p["diff_lam_q1"], p["diff_lam_k1"], p["diff_lam_q2"],
                            p["diff_lam_k2"], p["diff_subln"], lambda_init)

    cos_n, sin_n = _swap_tables(NSA_DK, seq)
    w_nq = _pad_halves(_heads(col("nsa_q"), NSA_HEADS, NSA_DK))
    q_n = proj_heads(n2d, bf(w_nq), batch, seq, rope=("swap", cos_n, sin_n), scale=NSA_DK ** -0.5)
    w_nk = _pad_halves(jnp.stack([col("nsa_kc"), col("nsa_ks"), col("nsa_kw")], axis=0))
    k_n = proj_heads(n2d, bf(w_nk), batch, seq, rope=("swap", cos_n, sin_n))
    dk_pad = w_nk.shape[-1]
    v_cmp = proj_plain(n2d, bf(col("nsa_vc")), BF16)
    vt_n = proj_heads_t(n2d, bf(jnp.stack([col("nsa_vs"), col("nsa_vw")], axis=0)), batch, seq)
    w_g = jnp.concatenate([col("nsa_g"), jnp.zeros((w_in.shape[0], LANES - NSA_HEADS * 3), F32)], axis=1)
    g_nsa = proj_plain(n2d, bf(w_g), F32, sigmoid=True)

    ng = seq // NSA_CMP_STRIDE
    w1k = _pad_halves(p["nsa_cmp_k_w1"].reshape(NSA_CMP_LEN, NSA_DK, NSA_CMP_HIDDEN), axis=1)
    kc = nsa_compress(k_n.reshape(batch, 3, ng, NSA_CMP_STRIDE * dk_pad), 0,
                      _pad_halves(p["nsa_cmp_k_pos"]), w1k.reshape(NSA_CMP_LEN * dk_pad, NSA_CMP_HIDDEN),
                      _pad_halves(p["nsa_cmp_k_w2"]))
    vc_tok = v_cmp.reshape(batch, 1, ng, NSA_CMP_STRIDE * NSA_DV)
    vct = nsa_compress(vc_tok, 0, p["nsa_cmp_v_pos"], p["nsa_cmp_v_w1"], p["nsa_cmp_v_w2"], transposed=True)
    o_c, sel = nsa_cmp_topk(q_n, kc, vct)
    o_s = nsa_selected(q_n, k_n, 1, vt_n, sel)
    o_w = nsa_window(q_n, k_n, 2, vt_n, 1)

    g_merge = proj_plain(n2d, bf(w_in[:, offs["merge"][0]:]), BF16, sigmoid=True)
    m = batch * seq
    return (o_mla.reshape(m, -1), o_c.reshape(m, -1), o_s.reshape(m, -1), o_w.reshape(m, -1), g_nsa,
            o_diff, g_merge)


def kernel(x, ffn1_norm, ffn1_w_in, ffn1_w_out, mix_norm, w_in, mla_q_norm, mla_kv_norm, mla_w_uq, mla_w_ukv, nsa_cmp_k_pos, nsa_cmp_k_w1, nsa_cmp_k_w2, nsa_cmp_v_pos, nsa_cmp_v_w1, nsa_cmp_v_w2, diff_lam_q1, diff_lam_k1, diff_lam_q2, diff_lam_k2, diff_subln, w_br_mla, w_br_nsa, w_br_diff, w_out, ffn2_norm, ffn2_w_in, ffn2_w_out, final_norm):
    batch, seq, d = x.shape
    depth = w_in.shape[0]
    x2d = x.reshape(batch * seq, d)
    bf = lambda a: a.astype(BF16)
    for l in range(depth):
        p = {"mla_q_norm": mla_q_norm[l], "mla_kv_norm": mla_kv_norm[l], "mla_w_uq": mla_w_uq[l],
             "mla_w_ukv": mla_w_ukv[l], "nsa_cmp_k_pos": nsa_cmp_k_pos[l], "nsa_cmp_k_w1": nsa_cmp_k_w1[l],
             "nsa_cmp_k_w2": nsa_cmp_k_w2[l], "nsa_cmp_v_pos": nsa_cmp_v_pos[l], "nsa_cmp_v_w1": nsa_cmp_v_w1[l],
             "nsa_cmp_v_w2": nsa_cmp_v_w2[l], "diff_lam_q1": diff_lam_q1[l], "diff_lam_k1": diff_lam_k1[l],
             "diff_lam_q2": diff_lam_q2[l], "diff_lam_k2": diff_lam_k2[l], "diff_subln": diff_subln[l]}
        x2d, n2d = ffn(x2d, ffn1_norm[l], bf(ffn1_w_in[l]), bf(ffn1_w_out[l]), post_norm=(mix_norm[l], BF16))
        o_mla, o_c, o_s, o_w, g_nsa, o_diff, g_merge = _mixers(n2d, batch, seq, l, w_in[l], p)
        x2d = merge(x2d, o_mla, o_c, o_s, o_w, g_nsa, o_diff, g_merge, bf(w_br_mla[l]), bf(w_br_nsa[l]),
                    bf(w_br_diff[l]).reshape(DIFF_HEADS, DIFF_VD, d), bf(w_out[l]), seq)
        if l + 1 < depth:
            x2d = ffn(x2d, ffn2_norm[l], bf(ffn2_w_in[l]), bf(ffn2_w_out[l]))
        else:
            _, out = ffn(x2d, ffn2_norm[l], bf(ffn2_w_in[l]), bf(ffn2_w_out[l]), post_norm=(final_norm, F32))
    return out.reshape(batch, seq, d)
```

```python
import functools
import math

import numpy as np
import jax
import jax.numpy as jnp
from jax import lax
from jax.experimental import pallas as pl
from jax.experimental.pallas import tpu as pltpu

F32 = jnp.float32
BF16 = jnp.bfloat16

NORM_EPS = 1e-6
ROPE_THETA = 10000.0
NEG_INF = -1e30
REMOVED = -3e38
N_BRANCH = 3

MLA_HEADS = 6
MLA_Q_LORA = 768
MLA_KV_LORA = 512
MLA_NOPE = 128
MLA_ROPE = 64
MLA_V = 128
MLA_QK = MLA_NOPE + MLA_ROPE

NSA_HEADS = 4
NSA_DK = 192
NSA_DV = 128
NSA_CMP_LEN = 32
NSA_CMP_STRIDE = 16
NSA_CMP_HIDDEN = 256
NSA_SEL_LEN = 64
NSA_TOPK = 16
NSA_WINDOW = 512
NSA_FORCE_SCORE = 1e6

DIFF_HEADS = 4
DIFF_HD = 96
DIFF_VD = 2 * DIFF_HD

LANES = 128
ONES_PAD = 16
SCORE_DTYPE = BF16
VMEM_LIMIT_MB = 56


def _cparams(dims, vmem_mb=VMEM_LIMIT_MB):
    return pltpu.CompilerParams(dimension_semantics=dims, vmem_limit_bytes=vmem_mb * 2**20)


def _sigmoid(x):
    return 1.0 / (1.0 + jnp.exp(-x))


def _dot(a, b):
    return jnp.dot(a, b, preferred_element_type=F32)


def _dot_nt(a, b):
    return lax.dot_general(a, b, (((1,), (1,)), ((), ())), preferred_element_type=F32)


def _tile(n, pref):
    t = min(n, pref)
    assert n % t == 0, (n, t)
    return t


def _mla_latent_kernel(n_ref, w_ref, wq_ref, wkv_ref, nq_ref, nkv_ref):
    lat = _dot(n_ref[...], w_ref[...])
    cq = lat[:, :MLA_Q_LORA]
    ckv = lat[:, MLA_Q_LORA:]
    nq = cq * lax.rsqrt(jnp.mean(cq * cq, axis=-1, keepdims=True) + NORM_EPS)
    nkv = ckv * lax.rsqrt(jnp.mean(ckv * ckv, axis=-1, keepdims=True) + NORM_EPS)
    nq_ref[...] = (nq * wq_ref[...]).astype(BF16)
    nkv_ref[...] = (nkv * wkv_ref[...]).astype(BF16)


def mla_latent(n2d, w_lat, wq, wkv):
    m, k = n2d.shape
    d = w_lat.shape[1]
    tm = _tile(m, 512)
    return pl.pallas_call(
        _mla_latent_kernel,
        grid=(m // tm,),
        in_specs=[pl.BlockSpec((tm, k), lambda i: (i, 0)),
                  pl.BlockSpec((k, d), lambda i: (0, 0)),
                  pl.BlockSpec((1, MLA_Q_LORA), lambda i: (0, 0)),
                  pl.BlockSpec((1, MLA_KV_LORA), lambda i: (0, 0))],
        out_specs=[pl.BlockSpec((tm, MLA_Q_LORA), lambda i: (i, 0)),
                   pl.BlockSpec((tm, MLA_KV_LORA), lambda i: (i, 0))],
        out_shape=[jax.ShapeDtypeStruct((m, MLA_Q_LORA), BF16), jax.ShapeDtypeStruct((m, MLA_KV_LORA), BF16)],
        compiler_params=_cparams(("parallel",)),
        name="mla_latent",
    )(n2d, w_lat, wq.reshape(1, -1).astype(F32), wkv.reshape(1, -1).astype(F32))


def _ffn_kernel(x_ref, nw_ref, wg_ref, wu_ref, wo_ref, *rest, n_f, post_norm):
    if post_norm:
        pw_ref, o_ref, p_ref, n_scr = rest
    else:
        o_ref, n_scr = rest
    f = pl.program_id(1)

    @pl.when(f == 0)
    def _():
        x = x_ref[...]
        y = x * lax.rsqrt(jnp.mean(x * x, axis=-1, keepdims=True) + NORM_EPS)
        n_scr[...] = (y * nw_ref[...]).astype(BF16)
        o_ref[...] = jnp.zeros_like(o_ref)

    n = n_scr[...]
    g = _dot(n, wg_ref[...])
    u = _dot(n, wu_ref[...])
    h = (g * _sigmoid(g) * u).astype(BF16)
    o_ref[...] += _dot(h, wo_ref[...])

    @pl.when(f == n_f - 1)
    def _():
        y = x_ref[...] + 0.5 * o_ref[...]
        o_ref[...] = y
        if post_norm:
            yn = y * lax.rsqrt(jnp.mean(y * y, axis=-1, keepdims=True) + NORM_EPS)
            p_ref[...] = (yn * pw_ref[...]).astype(p_ref.dtype)


def ffn(x2d, norm_w, w_in, w_out, post_norm=None):
    m, d = x2d.shape
    f_dim = w_out.shape[0]
    tm = _tile(m, 512)
    tf = _tile(f_dim, 512)
    n_f = f_dim // tf
    row = pl.BlockSpec((tm, d), lambda i, f: (i, 0))
    vec = pl.BlockSpec((1, d), lambda i, f: (0, 0))
    in_specs = [row, vec,
                pl.BlockSpec((d, tf), lambda i, f: (0, f)),
                pl.BlockSpec((d, tf), lambda i, f: (0, f + n_f)),
                pl.BlockSpec((tf, d), lambda i, f: (f, 0))]
    args = [x2d, norm_w.reshape(1, d).astype(F32), w_in, w_in, w_out]
    out_specs, out_shape = row, jax.ShapeDtypeStruct((m, d), F32)
    if post_norm is not None:
        in_specs.append(vec)
        args.append(post_norm[0].reshape(1, d).astype(F32))
        out_specs, out_shape = [row, row], [out_shape, jax.ShapeDtypeStruct((m, d), post_norm[1])]
    return pl.pallas_call(
        functools.partial(_ffn_kernel, n_f=n_f, post_norm=post_norm is not None),
        grid=(m // tm, n_f),
        in_specs=in_specs,
        out_specs=out_specs,
        out_shape=out_shape,
        scratch_shapes=[pltpu.VMEM((tm, d), BF16)],
        compiler_params=_cparams(("parallel", "arbitrary")),
        name="ffn",
    )(*args)


def _proj_plain_kernel(n_ref, w_ref, o_ref, *, sigmoid):
    y = _dot(n_ref[...], w_ref[...])
    if sigmoid:
        y = _sigmoid(y)
    o_ref[...] = y.astype(o_ref.dtype)


def proj_plain(n2d, w, out_dtype, sigmoid=False):
    m, k = n2d.shape
    n_out = w.shape[1]
    tm = _tile(m, 1024)
    tn = next(t for t in (512, 256, LANES) if n_out % t == 0)
    return pl.pallas_call(
        functools.partial(_proj_plain_kernel, sigmoid=sigmoid),
        grid=(m // tm, n_out // tn),
        in_specs=[pl.BlockSpec((tm, k), lambda i, j: (i, 0)),
                  pl.BlockSpec((k, tn), lambda i, j: (0, j))],
        out_specs=pl.BlockSpec((tm, tn), lambda i, j: (i, j)),
        out_shape=jax.ShapeDtypeStruct((m, n_out), out_dtype),
        compiler_params=_cparams(("parallel", "arbitrary")),
        name="proj_plain",
    )(n2d, w)


def _proj_heads_kernel(*refs, rope, has_add, has_masks, scale):
    n_ref, w_ref = refs[0], refs[1]
    o_ref = refs[-1]
    n = n_ref[...]
    n_copy = o_ref.shape[1] // w_ref.shape[0]
    for h in range(w_ref.shape[0]):
        pos = 2
        y = _dot(n, w_ref[h])
        if rope == "weights":
            wr_ref, cos_ref, sin_ref = refs[pos:pos + 3]
            pos += 3
            y = y * cos_ref[...] + _dot(n, wr_ref[h]) * sin_ref[...]
        elif rope == "swap":
            cos_ref, sin_ref = refs[pos:pos + 2]
            pos += 2
            half = y.shape[1] // 2
            y = y * cos_ref[...] + jnp.concatenate([y[:, half:], y[:, :half]], axis=1) * sin_ref[...]
        if has_add:
            y = y + refs[pos][0, 0].astype(F32)
            pos += 1
        if scale != 1.0:
            y = y * scale
        if has_masks:
            mask_ref = refs[pos]
            for c in range(n_copy):
                o_ref[0, h * n_copy + c] = (y * mask_ref[c:c + 1, :]).astype(o_ref.dtype)
        else:
            o_ref[0, h] = y.astype(o_ref.dtype)


def proj_heads(n2d, w, batch, seq, rope=None, add=None, scale=1.0, masks=None):
    m, k = n2d.shape
    g, _, dh = w.shape
    tm = _tile(seq, 1024)
    ns = seq // tm
    weights = pl.BlockSpec((g, k, dh), lambda b, i: (0, 0, 0))
    in_specs = [pl.BlockSpec((tm, k), lambda b, i: (b * ns + i, 0)), weights]
    args = [n2d, w]
    table = pl.BlockSpec((tm, dh), lambda b, i: (i, 0))
    if rope is not None and rope[0] == "weights":
        in_specs += [weights, table, table]
        args += list(rope[1:])
    elif rope is not None:
        assert rope[0] == "swap" and dh % (2 * LANES) == 0
        in_specs += [table, table]
        args += list(rope[1:])
    if add is not None:
        in_specs.append(pl.BlockSpec((1, 1, tm, dh), lambda b, i: (b, 0, i, 0)))
        args.append(add)
    n_copy = 1
    if masks is not None:
        n_copy = masks.shape[0]
        in_specs.append(pl.BlockSpec((n_copy, dh), lambda b, i: (0, 0)))
        args.append(masks)
    return pl.pallas_call(
        functools.partial(_proj_heads_kernel, rope=None if rope is None else rope[0], has_add=add is not None,
                          has_masks=masks is not None, scale=scale),
        grid=(batch, ns),
        in_specs=in_specs,
        out_specs=pl.BlockSpec((1, g * n_copy, tm, dh), lambda b, i: (b, 0, i, 0)),
        out_shape=jax.ShapeDtypeStruct((batch, g * n_copy, seq, dh), BF16),
        compiler_params=_cparams(("parallel", "parallel")),
        name="proj_heads",
    )(*args)


def _proj_heads_t_kernel(n_ref, wt_ref, o_ref):
    dh = wt_ref.shape[1]
    pad = o_ref.shape[2] - dh
    row = lax.broadcasted_iota(jnp.int32, (pad, o_ref.shape[3]), 0)
    ones_rows = jnp.where(row == 0, 1.0, 0.0).astype(o_ref.dtype)
    n = n_ref[...]
    for h in range(wt_ref.shape[0]):
        o_ref[0, h, :dh, :] = _dot_nt(wt_ref[h], n).astype(o_ref.dtype)
        o_ref[0, h, dh:, :] = ones_rows


def proj_heads_t(n2d, w, batch, seq):
    m, k = n2d.shape
    g, _, dh = w.shape
    tm = _tile(seq, 1024)
    ns = seq // tm
    return pl.pallas_call(
        _proj_heads_t_kernel,
        grid=(batch, ns),
        in_specs=[pl.BlockSpec((tm, k), lambda b, i: (b * ns + i, 0)),
                  pl.BlockSpec((g, dh, k), lambda b, i: (0, 0, 0))],
        out_specs=pl.BlockSpec((1, g, dh + ONES_PAD, tm), lambda b, i: (b, 0, 0, i)),
        out_shape=jax.ShapeDtypeStruct((batch, g, dh + ONES_PAD, seq), BF16),
        compiler_params=_cparams(("parallel", "parallel")),
        name="proj_heads_t",
    )(n2d, jnp.swapaxes(w, 1, 2))


def _softmax_step_t(st, vt, m_scr, acc_scr, exp_dtype):
    m_prev = m_scr[...]
    m_new = jnp.maximum(m_prev, jnp.max(st, axis=0, keepdims=True).astype(F32))
    alpha = jnp.exp(m_prev - m_new)
    pt = jnp.exp((st - m_new.astype(st.dtype)).astype(exp_dtype)).astype(BF16)
    acc_scr[...] = alpha * acc_scr[...] + _dot(vt, pt)
    m_scr[...] = m_new


def _diag_visibility(d, c, tq, tk):
    if d * tk >= (c + 1) * tq:
        return "none"
    if (d + 1) * tk - 1 <= c * tq:
        return "full"
    return "partial"


def _causal_where(st, d, c):
    tk, tq = st.shape
    key = lax.broadcasted_iota(jnp.int32, (tk, tq), 0) + d * tk
    qry = lax.broadcasted_iota(jnp.int32, (tk, tq), 1) + c * tq
    return jnp.where(key <= qry, st, NEG_INF)


def _flash_sweep(n, r, sub_of, tq, tk, scores, vt_tile, m_scr, acc_scr, st_scr, exp_dtype=F32):
    n_chain = m_scr.shape[0]
    for c in range(n_chain):
        m_scr[c] = jnp.full(m_scr.shape[1:], NEG_INF, F32)
        acc_scr[c] = jnp.zeros(acc_scr.shape[1:], F32)
    n_slot = st_scr.shape[0]
    assert r % n_slot == 0
    everyone = (True,) * n_chain
    for c, st in enumerate(scores(0, everyone)):
        st_scr[0, c] = st.astype(st_scr.dtype)

    def body(jj, carry):
        for slot in range(n_slot):
            j = jj * n_slot + slot
            nxt = scores(j + 1, everyone)
            vt = vt_tile(j)
            for c in range(n_chain):
                st = st_scr[slot, c]
                st_scr[(slot + 1) % n_slot, c] = nxt[c].astype(st_scr.dtype)
                _softmax_step_t(st, vt, m_scr.at[c], acc_scr.at[c], exp_dtype)
        return carry

    lax.fori_loop(0, n // n_slot, body, 0)
    for d in range(r):
        vis = [_diag_visibility(d, sub_of(c), tq, tk) for c in range(n_chain)]
        need = tuple(d + 1 < r and _diag_visibility(d + 1, sub_of(c), tq, tk) != "none" for c in range(n_chain))
        nxt = scores(n + d + 1, need) if any(need) else None
        vt = vt_tile(n + d)
        for c in range(n_chain):
            if vis[c] != "none":
                st = st_scr[d % n_slot, c]
                if vis[c] == "partial":
                    st = _causal_where(st.astype(F32), d, sub_of(c)).astype(st_scr.dtype)
            if need[c]:
                st_scr[(d + 1) % n_slot, c] = nxt[c].astype(st_scr.dtype)
            if vis[c] != "none":
                _softmax_step_t(st, vt, m_scr.at[c], acc_scr.at[c], exp_dtype)


def _flash_kernel(q_ref, k_ref, vt_ref, o_ref, m_scr, acc_scr, st_scr, *, n_sub, tq, tk):
    i = pl.program_id(2)
    dv = o_ref.shape[2]
    r = n_sub * tq // tk

    def scores(j, need):
        k = k_ref[0, 0, pl.ds(pl.multiple_of(j * tk, tk), tk), :]
        return [_dot_nt(k, q_ref[0, 0, c * tq:(c + 1) * tq, :]) if need[c] else None for c in range(n_sub)]

    def vt_tile(j):
        return vt_ref[0, 0, :, pl.ds(pl.multiple_of(j * tk, tk), tk)]

    _flash_sweep(i * r, r, lambda c: c, tq, tk, scores, vt_tile, m_scr, acc_scr, st_scr, exp_dtype=BF16)
    for c in range(n_sub):
        o_t = acc_scr[c, :dv, :] / acc_scr[c, dv:dv + 1, :]
        o_ref[0, c * tq:(c + 1) * tq, :] = o_t.T.astype(o_ref.dtype)


def _score_slots(r):
    return 2 if r % 2 == 0 else 1


def _flash_tiles(s, max_sub):
    tq = _tile(s, 512)
    n_sub = next(n for n in (4, 2, 1) if n <= max_sub and s % (n * tq) == 0)
    return n_sub, tq, tq


def flash_causal(q, k, vt):
    b, h, s, dk = q.shape
    dve = vt.shape[2]
    dv = dve - ONES_PAD
    n_sub, tq, tk = _flash_tiles(s, max_sub=4)
    t = n_sub * tq
    return pl.pallas_call(
        functools.partial(_flash_kernel, n_sub=n_sub, tq=tq, tk=tk),
        grid=(b, h, s // t),
        in_specs=[pl.BlockSpec((1, 1, t, dk), lambda b_, h_, i: (b_, h_, i, 0)),
                  pl.BlockSpec((1, 1, s, dk), lambda b_, h_, i: (b_, h_, 0, 0)),
                  pl.BlockSpec((1, 1, dve, s), lambda b_, h_, i: (b_, h_, 0, 0))],
        out_specs=pl.BlockSpec((1, t, dv), lambda b_, h_, i: (b_, i, h_)),
        out_shape=jax.ShapeDtypeStruct((b, s, h * dv), BF16),
        scratch_shapes=[pltpu.VMEM((n_sub, 1, tq), F32), pltpu.VMEM((n_sub, dve, tq), F32),
                        pltpu.VMEM((_score_slots(n_sub * tq // tk), n_sub, tk, tq), SCORE_DTYPE)],
        compiler_params=_cparams(("parallel", "parallel", "arbitrary")),
        name="mla_flash",
    )(q, k, vt)


def _diff_kernel(q_ref, k_ref, vt_ref, lq1_ref, lk1_ref, lq2_ref, lk2_ref, sub_ref, o_ref,
                 m_scr, acc_scr, st_scr, *, n_sub, tq, tk, lambda_init):
    i = pl.program_id(2)
    vd = o_ref.shape[3]
    r = n_sub * tq // tk

    def scores(j, need):
        k = k_ref[0, 0, pl.ds(pl.multiple_of(j * tk, tk), tk), :]
        return [_dot_nt(k, q_ref[0, ch // n_sub, (ch % n_sub) * tq:(ch % n_sub + 1) * tq, :]) if need[ch] else None
                for ch in range(2 * n_sub)]

    def vt_tile(j):
        return vt_ref[0, 0, :, pl.ds(pl.multiple_of(j * tk, tk), tk)]

    _flash_sweep(i * r, r, lambda ch: ch % n_sub, tq, tk, scores, vt_tile, m_scr, acc_scr, st_scr, exp_dtype=BF16)

    lam = (jnp.exp(jnp.sum(lq1_ref[...] * lk1_ref[...], axis=-1, keepdims=True))
           - jnp.exp(jnp.sum(lq2_ref[...] * lk2_ref[...], axis=-1, keepdims=True)) + lambda_init)
    lane_pad = (-vd) % LANES
    for c in range(n_sub):
        o_t = (acc_scr[c, :vd, :] / acc_scr[c, vd:vd + 1, :]
               - lam * (acc_scr[n_sub + c, :vd, :] / acc_scr[n_sub + c, vd:vd + 1, :]))
        o_t = o_t * lax.rsqrt(jnp.mean(o_t * o_t, axis=0, keepdims=True) + NORM_EPS)
        o_t = jnp.concatenate([o_t, jnp.zeros((lane_pad, tq), F32)], axis=0)
        o_ref[0, 0, c * tq:(c + 1) * tq, :] = (
            o_t.T[:, :vd] * sub_ref[...] * (1.0 - lambda_init)).astype(o_ref.dtype)


def diff_attention(q, k, vt, lq1, lk1, lq2, lk2, subln, lambda_init):
    b, h2, s, dk = q.shape
    h = h2 // 2
    hd = DIFF_HD
    vde = vt.shape[2]
    vd = vde - ONES_PAD
    n_sub, tq, tk = _flash_tiles(s, max_sub=4)
    t = n_sub * tq
    vec = lambda a: a.reshape(1, -1).astype(F32)
    small = lambda n: pl.BlockSpec((1, n), lambda b_, h_, i: (0, 0))
    return pl.pallas_call(
        functools.partial(_diff_kernel, n_sub=n_sub, tq=tq, tk=tk, lambda_init=lambda_init),
        grid=(b, h, s // t),
        in_specs=[pl.BlockSpec((1, 2, t, dk), lambda b_, h_, i: (b_, h_, i, 0)),
                  pl.BlockSpec((1, 1, s, dk), lambda b_, h_, i: (b_, h_, 0, 0)),
                  pl.BlockSpec((1, 1, vde, s), lambda b_, h_, i: (b_, h_, 0, 0)),
                  small(hd), small(hd), small(hd), small(hd), small(vd)],
        out_specs=pl.BlockSpec((1, 1, t, vd), lambda b_, h_, i: (b_, h_, i, 0)),
        out_shape=jax.ShapeDtypeStruct((b, h, s, vd), BF16),
        scratch_shapes=[pltpu.VMEM((2 * n_sub, 1, tq), F32), pltpu.VMEM((2 * n_sub, vde, tq), F32),
                        pltpu.VMEM((_score_slots(n_sub * tq // tk), 2 * n_sub, tk, tq), SCORE_DTYPE)],
        compiler_params=_cparams(("parallel", "parallel", "arbitrary")),
        name="diff_flash",
    )(q, k, vt, vec(lq1), vec(lk1), vec(lq2), vec(lk2), vec(subln))


def _compress_kernel(a_ref, pa_ref, pb_ref, w1a_ref, w1b_ref, w2_ref, o_ref, *, transposed):
    ng = a_ref.shape[2]
    rows = min(ng, 256)
    us, vs = [], []
    for r0 in range(0, ng, rows):
        a = a_ref[0, 0, r0:r0 + rows, :].astype(F32)
        us.append(_dot((a + pa_ref[...]).astype(BF16), w1a_ref[...]))
        vs.append(_dot((a + pb_ref[...]).astype(BF16), w1b_ref[...]))
    u = jnp.concatenate(us, axis=0)
    v = jnp.concatenate(vs, axis=0)
    hdn = u + pltpu.roll(v, ng - 1, 0)
    hdn = (hdn * _sigmoid(hdn)).astype(BF16)
    if transposed:
        d_out = w2_ref.shape[0]
        o_ref[0, :d_out, :] = _dot_nt(w2_ref[...], hdn).astype(o_ref.dtype)
        row = lax.broadcasted_iota(jnp.int32, (o_ref.shape[1] - d_out, ng), 0)
        o_ref[0, d_out:, :] = jnp.where(row == 0, 1.0, 0.0).astype(o_ref.dtype)
    else:
        o_ref[0] = _dot(hdn, w2_ref[...]).astype(o_ref.dtype)


def nsa_compress(tok, g_idx, pos, w1, w2, transposed=False):
    assert NSA_CMP_LEN == 2 * NSA_CMP_STRIDE
    b, _, ng, wd = tok.shape
    d = wd // NSA_CMP_STRIDE
    d_out = w2.shape[1]
    pos_flat = pos.astype(F32).reshape(1, NSA_CMP_LEN * d)
    w1 = w1.astype(BF16)
    w2 = w2.astype(BF16).T if transposed else w2.astype(BF16)
    out_block = (1, d_out + ONES_PAD, ng) if transposed else (1, ng, d_out)
    full = lambda shape: pl.BlockSpec(shape, lambda b_: (0,) * len(shape))
    return pl.pallas_call(
        functools.partial(_compress_kernel, transposed=transposed),
        grid=(b,),
        in_specs=[pl.BlockSpec((1, 1, ng, wd), lambda b_: (b_, g_idx, 0, 0)),
                  full((1, wd)), full((1, wd)), full((wd, NSA_CMP_HIDDEN)), full((wd, NSA_CMP_HIDDEN)),
                  full(w2.shape)],
        out_specs=pl.BlockSpec(out_block, lambda b_: (b_, 0, 0)),
        out_shape=jax.ShapeDtypeStruct((b,) + out_block[1:], BF16),
        compiler_params=_cparams(("parallel",)),
        name="nsa_compress",
    )(tok, pos_flat[:, :wd], pos_flat[:, wd:], w1[:wd], w1[wd:], w2)


def _split3(x):
    hi = x.astype(BF16)
    r = x - hi.astype(F32)
    mid = r.astype(BF16)
    lo = (r - mid.astype(F32)).astype(BF16)
    return hi, mid, lo


def _cmp_topk_kernel(q_ref, kc_ref, vct_ref, wselt_ref, oc_ref, selt_ref, *, tq, k_top):
    i = pl.program_id(1)
    qs = i * tq
    nc = kc_ref.shape[1]
    n_sel = wselt_ref.shape[0]
    kc = kc_ref[0]
    vct = vct_ref[0]
    qpos_c = qs + lax.broadcasted_iota(jnp.int32, (nc, tq), 1)
    cend = lax.broadcasted_iota(jnp.int32, (nc, tq), 0) * NSA_CMP_STRIDE + (NSA_CMP_LEN - 1)
    valid_c = cend <= qpos_c
    any_valid = qs + lax.broadcasted_iota(jnp.int32, (1, tq), 1) >= NSA_CMP_LEN - 1
    imp_t = jnp.zeros((nc, tq), F32)
    for h in range(NSA_HEADS):
        st = jnp.where(valid_c, _dot_nt(kc, q_ref[0, h]), NEG_INF)
        e = jnp.exp(st - jnp.max(st, axis=0, keepdims=True))
        inv = jnp.where(any_valid, 1.0 / jnp.sum(e, axis=0, keepdims=True), 0.0)
        o_t = _dot(vct, e.astype(BF16))[:NSA_DV, :] * inv
        oc_ref[0, :, h * NSA_DV:(h + 1) * NSA_DV] = o_t.T
        imp_t = imp_t + e * inv

    wselt = wselt_ref[...]
    hi, mid, lo = _split3(imp_t)
    imp_sel = _dot(wselt, hi) + _dot(wselt, mid) + _dot(wselt, lo)

    blk = lax.broadcasted_iota(jnp.int32, (n_sel, tq), 0)
    qpos = qs + lax.broadcasted_iota(jnp.int32, (n_sel, tq), 1)
    cur = lax.shift_right_arithmetic(qpos, int(math.log2(NSA_SEL_LEN)))
    forced = (blk == 0) | (blk == cur) | (blk == cur - 1)
    valid_s = blk * NSA_SEL_LEN <= qpos
    score = jnp.where(valid_s, jnp.where(forced, NSA_FORCE_SCORE, imp_sel), NEG_INF)
    blk_f = blk.astype(F32)
    sel = jnp.zeros((n_sel, tq), F32)
    for _ in range(k_top):
        mx = jnp.max(score, axis=0, keepdims=True)
        first = jnp.min(jnp.where(score == mx, blk_f, float(n_sel)), axis=0, keepdims=True)
        hit = blk_f == first
        sel = jnp.where(hit, 1.0, sel)
        score = jnp.where(hit, REMOVED, score)
    selt_ref[0] = sel.astype(selt_ref.dtype)


def _sel_weight_matrix(n_pad, n_sel):
    r_c = NSA_CMP_LEN // NSA_CMP_STRIDE
    ratio = NSA_SEL_LEN // NSA_CMP_STRIDE
    overlap_w = [max(0, min(o * NSA_CMP_STRIDE + NSA_CMP_LEN, NSA_SEL_LEN) - max(o * NSA_CMP_STRIDE, 0))
                 / NSA_CMP_STRIDE for o in range(-(r_c - 1), ratio)]
    w = np.zeros((n_pad, n_sel), np.float32)
    for n in range(n_sel):
        for u, w_u in enumerate(overlap_w):
            c = ratio * n + u - (r_c - 1)
            if 0 <= c < n_pad:
                w[c, n] = w_u
    return w


def nsa_cmp_topk(q, kc, vct):
    b, h, s, dk = q.shape
    nc = kc.shape[1]
    dve = vct.shape[1]
    n_sel = s // NSA_SEL_LEN
    assert NSA_SEL_LEN & (NSA_SEL_LEN - 1) == 0
    tq = _tile(s, 512)
    wselt = jnp.asarray(_sel_weight_matrix(nc, n_sel).T, BF16)
    return pl.pallas_call(
        functools.partial(_cmp_topk_kernel, tq=tq, k_top=min(NSA_TOPK, n_sel)),
        grid=(b, s // tq),
        in_specs=[pl.BlockSpec((1, h, tq, dk), lambda b_, i: (b_, 0, i, 0)),
                  pl.BlockSpec((1, nc, dk), lambda b_, i: (b_, 0, 0)),
                  pl.BlockSpec((1, dve, nc), lambda b_, i: (b_, 0, 0)),
                  pl.BlockSpec((n_sel, nc), lambda b_, i: (0, 0))],
        out_specs=[pl.BlockSpec((1, tq, h * NSA_DV), lambda b_, i: (b_, i, 0)),
                   pl.BlockSpec((1, n_sel, tq), lambda b_, i: (b_, 0, i))],
        out_shape=[jax.ShapeDtypeStruct((b, s, h * NSA_DV), F32), jax.ShapeDtypeStruct((b, n_sel, s), F32)],
        compiler_params=_cparams(("parallel", "parallel")),
        name="nsa_cmp_topk",
    )(q, kc, vct, wselt)


def _nsa_sel_kernel(q_ref, k_ref, vt_ref, sel_ref, o_ref, m_scr, acc_scr, st_scr, *, n_sub, tq, tk):
    i = pl.program_id(1)
    bpt = tk // NSA_SEL_LEN
    r = n_sub * tq // tk

    def scores(j, need):
        k = k_ref[0, 0, pl.ds(pl.multiple_of(j * tk, tk), tk), :]
        out = []
        for c in range(n_sub):
            rows = slice(c * tq, (c + 1) * tq)
            if not any(need[c * NSA_HEADS:(c + 1) * NSA_HEADS]):
                out += [None] * NSA_HEADS
                continue
            flags = sel_ref[0, pl.ds(pl.multiple_of(j * bpt, bpt), bpt), rows]
            keep_t = jnp.broadcast_to(flags[:, None, :], (bpt, NSA_SEL_LEN, tq)).reshape(tk, tq) > 0.5
            out += [jnp.where(keep_t, _dot_nt(k, q_ref[0, h, rows, :]), NEG_INF) if need[c * NSA_HEADS + h] else None
                    for h in range(NSA_HEADS)]
        return out

    def vt_tile(j):
        return vt_ref[0, 0, :, pl.ds(pl.multiple_of(j * tk, tk), tk)]

    _flash_sweep(i * r, r, lambda ch: ch // NSA_HEADS, tq, tk, scores, vt_tile, m_scr, acc_scr, st_scr,
                 exp_dtype=BF16)
    for ch in range(n_sub * NSA_HEADS):
        c, h = divmod(ch, NSA_HEADS)
        o_t = acc_scr[ch, :NSA_DV, :] / acc_scr[ch, NSA_DV:NSA_DV + 1, :]
        o_ref[0, c * tq:(c + 1) * tq, h * NSA_DV:(h + 1) * NSA_DV] = o_t.T


def _nsa_sel_tiles(s):
    return _flash_tiles(s, max_sub=2)


def nsa_selected(q, k3, k_idx, vt, sel):
    b, h, s, dk = q.shape
    n_sel = sel.shape[1]
    dve = vt.shape[2]
    n_sub, tq, tk = _nsa_sel_tiles(s)
    t = n_sub * tq
    n_chain = n_sub * h
    assert tk % NSA_SEL_LEN == 0 and t % tk == 0
    return pl.pallas_call(
        functools.partial(_nsa_sel_kernel, n_sub=n_sub, tq=tq, tk=tk),
        grid=(b, s // t),
        in_specs=[pl.BlockSpec((1, h, t, dk), lambda b_, i: (b_, 0, i, 0)),
                  pl.BlockSpec((1, 1, s, dk), lambda b_, i: (b_, k_idx, 0, 0)),
                  pl.BlockSpec((1, 1, dve, s), lambda b_, i: (b_, 0, 0, 0)),
                  pl.BlockSpec((1, n_sel, t), lambda b_, i: (b_, 0, i))],
        out_specs=pl.BlockSpec((1, t, h * NSA_DV), lambda b_, i: (b_, i, 0)),
        out_shape=jax.ShapeDtypeStruct((b, s, h * NSA_DV), F32),
        scratch_shapes=[pltpu.VMEM((n_chain, 1, tq), F32), pltpu.VMEM((n_chain, dve, tq), F32),
                        pltpu.VMEM((_score_slots(t // tk), n_chain, tk, tq), SCORE_DTYPE)],
        compiler_params=_cparams(("parallel", "arbitrary")),
        name="nsa_selected",
    )(q, k3, vt, sel)


def _nsa_win_kernel(q_ref, kp_ref, kc_ref, vtp_ref, vtc_ref, o_ref, *, t):
    i = pl.program_id(1)
    key = lax.broadcasted_iota(jnp.int32, (t, t), 0)
    qry = lax.broadcasted_iota(jnp.int32, (t, t), 1)
    keep_prev = (key > qry) & (i > 0)
    keep_cur = key <= qry
    kp, kc, vtp, vtc = kp_ref[0, 0], kc_ref[0, 0], vtp_ref[0, 0], vtc_ref[0, 0]
    for h in range(NSA_HEADS):
        q = q_ref[0, h]
        sp = jnp.where(keep_prev, _dot_nt(kp, q), NEG_INF)
        sc = jnp.where(keep_cur, _dot_nt(kc, q), NEG_INF)
        m = jnp.maximum(jnp.max(sp, axis=0, keepdims=True), jnp.max(sc, axis=0, keepdims=True))
        pp = jnp.exp((sp - m).astype(SCORE_DTYPE)).astype(BF16)
        pc = jnp.exp((sc - m).astype(SCORE_DTYPE)).astype(BF16)
        acc = _dot(vtp, pp) + _dot(vtc, pc)
        o_t = acc[:NSA_DV, :] / acc[NSA_DV:NSA_DV + 1, :]
        o_ref[0, :, h * NSA_DV:(h + 1) * NSA_DV] = o_t.T


def nsa_window(q, k3, k_idx, vt, v_idx):
    b, h, s, dk = q.shape
    dve = vt.shape[2]
    t = _tile(s, NSA_WINDOW)
    assert t == NSA_WINDOW, "window kernel needs the query tile to equal the window"
    prev = lambda i: jnp.maximum(i - 1, 0)
    return pl.pallas_call(
        functools.partial(_nsa_win_kernel, t=t),
        grid=(b, s // t),
        in_specs=[pl.BlockSpec((1, h, t, dk), lambda b_, i: (b_, 0, i, 0)),
                  pl.BlockSpec((1, 1, t, dk), lambda b_, i: (b_, k_idx, prev(i), 0)),
                  pl.BlockSpec((1, 1, t, dk), lambda b_, i: (b_, k_idx, i, 0)),
                  pl.BlockSpec((1, 1, dve, t), lambda b_, i: (b_, v_idx, 0, prev(i))),
                  pl.BlockSpec((1, 1, dve, t), lambda b_, i: (b_, v_idx, 0, i))],
        out_specs=pl.BlockSpec((1, t, h * NSA_DV), lambda b_, i: (b_, i, 0)),
        out_shape=jax.ShapeDtypeStruct((b, s, h * NSA_DV), F32),
        compiler_params=_cparams(("parallel", "parallel")),
        name="nsa_window",
    )(q, k3, k3, vt, vt)


def _merge_kernel(x_ref, om_ref, oc_ref, os_ref, ow_ref, gn_ref, od_ref, gm_ref, gs_ref, gd_ref,
                  wm_ref, wn_ref, wd_ref, wo_ref, ex_ref, o_ref, onsa_scr, *, n_j):
    j = pl.program_id(1)
    nsa_out = NSA_HEADS * NSA_DV

    @pl.when(j == 0)
    def _():
        g = gn_ref[...]
        hi = g.astype(BF16)
        lo = (g - hi.astype(F32)).astype(BF16)
        ge = _dot(hi, ex_ref[...]) + _dot(lo, ex_ref[...])
        onsa = (ge[:, :nsa_out] * oc_ref[...] + ge[:, nsa_out:2 * nsa_out] * os_ref[...]
                + ge[:, 2 * nsa_out:] * ow_ref[...])
        onsa_scr[...] = onsa.astype(BF16)
        o_ref[...] = jnp.zeros_like(o_ref)

    ym = _dot(om_ref[...], wm_ref[...])
    yn = _dot(onsa_scr[...], wn_ref[...])
    yd = _dot(od_ref[0, 0], wd_ref[0])
    for h in range(1, DIFF_HEADS):
        yd = yd + _dot(od_ref[0, h], wd_ref[h])
    mixed = (gm_ref[...].astype(F32) * ym + gs_ref[...].astype(F32) * yn + gd_ref[...].astype(F32) * yd)
    o_ref[...] += _dot(mixed.astype(BF16), wo_ref[...])

    @pl.when(j == n_j - 1)
    def _():
        o_ref[...] = x_ref[...] + o_ref[...]


def _gate_expand_matrix():
    nsa_out = NSA_HEADS * NSA_DV
    e = np.zeros((LANES, 3 * nsa_out), np.float32)
    for h in range(NSA_HEADS):
        for c in range(3):
            e[h * 3 + c, c * nsa_out + h * NSA_DV:c * nsa_out + (h + 1) * NSA_DV] = 1.0
    return e


def merge(x2d, o_mla, o_c, o_s, o_w, g_nsa, o_diff, g_merge, w_br_mla, w_br_nsa, w_br_diff, w_out, seq):
    m, d = x2d.shape
    tm = _tile(seq, 512)
    ns = seq // tm
    tn = _tile(d, 512)
    n_j = d // tn
    nsa_out = NSA_HEADS * NSA_DV
    mla_out = o_mla.shape[-1]
    ex = jnp.asarray(_gate_expand_matrix(), BF16)
    row = lambda w: pl.BlockSpec((tm, w), lambda i, j: (i, 0))
    return pl.pallas_call(
        functools.partial(_merge_kernel, n_j=n_j),
        grid=(m // tm, n_j),
        in_specs=[row(d), row(mla_out), row(nsa_out), row(nsa_out), row(nsa_out), row(LANES),
                  pl.BlockSpec((1, DIFF_HEADS, tm, DIFF_VD), lambda i, j: (i // ns, 0, i % ns, 0)),
                  pl.BlockSpec((tm, tn), lambda i, j: (i, j)),
                  pl.BlockSpec((tm, tn), lambda i, j: (i, n_j + j)),
                  pl.BlockSpec((tm, tn), lambda i, j: (i, 2 * n_j + j)),
                  pl.BlockSpec((mla_out, tn), lambda i, j: (0, j)),
                  pl.BlockSpec((nsa_out, tn), lambda i, j: (0, j)),
                  pl.BlockSpec((DIFF_HEADS, DIFF_VD, tn), lambda i, j: (0, 0, j)),
                  pl.BlockSpec((tn, d), lambda i, j: (j, 0)),
                  pl.BlockSpec((LANES, 3 * nsa_out), lambda i, j: (0, 0))],
        out_specs=pl.BlockSpec((tm, d), lambda i, j: (i, 0)),
        out_shape=jax.ShapeDtypeStruct((m, d), F32),
        scratch_shapes=[pltpu.VMEM((tm, nsa_out), BF16)],
        compiler_params=_cparams(("parallel", "arbitrary")),
        name="merge",
    )(x2d, o_mla, o_c, o_s, o_w, g_nsa, o_diff, g_merge, g_merge, g_merge,
      w_br_mla, w_br_nsa, w_br_diff, w_out, ex)


def _rope_tables(dim, seq, lead=0):
    inv = ROPE_THETA ** (-jnp.arange(0, dim, 2, dtype=F32) / dim)
    ang = jnp.arange(seq, dtype=F32)[:, None] * inv[None, :]
    cos, sin = jnp.cos(ang), jnp.sin(ang)
    cos = jnp.concatenate([jnp.ones((seq, lead), F32), cos, cos], axis=-1)
    sin = jnp.concatenate([jnp.zeros((seq, lead), F32), sin, sin], axis=-1)
    return cos, sin


def _heads(w, g, dh):
    return w.reshape(w.shape[0], g, dh).transpose(1, 0, 2)


def _rot_cols(w, lead=0):
    half = (w.shape[-1] - lead) // 2
    x1 = w[..., lead:lead + half]
    x2 = w[..., lead + half:]
    return jnp.concatenate([jnp.zeros_like(w[..., :lead]), -x2, x1], axis=-1)


def _pad_halves(w, axis=-1):
    w = jnp.moveaxis(w, axis, -1)
    half = w.shape[-1] // 2
    z = jnp.zeros(w.shape[:-1] + ((-half) % LANES,), w.dtype)
    out = jnp.concatenate([w[..., :half], z, w[..., half:], z], axis=-1)
    return jnp.moveaxis(out, -1, axis)


def _swap_tables(dim, seq, copies=1):
    inv = ROPE_THETA ** (-jnp.arange(0, dim, 2, dtype=F32) / dim)
    ang = jnp.arange(seq, dtype=F32)[:, None] * inv[None, :]
    cos = jnp.tile(jnp.cos(ang), (1, copies))
    sin = jnp.tile(jnp.sin(ang), (1, copies))
    return _pad_halves(jnp.concatenate([cos, cos], axis=-1)), _pad_halves(jnp.concatenate([-sin, sin], axis=-1))


def _pair_heads(w):
    k = w.shape[0]
    half = DIFF_HD // 2
    w = w.reshape(k, DIFF_HEADS, 2, 2, half)
    w = w.transpose(1, 0, 3, 2, 4).reshape(DIFF_HEADS, k, 2 * DIFF_HD)
    return _pad_halves(w)


def _pair_masks():
    half = DIFF_HD // 2
    m = np.zeros((2, 2, 2, half), np.float32)
    for c in range(2):
        m[c, :, c, :] = 1.0
    return _pad_halves(jnp.asarray(m.reshape(2, 2 * DIFF_HD)))


def _col_offsets():
    sizes = (MLA_Q_LORA, MLA_KV_LORA, MLA_ROPE,
             NSA_HEADS * NSA_DK, NSA_DK, NSA_DV, NSA_DK, NSA_DV, NSA_DK, NSA_DV, NSA_HEADS * 3,
             DIFF_HEADS * 2 * DIFF_HD, DIFF_HEADS * 2 * DIFF_HD, DIFF_HEADS * DIFF_VD)
    names = ("c_q", "c_kv", "k_rope", "nsa_q", "nsa_kc", "nsa_vc", "nsa_ks", "nsa_vs", "nsa_kw", "nsa_vw",
             "nsa_g", "d_q", "d_k", "d_v")
    offs = {}
    o = 0
    for nme, sz in zip(names, sizes):
        offs[nme] = (o, o + sz)
        o += sz
    offs["merge"] = (o, None)
    return offs


def _mixers(n2d, batch, seq, layer, w_in, p):
    offs = _col_offsets()
    col = lambda name: w_in[:, offs[name][0]:offs[name][1]]
    bf = lambda a: a.astype(BF16)

    n_q, n_kv = mla_latent(n2d, bf(jnp.concatenate([col("c_q"), col("c_kv")], axis=1)),
                           p["mla_q_norm"], p["mla_kv_norm"])
    cos_m, sin_m = _rope_tables(MLA_ROPE, seq, lead=MLA_NOPE)
    w_kr = jnp.concatenate([jnp.zeros((w_in.shape[0], MLA_NOPE), F32), col("k_rope")], axis=1)[None]
    kpe = proj_heads(n2d, bf(w_kr), batch, seq, rope=("weights", bf(_rot_cols(w_kr, MLA_NOPE)), cos_m, sin_m))
    w_uq = _heads(p["mla_w_uq"], MLA_HEADS, MLA_QK)
    q_mla = proj_heads(n_q, bf(w_uq), batch, seq, rope=("weights", bf(_rot_cols(w_uq, MLA_NOPE)), cos_m, sin_m),
                       scale=MLA_QK ** -0.5)
    w_ukv = _heads(p["mla_w_ukv"], MLA_HEADS, MLA_NOPE + MLA_V)
    w_uk = jnp.concatenate([w_ukv[..., :MLA_NOPE], jnp.zeros(w_ukv.shape[:2] + (MLA_ROPE,), F32)], axis=-1)
    k_mla = proj_heads(n_kv, bf(w_uk), batch, seq, add=kpe)
    vt_mla = proj_heads_t(n_kv, bf(w_ukv[..., MLA_NOPE:]), batch, seq)
    o_mla = flash_causal(q_mla, k_mla, vt_mla)

    cos_d, sin_d = _swap_tables(DIFF_HD, seq, copies=2)
    q_d = proj_heads(n2d, bf(_pair_heads(col("d_q"))), batch, seq, rope=("swap", cos_d, sin_d),
                     scale=DIFF_HD ** -0.5, masks=_pair_masks())
    k_d = proj_heads(n2d, bf(_pair_heads(col("d_k"))), batch, seq, rope=("swap", cos_d, sin_d))
    vt_d = proj_heads_t(n2d, bf(_heads(col("d_v"), DIFF_HEADS, DIFF_VD)), batch, seq)
    lambda_init = 0.8 - 0.6 * math.exp(-0.3 * layer)
    o_diff = diff_attention(q_d, k_d, vt_d, p["diff_lam_q1"], p["diff_lam_k1"], p["diff_lam_q2"],
                            p["diff_lam_k2"], p["diff_subln"], lambda_init)

    cos_n, sin_n = _swap_tables(NSA_DK, seq)
    w_nq = _pad_halves(_heads(col("nsa_q"), NSA_HEADS, NSA_DK))
    q_n = proj_heads(n2d, bf(w_nq), batch, seq, rope=("swap", cos_n, sin_n), scale=NSA_DK ** -0.5)
    w_nk = _pad_halves(jnp.stack([col("nsa_kc"), col("nsa_ks"), col("nsa_kw")], axis=0))
    k_n = proj_heads(n2d, bf(w_nk), batch, seq, rope=("swap", cos_n, sin_n))
    dk_pad = w_nk.shape[-1]
    v_cmp = proj_plain(n2d, bf(col("nsa_vc")), BF16)
    vt_n = proj_heads_t(n2d, bf(jnp.stack([col("nsa_vs"), col("nsa_vw")], axis=0)), batch, seq)
    w_g = jnp.concatenate([col("nsa_g"), jnp.zeros((w_in.shape[0], LANES - NSA_HEADS * 3), F32)], axis=1)
    g_nsa = proj_plain(n2d, bf(w_g), F32, sigmoid=True)

    ng = seq // NSA_CMP_STRIDE
    w1k = _pad_halves(p["nsa_cmp_k_w1"].reshape(NSA_CMP_LEN, NSA_DK, NSA_CMP_HIDDEN), axis=1)
    kc = nsa_compress(k_n.reshape(batch, 3, ng, NSA_CMP_STRIDE * dk_pad), 0,
                      _pad_halves(p["nsa_cmp_k_pos"]), w1k.reshape(NSA_CMP_LEN * dk_pad, NSA_CMP_HIDDEN),
                      _pad_halves(p["nsa_cmp_k_w2"]))
    vc_tok = v_cmp.reshape(batch, 1, ng, NSA_CMP_STRIDE * NSA_DV)
    vct = nsa_compress(vc_tok, 0, p["nsa_cmp_v_pos"], p["nsa_cmp_v_w1"], p["nsa_cmp_v_w2"], transposed=True)
    o_c, sel = nsa_cmp_topk(q_n, kc, vct)
    o_s = nsa_selected(q_n, k_n, 1, vt_n, sel)
    o_w = nsa_window(q_n, k_n, 2, vt_n, 1)

    g_merge = proj_plain(n2d, bf(w_in[:, offs["merge"][0]:]), BF16, sigmoid=True)
    m = batch * seq
    return (o_mla.reshape(m, -1), o_c.reshape(m, -1), o_s.reshape(m, -1), o_w.reshape(m, -1), g_nsa,
            o_diff, g_merge)


def kernel(x, ffn1_norm, ffn1_w_in, ffn1_w_out, mix_norm, w_in, mla_q_norm, mla_kv_norm, mla_w_uq, mla_w_ukv, nsa_cmp_k_pos, nsa_cmp_k_w1, nsa_cmp_k_w2, nsa_cmp_v_pos, nsa_cmp_v_w1, nsa_cmp_v_w2, diff_lam_q1, diff_lam_k1, diff_lam_q2, diff_lam_k2, diff_subln, w_br_mla, w_br_nsa, w_br_diff, w_out, ffn2_norm, ffn2_w_in, ffn2_w_out, final_norm):
    batch, seq, d = x.shape
    depth = w_in.shape[0]
    x2d = x.reshape(batch * seq, d)
    bf = lambda a: a.astype(BF16)
    for l in range(depth):
        p = {"mla_q_norm": mla_q_norm[l], "mla_kv_norm": mla_kv_norm[l], "mla_w_uq": mla_w_uq[l],
             "mla_w_ukv": mla_w_ukv[l], "nsa_cmp_k_pos": nsa_cmp_k_pos[l], "nsa_cmp_k_w1": nsa_cmp_k_w1[l],
             "nsa_cmp_k_w2": nsa_cmp_k_w2[l], "nsa_cmp_v_pos": nsa_cmp_v_pos[l], "nsa_cmp_v_w1": nsa_cmp_v_w1[l],
             "nsa_cmp_v_w2": nsa_cmp_v_w2[l], "diff_lam_q1": diff_lam_q1[l], "diff_lam_k1": diff_lam_k1[l],
             "diff_lam_q2": diff_lam_q2[l], "diff_lam_k2": diff_lam_k2[l], "diff_subln": diff_subln[l]}
        x2d, n2d = ffn(x2d, ffn1_norm[l], bf(ffn1_w_in[l]), bf(ffn1_w_out[l]), post_norm=(mix_norm[l], BF16))
        o_mla, o_c, o_s, o_w, g_nsa, o_diff, g_merge = _mixers(n2d, batch, seq, l, w_in[l], p)
        x2d = merge(x2d, o_mla, o_c, o_s, o_w, g_nsa, o_diff, g_merge, bf(w_br_mla[l]), bf(w_br_nsa[l]),
                    bf(w_br_diff[l]).reshape(DIFF_HEADS, DIFF_VD, d), bf(w_out[l]), seq)
        if l + 1 < depth:
            x2d = ffn(x2d, ffn2_norm[l], bf(ffn2_w_in[l]), bf(ffn2_w_out[l]))
        else:
            _, out = ffn(x2d, ffn2_norm[l], bf(ffn2_w_in[l]), bf(ffn2_w_out[l]), post_norm=(final_norm, F32))
    return out.reshape(batch, seq, d)
```

```python
import functools
import math

import numpy as np
import jax
import jax.numpy as jnp
from jax import lax
from jax.experimental import pallas as pl
from jax.experimental.pallas import tpu as pltpu

F32 = jnp.float32
BF16 = jnp.bfloat16

NORM_EPS = 1e-6
ROPE_THETA = 10000.0
NEG_INF = -1e30
REMOVED = -3e38
N_BRANCH = 3

MLA_HEADS = 6
MLA_Q_LORA = 768
MLA_KV_LORA = 512
MLA_NOPE = 128
MLA_ROPE = 64
MLA_V = 128
MLA_QK = MLA_NOPE + MLA_ROPE

NSA_HEADS = 4
NSA_DK = 192
NSA_DV = 128
NSA_CMP_LEN = 32
NSA_CMP_STRIDE = 16
NSA_CMP_HIDDEN = 256
NSA_SEL_LEN = 64
NSA_TOPK = 16
NSA_WINDOW = 512
NSA_FORCE_SCORE = 1e6

DIFF_HEADS = 4
DIFF_HD = 96
DIFF_VD = 2 * DIFF_HD

LANES = 128
ONES_PAD = 16
SCORE_DTYPE = BF16

PROJ_ROWS = 1024
FUSED_ROWS = 512
COL_TILE = 512
MERGE_COLS = 1024
ATTN_TILE = 512
VMEM_LIMIT_MB = 56


def _cparams(dims, vmem_mb=VMEM_LIMIT_MB):
    return pltpu.CompilerParams(dimension_semantics=dims, vmem_limit_bytes=vmem_mb * 2**20)


def _sigmoid(x):
    return 1.0 / (1.0 + jnp.exp(-x))


def _dot(a, b):
    return jnp.dot(a, b, preferred_element_type=F32)


def _dot_nt(a, b):
    return lax.dot_general(a, b, (((1,), (1,)), ((), ())), preferred_element_type=F32)


def _tile(n, pref):
    t = min(n, pref)
    assert n % t == 0, (n, t)
    return t


def _mla_latent_kernel(n_ref, w_ref, wq_ref, wkv_ref, nq_ref, nkv_ref):
    lat = _dot(n_ref[...], w_ref[...])
    cq = lat[:, :MLA_Q_LORA]
    ckv = lat[:, MLA_Q_LORA:]
    nq = cq * lax.rsqrt(jnp.mean(cq * cq, axis=-1, keepdims=True) + NORM_EPS)
    nkv = ckv * lax.rsqrt(jnp.mean(ckv * ckv, axis=-1, keepdims=True) + NORM_EPS)
    nq_ref[...] = (nq * wq_ref[...]).astype(BF16)
    nkv_ref[...] = (nkv * wkv_ref[...]).astype(BF16)


def mla_latent(n2d, w_lat, wq, wkv):
    m, k = n2d.shape
    d = w_lat.shape[1]
    tm = _tile(m, FUSED_ROWS)
    return pl.pallas_call(
        _mla_latent_kernel,
        grid=(m // tm,),
        in_specs=[pl.BlockSpec((tm, k), lambda i: (i, 0)),
                  pl.BlockSpec((k, d), lambda i: (0, 0)),
                  pl.BlockSpec((1, MLA_Q_LORA), lambda i: (0, 0)),
                  pl.BlockSpec((1, MLA_KV_LORA), lambda i: (0, 0))],
        out_specs=[pl.BlockSpec((tm, MLA_Q_LORA), lambda i: (i, 0)),
                   pl.BlockSpec((tm, MLA_KV_LORA), lambda i: (i, 0))],
        out_shape=[jax.ShapeDtypeStruct((m, MLA_Q_LORA), BF16), jax.ShapeDtypeStruct((m, MLA_KV_LORA), BF16)],
        compiler_params=_cparams(("parallel",)),
        name="mla_latent",
    )(n2d, w_lat, wq.reshape(1, -1).astype(F32), wkv.reshape(1, -1).astype(F32))


def _ffn_kernel(x_ref, nw_ref, wg_ref, wu_ref, wo_ref, *rest, n_f, post_norm):
    if post_norm:
        pw_ref, o_ref, p_ref, n_scr = rest
    else:
        o_ref, n_scr = rest
    f = pl.program_id(1)

    @pl.when(f == 0)
    def _():
        x = x_ref[...]
        y = x * lax.rsqrt(jnp.mean(x * x, axis=-1, keepdims=True) + NORM_EPS)
        n_scr[...] = (y * nw_ref[...]).astype(BF16)
        o_ref[...] = jnp.zeros_like(o_ref)

    n = n_scr[...]
    g = _dot(n, wg_ref[...])
    u = _dot(n, wu_ref[...])
    h = (g * _sigmoid(g) * u).astype(BF16)
    o_ref[...] += _dot(h, wo_ref[...])

    @pl.when(f == n_f - 1)
    def _():
        y = x_ref[...] + 0.5 * o_ref[...]
        o_ref[...] = y
        if post_norm:
            yn = y * lax.rsqrt(jnp.mean(y * y, axis=-1, keepdims=True) + NORM_EPS)
            p_ref[...] = (yn * pw_ref[...]).astype(p_ref.dtype)


def ffn(x2d, norm_w, w_in, w_out, post_norm=None):
    m, d = x2d.shape
    f_dim = w_out.shape[0]
    tm = _tile(m, FUSED_ROWS)
    tf = _tile(f_dim, COL_TILE)
    n_f = f_dim // tf
    row = pl.BlockSpec((tm, d), lambda i, f: (i, 0))
    vec = pl.BlockSpec((1, d), lambda i, f: (0, 0))
    in_specs = [row, vec,
                pl.BlockSpec((d, tf), lambda i, f: (0, f)),
                pl.BlockSpec((d, tf), lambda i, f: (0, f + n_f)),
                pl.BlockSpec((tf, d), lambda i, f: (f, 0))]
    args = [x2d, norm_w.reshape(1, d).astype(F32), w_in, w_in, w_out]
    out_specs, out_shape = row, jax.ShapeDtypeStruct((m, d), F32)
    if post_norm is not None:
        in_specs.append(vec)
        args.append(post_norm[0].reshape(1, d).astype(F32))
        out_specs, out_shape = [row, row], [out_shape, jax.ShapeDtypeStruct((m, d), post_norm[1])]
    return pl.pallas_call(
        functools.partial(_ffn_kernel, n_f=n_f, post_norm=post_norm is not None),
        grid=(m // tm, n_f),
        in_specs=in_specs,
        out_specs=out_specs,
        out_shape=out_shape,
        scratch_shapes=[pltpu.VMEM((tm, d), BF16)],
        compiler_params=_cparams(("parallel", "arbitrary")),
        name="ffn",
    )(*args)


def _proj_plain_kernel(n_ref, w_ref, o_ref, *, sigmoid):
    y = _dot(n_ref[...], w_ref[...])
    if sigmoid:
        y = _sigmoid(y)
    o_ref[...] = y.astype(o_ref.dtype)


def proj_plain(n2d, w, out_dtype, sigmoid=False):
    m, k = n2d.shape
    n_out = w.shape[1]
    tm = _tile(m, PROJ_ROWS)
    tn = next(t for t in (COL_TILE, COL_TILE // 2, LANES) if n_out % t == 0)
    return pl.pallas_call(
        functools.partial(_proj_plain_kernel, sigmoid=sigmoid),
        grid=(m // tm, n_out // tn),
        in_specs=[pl.BlockSpec((tm, k), lambda i, j: (i, 0)),
                  pl.BlockSpec((k, tn), lambda i, j: (0, j))],
        out_specs=pl.BlockSpec((tm, tn), lambda i, j: (i, j)),
        out_shape=jax.ShapeDtypeStruct((m, n_out), out_dtype),
        compiler_params=_cparams(("parallel", "arbitrary")),
        name="proj_plain",
    )(n2d, w)


def _proj_heads_kernel(*refs, rope, has_add, has_masks, scale):
    n_ref, w_ref = refs[0], refs[1]
    o_ref = refs[-1]
    n = n_ref[...]
    n_copy = o_ref.shape[1] // w_ref.shape[0]
    for h in range(w_ref.shape[0]):
        pos = 2
        y = _dot(n, w_ref[h])
        if rope == "weights":
            wr_ref, cos_ref, sin_ref = refs[pos:pos + 3]
            pos += 3
            y = y * cos_ref[...] + _dot(n, wr_ref[h]) * sin_ref[...]
        elif rope == "swap":
            cos_ref, sin_ref = refs[pos:pos + 2]
            pos += 2
            half = y.shape[1] // 2
            y = y * cos_ref[...] + jnp.concatenate([y[:, half:], y[:, :half]], axis=1) * sin_ref[...]
        if has_add:
            y = y + refs[pos][0, 0].astype(F32)
            pos += 1
        if scale != 1.0:
            y = y * scale
        if has_masks:
            mask_ref = refs[pos]
            for c in range(n_copy):
                o_ref[0, h * n_copy + c] = (y * mask_ref[c:c + 1, :]).astype(o_ref.dtype)
        else:
            o_ref[0, h] = y.astype(o_ref.dtype)


def proj_heads(n2d, w, batch, seq, rope=None, add=None, scale=1.0, masks=None):
    m, k = n2d.shape
    g, _, dh = w.shape
    tm = _tile(seq, PROJ_ROWS)
    ns = seq // tm
    weights = pl.BlockSpec((g, k, dh), lambda b, i: (0, 0, 0))
    in_specs = [pl.BlockSpec((tm, k), lambda b, i: (b * ns + i, 0)), weights]
    args = [n2d, w]
    table = pl.BlockSpec((tm, dh), lambda b, i: (i, 0))
    if rope is not None and rope[0] == "weights":
        in_specs += [weights, table, table]
        args += list(rope[1:])
    elif rope is not None:
        assert rope[0] == "swap" and dh % (2 * LANES) == 0
        in_specs += [table, table]
        args += list(rope[1:])
    if add is not None:
        in_specs.append(pl.BlockSpec((1, 1, tm, dh), lambda b, i: (b, 0, i, 0)))
        args.append(add)
    n_copy = 1
    if masks is not None:
        n_copy = masks.shape[0]
        in_specs.append(pl.BlockSpec((n_copy, dh), lambda b, i: (0, 0)))
        args.append(masks)
    return pl.pallas_call(
        functools.partial(_proj_heads_kernel, rope=None if rope is None else rope[0], has_add=add is not None,
                          has_masks=masks is not None, scale=scale),
        grid=(batch, ns),
        in_specs=in_specs,
        out_specs=pl.BlockSpec((1, g * n_copy, tm, dh), lambda b, i: (b, 0, i, 0)),
        out_shape=jax.ShapeDtypeStruct((batch, g * n_copy, seq, dh), BF16),
        compiler_params=_cparams(("parallel", "parallel")),
        name="proj_heads",
    )(*args)


def _proj_heads_t_kernel(n_ref, wt_ref, o_ref):
    dh = wt_ref.shape[1]
    pad = o_ref.shape[2] - dh
    row = lax.broadcasted_iota(jnp.int32, (pad, o_ref.shape[3]), 0)
    ones_rows = jnp.where(row == 0, 1.0, 0.0).astype(o_ref.dtype)
    n = n_ref[...]
    for h in range(wt_ref.shape[0]):
        o_ref[0, h, :dh, :] = _dot_nt(wt_ref[h], n).astype(o_ref.dtype)
        o_ref[0, h, dh:, :] = ones_rows


def proj_heads_t(n2d, w, batch, seq):
    m, k = n2d.shape
    g, _, dh = w.shape
    tm = _tile(seq, PROJ_ROWS)
    ns = seq // tm
    return pl.pallas_call(
        _proj_heads_t_kernel,
        grid=(batch, ns),
        in_specs=[pl.BlockSpec((tm, k), lambda b, i: (b * ns + i, 0)),
                  pl.BlockSpec((g, dh, k), lambda b, i: (0, 0, 0))],
        out_specs=pl.BlockSpec((1, g, dh + ONES_PAD, tm), lambda b, i: (b, 0, 0, i)),
        out_shape=jax.ShapeDtypeStruct((batch, g, dh + ONES_PAD, seq), BF16),
        compiler_params=_cparams(("parallel", "parallel")),
        name="proj_heads_t",
    )(n2d, jnp.swapaxes(w, 1, 2))


def _softmax_step_t(st, vt, m_scr, acc_scr, exp_dtype):
    m_prev = m_scr[...]
    m_new = jnp.maximum(m_prev, jnp.max(st, axis=0, keepdims=True).astype(F32))
    alpha = jnp.exp(m_prev - m_new)
    pt = jnp.exp((st - m_new.astype(st.dtype)).astype(exp_dtype)).astype(BF16)
    acc_scr[...] = alpha * acc_scr[...] + _dot(vt, pt)
    m_scr[...] = m_new


def _diag_visibility(d, c, tq, tk):
    if d * tk >= (c + 1) * tq:
        return "none"
    if (d + 1) * tk - 1 <= c * tq:
        return "full"
    return "partial"


def _causal_where(st, d, c):
    tk, tq = st.shape
    key = lax.broadcasted_iota(jnp.int32, (tk, tq), 0) + d * tk
    qry = lax.broadcasted_iota(jnp.int32, (tk, tq), 1) + c * tq
    return jnp.where(key <= qry, st, NEG_INF)


def _flash_sweep(n, r, sub_of, tq, tk, scores, vt_tile, m_scr, acc_scr, st_scr, exp_dtype=F32):
    n_chain = m_scr.shape[0]
    for c in range(n_chain):
        m_scr[c] = jnp.full(m_scr.shape[1:], NEG_INF, F32)
        acc_scr[c] = jnp.zeros(acc_scr.shape[1:], F32)
    n_slot = st_scr.shape[0]
    assert r % n_slot == 0
    everyone = (True,) * n_chain
    for c, st in enumerate(scores(0, everyone)):
        st_scr[0, c] = st.astype(st_scr.dtype)

    def body(jj, carry):
        for slot in range(n_slot):
            j = jj * n_slot + slot
            nxt = scores(j + 1, everyone)
            vt = vt_tile(j)
            for c in range(n_chain):
                st = st_scr[slot, c]
                st_scr[(slot + 1) % n_slot, c] = nxt[c].astype(st_scr.dtype)
                _softmax_step_t(st, vt, m_scr.at[c], acc_scr.at[c], exp_dtype)
        return carry

    lax.fori_loop(0, n // n_slot, body, 0)
    for d in range(r):
        vis = [_diag_visibility(d, sub_of(c), tq, tk) for c in range(n_chain)]
        need = tuple(d + 1 < r and _diag_visibility(d + 1, sub_of(c), tq, tk) != "none" for c in range(n_chain))
        nxt = scores(n + d + 1, need) if any(need) else None
        vt = vt_tile(n + d)
        for c in range(n_chain):
            if vis[c] != "none":
                st = st_scr[d % n_slot, c]
                if vis[c] == "partial":
                    st = _causal_where(st.astype(F32), d, sub_of(c)).astype(st_scr.dtype)
            if need[c]:
                st_scr[(d + 1) % n_slot, c] = nxt[c].astype(st_scr.dtype)
            if vis[c] != "none":
                _softmax_step_t(st, vt, m_scr.at[c], acc_scr.at[c], exp_dtype)


def _flash_kernel(q_ref, k_ref, vt_ref, o_ref, m_scr, acc_scr, st_scr, *, n_sub, tq, tk):
    i = pl.program_id(2)
    dv = o_ref.shape[2]
    r = n_sub * tq // tk

    def scores(j, need):
        k = k_ref[0, 0, pl.ds(pl.multiple_of(j * tk, tk), tk), :]
        return [_dot_nt(k, q_ref[0, 0, c * tq:(c + 1) * tq, :]) if need[c] else None for c in range(n_sub)]

    def vt_tile(j):
        return vt_ref[0, 0, :, pl.ds(pl.multiple_of(j * tk, tk), tk)]

    _flash_sweep(i * r, r, lambda c: c, tq, tk, scores, vt_tile, m_scr, acc_scr, st_scr, exp_dtype=BF16)
    for c in range(n_sub):
        o_t = acc_scr[c, :dv, :] / acc_scr[c, dv:dv + 1, :]
        o_ref[0, c * tq:(c + 1) * tq, :] = o_t.T.astype(o_ref.dtype)


def _score_slots(r):
    return 2 if r % 2 == 0 else 1


def _flash_tiles(s, max_sub):
    tq = _tile(s, ATTN_TILE)
    n_sub = next(n for n in (4, 2, 1) if n <= max_sub and s % (n * tq) == 0)
    return n_sub, tq, tq


def flash_causal(q, k, vt):
    b, h, s, dk = q.shape
    dve = vt.shape[2]
    dv = dve - ONES_PAD
    n_sub, tq, tk = _flash_tiles(s, max_sub=4)
    t = n_sub * tq
    return pl.pallas_call(
        functools.partial(_flash_kernel, n_sub=n_sub, tq=tq, tk=tk),
        grid=(b, h, s // t),
        in_specs=[pl.BlockSpec((1, 1, t, dk), lambda b_, h_, i: (b_, h_, i, 0)),
                  pl.BlockSpec((1, 1, s, dk), lambda b_, h_, i: (b_, h_, 0, 0)),
                  pl.BlockSpec((1, 1, dve, s), lambda b_, h_, i: (b_, h_, 0, 0))],
        out_specs=pl.BlockSpec((1, t, dv), lambda b_, h_, i: (b_, i, h_)),
        out_shape=jax.ShapeDtypeStruct((b, s, h * dv), BF16),
        scratch_shapes=[pltpu.VMEM((n_sub, 1, tq), F32), pltpu.VMEM((n_sub, dve, tq), F32),
                        pltpu.VMEM((_score_slots(n_sub * tq // tk), n_sub, tk, tq), SCORE_DTYPE)],
        compiler_params=_cparams(("parallel", "parallel", "arbitrary")),
        name="mla_flash",
    )(q, k, vt)


def _diff_kernel(q_ref, k_ref, vt_ref, lq1_ref, lk1_ref, lq2_ref, lk2_ref, sub_ref, o_ref,
                 m_scr, acc_scr, st_scr, *, n_sub, tq, tk, lambda_init):
    i = pl.program_id(2)
    vd = o_ref.shape[3]
    r = n_sub * tq // tk

    def scores(j, need):
        k = k_ref[0, 0, pl.ds(pl.multiple_of(j * tk, tk), tk), :]
        return [_dot_nt(k, q_ref[0, ch // n_sub, (ch % n_sub) * tq:(ch % n_sub + 1) * tq, :]) if need[ch] else None
                for ch in range(2 * n_sub)]

    def vt_tile(j):
        return vt_ref[0, 0, :, pl.ds(pl.multiple_of(j * tk, tk), tk)]

    _flash_sweep(i * r, r, lambda ch: ch % n_sub, tq, tk, scores, vt_tile, m_scr, acc_scr, st_scr, exp_dtype=BF16)

    lam = (jnp.exp(jnp.sum(lq1_ref[...] * lk1_ref[...], axis=-1, keepdims=True))
           - jnp.exp(jnp.sum(lq2_ref[...] * lk2_ref[...], axis=-1, keepdims=True)) + lambda_init)
    lane_pad = (-vd) % LANES
    for c in range(n_sub):
        o_t = (acc_scr[c, :vd, :] / acc_scr[c, vd:vd + 1, :]
               - lam * (acc_scr[n_sub + c, :vd, :] / acc_scr[n_sub + c, vd:vd + 1, :]))
        o_t = o_t * lax.rsqrt(jnp.mean(o_t * o_t, axis=0, keepdims=True) + NORM_EPS)
        o_t = jnp.concatenate([o_t, jnp.zeros((lane_pad, tq), F32)], axis=0)
        o_ref[0, 0, c * tq:(c + 1) * tq, :] = (
            o_t.T[:, :vd] * sub_ref[...] * (1.0 - lambda_init)).astype(o_ref.dtype)


def diff_attention(q, k, vt, lq1, lk1, lq2, lk2, subln, lambda_init):
    b, h2, s, dk = q.shape
    h = h2 // 2
    hd = DIFF_HD
    vde = vt.shape[2]
    vd = vde - ONES_PAD
    n_sub, tq, tk = _flash_tiles(s, max_sub=4)
    t = n_sub * tq
    vec = lambda a: a.reshape(1, -1).astype(F32)
    small = lambda n: pl.BlockSpec((1, n), lambda b_, h_, i: (0, 0))
    return pl.pallas_call(
        functools.partial(_diff_kernel, n_sub=n_sub, tq=tq, tk=tk, lambda_init=lambda_init),
        grid=(b, h, s // t),
        in_specs=[pl.BlockSpec((1, 2, t, dk), lambda b_, h_, i: (b_, h_, i, 0)),
                  pl.BlockSpec((1, 1, s, dk), lambda b_, h_, i: (b_, h_, 0, 0)),
                  pl.BlockSpec((1, 1, vde, s), lambda b_, h_, i: (b_, h_, 0, 0)),
                  small(hd), small(hd), small(hd), small(hd), small(vd)],
        out_specs=pl.BlockSpec((1, 1, t, vd), lambda b_, h_, i: (b_, h_, i, 0)),
        out_shape=jax.ShapeDtypeStruct((b, h, s, vd), BF16),
        scratch_shapes=[pltpu.VMEM((2 * n_sub, 1, tq), F32), pltpu.VMEM((2 * n_sub, vde, tq), F32),
                        pltpu.VMEM((_score_slots(n_sub * tq // tk), 2 * n_sub, tk, tq), SCORE_DTYPE)],
        compiler_params=_cparams(("parallel", "parallel", "arbitrary")),
        name="diff_flash",
    )(q, k, vt, vec(lq1), vec(lk1), vec(lq2), vec(lk2), vec(subln))


def _compress_kernel(a_ref, pa_ref, pb_ref, w1a_ref, w1b_ref, w2_ref, o_ref, *, transposed):
    ng = a_ref.shape[2]
    rows = min(ng, 256)
    us, vs = [], []
    for r0 in range(0, ng, rows):
        a = a_ref[0, 0, r0:r0 + rows, :].astype(F32)
        us.append(_dot((a + pa_ref[...]).astype(BF16), w1a_ref[...]))
        vs.append(_dot((a + pb_ref[...]).astype(BF16), w1b_ref[...]))
    u = jnp.concatenate(us, axis=0)
    v = jnp.concatenate(vs, axis=0)
    hdn = u + pltpu.roll(v, ng - 1, 0)
    hdn = (hdn * _sigmoid(hdn)).astype(BF16)
    if transposed:
        d_out = w2_ref.shape[0]
        o_ref[0, :d_out, :] = _dot_nt(w2_ref[...], hdn).astype(o_ref.dtype)
        row = lax.broadcasted_iota(jnp.int32, (o_ref.shape[1] - d_out, ng), 0)
        o_ref[0, d_out:, :] = jnp.where(row == 0, 1.0, 0.0).astype(o_ref.dtype)
    else:
        o_ref[0] = _dot(hdn, w2_ref[...]).astype(o_ref.dtype)


def nsa_compress(tok, g_idx, pos, w1, w2, transposed=False):
    assert NSA_CMP_LEN == 2 * NSA_CMP_STRIDE
    b, _, ng, wd = tok.shape
    d = wd // NSA_CMP_STRIDE
    d_out = w2.shape[1]
    pos_flat = pos.astype(F32).reshape(1, NSA_CMP_LEN * d)
    w1 = w1.astype(BF16)
    w2 = w2.astype(BF16).T if transposed else w2.astype(BF16)
    out_block = (1, d_out + ONES_PAD, ng) if transposed else (1, ng, d_out)
    full = lambda shape: pl.BlockSpec(shape, lambda b_: (0,) * len(shape))
    return pl.pallas_call(
        functools.partial(_compress_kernel, transposed=transposed),
        grid=(b,),
        in_specs=[pl.BlockSpec((1, 1, ng, wd), lambda b_: (b_, g_idx, 0, 0)),
                  full((1, wd)), full((1, wd)), full((wd, NSA_CMP_HIDDEN)), full((wd, NSA_CMP_HIDDEN)),
                  full(w2.shape)],
        out_specs=pl.BlockSpec(out_block, lambda b_: (b_, 0, 0)),
        out_shape=jax.ShapeDtypeStruct((b,) + out_block[1:], BF16),
        compiler_params=_cparams(("parallel",)),
        name="nsa_compress",
    )(tok, pos_flat[:, :wd], pos_flat[:, wd:], w1[:wd], w1[wd:], w2)


def _split3(x):
    hi = x.astype(BF16)
    r = x - hi.astype(F32)
    mid = r.astype(BF16)
    lo = (r - mid.astype(F32)).astype(BF16)
    return hi, mid, lo


def _cmp_topk_kernel(q_ref, kc_ref, vct_ref, wselt_ref, oc_ref, selt_ref, *, tq, k_top):
    i = pl.program_id(1)
    qs = i * tq
    nc = kc_ref.shape[1]
    n_sel = wselt_ref.shape[0]
    kc = kc_ref[0]
    vct = vct_ref[0]
    qpos_c = qs + lax.broadcasted_iota(jnp.int32, (nc, tq), 1)
    cend = lax.broadcasted_iota(jnp.int32, (nc, tq), 0) * NSA_CMP_STRIDE + (NSA_CMP_LEN - 1)
    valid_c = cend <= qpos_c
    any_valid = qs + lax.broadcasted_iota(jnp.int32, (1, tq), 1) >= NSA_CMP_LEN - 1
    imp_t = jnp.zeros((nc, tq), F32)
    for h in range(NSA_HEADS):
        st = jnp.where(valid_c, _dot_nt(kc, q_ref[0, h]), NEG_INF)
        e = jnp.exp(st - jnp.max(st, axis=0, keepdims=True))
        inv = jnp.where(any_valid, 1.0 / jnp.sum(e, axis=0, keepdims=True), 0.0)
        o_t = _dot(vct, e.astype(BF16))[:NSA_DV, :] * inv
        oc_ref[0, :, h * NSA_DV:(h + 1) * NSA_DV] = o_t.T
        imp_t = imp_t + e * inv

    wselt = wselt_ref[...]
    hi, mid, lo = _split3(imp_t)
    imp_sel = _dot(wselt, hi) + _dot(wselt, mid) + _dot(wselt, lo)

    blk = lax.broadcasted_iota(jnp.int32, (n_sel, tq), 0)
    qpos = qs + lax.broadcasted_iota(jnp.int32, (n_sel, tq), 1)
    cur = lax.shift_right_arithmetic(qpos, int(math.log2(NSA_SEL_LEN)))
    forced = (blk == 0) | (blk == cur) | (blk == cur - 1)
    valid_s = blk * NSA_SEL_LEN <= qpos
    score = jnp.where(valid_s, jnp.where(forced, NSA_FORCE_SCORE, imp_sel), NEG_INF)
    blk_f = blk.astype(F32)
    sel = jnp.zeros((n_sel, tq), F32)
    for _ in range(k_top):
        mx = jnp.max(score, axis=0, keepdims=True)
        first = jnp.min(jnp.where(score == mx, blk_f, float(n_sel)), axis=0, keepdims=True)
        hit = blk_f == first
        sel = jnp.where(hit, 1.0, sel)
        score = jnp.where(hit, REMOVED, score)
    selt_ref[0] = sel.astype(selt_ref.dtype)


def _sel_weight_matrix(n_pad, n_sel):
    r_c = NSA_CMP_LEN // NSA_CMP_STRIDE
    ratio = NSA_SEL_LEN // NSA_CMP_STRIDE
    overlap_w = [max(0, min(o * NSA_CMP_STRIDE + NSA_CMP_LEN, NSA_SEL_LEN) - max(o * NSA_CMP_STRIDE, 0))
                 / NSA_CMP_STRIDE for o in range(-(r_c - 1), ratio)]
    w = np.zeros((n_pad, n_sel), np.float32)
    for n in range(n_sel):
        for u, w_u in enumerate(overlap_w):
            c = ratio * n + u - (r_c - 1)
            if 0 <= c < n_pad:
                w[c, n] = w_u
    return w


def nsa_cmp_topk(q, kc, vct):
    b, h, s, dk = q.shape
    nc = kc.shape[1]
    dve = vct.shape[1]
    n_sel = s // NSA_SEL_LEN
    assert NSA_SEL_LEN & (NSA_SEL_LEN - 1) == 0
    tq = _tile(s, ATTN_TILE)
    wselt = jnp.asarray(_sel_weight_matrix(nc, n_sel).T, BF16)
    return pl.pallas_call(
        functools.partial(_cmp_topk_kernel, tq=tq, k_top=min(NSA_TOPK, n_sel)),
        grid=(b, s // tq),
        in_specs=[pl.BlockSpec((1, h, tq, dk), lambda b_, i: (b_, 0, i, 0)),
                  pl.BlockSpec((1, nc, dk), lambda b_, i: (b_, 0, 0)),
                  pl.BlockSpec((1, dve, nc), lambda b_, i: (b_, 0, 0)),
                  pl.BlockSpec((n_sel, nc), lambda b_, i: (0, 0))],
        out_specs=[pl.BlockSpec((1, tq, h * NSA_DV), lambda b_, i: (b_, i, 0)),
                   pl.BlockSpec((1, n_sel, tq), lambda b_, i: (b_, 0, i))],
        out_shape=[jax.ShapeDtypeStruct((b, s, h * NSA_DV), F32), jax.ShapeDtypeStruct((b, n_sel, s), F32)],
        compiler_params=_cparams(("parallel", "parallel")),
        name="nsa_cmp_topk",
    )(q, kc, vct, wselt)


def _nsa_sel_kernel(q_ref, k_ref, vt_ref, sel_ref, o_ref, m_scr, acc_scr, st_scr, *, n_sub, tq, tk):
    i = pl.program_id(1)
    bpt = tk // NSA_SEL_LEN
    r = n_sub * tq // tk

    def scores(j, need):
        k = k_ref[0, 0, pl.ds(pl.multiple_of(j * tk, tk), tk), :]
        out = []
        for c in range(n_sub):
            rows = slice(c * tq, (c + 1) * tq)
            if not any(need[c * NSA_HEADS:(c + 1) * NSA_HEADS]):
                out += [None] * NSA_HEADS
                continue
            flags = sel_ref[0, pl.ds(pl.multiple_of(j * bpt, bpt), bpt), rows]
            keep_t = jnp.broadcast_to(flags[:, None, :], (bpt, NSA_SEL_LEN, tq)).reshape(tk, tq) > 0.5
            out += [jnp.where(keep_t, _dot_nt(k, q_ref[0, h, rows, :]), NEG_INF) if need[c * NSA_HEADS + h] else None
                    for h in range(NSA_HEADS)]
        return out

    def vt_tile(j):
        return vt_ref[0, 0, :, pl.ds(pl.multiple_of(j * tk, tk), tk)]

    _flash_sweep(i * r, r, lambda ch: ch // NSA_HEADS, tq, tk, scores, vt_tile, m_scr, acc_scr, st_scr,
                 exp_dtype=BF16)
    for ch in range(n_sub * NSA_HEADS):
        c, h = divmod(ch, NSA_HEADS)
        o_t = acc_scr[ch, :NSA_DV, :] / acc_scr[ch, NSA_DV:NSA_DV + 1, :]
        o_ref[0, c * tq:(c + 1) * tq, h * NSA_DV:(h + 1) * NSA_DV] = o_t.T


def _nsa_sel_tiles(s):
    return _flash_tiles(s, max_sub=2)


def nsa_selected(q, k3, k_idx, vt, sel):
    b, h, s, dk = q.shape
    n_sel = sel.shape[1]
    dve = vt.shape[2]
    n_sub, tq, tk = _nsa_sel_tiles(s)
    t = n_sub * tq
    n_chain = n_sub * h
    assert tk % NSA_SEL_LEN == 0 and t % tk == 0
    return pl.pallas_call(
        functools.partial(_nsa_sel_kernel, n_sub=n_sub, tq=tq, tk=tk),
        grid=(b, s // t),
        in_specs=[pl.BlockSpec((1, h, t, dk), lambda b_, i: (b_, 0, i, 0)),
                  pl.BlockSpec((1, 1, s, dk), lambda b_, i: (b_, k_idx, 0, 0)),
                  pl.BlockSpec((1, 1, dve, s), lambda b_, i: (b_, 0, 0, 0)),
                  pl.BlockSpec((1, n_sel, t), lambda b_, i: (b_, 0, i))],
        out_specs=pl.BlockSpec((1, t, h * NSA_DV), lambda b_, i: (b_, i, 0)),
        out_shape=jax.ShapeDtypeStruct((b, s, h * NSA_DV), F32),
        scratch_shapes=[pltpu.VMEM((n_chain, 1, tq), F32), pltpu.VMEM((n_chain, dve, tq), F32),
                        pltpu.VMEM((_score_slots(t // tk), n_chain, tk, tq), SCORE_DTYPE)],
        compiler_params=_cparams(("parallel", "arbitrary")),
        name="nsa_selected",
    )(q, k3, vt, sel)


def _nsa_win_kernel(q_ref, kp_ref, kc_ref, vtp_ref, vtc_ref, o_ref, *, t):
    i = pl.program_id(1)
    key = lax.broadcasted_iota(jnp.int32, (t, t), 0)
    qry = lax.broadcasted_iota(jnp.int32, (t, t), 1)
    keep_prev = (key > qry) & (i > 0)
    keep_cur = key <= qry
    kp, kc, vtp, vtc = kp_ref[0, 0], kc_ref[0, 0], vtp_ref[0, 0], vtc_ref[0, 0]
    for h in range(NSA_HEADS):
        q = q_ref[0, h]
        sp = jnp.where(keep_prev, _dot_nt(kp, q), NEG_INF)
        sc = jnp.where(keep_cur, _dot_nt(kc, q), NEG_INF)
        m = jnp.maximum(jnp.max(sp, axis=0, keepdims=True), jnp.max(sc, axis=0, keepdims=True))
        pp = jnp.exp((sp - m).astype(SCORE_DTYPE)).astype(BF16)
        pc = jnp.exp((sc - m).astype(SCORE_DTYPE)).astype(BF16)
        acc = _dot(vtp, pp) + _dot(vtc, pc)
        o_t = acc[:NSA_DV, :] / acc[NSA_DV:NSA_DV + 1, :]
        o_ref[0, :, h * NSA_DV:(h + 1) * NSA_DV] = o_t.T


def nsa_window(q, k3, k_idx, vt, v_idx):
    b, h, s, dk = q.shape
    dve = vt.shape[2]
    t = _tile(s, NSA_WINDOW)
    assert t == NSA_WINDOW, "window kernel needs the query tile to equal the window"
    prev = lambda i: jnp.maximum(i - 1, 0)
    return pl.pallas_call(
        functools.partial(_nsa_win_kernel, t=t),
        grid=(b, s // t),
        in_specs=[pl.BlockSpec((1, h, t, dk), lambda b_, i: (b_, 0, i, 0)),
                  pl.BlockSpec((1, 1, t, dk), lambda b_, i: (b_, k_idx, prev(i), 0)),
                  pl.BlockSpec((1, 1, t, dk), lambda b_, i: (b_, k_idx, i, 0)),
                  pl.BlockSpec((1, 1, dve, t), lambda b_, i: (b_, v_idx, 0, prev(i))),
                  pl.BlockSpec((1, 1, dve, t), lambda b_, i: (b_, v_idx, 0, i))],
        out_specs=pl.BlockSpec((1, t, h * NSA_DV), lambda b_, i: (b_, i, 0)),
        out_shape=jax.ShapeDtypeStruct((b, s, h * NSA_DV), F32),
        compiler_params=_cparams(("parallel", "parallel")),
        name="nsa_window",
    )(q, k3, k3, vt, vt)


def _merge_kernel(x_ref, om_ref, oc_ref, os_ref, ow_ref, gn_ref, od_ref, gm_ref, gs_ref, gd_ref,
                  wm_ref, wn_ref, wd_ref, wo_ref, ex_ref, o_ref, onsa_scr, *, n_j):
    j = pl.program_id(1)
    nsa_out = NSA_HEADS * NSA_DV

    @pl.when(j == 0)
    def _():
        g = gn_ref[...]
        hi = g.astype(BF16)
        lo = (g - hi.astype(F32)).astype(BF16)
        ge = _dot(hi, ex_ref[...]) + _dot(lo, ex_ref[...])
        onsa = (ge[:, :nsa_out] * oc_ref[...] + ge[:, nsa_out:2 * nsa_out] * os_ref[...]
                + ge[:, 2 * nsa_out:] * ow_ref[...])
        onsa_scr[...] = onsa.astype(BF16)
        o_ref[...] = jnp.zeros_like(o_ref)

    ym = _dot(om_ref[...], wm_ref[...])
    yn = _dot(onsa_scr[...], wn_ref[...])
    yd = _dot(od_ref[0, 0], wd_ref[0])
    for h in range(1, DIFF_HEADS):
        yd = yd + _dot(od_ref[0, h], wd_ref[h])
    mixed = (gm_ref[...].astype(F32) * ym + gs_ref[...].astype(F32) * yn + gd_ref[...].astype(F32) * yd)
    o_ref[...] += _dot(mixed.astype(BF16), wo_ref[...])

    @pl.when(j == n_j - 1)
    def _():
        o_ref[...] = x_ref[...] + o_ref[...]


def _gate_expand_matrix():
    nsa_out = NSA_HEADS * NSA_DV
    e = np.zeros((LANES, 3 * nsa_out), np.float32)
    for h in range(NSA_HEADS):
        for c in range(3):
            e[h * 3 + c, c * nsa_out + h * NSA_DV:c * nsa_out + (h + 1) * NSA_DV] = 1.0
    return e


def merge(x2d, o_mla, o_c, o_s, o_w, g_nsa, o_diff, g_merge, w_br_mla, w_br_nsa, w_br_diff, w_out, seq):
    m, d = x2d.shape
    tm = _tile(seq, FUSED_ROWS)
    ns = seq // tm
    tn = _tile(d, MERGE_COLS)
    n_j = d // tn
    nsa_out = NSA_HEADS * NSA_DV
    mla_out = o_mla.shape[-1]
    ex = jnp.asarray(_gate_expand_matrix(), BF16)
    row = lambda w: pl.BlockSpec((tm, w), lambda i, j: (i, 0))
    return pl.pallas_call(
        functools.partial(_merge_kernel, n_j=n_j),
        grid=(m // tm, n_j),
        in_specs=[row(d), row(mla_out), row(nsa_out), row(nsa_out), row(nsa_out), row(LANES),
                  pl.BlockSpec((1, DIFF_HEADS, tm, DIFF_VD), lambda i, j: (i // ns, 0, i % ns, 0)),
                  pl.BlockSpec((tm, tn), lambda i, j: (i, j)),
                  pl.BlockSpec((tm, tn), lambda i, j: (i, n_j + j)),
                  pl.BlockSpec((tm, tn), lambda i, j: (i, 2 * n_j + j)),
                  pl.BlockSpec((mla_out, tn), lambda i, j: (0, j)),
                  pl.BlockSpec((nsa_out, tn), lambda i, j: (0, j)),
                  pl.BlockSpec((DIFF_HEADS, DIFF_VD, tn), lambda i, j: (0, 0, j)),
                  pl.BlockSpec((tn, d), lambda i, j: (j, 0)),
                  pl.BlockSpec((LANES, 3 * nsa_out), lambda i, j: (0, 0))],
        out_specs=pl.BlockSpec((tm, d), lambda i, j: (i, 0)),
        out_shape=jax.ShapeDtypeStruct((m, d), F32),
        scratch_shapes=[pltpu.VMEM((tm, nsa_out), BF16)],
        compiler_params=_cparams(("parallel", "arbitrary")),
        name="merge",
    )(x2d, o_mla, o_c, o_s, o_w, g_nsa, o_diff, g_merge, g_merge, g_merge,
      w_br_mla, w_br_nsa, w_br_diff, w_out, ex)


def _rope_tables(dim, seq, lead=0):
    inv = ROPE_THETA ** (-jnp.arange(0, dim, 2, dtype=F32) / dim)
    ang = jnp.arange(seq, dtype=F32)[:, None] * inv[None, :]
    cos, sin = jnp.cos(ang), jnp.sin(ang)
    cos = jnp.concatenate([jnp.ones((seq, lead), F32), cos, cos], axis=-1)
    sin = jnp.concatenate([jnp.zeros((seq, lead), F32), sin, sin], axis=-1)
    return cos, sin


def _heads(w, g, dh):
    return w.reshape(w.shape[0], g, dh).transpose(1, 0, 2)


def _rot_cols(w, lead=0):
    half = (w.shape[-1] - lead) // 2
    x1 = w[..., lead:lead + half]
    x2 = w[..., lead + half:]
    return jnp.concatenate([jnp.zeros_like(w[..., :lead]), -x2, x1], axis=-1)


def _pad_halves(w, axis=-1):
    w = jnp.moveaxis(w, axis, -1)
    half = w.shape[-1] // 2
    z = jnp.zeros(w.shape[:-1] + ((-half) % LANES,), w.dtype)
    out = jnp.concatenate([w[..., :half], z, w[..., half:], z], axis=-1)
    return jnp.moveaxis(out, -1, axis)


def _swap_tables(dim, seq, copies=1):
    inv = ROPE_THETA ** (-jnp.arange(0, dim, 2, dtype=F32) / dim)
    ang = jnp.arange(seq, dtype=F32)[:, None] * inv[None, :]
    cos = jnp.tile(jnp.cos(ang), (1, copies))
    sin = jnp.tile(jnp.sin(ang), (1, copies))
    return _pad_halves(jnp.concatenate([cos, cos], axis=-1)), _pad_halves(jnp.concatenate([-sin, sin], axis=-1))


def _pair_heads(w):
    k = w.shape[0]
    half = DIFF_HD // 2
    w = w.reshape(k, DIFF_HEADS, 2, 2, half)
    w = w.transpose(1, 0, 3, 2, 4).reshape(DIFF_HEADS, k, 2 * DIFF_HD)
    return _pad_halves(w)


def _pair_masks():
    half = DIFF_HD // 2
    m = np.zeros((2, 2, 2, half), np.float32)
    for c in range(2):
        m[c, :, c, :] = 1.0
    return _pad_halves(jnp.asarray(m.reshape(2, 2 * DIFF_HD)))


def _col_offsets():
    sizes = (MLA_Q_LORA, MLA_KV_LORA, MLA_ROPE,
             NSA_HEADS * NSA_DK, NSA_DK, NSA_DV, NSA_DK, NSA_DV, NSA_DK, NSA_DV, NSA_HEADS * 3,
             DIFF_HEADS * 2 * DIFF_HD, DIFF_HEADS * 2 * DIFF_HD, DIFF_HEADS * DIFF_VD)
    names = ("c_q", "c_kv", "k_rope", "nsa_q", "nsa_kc", "nsa_vc", "nsa_ks", "nsa_vs", "nsa_kw", "nsa_vw",
             "nsa_g", "d_q", "d_k", "d_v")
    offs = {}
    o = 0
    for nme, sz in zip(names, sizes):
        offs[nme] = (o, o + sz)
        o += sz
    offs["merge"] = (o, None)
    return offs


def _mixers(n2d, batch, seq, layer, w_in, p):
    offs = _col_offsets()
    col = lambda name: w_in[:, offs[name][0]:offs[name][1]]
    bf = lambda a: a.astype(BF16)

    n_q, n_kv = mla_latent(n2d, bf(jnp.concatenate([col("c_q"), col("c_kv")], axis=1)),
                           p["mla_q_norm"], p["mla_kv_norm"])
    cos_m, sin_m = _rope_tables(MLA_ROPE, seq, lead=MLA_NOPE)
    w_kr = jnp.concatenate([jnp.zeros((w_in.shape[0], MLA_NOPE), F32), col("k_rope")], axis=1)[None]
    kpe = proj_heads(n2d, bf(w_kr), batch, seq, rope=("weights", bf(_rot_cols(w_kr, MLA_NOPE)), cos_m, sin_m))
    w_uq = _heads(p["mla_w_uq"], MLA_HEADS, MLA_QK)
    q_mla = proj_heads(n_q, bf(w_uq), batch, seq, rope=("weights", bf(_rot_cols(w_uq, MLA_NOPE)), cos_m, sin_m),
                       scale=MLA_QK ** -0.5)
    w_ukv = _heads(p["mla_w_ukv"], MLA_HEADS, MLA_NOPE + MLA_V)
    w_uk = jnp.concatenate([w_ukv[..., :MLA_NOPE], jnp.zeros(w_ukv.shape[:2] + (MLA_ROPE,), F32)], axis=-1)
    k_mla = proj_heads(n_kv, bf(w_uk), batch, seq, add=kpe)
    vt_mla = proj_heads_t(n_kv, bf(w_ukv[..., MLA_NOPE:]), batch, seq)
    o_mla = flash_causal(q_mla, k_mla, vt_mla)

    cos_d, sin_d = _swap_tables(DIFF_HD, seq, copies=2)
    q_d = proj_heads(n2d, bf(_pair_heads(col("d_q"))), batch, seq, rope=("swap", cos_d, sin_d),
                     scale=DIFF_HD ** -0.5, masks=_pair_masks())
    k_d = proj_heads(n2d, bf(_pair_heads(col("d_k"))), batch, seq, rope=("swap", cos_d, sin_d))
    vt_d = proj_heads_t(n2d, bf(_heads(col("d_v"), DIFF_HEADS, DIFF_VD)), batch, seq)
    lambda_init = 0.8 - 0.6 * math.exp(-0.3 * layer)
    o_diff = diff_attention(q_d, k_d, vt_d, p["diff_lam_q1"], p["diff_lam_k1"], p["diff_lam_q2"],
                            p["diff_lam_k2"], p["diff_subln"], lambda_init)

    cos_n, sin_n = _swap_tables(NSA_DK, seq)
    w_nq = _pad_halves(_heads(col("nsa_q"), NSA_HEADS, NSA_DK))
    q_n = proj_heads(n2d, bf(w_nq), batch, seq, rope=("swap", cos_n, sin_n), scale=NSA_DK ** -0.5)
    w_nk = _pad_halves(jnp.stack([col("nsa_kc"), col("nsa_ks"), col("nsa_kw")], axis=0))
    k_n = proj_heads(n2d, bf(w_nk), batch, seq, rope=("swap", cos_n, sin_n))
    dk_pad = w_nk.shape[-1]
    v_cmp = proj_plain(n2d, bf(col("nsa_vc")), BF16)
    vt_n = proj_heads_t(n2d, bf(jnp.stack([col("nsa_vs"), col("nsa_vw")], axis=0)), batch, seq)
    w_g = jnp.concatenate([col("nsa_g"), jnp.zeros((w_in.shape[0], LANES - NSA_HEADS * 3), F32)], axis=1)
    g_nsa = proj_plain(n2d, bf(w_g), F32, sigmoid=True)

    ng = seq // NSA_CMP_STRIDE
    w1k = _pad_halves(p["nsa_cmp_k_w1"].reshape(NSA_CMP_LEN, NSA_DK, NSA_CMP_HIDDEN), axis=1)
    kc = nsa_compress(k_n.reshape(batch, 3, ng, NSA_CMP_STRIDE * dk_pad), 0,
                      _pad_halves(p["nsa_cmp_k_pos"]), w1k.reshape(NSA_CMP_LEN * dk_pad, NSA_CMP_HIDDEN),
                      _pad_halves(p["nsa_cmp_k_w2"]))
    vc_tok = v_cmp.reshape(batch, 1, ng, NSA_CMP_STRIDE * NSA_DV)
    vct = nsa_compress(vc_tok, 0, p["nsa_cmp_v_pos"], p["nsa_cmp_v_w1"], p["nsa_cmp_v_w2"], transposed=True)
    o_c, sel = nsa_cmp_topk(q_n, kc, vct)
    o_s = nsa_selected(q_n, k_n, 1, vt_n, sel)
    o_w = nsa_window(q_n, k_n, 2, vt_n, 1)

    g_merge = proj_plain(n2d, bf(w_in[:, offs["merge"][0]:]), BF16, sigmoid=True)
    m = batch * seq
    return (o_mla.reshape(m, -1), o_c.reshape(m, -1), o_s.reshape(m, -1), o_w.reshape(m, -1), g_nsa,
            o_diff, g_merge)


def kernel(x, ffn1_norm, ffn1_w_in, ffn1_w_out, mix_norm, w_in, mla_q_norm, mla_kv_norm, mla_w_uq, mla_w_ukv, nsa_cmp_k_pos, nsa_cmp_k_w1, nsa_cmp_k_w2, nsa_cmp_v_pos, nsa_cmp_v_w1, nsa_cmp_v_w2, diff_lam_q1, diff_lam_k1, diff_lam_q2, diff_lam_k2, diff_subln, w_br_mla, w_br_nsa, w_br_diff, w_out, ffn2_norm, ffn2_w_in, ffn2_w_out, final_norm):
    batch, seq, d = x.shape
    depth = w_in.shape[0]
    x2d = x.reshape(batch * seq, d)
    bf = lambda a: a.astype(BF16)
    for l in range(depth):
        p = {"mla_q_norm": mla_q_norm[l], "mla_kv_norm": mla_kv_norm[l], "mla_w_uq": mla_w_uq[l],
             "mla_w_ukv": mla_w_ukv[l], "nsa_cmp_k_pos": nsa_cmp_k_pos[l], "nsa_cmp_k_w1": nsa_cmp_k_w1[l],
             "nsa_cmp_k_w2": nsa_cmp_k_w2[l], "nsa_cmp_v_pos": nsa_cmp_v_pos[l], "nsa_cmp_v_w1": nsa_cmp_v_w1[l],
             "nsa_cmp_v_w2": nsa_cmp_v_w2[l], "diff_lam_q1": diff_lam_q1[l], "diff_lam_k1": diff_lam_k1[l],
             "diff_lam_q2": diff_lam_q2[l], "diff_lam_k2": diff_lam_k2[l], "diff_subln": diff_subln[l]}
        x2d, n2d = ffn(x2d, ffn1_norm[l], bf(ffn1_w_in[l]), bf(ffn1_w_out[l]), post_norm=(mix_norm[l], BF16))
        o_mla, o_c, o_s, o_w, g_nsa, o_diff, g_merge = _mixers(n2d, batch, seq, l, w_in[l], p)
        x2d = merge(x2d, o_mla, o_c, o_s, o_w, g_nsa, o_diff, g_merge, bf(w_br_mla[l]), bf(w_br_nsa[l]),
                    bf(w_br_diff[l]).reshape(DIFF_HEADS, DIFF_VD, d), bf(w_out[l]), seq)
        if l + 1 < depth:
            x2d = ffn(x2d, ffn2_norm[l], bf(ffn2_w_in[l]), bf(ffn2_w_out[l]))
        else:
            _, out = ffn(x2d, ffn2_norm[l], bf(ffn2_w_in[l]), bf(ffn2_w_out[l]), post_norm=(final_norm, F32))
    return out.reshape(batch, seq, d)
```

```python
import functools
import math

import numpy as np
import jax
import jax.numpy as jnp
from jax import lax
from jax.experimental import pallas as pl
from jax.experimental.pallas import tpu as pltpu

F32 = jnp.float32
BF16 = jnp.bfloat16

NORM_EPS = 1e-6
ROPE_THETA = 10000.0
NEG_INF = -1e30
REMOVED = -3e38
N_BRANCH = 3

MLA_HEADS = 6
MLA_Q_LORA = 768
MLA_KV_LORA = 512
MLA_NOPE = 128
MLA_ROPE = 64
MLA_V = 128
MLA_QK = MLA_NOPE + MLA_ROPE

NSA_HEADS = 4
NSA_DK = 192
NSA_DV = 128
NSA_CMP_LEN = 32
NSA_CMP_STRIDE = 16
NSA_CMP_HIDDEN = 256
NSA_SEL_LEN = 64
NSA_TOPK = 16
NSA_WINDOW = 512
NSA_FORCE_SCORE = 1e6

DIFF_HEADS = 4
DIFF_HD = 96
DIFF_VD = 2 * DIFF_HD

LANES = 128
ONES_PAD = 16
SCORE_DTYPE = BF16

PROJ_ROWS = 1024
FUSED_ROWS = 512
COL_TILE = 512
MERGE_COLS = 1024
ATTN_TILE = 512
VMEM_LIMIT_MB = 56


def _cparams(dims, vmem_mb=VMEM_LIMIT_MB):
    return pltpu.CompilerParams(dimension_semantics=dims, vmem_limit_bytes=vmem_mb * 2**20)


def _sigmoid(x):
    return 1.0 / (1.0 + jnp.exp(-x))


def _dot(a, b):
    return jnp.dot(a, b, preferred_element_type=F32)


def _dot_nt(a, b):
    return lax.dot_general(a, b, (((1,), (1,)), ((), ())), preferred_element_type=F32)


def _tile(n, pref):
    t = min(n, pref)
    assert n % t == 0, (n, t)
    return t


def _mla_latent_kernel(n_ref, w_ref, wq_ref, wkv_ref, nq_ref, nkv_ref):
    lat = _dot(n_ref[...], w_ref[...])
    cq = lat[:, :MLA_Q_LORA]
    ckv = lat[:, MLA_Q_LORA:]
    nq = cq * lax.rsqrt(jnp.mean(cq * cq, axis=-1, keepdims=True) + NORM_EPS)
    nkv = ckv * lax.rsqrt(jnp.mean(ckv * ckv, axis=-1, keepdims=True) + NORM_EPS)
    nq_ref[...] = (nq * wq_ref[...]).astype(BF16)
    nkv_ref[...] = (nkv * wkv_ref[...]).astype(BF16)


def mla_latent(n2d, w_lat, wq, wkv):
    m, k = n2d.shape
    d = w_lat.shape[1]
    tm = _tile(m, FUSED_ROWS)
    return pl.pallas_call(
        _mla_latent_kernel,
        grid=(m // tm,),
        in_specs=[pl.BlockSpec((tm, k), lambda i: (i, 0)),
                  pl.BlockSpec((k, d), lambda i: (0, 0)),
                  pl.BlockSpec((1, MLA_Q_LORA), lambda i: (0, 0)),
                  pl.BlockSpec((1, MLA_KV_LORA), lambda i: (0, 0))],
        out_specs=[pl.BlockSpec((tm, MLA_Q_LORA), lambda i: (i, 0)),
                   pl.BlockSpec((tm, MLA_KV_LORA), lambda i: (i, 0))],
        out_shape=[jax.ShapeDtypeStruct((m, MLA_Q_LORA), BF16), jax.ShapeDtypeStruct((m, MLA_KV_LORA), BF16)],
        compiler_params=_cparams(("parallel",)),
        name="mla_latent",
    )(n2d, w_lat, wq.reshape(1, -1).astype(F32), wkv.reshape(1, -1).astype(F32))


def _ffn_kernel(x_ref, nw_ref, wg_ref, wu_ref, wo_ref, *rest, n_f, post_norm):
    if post_norm:
        pw_ref, o_ref, p_ref, n_scr = rest
    else:
        o_ref, n_scr = rest
    f = pl.program_id(1)

    @pl.when(f == 0)
    def _():
        x = x_ref[...]
        y = x * lax.rsqrt(jnp.mean(x * x, axis=-1, keepdims=True) + NORM_EPS)
        n_scr[...] = (y * nw_ref[...]).astype(BF16)
        o_ref[...] = jnp.zeros_like(o_ref)

    n = n_scr[...]
    g = _dot(n, wg_ref[...])
    u = _dot(n, wu_ref[...])
    h = (g * _sigmoid(g) * u).astype(BF16)
    o_ref[...] += _dot(h, wo_ref[...])

    @pl.when(f == n_f - 1)
    def _():
        y = x_ref[...] + 0.5 * o_ref[...]
        o_ref[...] = y
        if post_norm:
            yn = y * lax.rsqrt(jnp.mean(y * y, axis=-1, keepdims=True) + NORM_EPS)
            p_ref[...] = (yn * pw_ref[...]).astype(p_ref.dtype)


def ffn(x2d, norm_w, w_in, w_out, post_norm=None):
    m, d = x2d.shape
    f_dim = w_out.shape[0]
    tm = _tile(m, FUSED_ROWS)
    tf = _tile(f_dim, COL_TILE)
    n_f = f_dim // tf
    row = pl.BlockSpec((tm, d), lambda i, f: (i, 0))
    vec = pl.BlockSpec((1, d), lambda i, f: (0, 0))
    in_specs = [row, vec,
                pl.BlockSpec((d, tf), lambda i, f: (0, f)),
                pl.BlockSpec((d, tf), lambda i, f: (0, f + n_f)),
                pl.BlockSpec((tf, d), lambda i, f: (f, 0))]
    args = [x2d, norm_w.reshape(1, d).astype(F32), w_in, w_in, w_out]
    out_specs, out_shape = row, jax.ShapeDtypeStruct((m, d), F32)
    if post_norm is not None:
        in_specs.append(vec)
        args.append(post_norm[0].reshape(1, d).astype(F32))
        out_specs, out_shape = [row, row], [out_shape, jax.ShapeDtypeStruct((m, d), post_norm[1])]
    return pl.pallas_call(
        functools.partial(_ffn_kernel, n_f=n_f, post_norm=post_norm is not None),
        grid=(m // tm, n_f),
        in_specs=in_specs,
        out_specs=out_specs,
        out_shape=out_shape,
        scratch_shapes=[pltpu.VMEM((tm, d), BF16)],
        compiler_params=_cparams(("parallel", "arbitrary")),
        name="ffn",
    )(*args)


def _proj_plain_kernel(n_ref, w_ref, o_ref, *, sigmoid):
    y = _dot(n_ref[...], w_ref[...])
    if sigmoid:
        y = _sigmoid(y)
    o_ref[...] = y.astype(o_ref.dtype)


def proj_plain(n2d, w, out_dtype, sigmoid=False):
    m, k = n2d.shape
    n_out = w.shape[1]
    tm = _tile(m, PROJ_ROWS)
    tn = next(t for t in (2 * COL_TILE, COL_TILE, COL_TILE // 2, LANES) if n_out % t == 0)
    return pl.pallas_call(
        functools.partial(_proj_plain_kernel, sigmoid=sigmoid),
        grid=(m // tm, n_out // tn),
        in_specs=[pl.BlockSpec((tm, k), lambda i, j: (i, 0)),
                  pl.BlockSpec((k, tn), lambda i, j: (0, j))],
        out_specs=pl.BlockSpec((tm, tn), lambda i, j: (i, j)),
        out_shape=jax.ShapeDtypeStruct((m, n_out), out_dtype),
        compiler_params=_cparams(("parallel", "arbitrary")),
        name="proj_plain",
    )(n2d, w)


def _proj_heads_kernel(*refs, rope, has_add, has_masks, scale):
    n_ref, w_ref = refs[0], refs[1]
    o_ref = refs[-1]
    n = n_ref[...]
    n_copy = o_ref.shape[1] // w_ref.shape[0]
    for h in range(w_ref.shape[0]):
        pos = 2
        y = _dot(n, w_ref[h])
        if rope == "weights":
            wr_ref, cos_ref, sin_ref = refs[pos:pos + 3]
            pos += 3
            y = y * cos_ref[...] + _dot(n, wr_ref[h]) * sin_ref[...]
        elif rope == "swap":
            cos_ref, sin_ref = refs[pos:pos + 2]
            pos += 2
            half = y.shape[1] // 2
            y = y * cos_ref[...] + jnp.concatenate([y[:, half:], y[:, :half]], axis=1) * sin_ref[...]
        if has_add:
            y = y + refs[pos][0, 0].astype(F32)
            pos += 1
        if scale != 1.0:
            y = y * scale
        if has_masks:
            mask_ref = refs[pos]
            for c in range(n_copy):
                o_ref[0, h * n_copy + c] = (y * mask_ref[c:c + 1, :]).astype(o_ref.dtype)
        else:
            o_ref[0, h] = y.astype(o_ref.dtype)


def proj_heads(n2d, w, batch, seq, rope=None, add=None, scale=1.0, masks=None):
    m, k = n2d.shape
    g, _, dh = w.shape
    tm = _tile(seq, PROJ_ROWS)
    ns = seq // tm
    weights = pl.BlockSpec((g, k, dh), lambda b, i: (0, 0, 0))
    in_specs = [pl.BlockSpec((tm, k), lambda b, i: (b * ns + i, 0)), weights]
    args = [n2d, w]
    table = pl.BlockSpec((tm, dh), lambda b, i: (i, 0))
    if rope is not None and rope[0] == "weights":
        in_specs += [weights, table, table]
        args += list(rope[1:])
    elif rope is not None:
        assert rope[0] == "swap" and dh % (2 * LANES) == 0
        in_specs += [table, table]
        args += list(rope[1:])
    if add is not None:
        in_specs.append(pl.BlockSpec((1, 1, tm, dh), lambda b, i: (b, 0, i, 0)))
        args.append(add)
    n_copy = 1
    if masks is not None:
        n_copy = masks.shape[0]
        in_specs.append(pl.BlockSpec((n_copy, dh), lambda b, i: (0, 0)))
        args.append(masks)
    return pl.pallas_call(
        functools.partial(_proj_heads_kernel, rope=None if rope is None else rope[0], has_add=add is not None,
                          has_masks=masks is not None, scale=scale),
        grid=(batch, ns),
        in_specs=in_specs,
        out_specs=pl.BlockSpec((1, g * n_copy, tm, dh), lambda b, i: (b, 0, i, 0)),
        out_shape=jax.ShapeDtypeStruct((batch, g * n_copy, seq, dh), BF16),
        compiler_params=_cparams(("parallel", "parallel")),
        name="proj_heads",
    )(*args)


def _proj_heads_t_kernel(n_ref, wt_ref, o_ref):
    dh = wt_ref.shape[1]
    pad = o_ref.shape[2] - dh
    row = lax.broadcasted_iota(jnp.int32, (pad, o_ref.shape[3]), 0)
    ones_rows = jnp.where(row == 0, 1.0, 0.0).astype(o_ref.dtype)
    n = n_ref[...]
    for h in range(wt_ref.shape[0]):
        o_ref[0, h, :dh, :] = _dot_nt(wt_ref[h], n).astype(o_ref.dtype)
        o_ref[0, h, dh:, :] = ones_rows


def proj_heads_t(n2d, w, batch, seq):
    m, k = n2d.shape
    g, _, dh = w.shape
    tm = _tile(seq, PROJ_ROWS)
    ns = seq // tm
    return pl.pallas_call(
        _proj_heads_t_kernel,
        grid=(batch, ns),
        in_specs=[pl.BlockSpec((tm, k), lambda b, i: (b * ns + i, 0)),
                  pl.BlockSpec((g, dh, k), lambda b, i: (0, 0, 0))],
        out_specs=pl.BlockSpec((1, g, dh + ONES_PAD, tm), lambda b, i: (b, 0, 0, i)),
        out_shape=jax.ShapeDtypeStruct((batch, g, dh + ONES_PAD, seq), BF16),
        compiler_params=_cparams(("parallel", "parallel")),
        name="proj_heads_t",
    )(n2d, jnp.swapaxes(w, 1, 2))


def _softmax_step_t(st, vt, m_scr, acc_scr, exp_dtype):
    m_prev = m_scr[...]
    m_new = jnp.maximum(m_prev, jnp.max(st, axis=0, keepdims=True).astype(F32))
    alpha = jnp.exp(m_prev - m_new)
    pt = jnp.exp((st - m_new.astype(st.dtype)).astype(exp_dtype)).astype(BF16)
    acc_scr[...] = alpha * acc_scr[...] + _dot(vt, pt)
    m_scr[...] = m_new


def _diag_visibility(d, c, tq, tk):
    if d * tk >= (c + 1) * tq:
        return "none"
    if (d + 1) * tk - 1 <= c * tq:
        return "full"
    return "partial"


def _causal_where(st, d, c):
    tk, tq = st.shape
    key = lax.broadcasted_iota(jnp.int32, (tk, tq), 0) + d * tk
    qry = lax.broadcasted_iota(jnp.int32, (tk, tq), 1) + c * tq
    return jnp.where(key <= qry, st, NEG_INF)


def _flash_sweep(n, r, sub_of, tq, tk, scores, vt_tile, m_scr, acc_scr, st_scr, exp_dtype=F32):
    n_chain = m_scr.shape[0]
    for c in range(n_chain):
        m_scr[c] = jnp.full(m_scr.shape[1:], NEG_INF, F32)
        acc_scr[c] = jnp.zeros(acc_scr.shape[1:], F32)
    n_slot = st_scr.shape[0]
    assert r % n_slot == 0
    everyone = (True,) * n_chain
    for c, st in enumerate(scores(0, everyone)):
        st_scr[0, c] = st.astype(st_scr.dtype)

    def body(jj, carry):
        for slot in range(n_slot):
            j = jj * n_slot + slot
            nxt = scores(j + 1, everyone)
            vt = vt_tile(j)
            for c in range(n_chain):
                st = st_scr[slot, c]
                st_scr[(slot + 1) % n_slot, c] = nxt[c].astype(st_scr.dtype)
                _softmax_step_t(st, vt, m_scr.at[c], acc_scr.at[c], exp_dtype)
        return carry

    lax.fori_loop(0, n // n_slot, body, 0)
    for d in range(r):
        vis = [_diag_visibility(d, sub_of(c), tq, tk) for c in range(n_chain)]
        need = tuple(d + 1 < r and _diag_visibility(d + 1, sub_of(c), tq, tk) != "none" for c in range(n_chain))
        nxt = scores(n + d + 1, need) if any(need) else None
        vt = vt_tile(n + d)
        for c in range(n_chain):
            if vis[c] != "none":
                st = st_scr[d % n_slot, c]
                if vis[c] == "partial":
                    st = _causal_where(st.astype(F32), d, sub_of(c)).astype(st_scr.dtype)
            if need[c]:
                st_scr[(d + 1) % n_slot, c] = nxt[c].astype(st_scr.dtype)
            if vis[c] != "none":
                _softmax_step_t(st, vt, m_scr.at[c], acc_scr.at[c], exp_dtype)


def _flash_kernel(q_ref, k_ref, vt_ref, o_ref, m_scr, acc_scr, st_scr, *, n_sub, tq, tk):
    i = pl.program_id(2)
    dv = o_ref.shape[2]
    r = n_sub * tq // tk

    def scores(j, need):
        k = k_ref[0, 0, pl.ds(pl.multiple_of(j * tk, tk), tk), :]
        return [_dot_nt(k, q_ref[0, 0, c * tq:(c + 1) * tq, :]) if need[c] else None for c in range(n_sub)]

    def vt_tile(j):
        return vt_ref[0, 0, :, pl.ds(pl.multiple_of(j * tk, tk), tk)]

    _flash_sweep(i * r, r, lambda c: c, tq, tk, scores, vt_tile, m_scr, acc_scr, st_scr, exp_dtype=BF16)
    for c in range(n_sub):
        o_t = acc_scr[c, :dv, :] / acc_scr[c, dv:dv + 1, :]
        o_ref[0, c * tq:(c + 1) * tq, :] = o_t.T.astype(o_ref.dtype)


def _score_slots(r):
    return 2 if r % 2 == 0 else 1


def _flash_tiles(s, max_sub):
    tq = _tile(s, ATTN_TILE)
    n_sub = next(n for n in (4, 2, 1) if n <= max_sub and s % (n * tq) == 0)
    return n_sub, tq, tq


def flash_causal(q, k, vt):
    b, h, s, dk = q.shape
    dve = vt.shape[2]
    dv = dve - ONES_PAD
    n_sub, tq, tk = _flash_tiles(s, max_sub=4)
    t = n_sub * tq
    return pl.pallas_call(
        functools.partial(_flash_kernel, n_sub=n_sub, tq=tq, tk=tk),
        grid=(b, h, s // t),
        in_specs=[pl.BlockSpec((1, 1, t, dk), lambda b_, h_, i: (b_, h_, i, 0)),
                  pl.BlockSpec((1, 1, s, dk), lambda b_, h_, i: (b_, h_, 0, 0)),
                  pl.BlockSpec((1, 1, dve, s), lambda b_, h_, i: (b_, h_, 0, 0))],
        out_specs=pl.BlockSpec((1, t, dv), lambda b_, h_, i: (b_, i, h_)),
        out_shape=jax.ShapeDtypeStruct((b, s, h * dv), BF16),
        scratch_shapes=[pltpu.VMEM((n_sub, 1, tq), F32), pltpu.VMEM((n_sub, dve, tq), F32),
                        pltpu.VMEM((_score_slots(n_sub * tq // tk), n_sub, tk, tq), SCORE_DTYPE)],
        compiler_params=_cparams(("parallel", "parallel", "arbitrary")),
        name="mla_flash",
    )(q, k, vt)


def _diff_kernel(q_ref, k_ref, vt_ref, lq1_ref, lk1_ref, lq2_ref, lk2_ref, sub_ref, o_ref,
                 m_scr, acc_scr, st_scr, *, n_sub, tq, tk, lambda_init):
    i = pl.program_id(2)
    vd = o_ref.shape[3]
    r = n_sub * tq // tk

    def scores(j, need):
        k = k_ref[0, 0, pl.ds(pl.multiple_of(j * tk, tk), tk), :]
        return [_dot_nt(k, q_ref[0, ch // n_sub, (ch % n_sub) * tq:(ch % n_sub + 1) * tq, :]) if need[ch] else None
                for ch in range(2 * n_sub)]

    def vt_tile(j):
        return vt_ref[0, 0, :, pl.ds(pl.multiple_of(j * tk, tk), tk)]

    _flash_sweep(i * r, r, lambda ch: ch % n_sub, tq, tk, scores, vt_tile, m_scr, acc_scr, st_scr, exp_dtype=BF16)

    lam = (jnp.exp(jnp.sum(lq1_ref[...] * lk1_ref[...], axis=-1, keepdims=True))
           - jnp.exp(jnp.sum(lq2_ref[...] * lk2_ref[...], axis=-1, keepdims=True)) + lambda_init)
    lane_pad = (-vd) % LANES
    for c in range(n_sub):
        o_t = (acc_scr[c, :vd, :] / acc_scr[c, vd:vd + 1, :]
               - lam * (acc_scr[n_sub + c, :vd, :] / acc_scr[n_sub + c, vd:vd + 1, :]))
        o_t = o_t * lax.rsqrt(jnp.mean(o_t * o_t, axis=0, keepdims=True) + NORM_EPS)
        o_t = jnp.concatenate([o_t, jnp.zeros((lane_pad, tq), F32)], axis=0)
        o_ref[0, 0, c * tq:(c + 1) * tq, :] = (
            o_t.T[:, :vd] * sub_ref[...] * (1.0 - lambda_init)).astype(o_ref.dtype)


def diff_attention(q, k, vt, lq1, lk1, lq2, lk2, subln, lambda_init):
    b, h2, s, dk = q.shape
    h = h2 // 2
    hd = DIFF_HD
    vde = vt.shape[2]
    vd = vde - ONES_PAD
    n_sub, tq, tk = _flash_tiles(s, max_sub=4)
    t = n_sub * tq
    vec = lambda a: a.reshape(1, -1).astype(F32)
    small = lambda n: pl.BlockSpec((1, n), lambda b_, h_, i: (0, 0))
    return pl.pallas_call(
        functools.partial(_diff_kernel, n_sub=n_sub, tq=tq, tk=tk, lambda_init=lambda_init),
        grid=(b, h, s // t),
        in_specs=[pl.BlockSpec((1, 2, t, dk), lambda b_, h_, i: (b_, h_, i, 0)),
                  pl.BlockSpec((1, 1, s, dk), lambda b_, h_, i: (b_, h_, 0, 0)),
                  pl.BlockSpec((1, 1, vde, s), lambda b_, h_, i: (b_, h_, 0, 0)),
                  small(hd), small(hd), small(hd), small(hd), small(vd)],
        out_specs=pl.BlockSpec((1, 1, t, vd), lambda b_, h_, i: (b_, h_, i, 0)),
        out_shape=jax.ShapeDtypeStruct((b, h, s, vd), BF16),
        scratch_shapes=[pltpu.VMEM((2 * n_sub, 1, tq), F32), pltpu.VMEM((2 * n_sub, vde, tq), F32),
                        pltpu.VMEM((_score_slots(n_sub * tq // tk), 2 * n_sub, tk, tq), SCORE_DTYPE)],
        compiler_params=_cparams(("parallel", "parallel", "arbitrary")),
        name="diff_flash",
    )(q, k, vt, vec(lq1), vec(lk1), vec(lq2), vec(lk2), vec(subln))


def _compress_kernel(a_ref, pa_ref, pb_ref, w1a_ref, w1b_ref, w2_ref, o_ref, *, transposed):
    ng = a_ref.shape[2]
    rows = min(ng, 256)
    us, vs = [], []
    for r0 in range(0, ng, rows):
        a = a_ref[0, 0, r0:r0 + rows, :].astype(F32)
        us.append(_dot((a + pa_ref[...]).astype(BF16), w1a_ref[...]))
        vs.append(_dot((a + pb_ref[...]).astype(BF16), w1b_ref[...]))
    u = jnp.concatenate(us, axis=0)
    v = jnp.concatenate(vs, axis=0)
    hdn = u + pltpu.roll(v, ng - 1, 0)
    hdn = (hdn * _sigmoid(hdn)).astype(BF16)
    if transposed:
        d_out = w2_ref.shape[0]
        o_ref[0, :d_out, :] = _dot_nt(w2_ref[...], hdn).astype(o_ref.dtype)
        row = lax.broadcasted_iota(jnp.int32, (o_ref.shape[1] - d_out, ng), 0)
        o_ref[0, d_out:, :] = jnp.where(row == 0, 1.0, 0.0).astype(o_ref.dtype)
    else:
        o_ref[0] = _dot(hdn, w2_ref[...]).astype(o_ref.dtype)


def nsa_compress(tok, g_idx, pos, w1, w2, transposed=False):
    assert NSA_CMP_LEN == 2 * NSA_CMP_STRIDE
    b, _, ng, wd = tok.shape
    d = wd // NSA_CMP_STRIDE
    d_out = w2.shape[1]
    pos_flat = pos.astype(F32).reshape(1, NSA_CMP_LEN * d)
    w1 = w1.astype(BF16)
    w2 = w2.astype(BF16).T if transposed else w2.astype(BF16)
    out_block = (1, d_out + ONES_PAD, ng) if transposed else (1, ng, d_out)
    full = lambda shape: pl.BlockSpec(shape, lambda b_: (0,) * len(shape))
    return pl.pallas_call(
        functools.partial(_compress_kernel, transposed=transposed),
        grid=(b,),
        in_specs=[pl.BlockSpec((1, 1, ng, wd), lambda b_: (b_, g_idx, 0, 0)),
                  full((1, wd)), full((1, wd)), full((wd, NSA_CMP_HIDDEN)), full((wd, NSA_CMP_HIDDEN)),
                  full(w2.shape)],
        out_specs=pl.BlockSpec(out_block, lambda b_: (b_, 0, 0)),
        out_shape=jax.ShapeDtypeStruct((b,) + out_block[1:], BF16),
        compiler_params=_cparams(("parallel",)),
        name="nsa_compress",
    )(tok, pos_flat[:, :wd], pos_flat[:, wd:], w1[:wd], w1[wd:], w2)


def _split3(x):
    hi = x.astype(BF16)
    r = x - hi.astype(F32)
    mid = r.astype(BF16)
    lo = (r - mid.astype(F32)).astype(BF16)
    return hi, mid, lo


def _cmp_topk_kernel(q_ref, kc_ref, vct_ref, wselt_ref, oc_ref, selt_ref, *, tq, k_top):
    i = pl.program_id(1)
    qs = i * tq
    nc = kc_ref.shape[1]
    n_sel = wselt_ref.shape[0]
    kc = kc_ref[0]
    vct = vct_ref[0]
    qpos_c = qs + lax.broadcasted_iota(jnp.int32, (nc, tq), 1)
    cend = lax.broadcasted_iota(jnp.int32, (nc, tq), 0) * NSA_CMP_STRIDE + (NSA_CMP_LEN - 1)
    valid_c = cend <= qpos_c
    any_valid = qs + lax.broadcasted_iota(jnp.int32, (1, tq), 1) >= NSA_CMP_LEN - 1
    imp_t = jnp.zeros((nc, tq), F32)
    for h in range(NSA_HEADS):
        st = jnp.where(valid_c, _dot_nt(kc, q_ref[0, h]), NEG_INF)
        e = jnp.exp(st - jnp.max(st, axis=0, keepdims=True))
        inv = jnp.where(any_valid, 1.0 / jnp.sum(e, axis=0, keepdims=True), 0.0)
        o_t = _dot(vct, e.astype(BF16))[:NSA_DV, :] * inv
        oc_ref[0, :, h * NSA_DV:(h + 1) * NSA_DV] = o_t.T
        imp_t = imp_t + e * inv

    wselt = wselt_ref[...]
    hi, mid, lo = _split3(imp_t)
    imp_sel = _dot(wselt, hi) + _dot(wselt, mid) + _dot(wselt, lo)

    blk = lax.broadcasted_iota(jnp.int32, (n_sel, tq), 0)
    qpos = qs + lax.broadcasted_iota(jnp.int32, (n_sel, tq), 1)
    cur = lax.shift_right_arithmetic(qpos, int(math.log2(NSA_SEL_LEN)))
    forced = (blk == 0) | (blk == cur) | (blk == cur - 1)
    valid_s = blk * NSA_SEL_LEN <= qpos
    score = jnp.where(valid_s, jnp.where(forced, NSA_FORCE_SCORE, imp_sel), NEG_INF)
    blk_f = blk.astype(F32)
    for _ in range(k_top):
        mx = jnp.max(score, axis=0, keepdims=True)
        first = jnp.min(jnp.where(score == mx, blk_f, float(n_sel)), axis=0, keepdims=True)
        score = jnp.where(blk_f == first, REMOVED, score)
    selt_ref[0] = jnp.where(score == REMOVED, 1.0, 0.0).astype(selt_ref.dtype)


def _sel_weight_matrix(n_pad, n_sel):
    r_c = NSA_CMP_LEN // NSA_CMP_STRIDE
    ratio = NSA_SEL_LEN // NSA_CMP_STRIDE
    overlap_w = [max(0, min(o * NSA_CMP_STRIDE + NSA_CMP_LEN, NSA_SEL_LEN) - max(o * NSA_CMP_STRIDE, 0))
                 / NSA_CMP_STRIDE for o in range(-(r_c - 1), ratio)]
    w = np.zeros((n_pad, n_sel), np.float32)
    for n in range(n_sel):
        for u, w_u in enumerate(overlap_w):
            c = ratio * n + u - (r_c - 1)
            if 0 <= c < n_pad:
                w[c, n] = w_u
    return w


def nsa_cmp_topk(q, kc, vct):
    b, h, s, dk = q.shape
    nc = kc.shape[1]
    dve = vct.shape[1]
    n_sel = s // NSA_SEL_LEN
    assert NSA_SEL_LEN & (NSA_SEL_LEN - 1) == 0
    tq = _tile(s, ATTN_TILE)
    wselt = jnp.asarray(_sel_weight_matrix(nc, n_sel).T, BF16)
    return pl.pallas_call(
        functools.partial(_cmp_topk_kernel, tq=tq, k_top=min(NSA_TOPK, n_sel)),
        grid=(b, s // tq),
        in_specs=[pl.BlockSpec((1, h, tq, dk), lambda b_, i: (b_, 0, i, 0)),
                  pl.BlockSpec((1, nc, dk), lambda b_, i: (b_, 0, 0)),
                  pl.BlockSpec((1, dve, nc), lambda b_, i: (b_, 0, 0)),
                  pl.BlockSpec((n_sel, nc), lambda b_, i: (0, 0))],
        out_specs=[pl.BlockSpec((1, tq, h * NSA_DV), lambda b_, i: (b_, i, 0)),
                   pl.BlockSpec((1, n_sel, tq), lambda b_, i: (b_, 0, i))],
        out_shape=[jax.ShapeDtypeStruct((b, s, h * NSA_DV), F32), jax.ShapeDtypeStruct((b, n_sel, s), F32)],
        compiler_params=_cparams(("parallel", "parallel")),
        name="nsa_cmp_topk",
    )(q, kc, vct, wselt)


def _nsa_sel_kernel(q_ref, k_ref, vt_ref, sel_ref, o_ref, m_scr, acc_scr, st_scr, *, n_sub, tq, tk):
    i = pl.program_id(1)
    bpt = tk // NSA_SEL_LEN
    r = n_sub * tq // tk

    def scores(j, need):
        k = k_ref[0, 0, pl.ds(pl.multiple_of(j * tk, tk), tk), :]
        out = []
        for c in range(n_sub):
            rows = slice(c * tq, (c + 1) * tq)
            if not any(need[c * NSA_HEADS:(c + 1) * NSA_HEADS]):
                out += [None] * NSA_HEADS
                continue
            flags = sel_ref[0, pl.ds(pl.multiple_of(j * bpt, bpt), bpt), rows]
            keep_t = jnp.broadcast_to(flags[:, None, :], (bpt, NSA_SEL_LEN, tq)).reshape(tk, tq) > 0.5
            out += [jnp.where(keep_t, _dot_nt(k, q_ref[0, h, rows, :]), NEG_INF) if need[c * NSA_HEADS + h] else None
                    for h in range(NSA_HEADS)]
        return out

    def vt_tile(j):
        return vt_ref[0, 0, :, pl.ds(pl.multiple_of(j * tk, tk), tk)]

    _flash_sweep(i * r, r, lambda ch: ch // NSA_HEADS, tq, tk, scores, vt_tile, m_scr, acc_scr, st_scr,
                 exp_dtype=BF16)
    for ch in range(n_sub * NSA_HEADS):
        c, h = divmod(ch, NSA_HEADS)
        o_t = acc_scr[ch, :NSA_DV, :] / acc_scr[ch, NSA_DV:NSA_DV + 1, :]
        o_ref[0, c * tq:(c + 1) * tq, h * NSA_DV:(h + 1) * NSA_DV] = o_t.T


def _nsa_sel_tiles(s):
    return _flash_tiles(s, max_sub=2)


def nsa_selected(q, k3, k_idx, vt, sel):
    b, h, s, dk = q.shape
    n_sel = sel.shape[1]
    dve = vt.shape[2]
    n_sub, tq, tk = _nsa_sel_tiles(s)
    t = n_sub * tq
    n_chain = n_sub * h
    assert tk % NSA_SEL_LEN == 0 and t % tk == 0
    return pl.pallas_call(
        functools.partial(_nsa_sel_kernel, n_sub=n_sub, tq=tq, tk=tk),
        grid=(b, s // t),
        in_specs=[pl.BlockSpec((1, h, t, dk), lambda b_, i: (b_, 0, i, 0)),
                  pl.BlockSpec((1, 1, s, dk), lambda b_, i: (b_, k_idx, 0, 0)),
                  pl.BlockSpec((1, 1, dve, s), lambda b_, i: (b_, 0, 0, 0)),
                  pl.BlockSpec((1, n_sel, t), lambda b_, i: (b_, 0, i))],
        out_specs=pl.BlockSpec((1, t, h * NSA_DV), lambda b_, i: (b_, i, 0)),
        out_shape=jax.ShapeDtypeStruct((b, s, h * NSA_DV), F32),
        scratch_shapes=[pltpu.VMEM((n_chain, 1, tq), F32), pltpu.VMEM((n_chain, dve, tq), F32),
                        pltpu.VMEM((_score_slots(t // tk), n_chain, tk, tq), SCORE_DTYPE)],
        compiler_params=_cparams(("parallel", "arbitrary")),
        name="nsa_selected",
    )(q, k3, vt, sel)


def _nsa_win_kernel(q_ref, kp_ref, kc_ref, vtp_ref, vtc_ref, o_ref, *, t):
    i = pl.program_id(1)
    key = lax.broadcasted_iota(jnp.int32, (t, t), 0)
    qry = lax.broadcasted_iota(jnp.int32, (t, t), 1)
    keep_prev = (key > qry) & (i > 0)
    keep_cur = key <= qry
    kp, kc, vtp, vtc = kp_ref[0, 0], kc_ref[0, 0], vtp_ref[0, 0], vtc_ref[0, 0]
    for h in range(NSA_HEADS):
        q = q_ref[0, h]
        sp = jnp.where(keep_prev, _dot_nt(kp, q), NEG_INF)
        sc = jnp.where(keep_cur, _dot_nt(kc, q), NEG_INF)
        m = jnp.maximum(jnp.max(sp, axis=0, keepdims=True), jnp.max(sc, axis=0, keepdims=True))
        pp = jnp.exp((sp - m).astype(SCORE_DTYPE)).astype(BF16)
        pc = jnp.exp((sc - m).astype(SCORE_DTYPE)).astype(BF16)
        acc = _dot(vtp, pp) + _dot(vtc, pc)
        o_t = acc[:NSA_DV, :] / acc[NSA_DV:NSA_DV + 1, :]
        o_ref[0, :, h * NSA_DV:(h + 1) * NSA_DV] = o_t.T


def nsa_window(q, k3, k_idx, vt, v_idx):
    b, h, s, dk = q.shape
    dve = vt.shape[2]
    t = _tile(s, NSA_WINDOW)
    assert t == NSA_WINDOW, "window kernel needs the query tile to equal the window"
    prev = lambda i: jnp.maximum(i - 1, 0)
    return pl.pallas_call(
        functools.partial(_nsa_win_kernel, t=t),
        grid=(b, s // t),
        in_specs=[pl.BlockSpec((1, h, t, dk), lambda b_, i: (b_, 0, i, 0)),
                  pl.BlockSpec((1, 1, t, dk), lambda b_, i: (b_, k_idx, prev(i), 0)),
                  pl.BlockSpec((1, 1, t, dk), lambda b_, i: (b_, k_idx, i, 0)),
                  pl.BlockSpec((1, 1, dve, t), lambda b_, i: (b_, v_idx, 0, prev(i))),
                  pl.BlockSpec((1, 1, dve, t), lambda b_, i: (b_, v_idx, 0, i))],
        out_specs=pl.BlockSpec((1, t, h * NSA_DV), lambda b_, i: (b_, i, 0)),
        out_shape=jax.ShapeDtypeStruct((b, s, h * NSA_DV), F32),
        compiler_params=_cparams(("parallel", "parallel")),
        name="nsa_window",
    )(q, k3, k3, vt, vt)


def _merge_kernel(x_ref, om_ref, oc_ref, os_ref, ow_ref, gn_ref, od_ref, gm_ref, gs_ref, gd_ref,
                  wm_ref, wn_ref, wd_ref, wo_ref, ex_ref, o_ref, onsa_scr, *, n_j):
    j = pl.program_id(1)
    nsa_out = NSA_HEADS * NSA_DV

    @pl.when(j == 0)
    def _():
        g = gn_ref[...]
        hi = g.astype(BF16)
        lo = (g - hi.astype(F32)).astype(BF16)
        ge = _dot(hi, ex_ref[...]) + _dot(lo, ex_ref[...])
        onsa = (ge[:, :nsa_out] * oc_ref[...] + ge[:, nsa_out:2 * nsa_out] * os_ref[...]
                + ge[:, 2 * nsa_out:] * ow_ref[...])
        onsa_scr[...] = onsa.astype(BF16)
        o_ref[...] = jnp.zeros_like(o_ref)

    ym = _dot(om_ref[...], wm_ref[...])
    yn = _dot(onsa_scr[...], wn_ref[...])
    yd = _dot(od_ref[0, 0], wd_ref[0])
    for h in range(1, DIFF_HEADS):
        yd = yd + _dot(od_ref[0, h], wd_ref[h])
    mixed = (gm_ref[...].astype(F32) * ym + gs_ref[...].astype(F32) * yn + gd_ref[...].astype(F32) * yd)
    o_ref[...] += _dot(mixed.astype(BF16), wo_ref[...])

    @pl.when(j == n_j - 1)
    def _():
        o_ref[...] = x_ref[...] + o_ref[...]


def _gate_expand_matrix():
    nsa_out = NSA_HEADS * NSA_DV
    e = np.zeros((LANES, 3 * nsa_out), np.float32)
    for h in range(NSA_HEADS):
        for c in range(3):
            e[h * 3 + c, c * nsa_out + h * NSA_DV:c * nsa_out + (h + 1) * NSA_DV] = 1.0
    return e


def merge(x2d, o_mla, o_c, o_s, o_w, g_nsa, o_diff, g_merge, w_br_mla, w_br_nsa, w_br_diff, w_out, seq):
    m, d = x2d.shape
    tm = _tile(seq, FUSED_ROWS)
    ns = seq // tm
    tn = _tile(d, MERGE_COLS)
    n_j = d // tn
    nsa_out = NSA_HEADS * NSA_DV
    mla_out = o_mla.shape[-1]
    ex = jnp.asarray(_gate_expand_matrix(), BF16)
    row = lambda w: pl.BlockSpec((tm, w), lambda i, j: (i, 0))
    return pl.pallas_call(
        functools.partial(_merge_kernel, n_j=n_j),
        grid=(m // tm, n_j),
        in_specs=[row(d), row(mla_out), row(nsa_out), row(nsa_out), row(nsa_out), row(LANES),
                  pl.BlockSpec((1, DIFF_HEADS, tm, DIFF_VD), lambda i, j: (i // ns, 0, i % ns, 0)),
                  pl.BlockSpec((tm, tn), lambda i, j: (i, j)),
                  pl.BlockSpec((tm, tn), lambda i, j: (i, n_j + j)),
                  pl.BlockSpec((tm, tn), lambda i, j: (i, 2 * n_j + j)),
                  pl.BlockSpec((mla_out, tn), lambda i, j: (0, j)),
                  pl.BlockSpec((nsa_out, tn), lambda i, j: (0, j)),
                  pl.BlockSpec((DIFF_HEADS, DIFF_VD, tn), lambda i, j: (0, 0, j)),
                  pl.BlockSpec((tn, d), lambda i, j: (j, 0)),
                  pl.BlockSpec((LANES, 3 * nsa_out), lambda i, j: (0, 0))],
        out_specs=pl.BlockSpec((tm, d), lambda i, j: (i, 0)),
        out_shape=jax.ShapeDtypeStruct((m, d), F32),
        scratch_shapes=[pltpu.VMEM((tm, nsa_out), BF16)],
        compiler_params=_cparams(("parallel", "arbitrary")),
        name="merge",
    )(x2d, o_mla, o_c, o_s, o_w, g_nsa, o_diff, g_merge, g_merge, g_merge,
      w_br_mla, w_br_nsa, w_br_diff, w_out, ex)


def _rope_tables(dim, seq, lead=0):
    inv = ROPE_THETA ** (-jnp.arange(0, dim, 2, dtype=F32) / dim)
    ang = jnp.arange(seq, dtype=F32)[:, None] * inv[None, :]
    cos, sin = jnp.cos(ang), jnp.sin(ang)
    cos = jnp.concatenate([jnp.ones((seq, lead), F32), cos, cos], axis=-1)
    sin = jnp.concatenate([jnp.zeros((seq, lead), F32), sin, sin], axis=-1)
    return cos, sin


def _heads(w, g, dh):
    return w.reshape(w.shape[0], g, dh).transpose(1, 0, 2)


def _rot_cols(w, lead=0):
    half = (w.shape[-1] - lead) // 2
    x1 = w[..., lead:lead + half]
    x2 = w[..., lead + half:]
    return jnp.concatenate([jnp.zeros_like(w[..., :lead]), -x2, x1], axis=-1)


def _pad_halves(w, axis=-1):
    w = jnp.moveaxis(w, axis, -1)
    half = w.shape[-1] // 2
    z = jnp.zeros(w.shape[:-1] + ((-half) % LANES,), w.dtype)
    out = jnp.concatenate([w[..., :half], z, w[..., half:], z], axis=-1)
    return jnp.moveaxis(out, -1, axis)


def _swap_tables(dim, seq, copies=1):
    inv = ROPE_THETA ** (-jnp.arange(0, dim, 2, dtype=F32) / dim)
    ang = jnp.arange(seq, dtype=F32)[:, None] * inv[None, :]
    cos = jnp.tile(jnp.cos(ang), (1, copies))
    sin = jnp.tile(jnp.sin(ang), (1, copies))
    return _pad_halves(jnp.concatenate([cos, cos], axis=-1)), _pad_halves(jnp.concatenate([-sin, sin], axis=-1))


def _pair_heads(w):
    k = w.shape[0]
    half = DIFF_HD // 2
    w = w.reshape(k, DIFF_HEADS, 2, 2, half)
    w = w.transpose(1, 0, 3, 2, 4).reshape(DIFF_HEADS, k, 2 * DIFF_HD)
    return _pad_halves(w)


def _pair_masks():
    half = DIFF_HD // 2
    m = np.zeros((2, 2, 2, half), np.float32)
    for c in range(2):
        m[c, :, c, :] = 1.0
    return _pad_halves(jnp.asarray(m.reshape(2, 2 * DIFF_HD)))


def _col_offsets():
    sizes = (MLA_Q_LORA, MLA_KV_LORA, MLA_ROPE,
             NSA_HEADS * NSA_DK, NSA_DK, NSA_DV, NSA_DK, NSA_DV, NSA_DK, NSA_DV, NSA_HEADS * 3,
             DIFF_HEADS * 2 * DIFF_HD, DIFF_HEADS * 2 * DIFF_HD, DIFF_HEADS * DIFF_VD)
    names = ("c_q", "c_kv", "k_rope", "nsa_q", "nsa_kc", "nsa_vc", "nsa_ks", "nsa_vs", "nsa_kw", "nsa_vw",
             "nsa_g", "d_q", "d_k", "d_v")
    offs = {}
    o = 0
    for nme, sz in zip(names, sizes):
        offs[nme] = (o, o + sz)
        o += sz
    offs["merge"] = (o, None)
    return offs


def _mixers(n2d, batch, seq, layer, w_in, p):
    offs = _col_offsets()
    col = lambda name: w_in[:, offs[name][0]:offs[name][1]]
    bf = lambda a: a.astype(BF16)

    n_q, n_kv = mla_latent(n2d, bf(jnp.concatenate([col("c_q"), col("c_kv")], axis=1)),
                           p["mla_q_norm"], p["mla_kv_norm"])
    cos_m, sin_m = _rope_tables(MLA_ROPE, seq, lead=MLA_NOPE)
    w_kr = jnp.concatenate([jnp.zeros((w_in.shape[0], MLA_NOPE), F32), col("k_rope")], axis=1)[None]
    kpe = proj_heads(n2d, bf(w_kr), batch, seq, rope=("weights", bf(_rot_cols(w_kr, MLA_NOPE)), cos_m, sin_m))
    w_uq = _heads(p["mla_w_uq"], MLA_HEADS, MLA_QK)
    q_mla = proj_heads(n_q, bf(w_uq), batch, seq, rope=("weights", bf(_rot_cols(w_uq, MLA_NOPE)), cos_m, sin_m),
                       scale=MLA_QK ** -0.5)
    w_ukv = _heads(p["mla_w_ukv"], MLA_HEADS, MLA_NOPE + MLA_V)
    w_uk = jnp.concatenate([w_ukv[..., :MLA_NOPE], jnp.zeros(w_ukv.shape[:2] + (MLA_ROPE,), F32)], axis=-1)
    k_mla = proj_heads(n_kv, bf(w_uk), batch, seq, add=kpe)
    vt_mla = proj_heads_t(n_kv, bf(w_ukv[..., MLA_NOPE:]), batch, seq)
    o_mla = flash_causal(q_mla, k_mla, vt_mla)

    cos_d, sin_d = _swap_tables(DIFF_HD, seq, copies=2)
    q_d = proj_heads(n2d, bf(_pair_heads(col("d_q"))), batch, seq, rope=("swap", cos_d, sin_d),
                     scale=DIFF_HD ** -0.5, masks=_pair_masks())
    k_d = proj_heads(n2d, bf(_pair_heads(col("d_k"))), batch, seq, rope=("swap", cos_d, sin_d))
    vt_d = proj_heads_t(n2d, bf(_heads(col("d_v"), DIFF_HEADS, DIFF_VD)), batch, seq)
    lambda_init = 0.8 - 0.6 * math.exp(-0.3 * layer)
    o_diff = diff_attention(q_d, k_d, vt_d, p["diff_lam_q1"], p["diff_lam_k1"], p["diff_lam_q2"],
                            p["diff_lam_k2"], p["diff_subln"], lambda_init)

    cos_n, sin_n = _swap_tables(NSA_DK, seq)
    w_nq = _pad_halves(_heads(col("nsa_q"), NSA_HEADS, NSA_DK))
    q_n = proj_heads(n2d, bf(w_nq), batch, seq, rope=("swap", cos_n, sin_n), scale=NSA_DK ** -0.5)
    w_nk = _pad_halves(jnp.stack([col("nsa_kc"), col("nsa_ks"), col("nsa_kw")], axis=0))
    k_n = proj_heads(n2d, bf(w_nk), batch, seq, rope=("swap", cos_n, sin_n))
    dk_pad = w_nk.shape[-1]
    v_cmp = proj_plain(n2d, bf(col("nsa_vc")), BF16)
    vt_n = proj_heads_t(n2d, bf(jnp.stack([col("nsa_vs"), col("nsa_vw")], axis=0)), batch, seq)
    w_g = jnp.concatenate([col("nsa_g"), jnp.zeros((w_in.shape[0], LANES - NSA_HEADS * 3), F32)], axis=1)
    g_nsa = proj_plain(n2d, bf(w_g), F32, sigmoid=True)

    ng = seq // NSA_CMP_STRIDE
    w1k = _pad_halves(p["nsa_cmp_k_w1"].reshape(NSA_CMP_LEN, NSA_DK, NSA_CMP_HIDDEN), axis=1)
    kc = nsa_compress(k_n.reshape(batch, 3, ng, NSA_CMP_STRIDE * dk_pad), 0,
                      _pad_halves(p["nsa_cmp_k_pos"]), w1k.reshape(NSA_CMP_LEN * dk_pad, NSA_CMP_HIDDEN),
                      _pad_halves(p["nsa_cmp_k_w2"]))
    vc_tok = v_cmp.reshape(batch, 1, ng, NSA_CMP_STRIDE * NSA_DV)
    vct = nsa_compress(vc_tok, 0, p["nsa_cmp_v_pos"], p["nsa_cmp_v_w1"], p["nsa_cmp_v_w2"], transposed=True)
    o_c, sel = nsa_cmp_topk(q_n, kc, vct)
    o_s = nsa_selected(q_n, k_n, 1, vt_n, sel)
    o_w = nsa_window(q_n, k_n, 2, vt_n, 1)

    g_merge = proj_plain(n2d, bf(w_in[:, offs["merge"][0]:]), BF16, sigmoid=True)
    m = batch * seq
    return (o_mla.reshape(m, -1), o_c.reshape(m, -1), o_s.reshape(m, -1), o_w.reshape(m, -1), g_nsa,
            o_diff, g_merge)


def kernel(x, ffn1_norm, ffn1_w_in, ffn1_w_out, mix_norm, w_in, mla_q_norm, mla_kv_norm, mla_w_uq, mla_w_ukv, nsa_cmp_k_pos, nsa_cmp_k_w1, nsa_cmp_k_w2, nsa_cmp_v_pos, nsa_cmp_v_w1, nsa_cmp_v_w2, diff_lam_q1, diff_lam_k1, diff_lam_q2, diff_lam_k2, diff_subln, w_br_mla, w_br_nsa, w_br_diff, w_out, ffn2_norm, ffn2_w_in, ffn2_w_out, final_norm):
    batch, seq, d = x.shape
    depth = w_in.shape[0]
    x2d = x.reshape(batch * seq, d)
    bf = lambda a: a.astype(BF16)
    for l in range(depth):
        p = {"mla_q_norm": mla_q_norm[l], "mla_kv_norm": mla_kv_norm[l], "mla_w_uq": mla_w_uq[l],
             "mla_w_ukv": mla_w_ukv[l], "nsa_cmp_k_pos": nsa_cmp_k_pos[l], "nsa_cmp_k_w1": nsa_cmp_k_w1[l],
             "nsa_cmp_k_w2": nsa_cmp_k_w2[l], "nsa_cmp_v_pos": nsa_cmp_v_pos[l], "nsa_cmp_v_w1": nsa_cmp_v_w1[l],
             "nsa_cmp_v_w2": nsa_cmp_v_w2[l], "diff_lam_q1": diff_lam_q1[l], "diff_lam_k1": diff_lam_k1[l],
             "diff_lam_q2": diff_lam_q2[l], "diff_lam_k2": diff_lam_k2[l], "diff_subln": diff_subln[l]}
        x2d, n2d = ffn(x2d, ffn1_norm[l], bf(ffn1_w_in[l]), bf(ffn1_w_out[l]), post_norm=(mix_norm[l], BF16))
        o_mla, o_c, o_s, o_w, g_nsa, o_diff, g_merge = _mixers(n2d, batch, seq, l, w_in[l], p)
        x2d = merge(x2d, o_mla, o_c, o_s, o_w, g_nsa, o_diff, g_merge, bf(w_br_mla[l]), bf(w_br_nsa[l]),
                    bf(w_br_diff[l]).reshape(DIFF_HEADS, DIFF_VD, d), bf(w_out[l]), seq)
        if l + 1 < depth:
            x2d = ffn(x2d, ffn2_norm[l], bf(ffn2_w_in[l]), bf(ffn2_w_out[l]))
        else:
            _, out = ffn(x2d, ffn2_norm[l], bf(ffn2_w_in[l]), bf(ffn2_w_out[l]), post_norm=(final_norm, F32))
    return out.reshape(batch, seq, d)
```

```python
import functools
import math

import numpy as np
import jax
import jax.numpy as jnp
from jax import lax
from jax.experimental import pallas as pl
from jax.experimental.pallas import tpu as pltpu

F32 = jnp.float32
BF16 = jnp.bfloat16

NORM_EPS = 1e-6
ROPE_THETA = 10000.0
NEG_INF = -1e30
REMOVED = -3e38

MLA_HEADS = 6
MLA_Q_LORA = 768
MLA_KV_LORA = 512
MLA_NOPE = 128
MLA_ROPE = 64
MLA_V = 128
MLA_QK = MLA_NOPE + MLA_ROPE

NSA_HEADS = 4
NSA_DK = 192
NSA_DV = 128
NSA_CMP_LEN = 32
NSA_CMP_STRIDE = 16
NSA_CMP_HIDDEN = 256
NSA_SEL_LEN = 64
NSA_TOPK = 16
NSA_WINDOW = 512
NSA_FORCE_SCORE = 1e6

DIFF_HEADS = 4
DIFF_HD = 96
DIFF_VD = 2 * DIFF_HD

LANES = 128
ONES_PAD = 16
SCORE_DTYPE = BF16

PROJ_ROWS = 1024
FUSED_ROWS = 512
COL_TILE = 512
MERGE_COLS = 1024
ATTN_TILE = 512
VMEM_LIMIT_MB = 56


def _cparams(dims, vmem_mb=VMEM_LIMIT_MB):
    return pltpu.CompilerParams(dimension_semantics=dims, vmem_limit_bytes=vmem_mb * 2**20)


def _sigmoid(x):
    return 1.0 / (1.0 + jnp.exp(-x))


def _dot(a, b):
    return jnp.dot(a, b, preferred_element_type=F32)


def _dot_nt(a, b):
    return lax.dot_general(a, b, (((1,), (1,)), ((), ())), preferred_element_type=F32)


def _tile(n, pref):
    t = min(n, pref)
    assert n % t == 0, (n, t)
    return t


def _mla_latent_kernel(n_ref, w_ref, wq_ref, wkv_ref, nq_ref, nkv_ref):
    lat = _dot(n_ref[...], w_ref[...])
    cq = lat[:, :MLA_Q_LORA]
    ckv = lat[:, MLA_Q_LORA:]
    nq = cq * lax.rsqrt(jnp.mean(cq * cq, axis=-1, keepdims=True) + NORM_EPS)
    nkv = ckv * lax.rsqrt(jnp.mean(ckv * ckv, axis=-1, keepdims=True) + NORM_EPS)
    nq_ref[...] = (nq * wq_ref[...]).astype(BF16)
    nkv_ref[...] = (nkv * wkv_ref[...]).astype(BF16)


def mla_latent(n2d, w_lat, wq, wkv):
    m, k = n2d.shape
    d = w_lat.shape[1]
    tm = _tile(m, FUSED_ROWS)
    return pl.pallas_call(
        _mla_latent_kernel,
        grid=(m // tm,),
        in_specs=[pl.BlockSpec((tm, k), lambda i: (i, 0)),
                  pl.BlockSpec((k, d), lambda i: (0, 0)),
                  pl.BlockSpec((1, MLA_Q_LORA), lambda i: (0, 0)),
                  pl.BlockSpec((1, MLA_KV_LORA), lambda i: (0, 0))],
        out_specs=[pl.BlockSpec((tm, MLA_Q_LORA), lambda i: (i, 0)),
                   pl.BlockSpec((tm, MLA_KV_LORA), lambda i: (i, 0))],
        out_shape=[jax.ShapeDtypeStruct((m, MLA_Q_LORA), BF16), jax.ShapeDtypeStruct((m, MLA_KV_LORA), BF16)],
        compiler_params=_cparams(("parallel",)),
        name="mla_latent",
    )(n2d, w_lat, wq.reshape(1, -1).astype(F32), wkv.reshape(1, -1).astype(F32))


def _ffn_kernel(x_ref, nw_ref, wg_ref, wu_ref, wo_ref, *rest, n_f, post_norm):
    if post_norm:
        pw_ref, o_ref, p_ref, n_scr = rest
    else:
        o_ref, n_scr = rest
    f = pl.program_id(1)

    @pl.when(f == 0)
    def _():
        x = x_ref[...]
        y = x * lax.rsqrt(jnp.mean(x * x, axis=-1, keepdims=True) + NORM_EPS)
        n_scr[...] = (y * nw_ref[...]).astype(BF16)
        o_ref[...] = jnp.zeros_like(o_ref)

    n = n_scr[...]
    g = _dot(n, wg_ref[...])
    u = _dot(n, wu_ref[...])
    h = (g * _sigmoid(g) * u).astype(BF16)
    o_ref[...] += _dot(h, wo_ref[...])

    @pl.when(f == n_f - 1)
    def _():
        y = x_ref[...] + 0.5 * o_ref[...]
        o_ref[...] = y
        if post_norm:
            yn = y * lax.rsqrt(jnp.mean(y * y, axis=-1, keepdims=True) + NORM_EPS)
            p_ref[...] = (yn * pw_ref[...]).astype(p_ref.dtype)


def ffn(x2d, norm_w, w_in, w_out, post_norm=None):
    m, d = x2d.shape
    f_dim = w_out.shape[0]
    tm = _tile(m, FUSED_ROWS)
    tf = _tile(f_dim, COL_TILE)
    n_f = f_dim // tf
    row = pl.BlockSpec((tm, d), lambda i, f: (i, 0))
    vec = pl.BlockSpec((1, d), lambda i, f: (0, 0))
    in_specs = [row, vec,
                pl.BlockSpec((d, tf), lambda i, f: (0, f)),
                pl.BlockSpec((d, tf), lambda i, f: (0, f + n_f)),
                pl.BlockSpec((tf, d), lambda i, f: (f, 0))]
    args = [x2d, norm_w.reshape(1, d).astype(F32), w_in, w_in, w_out]
    out_specs, out_shape = row, jax.ShapeDtypeStruct((m, d), F32)
    if post_norm is not None:
        in_specs.append(vec)
        args.append(post_norm[0].reshape(1, d).astype(F32))
        out_specs, out_shape = [row, row], [out_shape, jax.ShapeDtypeStruct((m, d), post_norm[1])]
    return pl.pallas_call(
        functools.partial(_ffn_kernel, n_f=n_f, post_norm=post_norm is not None),
        grid=(m // tm, n_f),
        in_specs=in_specs,
        out_specs=out_specs,
        out_shape=out_shape,
        scratch_shapes=[pltpu.VMEM((tm, d), BF16)],
        compiler_params=_cparams(("parallel", "arbitrary")),
        name="ffn",
    )(*args)


def _proj_plain_kernel(n_ref, w_ref, o_ref, *, sigmoid):
    y = _dot(n_ref[...], w_ref[...])
    if sigmoid:
        y = _sigmoid(y)
    o_ref[...] = y.astype(o_ref.dtype)


def proj_plain(n2d, w, out_dtype, sigmoid=False):
    m, k = n2d.shape
    n_out = w.shape[1]
    tm = _tile(m, PROJ_ROWS)
    tn = next(t for t in (2 * COL_TILE, COL_TILE, COL_TILE // 2, LANES) if n_out % t == 0)
    return pl.pallas_call(
        functools.partial(_proj_plain_kernel, sigmoid=sigmoid),
        grid=(m // tm, n_out // tn),
        in_specs=[pl.BlockSpec((tm, k), lambda i, j: (i, 0)),
                  pl.BlockSpec((k, tn), lambda i, j: (0, j))],
        out_specs=pl.BlockSpec((tm, tn), lambda i, j: (i, j)),
        out_shape=jax.ShapeDtypeStruct((m, n_out), out_dtype),
        compiler_params=_cparams(("parallel", "arbitrary")),
        name="proj_plain",
    )(n2d, w)


def _proj_heads_kernel(*refs, rope, has_add, has_masks, scale):
    n_ref, w_ref = refs[0], refs[1]
    o_ref = refs[-1]
    n = n_ref[...]
    n_copy = o_ref.shape[1] // w_ref.shape[0]
    for h in range(w_ref.shape[0]):
        pos = 2
        y = _dot(n, w_ref[h])
        if rope == "weights":
            wr_ref, cos_ref, sin_ref = refs[pos:pos + 3]
            pos += 3
            y = y * cos_ref[...] + _dot(n, wr_ref[h]) * sin_ref[...]
        elif rope == "swap":
            cos_ref, sin_ref = refs[pos:pos + 2]
            pos += 2
            half = y.shape[1] // 2
            y = y * cos_ref[...] + jnp.concatenate([y[:, half:], y[:, :half]], axis=1) * sin_ref[...]
        if has_add:
            y = y + refs[pos][0, 0].astype(F32)
            pos += 1
        if scale != 1.0:
            y = y * scale
        if has_masks:
            mask_ref = refs[pos]
            for c in range(n_copy):
                o_ref[0, h * n_copy + c] = (y * mask_ref[c:c + 1, :]).astype(o_ref.dtype)
        else:
            o_ref[0, h] = y.astype(o_ref.dtype)


def proj_heads(n2d, w, batch, seq, rope=None, add=None, scale=1.0, masks=None):
    m, k = n2d.shape
    g, _, dh = w.shape
    tm = _tile(seq, PROJ_ROWS)
    ns = seq // tm
    weights = pl.BlockSpec((g, k, dh), lambda b, i: (0, 0, 0))
    in_specs = [pl.BlockSpec((tm, k), lambda b, i: (b * ns + i, 0)), weights]
    args = [n2d, w]
    table = pl.BlockSpec((tm, dh), lambda b, i: (i, 0))
    if rope is not None and rope[0] == "weights":
        in_specs += [weights, table, table]
        args += list(rope[1:])
    elif rope is not None:
        assert rope[0] == "swap" and dh % (2 * LANES) == 0
        in_specs += [table, table]
        args += list(rope[1:])
    if add is not None:
        in_specs.append(pl.BlockSpec((1, 1, tm, dh), lambda b, i: (b, 0, i, 0)))
        args.append(add)
    n_copy = 1
    if masks is not None:
        n_copy = masks.shape[0]
        in_specs.append(pl.BlockSpec((n_copy, dh), lambda b, i: (0, 0)))
        args.append(masks)
    return pl.pallas_call(
        functools.partial(_proj_heads_kernel, rope=None if rope is None else rope[0], has_add=add is not None,
                          has_masks=masks is not None, scale=scale),
        grid=(batch, ns),
        in_specs=in_specs,
        out_specs=pl.BlockSpec((1, g * n_copy, tm, dh), lambda b, i: (b, 0, i, 0)),
        out_shape=jax.ShapeDtypeStruct((batch, g * n_copy, seq, dh), BF16),
        compiler_params=_cparams(("parallel", "parallel")),
        name="proj_heads",
    )(*args)


def _proj_heads_t_kernel(n_ref, wt_ref, o_ref):
    dh = wt_ref.shape[1]
    pad = o_ref.shape[2] - dh
    row = lax.broadcasted_iota(jnp.int32, (pad, o_ref.shape[3]), 0)
    ones_rows = jnp.where(row == 0, 1.0, 0.0).astype(o_ref.dtype)
    n = n_ref[...]
    for h in range(wt_ref.shape[0]):
        o_ref[0, h, :dh, :] = _dot_nt(wt_ref[h], n).astype(o_ref.dtype)
        o_ref[0, h, dh:, :] = ones_rows


def proj_heads_t(n2d, w, batch, seq):
    m, k = n2d.shape
    g, _, dh = w.shape
    tm = _tile(seq, PROJ_ROWS)
    ns = seq // tm
    return pl.pallas_call(
        _proj_heads_t_kernel,
        grid=(batch, ns),
        in_specs=[pl.BlockSpec((tm, k), lambda b, i: (b * ns + i, 0)),
                  pl.BlockSpec((g, dh, k), lambda b, i: (0, 0, 0))],
        out_specs=pl.BlockSpec((1, g, dh + ONES_PAD, tm), lambda b, i: (b, 0, 0, i)),
        out_shape=jax.ShapeDtypeStruct((batch, g, dh + ONES_PAD, seq), BF16),
        compiler_params=_cparams(("parallel", "parallel")),
        name="proj_heads_t",
    )(n2d, jnp.swapaxes(w, 1, 2))


def _softmax_step_t(st, vt, m_scr, acc_scr, exp_dtype):
    m_prev = m_scr[...]
    m_new = jnp.maximum(m_prev, jnp.max(st, axis=0, keepdims=True).astype(F32))
    alpha = jnp.exp(m_prev - m_new)
    pt = jnp.exp((st - m_new.astype(st.dtype)).astype(exp_dtype)).astype(BF16)
    acc_scr[...] = alpha * acc_scr[...] + _dot(vt, pt)
    m_scr[...] = m_new


def _diag_visibility(d, c, tq, tk):
    if d * tk >= (c + 1) * tq:
        return "none"
    if (d + 1) * tk - 1 <= c * tq:
        return "full"
    return "partial"


def _causal_where(st, d, c):
    tk, tq = st.shape
    key = lax.broadcasted_iota(jnp.int32, (tk, tq), 0) + d * tk
    qry = lax.broadcasted_iota(jnp.int32, (tk, tq), 1) + c * tq
    return jnp.where(key <= qry, st, NEG_INF)


def _flash_sweep(n, r, sub_of, tq, tk, scores, vt_tile, m_scr, acc_scr, st_scr, exp_dtype=F32):
    n_chain = m_scr.shape[0]
    for c in range(n_chain):
        m_scr[c] = jnp.full(m_scr.shape[1:], NEG_INF, F32)
        acc_scr[c] = jnp.zeros(acc_scr.shape[1:], F32)
    n_slot = st_scr.shape[0]
    assert r % n_slot == 0
    everyone = (True,) * n_chain
    for c, st in enumerate(scores(0, everyone)):
        st_scr[0, c] = st.astype(st_scr.dtype)

    def body(jj, carry):
        for slot in range(n_slot):
            j = jj * n_slot + slot
            nxt = scores(j + 1, everyone)
            vt = vt_tile(j)
            for c in range(n_chain):
                st = st_scr[slot, c]
                st_scr[(slot + 1) % n_slot, c] = nxt[c].astype(st_scr.dtype)
                _softmax_step_t(st, vt, m_scr.at[c], acc_scr.at[c], exp_dtype)
        return carry

    lax.fori_loop(0, n // n_slot, body, 0)
    for d in range(r):
        vis = [_diag_visibility(d, sub_of(c), tq, tk) for c in range(n_chain)]
        need = tuple(d + 1 < r and _diag_visibility(d + 1, sub_of(c), tq, tk) != "none" for c in range(n_chain))
        nxt = scores(n + d + 1, need) if any(need) else None
        vt = vt_tile(n + d)
        for c in range(n_chain):
            if vis[c] != "none":
                st = st_scr[d % n_slot, c]
                if vis[c] == "partial":
                    st = _causal_where(st.astype(F32), d, sub_of(c)).astype(st_scr.dtype)
            if need[c]:
                st_scr[(d + 1) % n_slot, c] = nxt[c].astype(st_scr.dtype)
            if vis[c] != "none":
                _softmax_step_t(st, vt, m_scr.at[c], acc_scr.at[c], exp_dtype)


def _flash_kernel(q_ref, k_ref, vt_ref, o_ref, m_scr, acc_scr, st_scr, *, n_sub, tq, tk):
    i = pl.program_id(2)
    dv = o_ref.shape[2]
    r = n_sub * tq // tk

    def scores(j, need):
        k = k_ref[0, 0, pl.ds(pl.multiple_of(j * tk, tk), tk), :]
        return [_dot_nt(k, q_ref[0, 0, c * tq:(c + 1) * tq, :]) if need[c] else None for c in range(n_sub)]

    def vt_tile(j):
        return vt_ref[0, 0, :, pl.ds(pl.multiple_of(j * tk, tk), tk)]

    _flash_sweep(i * r, r, lambda c: c, tq, tk, scores, vt_tile, m_scr, acc_scr, st_scr, exp_dtype=BF16)
    for c in range(n_sub):
        o_t = acc_scr[c, :dv, :] / acc_scr[c, dv:dv + 1, :]
        o_ref[0, c * tq:(c + 1) * tq, :] = o_t.T.astype(o_ref.dtype)


def _score_slots(r):
    return 2 if r % 2 == 0 else 1


def _flash_tiles(s, max_sub):
    tq = _tile(s, ATTN_TILE)
    n_sub = next(n for n in (4, 2, 1) if n <= max_sub and s % (n * tq) == 0)
    return n_sub, tq, tq


def flash_causal(q, k, vt):
    b, h, s, dk = q.shape
    dve = vt.shape[2]
    dv = dve - ONES_PAD
    n_sub, tq, tk = _flash_tiles(s, max_sub=4)
    t = n_sub * tq
    return pl.pallas_call(
        functools.partial(_flash_kernel, n_sub=n_sub, tq=tq, tk=tk),
        grid=(b, h, s // t),
        in_specs=[pl.BlockSpec((1, 1, t, dk), lambda b_, h_, i: (b_, h_, i, 0)),
                  pl.BlockSpec((1, 1, s, dk), lambda b_, h_, i: (b_, h_, 0, 0)),
                  pl.BlockSpec((1, 1, dve, s), lambda b_, h_, i: (b_, h_, 0, 0))],
        out_specs=pl.BlockSpec((1, t, dv), lambda b_, h_, i: (b_, i, h_)),
        out_shape=jax.ShapeDtypeStruct((b, s, h * dv), BF16),
        scratch_shapes=[pltpu.VMEM((n_sub, 1, tq), F32), pltpu.VMEM((n_sub, dve, tq), F32),
                        pltpu.VMEM((_score_slots(n_sub * tq // tk), n_sub, tk, tq), SCORE_DTYPE)],
        compiler_params=_cparams(("parallel", "parallel", "arbitrary")),
        name="mla_flash",
    )(q, k, vt)


def _diff_kernel(q_ref, k_ref, vt_ref, lq1_ref, lk1_ref, lq2_ref, lk2_ref, sub_ref, o_ref,
                 m_scr, acc_scr, st_scr, *, n_sub, tq, tk, lambda_init):
    i = pl.program_id(2)
    vd = o_ref.shape[3]
    r = n_sub * tq // tk

    def scores(j, need):
        k = k_ref[0, 0, pl.ds(pl.multiple_of(j * tk, tk), tk), :]
        return [_dot_nt(k, q_ref[0, ch // n_sub, (ch % n_sub) * tq:(ch % n_sub + 1) * tq, :]) if need[ch] else None
                for ch in range(2 * n_sub)]

    def vt_tile(j):
        return vt_ref[0, 0, :, pl.ds(pl.multiple_of(j * tk, tk), tk)]

    _flash_sweep(i * r, r, lambda ch: ch % n_sub, tq, tk, scores, vt_tile, m_scr, acc_scr, st_scr, exp_dtype=BF16)

    lam = (jnp.exp(jnp.sum(lq1_ref[...] * lk1_ref[...], axis=-1, keepdims=True))
           - jnp.exp(jnp.sum(lq2_ref[...] * lk2_ref[...], axis=-1, keepdims=True)) + lambda_init)
    lane_pad = (-vd) % LANES
    for c in range(n_sub):
        o_t = (acc_scr[c, :vd, :] / acc_scr[c, vd:vd + 1, :]
               - lam * (acc_scr[n_sub + c, :vd, :] / acc_scr[n_sub + c, vd:vd + 1, :]))
        o_t = o_t * lax.rsqrt(jnp.mean(o_t * o_t, axis=0, keepdims=True) + NORM_EPS)
        o_t = jnp.concatenate([o_t, jnp.zeros((lane_pad, tq), F32)], axis=0)
        o_ref[0, 0, c * tq:(c + 1) * tq, :] = (
            o_t.T[:, :vd] * sub_ref[...] * (1.0 - lambda_init)).astype(o_ref.dtype)


def diff_attention(q, k, vt, lq1, lk1, lq2, lk2, subln, lambda_init):
    b, h2, s, dk = q.shape
    h = h2 // 2
    hd = DIFF_HD
    vde = vt.shape[2]
    vd = vde - ONES_PAD
    n_sub, tq, tk = _flash_tiles(s, max_sub=4)
    t = n_sub * tq
    vec = lambda a: a.reshape(1, -1).astype(F32)
    small = lambda n: pl.BlockSpec((1, n), lambda b_, h_, i: (0, 0))
    return pl.pallas_call(
        functools.partial(_diff_kernel, n_sub=n_sub, tq=tq, tk=tk, lambda_init=lambda_init),
        grid=(b, h, s // t),
        in_specs=[pl.BlockSpec((1, 2, t, dk), lambda b_, h_, i: (b_, h_, i, 0)),
                  pl.BlockSpec((1, 1, s, dk), lambda b_, h_, i: (b_, h_, 0, 0)),
                  pl.BlockSpec((1, 1, vde, s), lambda b_, h_, i: (b_, h_, 0, 0)),
                  small(hd), small(hd), small(hd), small(hd), small(vd)],
        out_specs=pl.BlockSpec((1, 1, t, vd), lambda b_, h_, i: (b_, h_, i, 0)),
        out_shape=jax.ShapeDtypeStruct((b, h, s, vd), BF16),
        scratch_shapes=[pltpu.VMEM((2 * n_sub, 1, tq), F32), pltpu.VMEM((2 * n_sub, vde, tq), F32),
                        pltpu.VMEM((_score_slots(n_sub * tq // tk), 2 * n_sub, tk, tq), SCORE_DTYPE)],
        compiler_params=_cparams(("parallel", "parallel", "arbitrary")),
        name="diff_flash",
    )(q, k, vt, vec(lq1), vec(lk1), vec(lq2), vec(lk2), vec(subln))


def _compress_kernel(a_ref, pa_ref, pb_ref, w1a_ref, w1b_ref, w2_ref, o_ref, *, transposed):
    ng = a_ref.shape[2]
    rows = min(ng, 256)
    us, vs = [], []
    for r0 in range(0, ng, rows):
        a = a_ref[0, 0, r0:r0 + rows, :].astype(F32)
        us.append(_dot((a + pa_ref[...]).astype(BF16), w1a_ref[...]))
        vs.append(_dot((a + pb_ref[...]).astype(BF16), w1b_ref[...]))
    u = jnp.concatenate(us, axis=0)
    v = jnp.concatenate(vs, axis=0)
    hdn = u + pltpu.roll(v, ng - 1, 0)
    hdn = (hdn * _sigmoid(hdn)).astype(BF16)
    if transposed:
        d_out = w2_ref.shape[0]
        o_ref[0, :d_out, :] = _dot_nt(w2_ref[...], hdn).astype(o_ref.dtype)
        row = lax.broadcasted_iota(jnp.int32, (o_ref.shape[1] - d_out, ng), 0)
        o_ref[0, d_out:, :] = jnp.where(row == 0, 1.0, 0.0).astype(o_ref.dtype)
    else:
        o_ref[0] = _dot(hdn, w2_ref[...]).astype(o_ref.dtype)


def nsa_compress(tok, g_idx, pos, w1, w2, transposed=False):
    assert NSA_CMP_LEN == 2 * NSA_CMP_STRIDE
    b, _, ng, wd = tok.shape
    d = wd // NSA_CMP_STRIDE
    d_out = w2.shape[1]
    pos_flat = pos.astype(F32).reshape(1, NSA_CMP_LEN * d)
    w1 = w1.astype(BF16)
    w2 = w2.astype(BF16).T if transposed else w2.astype(BF16)
    out_block = (1, d_out + ONES_PAD, ng) if transposed else (1, ng, d_out)
    full = lambda shape: pl.BlockSpec(shape, lambda b_: (0,) * len(shape))
    return pl.pallas_call(
        functools.partial(_compress_kernel, transposed=transposed),
        grid=(b,),
        in_specs=[pl.BlockSpec((1, 1, ng, wd), lambda b_: (b_, g_idx, 0, 0)),
                  full((1, wd)), full((1, wd)), full((wd, NSA_CMP_HIDDEN)), full((wd, NSA_CMP_HIDDEN)),
                  full(w2.shape)],
        out_specs=pl.BlockSpec(out_block, lambda b_: (b_, 0, 0)),
        out_shape=jax.ShapeDtypeStruct((b,) + out_block[1:], BF16),
        compiler_params=_cparams(("parallel",)),
        name="nsa_compress",
    )(tok, pos_flat[:, :wd], pos_flat[:, wd:], w1[:wd], w1[wd:], w2)


def _split3(x):
    hi = x.astype(BF16)
    r = x - hi.astype(F32)
    mid = r.astype(BF16)
    lo = (r - mid.astype(F32)).astype(BF16)
    return hi, mid, lo


def _cmp_topk_kernel(q_ref, kc_ref, vct_ref, wselt_ref, oc_ref, selt_ref, *, tq, k_top):
    i = pl.program_id(1)
    qs = i * tq
    nc = kc_ref.shape[1]
    n_sel = wselt_ref.shape[0]
    kc = kc_ref[0]
    vct = vct_ref[0]
    qpos_c = qs + lax.broadcasted_iota(jnp.int32, (nc, tq), 1)
    cend = lax.broadcasted_iota(jnp.int32, (nc, tq), 0) * NSA_CMP_STRIDE + (NSA_CMP_LEN - 1)
    valid_c = cend <= qpos_c
    any_valid = qs + lax.broadcasted_iota(jnp.int32, (1, tq), 1) >= NSA_CMP_LEN - 1
    imp_t = jnp.zeros((nc, tq), F32)
    for h in range(NSA_HEADS):
        st = jnp.where(valid_c, _dot_nt(kc, q_ref[0, h]), NEG_INF)
        e = jnp.exp(st - jnp.max(st, axis=0, keepdims=True))
        inv = jnp.where(any_valid, 1.0 / jnp.sum(e, axis=0, keepdims=True), 0.0)
        o_t = _dot(vct, e.astype(BF16))[:NSA_DV, :] * inv
        oc_ref[0, :, h * NSA_DV:(h + 1) * NSA_DV] = o_t.T.astype(oc_ref.dtype)
        imp_t = imp_t + e * inv

    wselt = wselt_ref[...]
    hi, mid, lo = _split3(imp_t)
    imp_sel = _dot(wselt, hi) + _dot(wselt, mid) + _dot(wselt, lo)

    blk = lax.broadcasted_iota(jnp.int32, (n_sel, tq), 0)
    qpos = qs + lax.broadcasted_iota(jnp.int32, (n_sel, tq), 1)
    cur = lax.shift_right_arithmetic(qpos, int(math.log2(NSA_SEL_LEN)))
    forced = (blk == 0) | (blk == cur) | (blk == cur - 1)
    valid_s = blk * NSA_SEL_LEN <= qpos
    score = jnp.where(valid_s, jnp.where(forced, NSA_FORCE_SCORE, imp_sel), NEG_INF)
    blk_f = blk.astype(F32)
    for _ in range(k_top):
        mx = jnp.max(score, axis=0, keepdims=True)
        first = jnp.min(jnp.where(score == mx, blk_f, float(n_sel)), axis=0, keepdims=True)
        score = jnp.where(blk_f == first, REMOVED, score)
    selt_ref[0] = jnp.where(score == REMOVED, 1.0, 0.0).astype(selt_ref.dtype)


def _sel_weight_matrix(n_pad, n_sel):
    r_c = NSA_CMP_LEN // NSA_CMP_STRIDE
    ratio = NSA_SEL_LEN // NSA_CMP_STRIDE
    overlap_w = [max(0, min(o * NSA_CMP_STRIDE + NSA_CMP_LEN, NSA_SEL_LEN) - max(o * NSA_CMP_STRIDE, 0))
                 / NSA_CMP_STRIDE for o in range(-(r_c - 1), ratio)]
    w = np.zeros((n_pad, n_sel), np.float32)
    for n in range(n_sel):
        for u, w_u in enumerate(overlap_w):
            c = ratio * n + u - (r_c - 1)
            if 0 <= c < n_pad:
                w[c, n] = w_u
    return w


def nsa_cmp_topk(q, kc, vct):
    b, h, s, dk = q.shape
    nc = kc.shape[1]
    dve = vct.shape[1]
    n_sel = s // NSA_SEL_LEN
    assert NSA_SEL_LEN & (NSA_SEL_LEN - 1) == 0
    tq = _tile(s, ATTN_TILE)
    wselt = jnp.asarray(_sel_weight_matrix(nc, n_sel).T, BF16)
    return pl.pallas_call(
        functools.partial(_cmp_topk_kernel, tq=tq, k_top=min(NSA_TOPK, n_sel)),
        grid=(b, s // tq),
        in_specs=[pl.BlockSpec((1, h, tq, dk), lambda b_, i: (b_, 0, i, 0)),
                  pl.BlockSpec((1, nc, dk), lambda b_, i: (b_, 0, 0)),
                  pl.BlockSpec((1, dve, nc), lambda b_, i: (b_, 0, 0)),
                  pl.BlockSpec((n_sel, nc), lambda b_, i: (0, 0))],
        out_specs=[pl.BlockSpec((1, tq, h * NSA_DV), lambda b_, i: (b_, i, 0)),
                   pl.BlockSpec((1, n_sel, tq), lambda b_, i: (b_, 0, i))],
        out_shape=[jax.ShapeDtypeStruct((b, s, h * NSA_DV), BF16), jax.ShapeDtypeStruct((b, n_sel, s), F32)],
        compiler_params=_cparams(("parallel", "parallel")),
        name="nsa_cmp_topk",
    )(q, kc, vct, wselt)


def _nsa_sel_kernel(q_ref, k_ref, vt_ref, sel_ref, o_ref, m_scr, acc_scr, st_scr, *, n_sub, tq, tk):
    i = pl.program_id(1)
    bpt = tk // NSA_SEL_LEN
    r = n_sub * tq // tk

    def scores(j, need):
        k = k_ref[0, 0, pl.ds(pl.multiple_of(j * tk, tk), tk), :]
        out = []
        for c in range(n_sub):
            rows = slice(c * tq, (c + 1) * tq)
            if not any(need[c * NSA_HEADS:(c + 1) * NSA_HEADS]):
                out += [None] * NSA_HEADS
                continue
            flags = sel_ref[0, pl.ds(pl.multiple_of(j * bpt, bpt), bpt), rows]
            keep_t = jnp.broadcast_to(flags[:, None, :], (bpt, NSA_SEL_LEN, tq)).reshape(tk, tq) > 0.5
            out += [jnp.where(keep_t, _dot_nt(k, q_ref[0, h, rows, :]), NEG_INF) if need[c * NSA_HEADS + h] else None
                    for h in range(NSA_HEADS)]
        return out

    def vt_tile(j):
        return vt_ref[0, 0, :, pl.ds(pl.multiple_of(j * tk, tk), tk)]

    _flash_sweep(i * r, r, lambda ch: ch // NSA_HEADS, tq, tk, scores, vt_tile, m_scr, acc_scr, st_scr,
                 exp_dtype=BF16)
    for ch in range(n_sub * NSA_HEADS):
        c, h = divmod(ch, NSA_HEADS)
        o_t = acc_scr[ch, :NSA_DV, :] / acc_scr[ch, NSA_DV:NSA_DV + 1, :]
        o_ref[0, c * tq:(c + 1) * tq, h * NSA_DV:(h + 1) * NSA_DV] = o_t.T.astype(o_ref.dtype)


def _nsa_sel_tiles(s):
    return _flash_tiles(s, max_sub=2)


def nsa_selected(q, k3, k_idx, vt, sel):
    b, h, s, dk = q.shape
    n_sel = sel.shape[1]
    dve = vt.shape[2]
    n_sub, tq, tk = _nsa_sel_tiles(s)
    t = n_sub * tq
    n_chain = n_sub * h
    assert tk % NSA_SEL_LEN == 0 and t % tk == 0
    return pl.pallas_call(
        functools.partial(_nsa_sel_kernel, n_sub=n_sub, tq=tq, tk=tk),
        grid=(b, s // t),
        in_specs=[pl.BlockSpec((1, h, t, dk), lambda b_, i: (b_, 0, i, 0)),
                  pl.BlockSpec((1, 1, s, dk), lambda b_, i: (b_, k_idx, 0, 0)),
                  pl.BlockSpec((1, 1, dve, s), lambda b_, i: (b_, 0, 0, 0)),
                  pl.BlockSpec((1, n_sel, t), lambda b_, i: (b_, 0, i))],
        out_specs=pl.BlockSpec((1, t, h * NSA_DV), lambda b_, i: (b_, i, 0)),
        out_shape=jax.ShapeDtypeStruct((b, s, h * NSA_DV), BF16),
        scratch_shapes=[pltpu.VMEM((n_chain, 1, tq), F32), pltpu.VMEM((n_chain, dve, tq), F32),
                        pltpu.VMEM((_score_slots(t // tk), n_chain, tk, tq), SCORE_DTYPE)],
        compiler_params=_cparams(("parallel", "arbitrary")),
        name="nsa_selected",
    )(q, k3, vt, sel)


def _nsa_win_kernel(q_ref, kp_ref, kc_ref, vtp_ref, vtc_ref, o_ref, *, t):
    i = pl.program_id(1)
    key = lax.broadcasted_iota(jnp.int32, (t, t), 0)
    qry = lax.broadcasted_iota(jnp.int32, (t, t), 1)
    keep_prev = (key > qry) & (i > 0)
    keep_cur = key <= qry
    kp, kc, vtp, vtc = kp_ref[0, 0], kc_ref[0, 0], vtp_ref[0, 0], vtc_ref[0, 0]
    for h in range(NSA_HEADS):
        q = q_ref[0, h]
        sp = jnp.where(keep_prev, _dot_nt(kp, q), NEG_INF)
        sc = jnp.where(keep_cur, _dot_nt(kc, q), NEG_INF)
        m = jnp.maximum(jnp.max(sp, axis=0, keepdims=True), jnp.max(sc, axis=0, keepdims=True))
        pp = jnp.exp((sp - m).astype(SCORE_DTYPE)).astype(BF16)
        pc = jnp.exp((sc - m).astype(SCORE_DTYPE)).astype(BF16)
        acc = _dot(vtp, pp) + _dot(vtc, pc)
        o_t = acc[:NSA_DV, :] / acc[NSA_DV:NSA_DV + 1, :]
        o_ref[0, :, h * NSA_DV:(h + 1) * NSA_DV] = o_t.T.astype(o_ref.dtype)


def nsa_window(q, k3, k_idx, vt, v_idx):
    b, h, s, dk = q.shape
    dve = vt.shape[2]
    t = _tile(s, NSA_WINDOW)
    assert t == NSA_WINDOW, "window kernel needs the query tile to equal the window"
    prev = lambda i: jnp.maximum(i - 1, 0)
    return pl.pallas_call(
        functools.partial(_nsa_win_kernel, t=t),
        grid=(b, s // t),
        in_specs=[pl.BlockSpec((1, h, t, dk), lambda b_, i: (b_, 0, i, 0)),
                  pl.BlockSpec((1, 1, t, dk), lambda b_, i: (b_, k_idx, prev(i), 0)),
                  pl.BlockSpec((1, 1, t, dk), lambda b_, i: (b_, k_idx, i, 0)),
                  pl.BlockSpec((1, 1, dve, t), lambda b_, i: (b_, v_idx, 0, prev(i))),
                  pl.BlockSpec((1, 1, dve, t), lambda b_, i: (b_, v_idx, 0, i))],
        out_specs=pl.BlockSpec((1, t, h * NSA_DV), lambda b_, i: (b_, i, 0)),
        out_shape=jax.ShapeDtypeStruct((b, s, h * NSA_DV), BF16),
        compiler_params=_cparams(("parallel", "parallel")),
        name="nsa_window",
    )(q, k3, k3, vt, vt)


def _merge_kernel(x_ref, om_ref, oc_ref, os_ref, ow_ref, gn_ref, od_ref, gm_ref, gs_ref, gd_ref,
                  wm_ref, wn_ref, wd_ref, wo_ref, ex_ref, o_ref, onsa_scr, *, n_j):
    j = pl.program_id(1)
    nsa_out = NSA_HEADS * NSA_DV

    @pl.when(j == 0)
    def _():
        g = gn_ref[...]
        hi = g.astype(BF16)
        lo = (g - hi.astype(F32)).astype(BF16)
        ge = _dot(hi, ex_ref[...]) + _dot(lo, ex_ref[...])
        onsa = (ge[:, :nsa_out] * oc_ref[...] + ge[:, nsa_out:2 * nsa_out] * os_ref[...]
                + ge[:, 2 * nsa_out:] * ow_ref[...])
        onsa_scr[...] = onsa.astype(BF16)
        o_ref[...] = jnp.zeros_like(o_ref)

    ym = _dot(om_ref[...], wm_ref[...])
    yn = _dot(onsa_scr[...], wn_ref[...])
    yd = _dot(od_ref[0, 0], wd_ref[0])
    for h in range(1, DIFF_HEADS):
        yd = yd + _dot(od_ref[0, h], wd_ref[h])
    mixed = (gm_ref[...].astype(F32) * ym + gs_ref[...].astype(F32) * yn + gd_ref[...].astype(F32) * yd)
    o_ref[...] += _dot(mixed.astype(BF16), wo_ref[...])

    @pl.when(j == n_j - 1)
    def _():
        o_ref[...] = x_ref[...] + o_ref[...]


def _gate_expand_matrix():
    nsa_out = NSA_HEADS * NSA_DV
    e = np.zeros((LANES, 3 * nsa_out), np.float32)
    for h in range(NSA_HEADS):
        for c in range(3):
            e[h * 3 + c, c * nsa_out + h * NSA_DV:c * nsa_out + (h + 1) * NSA_DV] = 1.0
    return e


def merge(x2d, o_mla, o_c, o_s, o_w, g_nsa, o_diff, g_merge, w_br_mla, w_br_nsa, w_br_diff, w_out, seq):
    m, d = x2d.shape
    tm = _tile(seq, FUSED_ROWS)
    ns = seq // tm
    tn = _tile(d, MERGE_COLS)
    n_j = d // tn
    nsa_out = NSA_HEADS * NSA_DV
    mla_out = o_mla.shape[-1]
    ex = jnp.asarray(_gate_expand_matrix(), BF16)
    row = lambda w: pl.BlockSpec((tm, w), lambda i, j: (i, 0))
    return pl.pallas_call(
        functools.partial(_merge_kernel, n_j=n_j),
        grid=(m // tm, n_j),
        in_specs=[row(d), row(mla_out), row(nsa_out), row(nsa_out), row(nsa_out), row(LANES),
                  pl.BlockSpec((1, DIFF_HEADS, tm, DIFF_VD), lambda i, j: (i // ns, 0, i % ns, 0)),
                  pl.BlockSpec((tm, tn), lambda i, j: (i, j)),
                  pl.BlockSpec((tm, tn), lambda i, j: (i, n_j + j)),
                  pl.BlockSpec((tm, tn), lambda i, j: (i, 2 * n_j + j)),
                  pl.BlockSpec((mla_out, tn), lambda i, j: (0, j)),
                  pl.BlockSpec((nsa_out, tn), lambda i, j: (0, j)),
                  pl.BlockSpec((DIFF_HEADS, DIFF_VD, tn), lambda i, j: (0, 0, j)),
                  pl.BlockSpec((tn, d), lambda i, j: (j, 0)),
                  pl.BlockSpec((LANES, 3 * nsa_out), lambda i, j: (0, 0))],
        out_specs=pl.BlockSpec((tm, d), lambda i, j: (i, 0)),
        out_shape=jax.ShapeDtypeStruct((m, d), F32),
        scratch_shapes=[pltpu.VMEM((tm, nsa_out), BF16)],
        compiler_params=_cparams(("parallel", "arbitrary")),
        name="merge",
    )(x2d, o_mla, o_c, o_s, o_w, g_nsa, o_diff, g_merge, g_merge, g_merge,
      w_br_mla, w_br_nsa, w_br_diff, w_out, ex)


def _rope_tables(dim, seq, lead=0):
    inv = ROPE_THETA ** (-jnp.arange(0, dim, 2, dtype=F32) / dim)
    ang = jnp.arange(seq, dtype=F32)[:, None] * inv[None, :]
    cos, sin = jnp.cos(ang), jnp.sin(ang)
    cos = jnp.concatenate([jnp.ones((seq, lead), F32), cos, cos], axis=-1)
    sin = jnp.concatenate([jnp.zeros((seq, lead), F32), sin, sin], axis=-1)
    return cos, sin


def _heads(w, g, dh):
    return w.reshape(w.shape[0], g, dh).transpose(1, 0, 2)


def _rot_cols(w, lead=0):
    half = (w.shape[-1] - lead) // 2
    x1 = w[..., lead:lead + half]
    x2 = w[..., lead + half:]
    return jnp.concatenate([jnp.zeros_like(w[..., :lead]), -x2, x1], axis=-1)


def _pad_halves(w, axis=-1):
    w = jnp.moveaxis(w, axis, -1)
    half = w.shape[-1] // 2
    z = jnp.zeros(w.shape[:-1] + ((-half) % LANES,), w.dtype)
    out = jnp.concatenate([w[..., :half], z, w[..., half:], z], axis=-1)
    return jnp.moveaxis(out, -1, axis)


def _swap_tables(dim, seq, copies=1):
    inv = ROPE_THETA ** (-jnp.arange(0, dim, 2, dtype=F32) / dim)
    ang = jnp.arange(seq, dtype=F32)[:, None] * inv[None, :]
    cos = jnp.tile(jnp.cos(ang), (1, copies))
    sin = jnp.tile(jnp.sin(ang), (1, copies))
    return _pad_halves(jnp.concatenate([cos, cos], axis=-1)), _pad_halves(jnp.concatenate([-sin, sin], axis=-1))


def _pair_heads(w):
    k = w.shape[0]
    half = DIFF_HD // 2
    w = w.reshape(k, DIFF_HEADS, 2, 2, half)
    w = w.transpose(1, 0, 3, 2, 4).reshape(DIFF_HEADS, k, 2 * DIFF_HD)
    return _pad_halves(w)


def _pair_masks():
    half = DIFF_HD // 2
    m = np.zeros((2, 2, 2, half), np.float32)
    for c in range(2):
        m[c, :, c, :] = 1.0
    return _pad_halves(jnp.asarray(m.reshape(2, 2 * DIFF_HD)))


def _col_offsets():
    sizes = (MLA_Q_LORA, MLA_KV_LORA, MLA_ROPE,
             NSA_HEADS * NSA_DK, NSA_DK, NSA_DV, NSA_DK, NSA_DV, NSA_DK, NSA_DV, NSA_HEADS * 3,
             DIFF_HEADS * 2 * DIFF_HD, DIFF_HEADS * 2 * DIFF_HD, DIFF_HEADS * DIFF_VD)
    names = ("c_q", "c_kv", "k_rope", "nsa_q", "nsa_kc", "nsa_vc", "nsa_ks", "nsa_vs", "nsa_kw", "nsa_vw",
             "nsa_g", "d_q", "d_k", "d_v")
    offs = {}
    o = 0
    for nme, sz in zip(names, sizes):
        offs[nme] = (o, o + sz)
        o += sz
    offs["merge"] = (o, None)
    return offs


def _mixers(n2d, batch, seq, layer, w_in, p):
    offs = _col_offsets()
    col = lambda name: w_in[:, offs[name][0]:offs[name][1]]
    bf = lambda a: a.astype(BF16)

    n_q, n_kv = mla_latent(n2d, bf(jnp.concatenate([col("c_q"), col("c_kv")], axis=1)),
                           p["mla_q_norm"], p["mla_kv_norm"])
    cos_m, sin_m = _rope_tables(MLA_ROPE, seq, lead=MLA_NOPE)
    w_kr = jnp.concatenate([jnp.zeros((w_in.shape[0], MLA_NOPE), F32), col("k_rope")], axis=1)[None]
    kpe = proj_heads(n2d, bf(w_kr), batch, seq, rope=("weights", bf(_rot_cols(w_kr, MLA_NOPE)), cos_m, sin_m))
    w_uq = _heads(p["mla_w_uq"], MLA_HEADS, MLA_QK)
    q_mla = proj_heads(n_q, bf(w_uq), batch, seq, rope=("weights", bf(_rot_cols(w_uq, MLA_NOPE)), cos_m, sin_m),
                       scale=MLA_QK ** -0.5)
    w_ukv = _heads(p["mla_w_ukv"], MLA_HEADS, MLA_NOPE + MLA_V)
    w_uk = jnp.concatenate([w_ukv[..., :MLA_NOPE], jnp.zeros(w_ukv.shape[:2] + (MLA_ROPE,), F32)], axis=-1)
    k_mla = proj_heads(n_kv, bf(w_uk), batch, seq, add=kpe)
    vt_mla = proj_heads_t(n_kv, bf(w_ukv[..., MLA_NOPE:]), batch, seq)
    o_mla = flash_causal(q_mla, k_mla, vt_mla)

    cos_d, sin_d = _swap_tables(DIFF_HD, seq, copies=2)
    q_d = proj_heads(n2d, bf(_pair_heads(col("d_q"))), batch, seq, rope=("swap", cos_d, sin_d),
                     scale=DIFF_HD ** -0.5, masks=_pair_masks())
    k_d = proj_heads(n2d, bf(_pair_heads(col("d_k"))), batch, seq, rope=("swap", cos_d, sin_d))
    vt_d = proj_heads_t(n2d, bf(_heads(col("d_v"), DIFF_HEADS, DIFF_VD)), batch, seq)
    lambda_init = 0.8 - 0.6 * math.exp(-0.3 * layer)
    o_diff = diff_attention(q_d, k_d, vt_d, p["diff_lam_q1"], p["diff_lam_k1"], p["diff_lam_q2"],
                            p["diff_lam_k2"], p["diff_subln"], lambda_init)

    cos_n, sin_n = _swap_tables(NSA_DK, seq)
    w_nq = _pad_halves(_heads(col("nsa_q"), NSA_HEADS, NSA_DK))
    q_n = proj_heads(n2d, bf(w_nq), batch, seq, rope=("swap", cos_n, sin_n), scale=NSA_DK ** -0.5)
    w_nk = _pad_halves(jnp.stack([col("nsa_kc"), col("nsa_ks"), col("nsa_kw")], axis=0))
    k_n = proj_heads(n2d, bf(w_nk), batch, seq, rope=("swap", cos_n, sin_n))
    dk_pad = w_nk.shape[-1]
    v_cmp = proj_plain(n2d, bf(col("nsa_vc")), BF16)
    vt_n = proj_heads_t(n2d, bf(jnp.stack([col("nsa_vs"), col("nsa_vw")], axis=0)), batch, seq)
    w_g = jnp.concatenate([col("nsa_g"), jnp.zeros((w_in.shape[0], LANES - NSA_HEADS * 3), F32)], axis=1)
    g_nsa = proj_plain(n2d, bf(w_g), F32, sigmoid=True)

    ng = seq // NSA_CMP_STRIDE
    w1k = _pad_halves(p["nsa_cmp_k_w1"].reshape(NSA_CMP_LEN, NSA_DK, NSA_CMP_HIDDEN), axis=1)
    kc = nsa_compress(k_n.reshape(batch, 3, ng, NSA_CMP_STRIDE * dk_pad), 0,
                      _pad_halves(p["nsa_cmp_k_pos"]), w1k.reshape(NSA_CMP_LEN * dk_pad, NSA_CMP_HIDDEN),
                      _pad_halves(p["nsa_cmp_k_w2"]))
    vc_tok = v_cmp.reshape(batch, 1, ng, NSA_CMP_STRIDE * NSA_DV)
    vct = nsa_compress(vc_tok, 0, p["nsa_cmp_v_pos"], p["nsa_cmp_v_w1"], p["nsa_cmp_v_w2"], transposed=True)
    o_c, sel = nsa_cmp_topk(q_n, kc, vct)
    o_s = nsa_selected(q_n, k_n, 1, vt_n, sel)
    o_w = nsa_window(q_n, k_n, 2, vt_n, 1)

    g_merge = proj_plain(n2d, bf(w_in[:, offs["merge"][0]:]), BF16, sigmoid=True)
    m = batch * seq
    return (o_mla.reshape(m, -1), o_c.reshape(m, -1), o_s.reshape(m, -1), o_w.reshape(m, -1), g_nsa,
            o_diff, g_merge)


def kernel(x, ffn1_norm, ffn1_w_in, ffn1_w_out, mix_norm, w_in, mla_q_norm, mla_kv_norm, mla_w_uq, mla_w_ukv, nsa_cmp_k_pos, nsa_cmp_k_w1, nsa_cmp_k_w2, nsa_cmp_v_pos, nsa_cmp_v_w1, nsa_cmp_v_w2, diff_lam_q1, diff_lam_k1, diff_lam_q2, diff_lam_k2, diff_subln, w_br_mla, w_br_nsa, w_br_diff, w_out, ffn2_norm, ffn2_w_in, ffn2_w_out, final_norm):
    batch, seq, d = x.shape
    depth = w_in.shape[0]
    x2d = x.reshape(batch * seq, d)
    bf = lambda a: a.astype(BF16)
    for l in range(depth):
        p = {"mla_q_norm": mla_q_norm[l], "mla_kv_norm": mla_kv_norm[l], "mla_w_uq": mla_w_uq[l],
             "mla_w_ukv": mla_w_ukv[l], "nsa_cmp_k_pos": nsa_cmp_k_pos[l], "nsa_cmp_k_w1": nsa_cmp_k_w1[l],
             "nsa_cmp_k_w2": nsa_cmp_k_w2[l], "nsa_cmp_v_pos": nsa_cmp_v_pos[l], "nsa_cmp_v_w1": nsa_cmp_v_w1[l],
             "nsa_cmp_v_w2": nsa_cmp_v_w2[l], "diff_lam_q1": diff_lam_q1[l], "diff_lam_k1": diff_lam_k1[l],
             "diff_lam_q2": diff_lam_q2[l], "diff_lam_k2": diff_lam_k2[l], "diff_subln": diff_subln[l]}
        x2d, n2d = ffn(x2d, ffn1_norm[l], bf(ffn1_w_in[l]), bf(ffn1_w_out[l]), post_norm=(mix_norm[l], BF16))
        o_mla, o_c, o_s, o_w, g_nsa, o_diff, g_merge = _mixers(n2d, batch, seq, l, w_in[l], p)
        x2d = merge(x2d, o_mla, o_c, o_s, o_w, g_nsa, o_diff, g_merge, bf(w_br_mla[l]), bf(w_br_nsa[l]),
                    bf(w_br_diff[l]).reshape(DIFF_HEADS, DIFF_VD, d), bf(w_out[l]), seq)
        if l + 1 < depth:
            x2d = ffn(x2d, ffn2_norm[l], bf(ffn2_w_in[l]), bf(ffn2_w_out[l]))
        else:
            _, out = ffn(x2d, ffn2_norm[l], bf(ffn2_w_in[l]), bf(ffn2_w_out[l]), post_norm=(final_norm, F32))
    return out.reshape(batch, seq, d)
```
